```python
import jax, jax.numpy as jnp
from jax import lax
import numpy as np

D_MODEL = 1024
BATCH = 8
SEQ = 4096
DEPTH = 2

CTX_LEN = 256
GRID_W = 64

RNN_WIDTH = 1024
RNN_HEADS = 8
RNN_BLOCK = RNN_WIDTH // RNN_HEADS
CONV_WIDTH = 4
CONV_LEFT = 2
LRU_C = 8.0

FOURIER_WIDTH = 512
FOURIER_GROUPS = 4
FOURIER_GROUP_DIM = FOURIER_WIDTH // FOURIER_GROUPS

N_BRANCHES = 2
IN_PROJ_WIDTH = 2 * RNN_WIDTH + FOURIER_WIDTH + N_BRANCHES * D_MODEL

N_EXPERTS = 16
N_EXPERT_GROUPS = 4
EXPERTS_PER_GROUP = N_EXPERTS // N_EXPERT_GROUPS
TOP_K = 2
D_EXPERT = 1024

N_MOD = 6
RMS_EPS = 1e-6

kernel_name = "hybrid_rglru_fourier_grouped_moe_dit"


def rms_norm(x, g):
    xf = x.astype(jnp.float32)
    y = xf * lax.rsqrt(jnp.mean(xf * xf, axis=-1, keepdims=True) + RMS_EPS)
    return (y * g.astype(jnp.float32)).astype(x.dtype)


def modulate(x, shift, scale):
    return x * (1 + scale) + shift


def centred_dwconv(u, w, b):
    L = u.shape[-2]
    pad = [(0, 0)] * (u.ndim - 2) + [(CONV_LEFT, CONV_WIDTH - 1 - CONV_LEFT), (0, 0)]
    up = jnp.pad(u, pad)
    out = b
    for k in range(CONV_WIDTH):
        out = out + up[..., k:k + L, :] * w[k]
    return out


def conv_tokens(u, w, b, on_grid):
    if on_grid:
        n_b, n_tok, ch = u.shape
        rows = n_tok // GRID_W
        return centred_dwconv(u.reshape(n_b, rows, GRID_W, ch), w, b).reshape(n_b, n_tok, ch)
    return centred_dwconv(u, w, b)


def rglru_coeffs(u, w_a, b_a, w_x, b_x, lam):
    n_b, n_tok, _ = u.shape
    uh = u.reshape(n_b, n_tok, RNN_HEADS, RNN_BLOCK)
    r = jax.nn.sigmoid(jnp.einsum('blhi,hij->blhj', uh, w_a) + b_a).reshape(n_b, n_tok, RNN_WIDTH)
    i = jax.nn.sigmoid(jnp.einsum('blhi,hij->blhj', uh, w_x) + b_x).reshape(n_b, n_tok, RNN_WIDTH)
    log_a = -LRU_C * r.astype(jnp.float32) * jax.nn.softplus(-lam.astype(jnp.float32))
    a = jnp.exp(log_a)
    mult = jnp.sqrt(-jnp.expm1(2.0 * log_a))
    return a, mult * (i * u).astype(jnp.float32)


def linear_scan(a, b, h0):
    b = b.at[:, 0].add(a[:, 0] * h0)

    def combine(left, right):
        a_l, b_l = left
        a_r, b_r = right
        return a_l * a_r, a_r * b_l + b_r

    _, h = lax.associative_scan(combine, (a, b), axis=1)
    return h


def rglru_bidirectional(u, h0_f, h0_b, lru_f, lru_b):
    a_f, b_f = rglru_coeffs(u, *lru_f)
    h_f = linear_scan(a_f, b_f, h0_f)
    a_b, b_b = rglru_coeffs(jnp.flip(u, 1), *lru_b)
    h_b = jnp.flip(linear_scan(a_b, b_b, h0_b), 1)
    return h_f, h_b


def fourier_mix(u):
    n_b, n_tok, _ = u.shape
    ug = u.astype(jnp.float32).reshape(n_b, n_tok, FOURIER_GROUPS, FOURIER_GROUP_DIM)
    y = jnp.fft.fft2(ug, axes=(1, 3), norm="ortho").real
    return y.reshape(n_b, n_tok, FOURIER_WIDTH).astype(u.dtype)


def mixer(h, on_grid, h0_f, h0_b, w_in, conv_w, conv_b, lru_f, lru_b, w_proj_rnn, w_proj_fourier, w_out):
    proj = h @ w_in
    s1 = RNN_WIDTH
    s2 = 2 * RNN_WIDTH
    s3 = s2 + FOURIER_WIDTH
    s4 = s3 + D_MODEL
    u_rnn, u_gate, u_four, g_rnn, g_four = jnp.split(proj, [s1, s2, s3, s4], axis=-1)
    u = conv_tokens(u_rnn, conv_w, conv_b, on_grid)
    h_f, h_b = rglru_bidirectional(u, h0_f, h0_b, lru_f, lru_b)
    y_rnn = ((h_f + h_b).astype(h.dtype) * jax.nn.gelu(u_gate)) @ w_proj_rnn
    y_four = fourier_mix(u_four) @ w_proj_fourier
    merged = jax.nn.sigmoid(g_rnn) * y_rnn + jax.nn.sigmoid(g_four) * y_four
    return merged @ w_out, h_f[:, -1], h_b[:, 0]


def context_scan_states(h, w_in, conv_w, conv_b, lru_f, lru_b):
    u = conv_tokens(h @ w_in[:, :RNN_WIDTH], conv_w, conv_b, False)
    zeros = jnp.zeros((h.shape[0], RNN_WIDTH), jnp.float32)
    h_f, h_b = rglru_bidirectional(u, zeros, zeros, lru_f, lru_b)
    return h_f[:, -1], h_b[:, 0]


def grouped_moe(h, router_w, router_b, w1, w3, w2):
    n_b, n_tok, d = h.shape
    t = h.reshape(-1, d)
    n_t = t.shape[0]
    aff = jax.nn.sigmoid((t @ router_w).astype(jnp.float32))
    sel = (aff + router_b.astype(jnp.float32)).reshape(n_t, N_EXPERT_GROUPS, EXPERTS_PER_GROUP)
    group_score = lax.top_k(sel, TOP_K)[0].sum(-1)
    grp = jnp.argmax(group_score, axis=-1)
    in_group = jax.nn.one_hot(grp, N_EXPERT_GROUPS, dtype=jnp.bool_)[:, :, None]
    masked = jnp.where(in_group, sel, -jnp.inf).reshape(n_t, N_EXPERTS)
    _, expert_idx = lax.top_k(masked, TOP_K)
    w_sel = jnp.take_along_axis(aff, expert_idx, axis=1)
    w_sel = w_sel / jnp.sum(w_sel, axis=-1, keepdims=True)
    gates = jnp.sum(jax.nn.one_hot(expert_idx, N_EXPERTS, dtype=jnp.float32) * w_sel[..., None], axis=1)
    gates = gates.astype(t.dtype)
    y = jnp.zeros_like(t)
    for e in range(N_EXPERTS):
        he = jax.nn.silu(t @ w1[e]) * (t @ w3[e])
        y = y + gates[:, e:e + 1] * (he @ w2[e])
    return y.reshape(n_b, n_tok, d)


def setup_inputs(seed: int = 0) -> dict:
    key = jax.random.key(seed)
    ks = jax.random.split(key, 32)
    f32 = jnp.float32

    def nrm(k, shape, scale):
        return jax.random.normal(k, shape, f32) * scale

    u_decay = jax.random.uniform(ks[20], (DEPTH, 2, RNN_WIDTH), f32, 0.9, 0.999)
    s = u_decay ** (1.0 / LRU_C)
    lru_lambda = jnp.log(s) - jnp.log1p(-s)

    return {
        "x": nrm(ks[0], (BATCH, SEQ, D_MODEL), 1.0),
        "c": nrm(ks[1], (BATCH, D_MODEL), 1.0),
        "ctx": nrm(ks[2], (BATCH, CTX_LEN, D_MODEL), 1.0),
        "c_ctx": nrm(ks[3], (D_MODEL,), 1.0),
        "ada_w": nrm(ks[4], (DEPTH, D_MODEL, N_MOD * D_MODEL), 0.5 * D_MODEL ** -0.5),
        "ada_b": nrm(ks[5], (DEPTH, N_MOD * D_MODEL), 0.02),
        "norm_mix_g": 1.0 + nrm(ks[6], (DEPTH, D_MODEL), 0.02),
        "w_in": nrm(ks[7], (DEPTH, D_MODEL, IN_PROJ_WIDTH), D_MODEL ** -0.5),
        "conv_w": nrm(ks[8], (DEPTH, CONV_WIDTH, RNN_WIDTH), CONV_WIDTH ** -0.5),
        "conv_b": nrm(ks[9], (DEPTH, RNN_WIDTH), 0.02),
        "lru_w_a": nrm(ks[10], (DEPTH, 2, RNN_HEADS, RNN_BLOCK, RNN_BLOCK), RNN_BLOCK ** -0.5),
        "lru_b_a": nrm(ks[11], (DEPTH, 2, RNN_HEADS, RNN_BLOCK), 0.02),
        "lru_w_x": nrm(ks[12], (DEPTH, 2, RNN_HEADS, RNN_BLOCK, RNN_BLOCK), RNN_BLOCK ** -0.5),
        "lru_b_x": nrm(ks[13], (DEPTH, 2, RNN_HEADS, RNN_BLOCK), 0.02),
        "lru_lambda": lru_lambda,
        "w_proj_rnn": nrm(ks[14], (DEPTH, RNN_WIDTH, D_MODEL), RNN_WIDTH ** -0.5),
        "w_proj_fourier": nrm(ks[15], (DEPTH, FOURIER_WIDTH, D_MODEL), FOURIER_WIDTH ** -0.5),
        "w_out": nrm(ks[16], (DEPTH, D_MODEL, D_MODEL), D_MODEL ** -0.5),
        "norm_ffn_g": 1.0 + nrm(ks[17], (DEPTH, D_MODEL), 0.02),
        "router_w": nrm(ks[18], (D_MODEL, N_EXPERTS), D_MODEL ** -0.5),
        "router_b": nrm(ks[19], (N_EXPERTS,), 0.01),
        "moe_w1": nrm(ks[21], (DEPTH, N_EXPERTS, D_MODEL, D_EXPERT), D_MODEL ** -0.5),
        "moe_w3": nrm(ks[22], (DEPTH, N_EXPERTS, D_MODEL, D_EXPERT), D_MODEL ** -0.5),
        "moe_w2": nrm(ks[23], (DEPTH, N_EXPERTS, D_EXPERT, D_MODEL), D_EXPERT ** -0.5),
        "final_norm_g": 1.0 + nrm(ks[24], (D_MODEL,), 0.02),
    }


def reference(x, c, ctx, c_ctx, ada_w, ada_b, norm_mix_g, w_in, conv_w, conv_b, lru_w_a, lru_b_a,
              lru_w_x, lru_b_x, lru_lambda, w_proj_rnn, w_proj_fourier, w_out, norm_ffn_g,
              router_w, router_b, moe_w1, moe_w3, moe_w2, final_norm_g):
    h_ctx = ctx
    n_b = x.shape[0]
    zeros = jnp.zeros((n_b, RNN_WIDTH), jnp.float32)
    for l in range(DEPTH):
        last = l == DEPTH - 1
        mod_x = (jax.nn.silu(c) @ ada_w[l] + ada_b[l])[:, None, :]
        mod_c = (jax.nn.silu(c_ctx) @ ada_w[l] + ada_b[l])[None, None, :]
        sh1_x, sc1_x, g1_x, sh2_x, sc2_x, g2_x = jnp.split(mod_x, N_MOD, axis=-1)
        sh1_c, sc1_c, g1_c, sh2_c, sc2_c, g2_c = jnp.split(mod_c, N_MOD, axis=-1)
        lru_f = (lru_w_a[l, 0], lru_b_a[l, 0], lru_w_x[l, 0], lru_b_x[l, 0], lru_lambda[l, 0])
        lru_b = (lru_w_a[l, 1], lru_b_a[l, 1], lru_w_x[l, 1], lru_b_x[l, 1], lru_lambda[l, 1])

        hc = modulate(rms_norm(h_ctx, norm_mix_g[l]), sh1_c, sc1_c)
        if last:
            state_f, state_b = context_scan_states(hc, w_in[l], conv_w[l], conv_b[l], lru_f, lru_b)
        else:
            out_c, state_f, state_b = mixer(hc, False, zeros, zeros, w_in[l], conv_w[l], conv_b[l],
                                            lru_f, lru_b, w_proj_rnn[l], w_proj_fourier[l], w_out[l])
            h_ctx = h_ctx + g1_c * out_c

        hx = modulate(rms_norm(x, norm_mix_g[l]), sh1_x, sc1_x)
        out_x, _, _ = mixer(hx, True, state_f, state_b, w_in[l], conv_w[l], conv_b[l],
                            lru_f, lru_b, w_proj_rnn[l], w_proj_fourier[l], w_out[l])
        x = x + g1_x * out_x

        if not last:
            hc2 = modulate(rms_norm(h_ctx, norm_ffn_g[l]), sh2_c, sc2_c)
            h_ctx = h_ctx + g2_c * grouped_moe(hc2, router_w, router_b, moe_w1[l], moe_w3[l], moe_w2[l])
        hx2 = modulate(rms_norm(x, norm_ffn_g[l]), sh2_x, sc2_x)
        x = x + g2_x * grouped_moe(hx2, router_w, router_b, moe_w1[l], moe_w3[l], moe_w2[l])

    return rms_norm(x, final_norm_g)
```

```python
import functools
import math

import jax
import jax.numpy as jnp
from jax import lax
from jax.experimental import pallas as pl
from jax.experimental.pallas import tpu as pltpu

F32 = jnp.float32
BF16 = jnp.bfloat16

RMS_EPS = 1e-6
LRU_C = 8.0
GRID_W = 64
N_MOD = 6
FOURIER_GROUPS = 4
N_EXPERT_GROUPS = 4
LANES = 128
SUBLANES = 8
TM = 256
TS = 128
SUB = 64
MOD_ROWS = 16
VMEM_LIMIT = 56 * 1024 * 1024


def _dot(a, b):
    return jnp.dot(a, b, preferred_element_type=F32)


def _split(a):
    hi = a.astype(BF16)
    lo = (a - hi.astype(F32)).astype(BF16)
    return hi, lo


def _dot3(a, b):
    ah, al = _split(a)
    bh, bl = _split(b)
    return _dot(ah, bh) + _dot(al, bh) + _dot(ah, bl)


def _dot3_nt(a, b):
    dn = (((1,), (1,)), ((), ()))
    ah, al = _split(a)
    bh, bl = _split(b)
    f = lambda p, q: lax.dot_general(p, q, dn, preferred_element_type=F32)
    return f(ah, bh) + f(al, bh) + f(ah, bl)


def _gelu_tanh(x):
    return 0.5 * x * (1.0 + jnp.tanh(math.sqrt(2.0 / math.pi) * (x + 0.044715 * (x * x * x))))


def _rms(x, g):
    return x * lax.rsqrt(jnp.mean(x * x, axis=-1, keepdims=True) + RMS_EPS) * g


def _params(sem, vmem=VMEM_LIMIT):
    return pltpu.CompilerParams(dimension_semantics=sem, vmem_limit_bytes=vmem)


def _ada_body(cc_ref, w_ref, b_ref, o_ref):
    cc = cc_ref[...]
    s = cc * jax.nn.sigmoid(cc)
    o_ref[0] = _dot3(s, w_ref[0]) + b_ref[0]


def _ada(cc, ada_w, ada_b):
    depth, d, n = ada_w.shape
    tn = 1024
    return pl.pallas_call(
        _ada_body,
        grid=(depth, n // tn),
        in_specs=[
            pl.BlockSpec((MOD_ROWS, d), lambda l, j: (0, 0)),
            pl.BlockSpec((1, d, tn), lambda l, j: (l, 0, j)),
            pl.BlockSpec((1, 1, tn), lambda l, j: (l, 0, j)),
        ],
        out_specs=pl.BlockSpec((1, MOD_ROWS, tn), lambda l, j: (l, 0, j)),
        out_shape=jax.ShapeDtypeStruct((depth, MOD_ROWS, n), F32),
        compiler_params=_params(("arbitrary", "arbitrary")),
        name="ada_mod",
    )(cc, ada_w, ada_b.reshape(depth, 1, n))


def _inproj_body(x_ref, mod_ref, g_ref, w_ref, cw_ref, cb_ref, fcs_ref,
                 u_ref, gl_ref, sa_ref, sb_ref, pq_ref, *, n_lat, d, r, f):
    i = pl.program_id(0)
    b = pl.program_id(1)
    mod = mod_ref[0]
    h = _rms(x_ref[0], g_ref[...])
    h = (h * (1.0 + mod[:, d:2 * d]) + mod[:, 0:d]).astype(BF16)

    u_raw = _dot(h, w_ref[:, 0:r])
    gw = jnp.where(i < n_lat, GRID_W, TM)
    pos = lax.broadcasted_iota(jnp.int32, (TM, 1), 0) & (gw - 1)
    u = cb_ref[...] + cw_ref[2:3, :] * u_raw
    for k, s in ((0, -2), (1, -1), (3, 1)):
        shifted = pltpu.roll(u_raw, (-s) % TM, 0)
        ok = (pos + s >= 0) & (pos + s < gw)
        u = u + cw_ref[k:k + 1, :] * jnp.where(ok, shifted, 0.0)
    for g in range(r // LANES):
        u_ref[g, pl.ds(b, TM, stride=SUBLANES), :] = u[:, g * LANES:(g + 1) * LANES]

    gl_ref[0] = _gelu_tanh(_dot(h, w_ref[:, r:2 * r])).astype(BF16)
    u4 = _dot(h, w_ref[:, 2 * r:2 * r + f]).astype(BF16)
    pq_ref[...] = _dot(u4, fcs_ref[...]).astype(BF16)
    s3 = 2 * r + f
    sa_ref[0] = jax.nn.sigmoid(_dot(h, w_ref[:, s3:s3 + d])).astype(BF16)
    sb_ref[0] = jax.nn.sigmoid(_dot(h, w_ref[:, s3 + d:s3 + 2 * d])).astype(BF16)


def _inproj(xa, mod, g, w_in, conv_w, conv_b, fcs, *, n_lat):
    nb, s, d = xa.shape
    r = conv_w.shape[1]
    f = fcs.shape[0]
    nt = s // TM
    nh = r // LANES
    mod_row = lambda i, b: (jnp.where(i < n_lat, b, nb), 0, 0)
    act = pl.BlockSpec((1, TM, d), lambda i, b: (b, i, 0))
    full = lambda shape: pl.BlockSpec(shape, lambda i, b: (0,) * len(shape))
    return pl.pallas_call(
        functools.partial(_inproj_body, n_lat=n_lat, d=d, r=r, f=f),
        grid=(nt, nb),
        in_specs=[
            act,
            pl.BlockSpec((1, 1, N_MOD * d), mod_row),
            full((1, d)),
            full(w_in.shape),
            full(conv_w.shape),
            full((1, r)),
            full(fcs.shape),
        ],
        out_specs=[
            pl.BlockSpec((nh, TM * nb, LANES), lambda i, b: (0, i, 0)),
            act, act, act,
            pl.BlockSpec((TM, 2 * f), lambda i, b: (i, b)),
        ],
        out_shape=[
            jax.ShapeDtypeStruct((nh, s * nb, LANES), F32),
            jax.ShapeDtypeStruct((nb, s, d), BF16),
            jax.ShapeDtypeStruct((nb, s, d), BF16),
            jax.ShapeDtypeStruct((nb, s, d), BF16),
            jax.ShapeDtypeStruct((s, nb * 2 * f), BF16),
        ],
        compiler_params=_params(("arbitrary", "arbitrary")),
        name="in_proj",
    )(xa, mod, g.reshape(1, d), w_in, conv_w, conv_b.reshape(1, r), fcs)


def _scan_body(*refs, reverse, merge, nb):
    if merge:
        u_ref, wg_ref, bg_ref, lam_ref, hb_ref, gl_ref, o_ref, h_scr, a_scr, b_scr, hs_scr = refs
    else:
        u_ref, wg_ref, bg_ref, lam_ref, o_ref, h_scr, a_scr, b_scr = refs
        hs_scr = o_ref
    nh = u_ref.shape[0]
    rows = SUB * nb

    @pl.when(pl.program_id(0) == 0)
    def _():
        h_scr[...] = jnp.zeros_like(h_scr)

    lam = lam_ref[...]
    sp = jnp.maximum(-lam, 0.0) + jnp.log1p(jnp.exp(-jnp.abs(lam)))

    subs = range(TS // SUB)
    for sub in (reversed(subs) if reverse else subs):
        r0 = sub * rows
        for g in range(nh):
            ug = u_ref[g, r0:r0 + rows, :]
            gates = _dot(ug.astype(BF16), wg_ref[g]) + bg_ref[g]
            rg = jax.nn.sigmoid(gates[:, :LANES])
            ig = jax.nn.sigmoid(gates[:, LANES:])
            log_a = (-LRU_C) * rg * sp[g]
            a = jnp.exp(log_a)
            mult = jnp.sqrt(-jnp.tanh(log_a) * (a * a + 1.0))
            a_scr[g] = a
            b_scr[g] = mult * (ig * ug)

        def step(k, hs):
            t = (SUB - 1 - k) if reverse else k
            off = pl.multiple_of(t * nb, nb)
            new = []
            for g in range(nh):
                hg = a_scr[g, pl.ds(off, nb), :] * hs[g] + b_scr[g, pl.ds(off, nb), :]
                hs_scr[g, pl.ds(r0 + off, nb), :] = hg
                new.append(hg)
            return tuple(new)

        hs = lax.fori_loop(0, SUB, step, tuple(h_scr[g] for g in range(nh)), unroll=8)
        for g in range(nh):
            h_scr[g] = hs[g]

    if merge:
        for b in range(nb):
            hrow = jnp.concatenate(
                [hs_scr[g, pl.ds(b, TS, stride=nb), :] + hb_ref[g, pl.ds(b, TS, stride=nb), :]
                 for g in range(nh)], axis=-1)
            o_ref[b] = (hrow * gl_ref[b].astype(F32)).astype(BF16)


def _scan(u_tm, wg, bg, lam, *, nb, n_lat_steps, reverse, hb=None, gl=None):
    nh, rows_all, _ = u_tm.shape
    s = rows_all // nb
    n = s // TS
    merge = hb is not None
    n_ctx = n - n_lat_steps
    if reverse:
        order = lambda j: jnp.where(j < n_ctx, n - 1 - j, n_lat_steps - 1 - (j - n_ctx))
    else:
        order = lambda j: jnp.where(j < n_ctx, n_lat_steps + j, j - n_ctx)
    slab = pl.BlockSpec((nh, TS * nb, LANES), lambda j: (0, order(j), 0))
    full = lambda shape: pl.BlockSpec(shape, lambda j: (0,) * len(shape))
    in_specs = [slab, full(wg.shape), full(bg.shape), full(lam.shape)]
    args = [u_tm, wg, bg, lam]
    scratch = [
        pltpu.VMEM((nh, nb, LANES), F32),
        pltpu.VMEM((nh, SUB * nb, LANES), F32),
        pltpu.VMEM((nh, SUB * nb, LANES), F32),
    ]
    if merge:
        d = nh * LANES
        bm = pl.BlockSpec((nb, TS, d), lambda j: (0, order(j), 0))
        in_specs += [slab, bm]
        args += [hb, gl]
        out_specs = bm
        out_shape = jax.ShapeDtypeStruct((nb, s, d), BF16)
        scratch.append(pltpu.VMEM((nh, TS * nb, LANES), F32))
    else:
        out_specs = slab
        out_shape = jax.ShapeDtypeStruct(u_tm.shape, F32)
    return pl.pallas_call(
        functools.partial(_scan_body, reverse=reverse, merge=merge, nb=nb),
        grid=(n,),
        in_specs=in_specs,
        out_specs=out_specs,
        out_shape=out_shape,
        scratch_shapes=scratch,
        compiler_params=_params(("arbitrary",)),
        name="scan_fwd_merge" if merge else "scan_bwd",
    )(*args)


def _dft_body(c_ref, s_ref, p_ref, q_ref, *rest):
    o_ref = rest[-1]
    o_ref[...] = (_dot(c_ref[...], p_ref[...]) - _dot(s_ref[...], q_ref[...])).astype(BF16)


def _dft(cm, sm, pq, *, nb, f, row0, prev=None):
    n = cm.shape[0]
    s = pq.shape[0]
    tmk = min(n, 512)
    rb = row0 // n
    in_specs = [
        pl.BlockSpec((tmk, n), lambda m, b: (m, 0)),
        pl.BlockSpec((tmk, n), lambda m, b: (m, 0)),
        pl.BlockSpec((n, f), lambda m, b: (rb, 2 * b)),
        pl.BlockSpec((n, f), lambda m, b: (rb, 2 * b + 1)),
    ]
    args = [cm, sm, pq, pq]
    aliases = {}
    if prev is not None:
        in_specs.append(pl.BlockSpec(memory_space=pl.ANY))
        args.append(prev)
        aliases = {4: 0}
    return pl.pallas_call(
        _dft_body,
        grid=(n // tmk, nb),
        in_specs=in_specs,
        out_specs=pl.BlockSpec((tmk, f), lambda m, b: (row0 // tmk + m, b)),
        out_shape=jax.ShapeDtypeStruct((s, nb * f), BF16),
        input_output_aliases=aliases,
        compiler_params=_params(("arbitrary", "arbitrary")),
        name="pos_dft",
    )(*args)


def _top2(vals):
    def first_max(vs):
        m = functools.reduce(jnp.maximum, vs)
        idx = jnp.full(m.shape, len(vs) - 1, jnp.int32)
        for k in range(len(vs) - 2, -1, -1):
            idx = jnp.where(vs[k] == m, k, idx)
        return m, idx
    m1, i1 = first_max(vals)
    rest = [jnp.where(i1 == k, -jnp.inf, v) for k, v in enumerate(vals)]
    m2, i2 = first_max(rest)
    return m1, i1, m2, i2


def _pick(idx, vals):
    out = vals[-1]
    for k in range(len(vals) - 2, -1, -1):
        out = jnp.where(idx == k, vals[k], out)
    return out


def _mix_body(hg_ref, yf_ref, sa_ref, sb_ref, x_ref, mod_ref, wpr_ref, wpf_ref, wo_ref,
              gffn_ref, rwt_ref, rb_ref, xo_ref, hx_ref, gt_ref, *, d, n_exp):
    mod = mod_ref[0]
    y_r = _dot(hg_ref[0], wpr_ref[...])
    y_f = _dot(yf_ref[...], wpf_ref[...])
    merged = sa_ref[0].astype(F32) * y_r + sb_ref[0].astype(F32) * y_f
    out = _dot(merged.astype(BF16), wo_ref[...])
    xn = x_ref[0] + mod[:, 2 * d:3 * d] * out
    xo_ref[0] = xn
    h2 = _rms(xn, gffn_ref[...])
    h2 = h2 * (1.0 + mod[:, 4 * d:5 * d]) + mod[:, 3 * d:4 * d]
    hx_ref[0] = h2.astype(BF16)

    aff = jax.nn.sigmoid(_dot3_nt(rwt_ref[...], h2))
    sel = aff + rb_ref[...]
    per = n_exp // N_EXPERT_GROUPS
    srow = [sel[e:e + 1, :] for e in range(n_exp)]
    arow = [aff[e:e + 1, :] for e in range(n_exp)]
    scores = []
    for gi in range(N_EXPERT_GROUPS):
        m1, _, m2, _ = _top2(srow[gi * per:(gi + 1) * per])
        scores.append(m1 + m2)
    best = functools.reduce(jnp.maximum, scores)
    grp = jnp.full(best.shape, N_EXPERT_GROUPS - 1, jnp.int32)
    for gi in range(N_EXPERT_GROUPS - 2, -1, -1):
        grp = jnp.where(scores[gi] == best, gi, grp)
    v = [_pick(grp, [srow[gi * per + k] for gi in range(N_EXPERT_GROUPS)]) for k in range(per)]
    a = [_pick(grp, [arow[gi * per + k] for gi in range(N_EXPERT_GROUPS)]) for k in range(per)]
    _, i1, _, i2 = _top2(v)
    w1 = _pick(i1, a)
    w2 = _pick(i2, a)
    den = w1 + w2
    e1 = grp * per + i1
    e2 = grp * per + i2
    erow = lax.broadcasted_iota(jnp.int32, (LANES, TM), 0)
    gates_t = jnp.where(erow == e1, w1 / den, 0.0) + jnp.where(erow == e2, w2 / den, 0.0)
    gt_ref[0] = gates_t.T


def _mix(hg, yf, sa, sb, xa, mod, wpr, wpf, wo, gffn, rwt, rb, *, n_tiles, n_lat):
    nb, s, d = xa.shape
    f = wpf.shape[0]
    n_exp = rwt.shape[0]
    act = pl.BlockSpec((1, TM, d), lambda i, b: (b, i, 0))
    full = lambda shape: pl.BlockSpec(shape, lambda i, b: (0,) * len(shape))
    mod_row = lambda i, b: (jnp.where(i < n_lat, b, nb), 0, 0)
    return pl.pallas_call(
        functools.partial(_mix_body, d=d, n_exp=n_exp),
        grid=(n_tiles, nb),
        in_specs=[
            act,
            pl.BlockSpec((TM, f), lambda i, b: (i, b)),
            act, act, act,
            pl.BlockSpec((1, 1, N_MOD * d), mod_row),
            full(wpr.shape), full(wpf.shape), full(wo.shape),
            full((1, d)), full(rwt.shape), full((n_exp, 1)),
        ],
        out_specs=[act, act, pl.BlockSpec((1, TM, LANES), lambda i, b: (b, i, 0))],
        out_shape=[
            jax.ShapeDtypeStruct((nb, s, d), F32),
            jax.ShapeDtypeStruct((nb, s, d), BF16),
            jax.ShapeDtypeStruct((nb, s, LANES), F32),
        ],
        compiler_params=_params(("arbitrary", "arbitrary")),
        name="mix_out",
    )(hg, yf, sa, sb, xa, mod, wpr, wpf, wo, gffn.reshape(1, d), rwt, rb.reshape(n_exp, 1))


def _moe_body(*refs, d, n_exp, final):
    if final:
        hx_ref, gt_ref, xr_ref, mod_ref, w1_ref, w3_ref, w2_ref, gfin_ref, o_ref, acc_ref = refs
    else:
        hx_ref, gt_ref, xr_ref, mod_ref, w1_ref, w3_ref, w2_ref, o_ref, acc_ref = refs
    e = pl.program_id(2)

    @pl.when(e == 0)
    def _():
        acc_ref[...] = jnp.zeros_like(acc_ref)

    x = hx_ref[0]
    a = _dot(x, w1_ref[0])
    h = (a * jax.nn.sigmoid(a)) * _dot(x, w3_ref[0])
    y = _dot(h.astype(BF16), w2_ref[0])
    gt = gt_ref[0]
    lane = lax.broadcasted_iota(jnp.int32, gt.shape, 1)
    ge = jnp.sum(jnp.where(lane == e, gt, 0.0), axis=1, keepdims=True)
    acc_ref[...] += ge * y

    @pl.when(e == n_exp - 1)
    def _():
        xn = xr_ref[0] + mod_ref[0][:, 5 * d:6 * d] * acc_ref[...]
        if final:
            o_ref[0] = _rms(xn, gfin_ref[...])
        else:
            o_ref[0] = xn


def _moe(hx, gt, xres, mod, w1, w3, w2, *, tm, blk0, n_blk, ctx_mod, gfin=None, seq_out=None):
    nb, s, d = xres.shape
    n_exp = w1.shape[0]
    final = gfin is not None
    act = lambda w: pl.BlockSpec((1, tm, w), lambda b, i, e: (b, blk0 + i, 0))
    wspec = lambda w: pl.BlockSpec((1,) + w.shape[1:], lambda b, i, e: (e, 0, 0))
    mod_row = (lambda b, i, e: (nb, 0, 0)) if ctx_mod else (lambda b, i, e: (b, 0, 0))
    in_specs = [act(d), act(LANES), act(d), pl.BlockSpec((1, 1, N_MOD * d), mod_row),
                wspec(w1), wspec(w3), wspec(w2)]
    args = [hx, gt, xres, mod, w1, w3, w2]
    if final:
        in_specs.append(pl.BlockSpec((1, d), lambda b, i, e: (0, 0)))
        args.append(gfin.reshape(1, d))
        out_shape = jax.ShapeDtypeStruct((nb, seq_out, d), F32)
        out_specs = pl.BlockSpec((1, tm, d), lambda b, i, e: (b, i, 0))
        aliases = {}
    else:
        out_shape = jax.ShapeDtypeStruct((nb, s, d), F32)
        out_specs = act(d)
        aliases = {2: 0}
    return pl.pallas_call(
        functools.partial(_moe_body, d=d, n_exp=n_exp, final=final),
        grid=(nb, n_blk, n_exp),
        in_specs=in_specs,
        out_specs=out_specs,
        out_shape=out_shape,
        scratch_shapes=[pltpu.VMEM((tm, d), F32)],
        input_output_aliases=aliases,
        compiler_params=_params(("arbitrary", "arbitrary", "arbitrary")),
        name="moe_dense",
    )(*args)


def _dft_tables(n):
    k = jnp.arange(n, dtype=jnp.int32)
    ang = ((k[:, None] * k[None, :]) % n).astype(F32) * (2.0 * math.pi / n)
    scale = 1.0 / math.sqrt(n)
    return jnp.cos(ang) * scale, jnp.sin(ang) * scale


def kernel(x, c, ctx, c_ctx, ada_w, ada_b, norm_mix_g, w_in, conv_w, conv_b, lru_w_a, lru_b_a, lru_w_x, lru_b_x, lru_lambda, w_proj_rnn, w_proj_fourier, w_out, norm_ffn_g, router_w, router_b, moe_w1, moe_w3, moe_w2, final_norm_g):
    nb, l, d = x.shape
    lc = ctx.shape[1]
    depth = ada_w.shape[0]
    r = conv_w.shape[2]
    f = w_proj_fourier.shape[1]
    nh, blk = lru_w_a.shape[2], lru_w_a.shape[3]
    assert nb == SUBLANES and blk == LANES and lc == TM and l % 1024 == 0 and l % lc == 0
    assert GRID_W & (GRID_W - 1) == 0 and TM % GRID_W == 0 and nb < MOD_ROWS
    s = l + lc
    n_lat = l // TM

    xa = jnp.concatenate([x, ctx], axis=1)
    cc = jnp.zeros((MOD_ROWS, d), F32).at[:nb].set(c).at[nb].set(c_ctx)
    mod = _ada(cc, ada_w, ada_b).reshape(depth, MOD_ROWS, 1, N_MOD * d)

    cl, sl = (t.astype(BF16) for t in _dft_tables(l))
    cx, sx = (t.astype(BF16) for t in _dft_tables(lc))
    cch, sch = _dft_tables(f // FOURIER_GROUPS)
    eye = jnp.eye(FOURIER_GROUPS, dtype=F32)
    fcs = jnp.concatenate([jnp.kron(eye, cch), jnp.kron(eye, sch)], axis=1).astype(BF16)
    rwt = router_w.T

    out = None
    for li in range(depth):
        last = li == depth - 1
        u_tm, gl, sa, sb, pq = _inproj(xa, mod[li], norm_mix_g[li], w_in[li].astype(BF16),
                                       conv_w[li], conv_b[li], fcs, n_lat=n_lat)
        wg = [jnp.concatenate([lru_w_a[li, k], lru_w_x[li, k]], axis=-1).astype(BF16) for k in range(2)]
        bg = [jnp.concatenate([lru_b_a[li, k], lru_b_x[li, k]], axis=-1).reshape(nh, 1, 2 * blk) for k in range(2)]
        lam = [lru_lambda[li, k].reshape(nh, 1, blk) for k in range(2)]
        hb = _scan(u_tm, wg[1], bg[1], lam[1], nb=nb, n_lat_steps=l // TS, reverse=True)
        hg = _scan(u_tm, wg[0], bg[0], lam[0], nb=nb, n_lat_steps=l // TS, reverse=False, hb=hb, gl=gl)
        yf = _dft(cl, sl, pq, nb=nb, f=f, row0=0)
        if not last:
            yf = _dft(cx, sx, pq, nb=nb, f=f, row0=l, prev=yf)
        x1, hx, gt = _mix(hg, yf, sa, sb, xa, mod[li], w_proj_rnn[li].astype(BF16),
                          w_proj_fourier[li].astype(BF16), w_out[li].astype(BF16), norm_ffn_g[li],
                          rwt, router_b, n_tiles=n_lat if last else s // TM, n_lat=n_lat)
        w1, w3, w2 = (w[li].astype(BF16) for w in (moe_w1, moe_w3, moe_w2))
        if last:
            out = _moe(hx, gt, x1, mod[li], w1, w3, w2, tm=1024, blk0=0, n_blk=l // 1024,
                       ctx_mod=False, gfin=final_norm_g, seq_out=l)
        else:
            x2 = _moe(hx, gt, x1, mod[li], w1, w3, w2, tm=1024, blk0=0, n_blk=l // 1024, ctx_mod=False)
            xa = _moe(hx, gt, x2, mod[li], w1, w3, w2, tm=TM, blk0=l // TM, n_blk=lc // TM, ctx_mod=True)
    return out
```

```python
import functools
import math

import jax
import jax.numpy as jnp
from jax import lax
from jax.experimental import pallas as pl
from jax.experimental.pallas import tpu as pltpu

F32 = jnp.float32
BF16 = jnp.bfloat16

RMS_EPS = 1e-6
LRU_C = 8.0
GRID_W = 64
N_MOD = 6
FOURIER_GROUPS = 4
N_EXPERT_GROUPS = 4
LANES = 128
SUBLANES = 8
TM = 256
TS = 128
SUB = 64
MOE_TM = 512
MOD_ROWS = 16
VMEM_LIMIT = 56 * 1024 * 1024


def _dot(a, b):
    return jnp.dot(a, b, preferred_element_type=F32)


def _split(a):
    hi = a.astype(BF16)
    lo = (a - hi.astype(F32)).astype(BF16)
    return hi, lo


def _dot3(a, b):
    ah, al = _split(a)
    bh, bl = _split(b)
    return _dot(ah, bh) + _dot(al, bh) + _dot(ah, bl)


def _dot3_nt(a, b):
    dn = (((1,), (1,)), ((), ()))
    ah, al = _split(a)
    bh, bl = _split(b)
    f = lambda p, q: lax.dot_general(p, q, dn, preferred_element_type=F32)
    return f(ah, bh) + f(al, bh) + f(ah, bl)


def _gelu_tanh(x):
    return 0.5 * x * (1.0 + jnp.tanh(math.sqrt(2.0 / math.pi) * (x + 0.044715 * (x * x * x))))


def _rms(x, g):
    return x * lax.rsqrt(jnp.mean(x * x, axis=-1, keepdims=True) + RMS_EPS) * g


def _params(sem, vmem=VMEM_LIMIT):
    return pltpu.CompilerParams(dimension_semantics=sem, vmem_limit_bytes=vmem)


def _ada_body(cc_ref, w_ref, b_ref, o_ref):
    cc = cc_ref[...]
    s = cc * jax.nn.sigmoid(cc)
    o_ref[0] = _dot3(s, w_ref[0]) + b_ref[0]


def _ada(cc, ada_w, ada_b):
    depth, d, n = ada_w.shape
    tn = 1024
    return pl.pallas_call(
        _ada_body,
        grid=(depth, n // tn),
        in_specs=[
            pl.BlockSpec((MOD_ROWS, d), lambda l, j: (0, 0)),
            pl.BlockSpec((1, d, tn), lambda l, j: (l, 0, j)),
            pl.BlockSpec((1, 1, tn), lambda l, j: (l, 0, j)),
        ],
        out_specs=pl.BlockSpec((1, MOD_ROWS, tn), lambda l, j: (l, 0, j)),
        out_shape=jax.ShapeDtypeStruct((depth, MOD_ROWS, n), F32),
        compiler_params=_params(("arbitrary", "arbitrary")),
        name="ada_mod",
    )(cc, ada_w, ada_b.reshape(depth, 1, n))


def _inproj_body(x_ref, mod_ref, g_ref, w_ref, cw_ref, cb_ref, fcs_ref,
                 u_ref, gl_ref, sa_ref, sb_ref, pq_ref, *, n_lat, d, r, f):
    i = pl.program_id(0)
    b = pl.program_id(1)
    mod = mod_ref[0]
    h = _rms(x_ref[0], g_ref[...])
    h = (h * (1.0 + mod[:, d:2 * d]) + mod[:, 0:d]).astype(BF16)

    u_raw = _dot(h, w_ref[:, 0:r])
    gw = jnp.where(i < n_lat, GRID_W, TM)
    pos = lax.broadcasted_iota(jnp.int32, (TM, 1), 0) & (gw - 1)
    u = cb_ref[...] + cw_ref[2:3, :] * u_raw
    for k, s in ((0, -2), (1, -1), (3, 1)):
        shifted = pltpu.roll(u_raw, (-s) % TM, 0)
        ok = (pos + s >= 0) & (pos + s < gw)
        u = u + cw_ref[k:k + 1, :] * jnp.where(ok, shifted, 0.0)
    for g in range(r // LANES):
        u_ref[g, pl.ds(b, TM, stride=SUBLANES), :] = u[:, g * LANES:(g + 1) * LANES]

    gl_ref[0] = _gelu_tanh(_dot(h, w_ref[:, r:2 * r])).astype(BF16)
    u4 = _dot(h, w_ref[:, 2 * r:2 * r + f]).astype(BF16)
    pq_ref[...] = _dot(u4, fcs_ref[...]).astype(BF16)
    s3 = 2 * r + f
    sa_ref[0] = jax.nn.sigmoid(_dot(h, w_ref[:, s3:s3 + d])).astype(BF16)
    sb_ref[0] = jax.nn.sigmoid(_dot(h, w_ref[:, s3 + d:s3 + 2 * d])).astype(BF16)


def _inproj(xa, mod, g, w_in, conv_w, conv_b, fcs, *, n_lat):
    nb, s, d = xa.shape
    r = conv_w.shape[1]
    f = fcs.shape[0]
    nt = s // TM
    nh = r // LANES
    mod_row = lambda i, b: (jnp.where(i < n_lat, b, nb), 0, 0)
    act = pl.BlockSpec((1, TM, d), lambda i, b: (b, i, 0))
    full = lambda shape: pl.BlockSpec(shape, lambda i, b: (0,) * len(shape))
    return pl.pallas_call(
        functools.partial(_inproj_body, n_lat=n_lat, d=d, r=r, f=f),
        grid=(nt, nb),
        in_specs=[
            act,
            pl.BlockSpec((1, 1, N_MOD * d), mod_row),
            full((1, d)),
            full(w_in.shape),
            full(conv_w.shape),
            full((1, r)),
            full(fcs.shape),
        ],
        out_specs=[
            pl.BlockSpec((nh, TM * nb, LANES), lambda i, b: (0, i, 0)),
            act, act, act,
            pl.BlockSpec((TM, 2 * f), lambda i, b: (i, b)),
        ],
        out_shape=[
            jax.ShapeDtypeStruct((nh, s * nb, LANES), F32),
            jax.ShapeDtypeStruct((nb, s, d), BF16),
            jax.ShapeDtypeStruct((nb, s, d), BF16),
            jax.ShapeDtypeStruct((nb, s, d), BF16),
            jax.ShapeDtypeStruct((s, nb * 2 * f), BF16),
        ],
        compiler_params=_params(("arbitrary", "arbitrary")),
        name="in_proj",
    )(xa, mod, g.reshape(1, d), w_in, conv_w, conv_b.reshape(1, r), fcs)


def _scan_body(*refs, reverse, merge, nb):
    if merge:
        u_ref, wg_ref, bg_ref, lam_ref, hb_ref, gl_ref, o_ref, h_scr, a_scr, b_scr, hs_scr = refs
    else:
        u_ref, wg_ref, bg_ref, lam_ref, o_ref, h_scr, a_scr, b_scr = refs
        hs_scr = o_ref
    nh = u_ref.shape[0]
    rows = SUB * nb

    @pl.when(pl.program_id(0) == 0)
    def _():
        h_scr[...] = jnp.zeros_like(h_scr)

    lam = lam_ref[...]
    sp = jnp.maximum(-lam, 0.0) + jnp.log1p(jnp.exp(-jnp.abs(lam)))

    subs = range(TS // SUB)
    for sub in (reversed(subs) if reverse else subs):
        r0 = sub * rows
        for g in range(nh):
            ug = u_ref[g, r0:r0 + rows, :]
            gates = _dot(ug.astype(BF16), wg_ref[g]) + bg_ref[g]
            rg = jax.nn.sigmoid(gates[:, :LANES])
            ig = jax.nn.sigmoid(gates[:, LANES:])
            log_a = (-LRU_C) * rg * sp[g]
            a = jnp.exp(log_a)
            mult = jnp.sqrt(-jnp.tanh(log_a) * (a * a + 1.0))
            a_scr[g] = a
            b_scr[g] = mult * (ig * ug)

        def step(k, hs):
            t = (SUB - 1 - k) if reverse else k
            off = pl.multiple_of(t * nb, nb)
            new = []
            for g in range(nh):
                hg = a_scr[g, pl.ds(off, nb), :] * hs[g] + b_scr[g, pl.ds(off, nb), :]
                hs_scr[g, pl.ds(r0 + off, nb), :] = hg
                new.append(hg)
            return tuple(new)

        hs = lax.fori_loop(0, SUB, step, tuple(h_scr[g] for g in range(nh)), unroll=8)
        for g in range(nh):
            h_scr[g] = hs[g]

    if merge:
        for b in range(nb):
            hrow = jnp.concatenate(
                [hs_scr[g, pl.ds(b, TS, stride=nb), :] + hb_ref[g, pl.ds(b, TS, stride=nb), :]
                 for g in range(nh)], axis=-1)
            o_ref[b] = (hrow * gl_ref[b].astype(F32)).astype(BF16)


def _scan(u_tm, wg, bg, lam, *, nb, n_lat_steps, reverse, hb=None, gl=None):
    nh, rows_all, _ = u_tm.shape
    s = rows_all // nb
    n = s // TS
    merge = hb is not None
    n_ctx = n - n_lat_steps
    if reverse:
        order = lambda j: jnp.where(j < n_ctx, n - 1 - j, n_lat_steps - 1 - (j - n_ctx))
    else:
        order = lambda j: jnp.where(j < n_ctx, n_lat_steps + j, j - n_ctx)
    slab = pl.BlockSpec((nh, TS * nb, LANES), lambda j: (0, order(j), 0))
    full = lambda shape: pl.BlockSpec(shape, lambda j: (0,) * len(shape))
    in_specs = [slab, full(wg.shape), full(bg.shape), full(lam.shape)]
    args = [u_tm, wg, bg, lam]
    scratch = [
        pltpu.VMEM((nh, nb, LANES), F32),
        pltpu.VMEM((nh, SUB * nb, LANES), F32),
        pltpu.VMEM((nh, SUB * nb, LANES), F32),
    ]
    if merge:
        d = nh * LANES
        bm = pl.BlockSpec((nb, TS, d), lambda j: (0, order(j), 0))
        in_specs += [slab, bm]
        args += [hb, gl]
        out_specs = bm
        out_shape = jax.ShapeDtypeStruct((nb, s, d), BF16)
        scratch.append(pltpu.VMEM((nh, TS * nb, LANES), F32))
    else:
        out_specs = slab
        out_shape = jax.ShapeDtypeStruct(u_tm.shape, F32)
    return pl.pallas_call(
        functools.partial(_scan_body, reverse=reverse, merge=merge, nb=nb),
        grid=(n,),
        in_specs=in_specs,
        out_specs=out_specs,
        out_shape=out_shape,
        scratch_shapes=scratch,
        compiler_params=_params(("arbitrary",)),
        name="scan_fwd_merge" if merge else "scan_bwd",
    )(*args)


def _dft_body(c_ref, s_ref, p_ref, q_ref, *rest):
    o_ref = rest[-1]
    o_ref[...] = (_dot(c_ref[...], p_ref[...]) - _dot(s_ref[...], q_ref[...])).astype(BF16)


def _dft(cm, sm, pq, *, nb, f, row0, prev=None):
    n = cm.shape[0]
    s = pq.shape[0]
    tmk = min(n, 512)
    rb = row0 // n
    in_specs = [
        pl.BlockSpec((tmk, n), lambda m, b: (m, 0)),
        pl.BlockSpec((tmk, n), lambda m, b: (m, 0)),
        pl.BlockSpec((n, f), lambda m, b: (rb, 2 * b)),
        pl.BlockSpec((n, f), lambda m, b: (rb, 2 * b + 1)),
    ]
    args = [cm, sm, pq, pq]
    aliases = {}
    if prev is not None:
        in_specs.append(pl.BlockSpec(memory_space=pl.ANY))
        args.append(prev)
        aliases = {4: 0}
    return pl.pallas_call(
        _dft_body,
        grid=(n // tmk, nb),
        in_specs=in_specs,
        out_specs=pl.BlockSpec((tmk, f), lambda m, b: (row0 // tmk + m, b)),
        out_shape=jax.ShapeDtypeStruct((s, nb * f), BF16),
        input_output_aliases=aliases,
        compiler_params=_params(("arbitrary", "arbitrary")),
        name="pos_dft",
    )(*args)


def _top2(vals):
    def first_max(vs):
        m = functools.reduce(jnp.maximum, vs)
        idx = jnp.full(m.shape, len(vs) - 1, jnp.int32)
        for k in range(len(vs) - 2, -1, -1):
            idx = jnp.where(vs[k] == m, k, idx)
        return m, idx
    m1, i1 = first_max(vals)
    rest = [jnp.where(i1 == k, -jnp.inf, v) for k, v in enumerate(vals)]
    m2, i2 = first_max(rest)
    return m1, i1, m2, i2


def _pick(idx, vals):
    out = vals[-1]
    for k in range(len(vals) - 2, -1, -1):
        out = jnp.where(idx == k, vals[k], out)
    return out


def _mix_body(hg_ref, yf_ref, sa_ref, sb_ref, x_ref, mod_ref, wpr_ref, wpf_ref, wo_ref,
              gffn_ref, rwt_ref, rb_ref, xo_ref, hx_ref, rr_ref, rc_ref, cnt_ref, *, d, n_exp):
    mod = mod_ref[0]
    y_r = _dot(hg_ref[0], wpr_ref[...])
    y_f = _dot(yf_ref[...], wpf_ref[...])
    merged = sa_ref[0].astype(F32) * y_r + sb_ref[0].astype(F32) * y_f
    out = _dot(merged.astype(BF16), wo_ref[...])
    xn = x_ref[0] + mod[:, 2 * d:3 * d] * out
    xo_ref[0] = xn
    h2 = _rms(xn, gffn_ref[...])
    h2 = h2 * (1.0 + mod[:, 4 * d:5 * d]) + mod[:, 3 * d:4 * d]
    hx_ref[0] = h2.astype(BF16)

    aff = jax.nn.sigmoid(_dot3_nt(rwt_ref[...], h2))
    sel = aff + rb_ref[...]
    per = n_exp // N_EXPERT_GROUPS
    srow = [sel[e:e + 1, :] for e in range(n_exp)]
    arow = [aff[e:e + 1, :] for e in range(n_exp)]
    scores = []
    for gi in range(N_EXPERT_GROUPS):
        m1, _, m2, _ = _top2(srow[gi * per:(gi + 1) * per])
        scores.append(m1 + m2)
    best = functools.reduce(jnp.maximum, scores)
    grp = jnp.full(best.shape, N_EXPERT_GROUPS - 1, jnp.int32)
    for gi in range(N_EXPERT_GROUPS - 2, -1, -1):
        grp = jnp.where(scores[gi] == best, gi, grp)
    v = [_pick(grp, [srow[gi * per + k] for gi in range(N_EXPERT_GROUPS)]) for k in range(per)]
    a = [_pick(grp, [arow[gi * per + k] for gi in range(N_EXPERT_GROUPS)]) for k in range(per)]
    _, i1, _, i2 = _top2(v)
    w1 = _pick(i1, a)
    w2 = _pick(i2, a)
    den = w1 + w2
    e1 = grp * per + i1
    e2 = grp * per + i2

    erow = lax.broadcasted_iota(jnp.int32, (n_exp, TM), 0)
    hit1 = erow == e1
    hit2 = erow == e2
    onehot = jnp.where(hit1 | hit2, 1.0, 0.0)
    cnt = jnp.sum(onehot, axis=1, keepdims=True)
    before = (lax.broadcasted_iota(jnp.int32, (TM, TM), 0)
              < lax.broadcasted_iota(jnp.int32, (TM, TM), 1)).astype(BF16)
    rank = _dot(onehot.astype(BF16), before)
    padded = ((cnt.astype(jnp.int32) + (SUBLANES - 1)) & (-SUBLANES)).astype(F32)
    lower = (lax.broadcasted_iota(jnp.int32, (n_exp, n_exp), 1)
             < lax.broadcasted_iota(jnp.int32, (n_exp, n_exp), 0)).astype(BF16)
    starts = _dot(lower, jnp.broadcast_to(padded, (n_exp, LANES)).astype(BF16))
    pos = starts[:, 0:1] + rank
    r1 = jnp.sum(jnp.where(hit1, pos, 0.0), axis=0, keepdims=True)
    r2 = jnp.sum(jnp.where(hit2, pos, 0.0), axis=0, keepdims=True)
    route = jnp.concatenate([r1, r2, w1 / den, w2 / den, jnp.zeros((LANES - 4, TM), F32)], axis=0)
    rr_ref[0, 0] = route[0:SUBLANES, :]
    rc_ref[0] = route.T
    cnt_ref[0, 0] = jnp.broadcast_to(cnt, (n_exp, LANES))


def _mix(hg, yf, sa, sb, xa, mod, wpr, wpf, wo, gffn, rwt, rb, *, n_tiles, n_lat):
    nb, s, d = xa.shape
    f = wpf.shape[0]
    n_exp = rwt.shape[0]
    act = pl.BlockSpec((1, TM, d), lambda i, b: (b, i, 0))
    full = lambda shape: pl.BlockSpec(shape, lambda i, b: (0,) * len(shape))
    mod_row = lambda i, b: (jnp.where(i < n_lat, b, nb), 0, 0)
    return pl.pallas_call(
        functools.partial(_mix_body, d=d, n_exp=n_exp),
        grid=(n_tiles, nb),
        in_specs=[
            act,
            pl.BlockSpec((TM, f), lambda i, b: (i, b)),
            act, act, act,
            pl.BlockSpec((1, 1, N_MOD * d), mod_row),
            full(wpr.shape), full(wpf.shape), full(wo.shape),
            full((1, d)), full(rwt.shape), full((n_exp, 1)),
        ],
        out_specs=[
            act, act,
            pl.BlockSpec((1, 1, SUBLANES, TM), lambda i, b: (i, b, 0, 0)),
            pl.BlockSpec((1, TM, LANES), lambda i, b: (b, i, 0)),
            pl.BlockSpec((1, 1, n_exp, LANES), lambda i, b: (i, b, 0, 0)),
        ],
        out_shape=[
            jax.ShapeDtypeStruct((nb, s, d), F32),
            jax.ShapeDtypeStruct((nb, s, d), BF16),
            jax.ShapeDtypeStruct((n_tiles, nb, SUBLANES, TM), F32),
            jax.ShapeDtypeStruct((nb, s, LANES), F32),
            jax.ShapeDtypeStruct((n_tiles, nb, n_exp, LANES), F32),
        ],
        compiler_params=_params(("arbitrary", "arbitrary")),
        name="mix_out",
    )(hg, yf, sa, sb, xa, mod, wpr, wpf, wo, gffn.reshape(1, d), rwt, rb.reshape(n_exp, 1))


def _local_rows(n_exp):
    return -(-(2 * TM + n_exp * (SUBLANES - 1)) // LANES) * LANES


def _run_copies(np_ref, goff_ref, win, *, n_exp, local, remote, sem, to_remote, wait):
    src = 0
    for e in range(n_exp):
        n = np_ref[win * n_exp + e]
        dst = goff_ref[win * n_exp + e]

        def one(c, carry, src=src, dst=dst):
            lo = local.at[pl.ds(pl.multiple_of(src + c * SUBLANES, SUBLANES), SUBLANES)]
            re = remote.at[pl.ds(pl.multiple_of(dst + c * SUBLANES, SUBLANES), SUBLANES)]
            cp = pltpu.make_async_copy(lo, re, sem) if to_remote else pltpu.make_async_copy(re, lo, sem)
            if wait:
                cp.wait()
            else:
                cp.start()
            return carry

        lax.fori_loop(0, lax.shift_right_logical(n, 3), one, 0)
        src = src + n


def _dispatch_body(np_ref, goff_ref, hx_ref, rr_ref, xs_ref, xloc, sem, *, n_exp, n_win, rows):
    w = pl.program_id(0)
    slot = lax.rem(w, 2)
    copies = functools.partial(_run_copies, np_ref, goff_ref, n_exp=n_exp, remote=xs_ref, to_remote=True)

    @pl.when(w >= 2)
    def _():
        copies(w - 2, local=xloc.at[slot], sem=sem.at[slot], wait=True)

    rr = rr_ref[0, 0].astype(jnp.int32)
    riota = lax.broadcasted_iota(jnp.int32, (rows, TM), 0)
    p = jnp.where((riota == rr[0:1, :]) | (riota == rr[1:2, :]), 1.0, 0.0).astype(BF16)
    xloc[slot] = _dot(p, hx_ref[0])
    copies(w, local=xloc.at[slot], sem=sem.at[slot], wait=False)

    @pl.when(w == n_win - 1)
    def _():
        copies(w - 1, local=xloc.at[1 - slot], sem=sem.at[1 - slot], wait=True)
        copies(w, local=xloc.at[slot], sem=sem.at[slot], wait=True)


def _dispatch(npad, goff, hx, rr, *, n_exp, n_win, rows_max):
    nb, s, d = hx.shape
    rows = _local_rows(n_exp)
    assert n_win >= 2
    grid_spec = pltpu.PrefetchScalarGridSpec(
        num_scalar_prefetch=2,
        grid=(n_win,),
        in_specs=[
            pl.BlockSpec((1, TM, d), lambda w, a, g: (w % nb, w // nb, 0)),
            pl.BlockSpec((1, 1, SUBLANES, TM), lambda w, a, g: (w // nb, w % nb, 0, 0)),
        ],
        out_specs=pl.BlockSpec(memory_space=pl.ANY),
        scratch_shapes=[pltpu.VMEM((2, rows, d), F32), pltpu.SemaphoreType.DMA((2,))],
    )
    return pl.pallas_call(
        functools.partial(_dispatch_body, n_exp=n_exp, n_win=n_win, rows=rows),
        grid_spec=grid_spec,
        out_shape=jax.ShapeDtypeStruct((rows_max, d), F32),
        compiler_params=_params(("arbitrary",)),
        name="moe_dispatch",
    )(npad, goff, hx, rr)


def _gmm_body(te_ref, nu_ref, xs_ref, w1_ref, w3_ref, w2_ref, ys_ref, w1b, w3b, w2b):
    t = pl.program_id(0)

    @pl.when((t == 0) | (te_ref[t] != te_ref[jnp.maximum(t - 1, 0)]))
    def _():
        w1b[...] = w1_ref[0].astype(BF16)
        w3b[...] = w3_ref[0].astype(BF16)
        w2b[...] = w2_ref[0].astype(BF16)

    @pl.when(t < nu_ref[0])
    def _():
        x = xs_ref[...].astype(BF16)
        a = _dot(x, w1b[...])
        h = (a * jax.nn.sigmoid(a)) * _dot(x, w3b[...])
        ys_ref[...] = _dot(h.astype(BF16), w2b[...])


def _gmm(tile_exp, n_used, xs, w1, w3, w2):
    rows_max, d = xs.shape
    de = w1.shape[2]
    row_tile = lambda t, te, nu: (jnp.minimum(t, nu[0] - 1), 0)
    wspec = lambda w: pl.BlockSpec((1,) + w.shape[1:], lambda t, te, nu: (te[t], 0, 0))
    grid_spec = pltpu.PrefetchScalarGridSpec(
        num_scalar_prefetch=2,
        grid=(rows_max // MOE_TM,),
        in_specs=[pl.BlockSpec((MOE_TM, d), row_tile), wspec(w1), wspec(w3), wspec(w2)],
        out_specs=pl.BlockSpec((MOE_TM, d), row_tile),
        scratch_shapes=[pltpu.VMEM((d, de), BF16), pltpu.VMEM((d, de), BF16), pltpu.VMEM((de, d), BF16)],
    )
    return pl.pallas_call(
        _gmm_body,
        grid_spec=grid_spec,
        out_shape=jax.ShapeDtypeStruct((rows_max, d), F32),
        compiler_params=_params(("arbitrary",)),
        name="moe_experts",
    )(tile_exp, n_used, xs, w1, w3, w2)


def _combine_body(*refs, d, n_exp, n_win, rows, final):
    if final:
        np_ref, goff_ref, ys_ref, rc_ref, xr_ref, mod_ref, gfin_ref, o_ref, yloc, sem = refs
    else:
        np_ref, goff_ref, ys_ref, rc_ref, xr_ref, mod_ref, o_ref, yloc, sem = refs
    w = pl.program_id(0)
    slot = lax.rem(w, 2)
    copies = functools.partial(_run_copies, np_ref, goff_ref, n_exp=n_exp, remote=ys_ref, to_remote=False)

    @pl.when(w == 0)
    def _():
        copies(w, local=yloc.at[slot], sem=sem.at[slot], wait=False)

    @pl.when(w + 1 < n_win)
    def _():
        copies(w + 1, local=yloc.at[1 - slot], sem=sem.at[1 - slot], wait=False)

    copies(w, local=yloc.at[slot], sem=sem.at[slot], wait=True)

    total = np_ref[w * n_exp]
    for e in range(1, n_exp):
        total = total + np_ref[w * n_exp + e]
    y = jnp.where(lax.broadcasted_iota(jnp.int32, (rows, 1), 0) < total, yloc[slot], 0.0).astype(BF16)
    rc = rc_ref[0]
    ci = lax.broadcasted_iota(jnp.int32, (TM, rows), 1)
    pw = (jnp.where(ci == rc[:, 0:1].astype(jnp.int32), rc[:, 2:3], 0.0)
          + jnp.where(ci == rc[:, 1:2].astype(jnp.int32), rc[:, 3:4], 0.0))
    pw_hi, pw_lo = _split(pw)
    moe = _dot(pw_hi, y) + _dot(pw_lo, y)
    xn = xr_ref[0] + mod_ref[0][:, 5 * d:6 * d] * moe
    if final:
        o_ref[0] = _rms(xn, gfin_ref[...])
    else:
        o_ref[0] = xn


def _combine(npad, goff, ys, rc, xres, mod, *, n_exp, n_win, n_lat, gfin=None, seq_out=None):
    nb, s, d = xres.shape
    rows = _local_rows(n_exp)
    final = gfin is not None
    act = lambda width: pl.BlockSpec((1, TM, width), lambda w, a, g: (w % nb, w // nb, 0))
    mod_row = lambda w, a, g: (jnp.where(w // nb < n_lat, w % nb, nb), 0, 0)
    in_specs = [pl.BlockSpec(memory_space=pl.ANY), act(LANES), act(d),
                pl.BlockSpec((1, 1, N_MOD * d), mod_row)]
    args = [npad, goff, ys, rc, xres, mod]
    if final:
        in_specs.append(pl.BlockSpec((1, d), lambda w, a, g: (0, 0)))
        args.append(gfin.reshape(1, d))
        out_shape = jax.ShapeDtypeStruct((nb, seq_out, d), F32)
        aliases = {}
    else:
        out_shape = jax.ShapeDtypeStruct((nb, s, d), F32)
        aliases = {4: 0}
    grid_spec = pltpu.PrefetchScalarGridSpec(
        num_scalar_prefetch=2,
        grid=(n_win,),
        in_specs=in_specs,
        out_specs=act(d),
        scratch_shapes=[pltpu.VMEM((2, rows, d), F32), pltpu.SemaphoreType.DMA((2,))],
    )
    return pl.pallas_call(
        functools.partial(_combine_body, d=d, n_exp=n_exp, n_win=n_win, rows=rows, final=final),
        grid_spec=grid_spec,
        out_shape=out_shape,
        input_output_aliases=aliases,
        compiler_params=_params(("arbitrary",)),
        name="moe_combine",
    )(*args)


def _route_tables(cnt, *, n_exp, n_tiles):
    cnt = cnt[..., 0].astype(jnp.int32).reshape(-1, n_exp)
    npad = (cnt + (SUBLANES - 1)) & (-SUBLANES)
    seg = -(-jnp.sum(npad, axis=0) // MOE_TM) * MOE_TM
    ends = jnp.cumsum(seg)
    goff = (ends - seg)[None, :] + jnp.cumsum(npad, axis=0) - npad
    tile_ends = ends // MOE_TM
    n_used = tile_ends[-1]
    tile = jnp.minimum(jnp.arange(n_tiles, dtype=jnp.int32), n_used - 1)
    tile_exp = jnp.sum(tile[:, None] >= tile_ends[None, :], axis=1).astype(jnp.int32)
    return npad.reshape(-1), goff.reshape(-1).astype(jnp.int32), tile_exp, n_used.reshape(1).astype(jnp.int32)


def _dft_tables(n):
    k = jnp.arange(n, dtype=jnp.int32)
    ang = ((k[:, None] * k[None, :]) % n).astype(F32) * (2.0 * math.pi / n)
    scale = 1.0 / math.sqrt(n)
    return jnp.cos(ang) * scale, jnp.sin(ang) * scale


def kernel(x, c, ctx, c_ctx, ada_w, ada_b, norm_mix_g, w_in, conv_w, conv_b, lru_w_a, lru_b_a, lru_w_x, lru_b_x, lru_lambda, w_proj_rnn, w_proj_fourier, w_out, norm_ffn_g, router_w, router_b, moe_w1, moe_w3, moe_w2, final_norm_g):
    nb, l, d = x.shape
    lc = ctx.shape[1]
    depth = ada_w.shape[0]
    r = conv_w.shape[2]
    f = w_proj_fourier.shape[1]
    nh, blk = lru_w_a.shape[2], lru_w_a.shape[3]
    assert nb == SUBLANES and blk == LANES and lc == TM and l % 1024 == 0 and l % lc == 0
    assert GRID_W & (GRID_W - 1) == 0 and TM % GRID_W == 0 and nb < MOD_ROWS
    s = l + lc
    n_lat = l // TM

    xa = jnp.concatenate([x, ctx], axis=1)
    cc = jnp.zeros((MOD_ROWS, d), F32).at[:nb].set(c).at[nb].set(c_ctx)
    mod = _ada(cc, ada_w, ada_b).reshape(depth, MOD_ROWS, 1, N_MOD * d)

    cl, sl = (t.astype(BF16) for t in _dft_tables(l))
    cx, sx = (t.astype(BF16) for t in _dft_tables(lc))
    cch, sch = _dft_tables(f // FOURIER_GROUPS)
    eye = jnp.eye(FOURIER_GROUPS, dtype=F32)
    fcs = jnp.concatenate([jnp.kron(eye, cch), jnp.kron(eye, sch)], axis=1).astype(BF16)
    rwt = router_w.T
    n_exp = rwt.shape[0]

    out = None
    for li in range(depth):
        last = li == depth - 1
        u_tm, gl, sa, sb, pq = _inproj(xa, mod[li], norm_mix_g[li], w_in[li].astype(BF16),
                                       conv_w[li], conv_b[li], fcs, n_lat=n_lat)
        wg = [jnp.concatenate([lru_w_a[li, k], lru_w_x[li, k]], axis=-1).astype(BF16) for k in range(2)]
        bg = [jnp.concatenate([lru_b_a[li, k], lru_b_x[li, k]], axis=-1).reshape(nh, 1, 2 * blk) for k in range(2)]
        lam = [lru_lambda[li, k].reshape(nh, 1, blk) for k in range(2)]
        hb = _scan(u_tm, wg[1], bg[1], lam[1], nb=nb, n_lat_steps=l // TS, reverse=True)
        hg = _scan(u_tm, wg[0], bg[0], lam[0], nb=nb, n_lat_steps=l // TS, reverse=False, hb=hb, gl=gl)
        yf = _dft(cl, sl, pq, nb=nb, f=f, row0=0)
        if not last:
            yf = _dft(cx, sx, pq, nb=nb, f=f, row0=l, prev=yf)
        n_tiles = n_lat if last else s // TM
        x1, hx, rr, rc, cnt = _mix(hg, yf, sa, sb, xa, mod[li], w_proj_rnn[li].astype(BF16),
                                   w_proj_fourier[li].astype(BF16), w_out[li].astype(BF16), norm_ffn_g[li],
                                   rwt, router_b, n_tiles=n_tiles, n_lat=n_lat)
        n_win = n_tiles * nb
        rows_max = -(-(n_win * (2 * TM + n_exp * (SUBLANES - 1)) + n_exp * MOE_TM) // MOE_TM) * MOE_TM
        npad, goff, tile_exp, n_used = _route_tables(cnt, n_exp=n_exp, n_tiles=rows_max // MOE_TM)
        xs = _dispatch(npad, goff, hx, rr, n_exp=n_exp, n_win=n_win, rows_max=rows_max)
        ys = _gmm(tile_exp, n_used, xs, moe_w1[li], moe_w3[li], moe_w2[li])
        if last:
            out = _combine(npad, goff, ys, rc, x1, mod[li], n_exp=n_exp, n_win=n_win, n_lat=n_lat,
                           gfin=final_norm_g, seq_out=l)
        else:
            xa = _combine(npad, goff, ys, rc, x1, mod[li], n_exp=n_exp, n_win=n_win, n_lat=n_lat)
    return out
```

```python
import functools
import math

import jax
import jax.numpy as jnp
from jax import lax
from jax.experimental import pallas as pl
from jax.experimental.pallas import tpu as pltpu

F32 = jnp.float32
BF16 = jnp.bfloat16

RMS_EPS = 1e-6
LRU_C = 8.0
GRID_W = 64
CONV_TAPS = ((0, -2), (1, -1), (2, 0), (3, 1))
N_MOD = 6
FOURIER_GROUPS = 4
N_EXPERT_GROUPS = 4
LANES = 128
SUBLANES = 8
PACKED_ROWS = 16
TR = 512
TM = 256
TS = 128
SUB = 64
MOE_TM = 512
MOD_ROWS = 16
VMEM_LIMIT = 56 * 1024 * 1024


def _dot(a, b):
    return jnp.dot(a, b, preferred_element_type=F32)


def _split(a):
    hi = a.astype(BF16)
    lo = (a - hi.astype(F32)).astype(BF16)
    return hi, lo


def _dot3(a, b):
    ah, al = _split(a)
    bh, bl = _split(b)
    return _dot(ah, bh) + _dot(al, bh) + _dot(ah, bl)


def _dot3_nt(a, b):
    dn = (((1,), (1,)), ((), ()))
    ah, al = _split(a)
    bh, bl = _split(b)
    f = lambda p, q: lax.dot_general(p, q, dn, preferred_element_type=F32)
    return f(ah, bh) + f(al, bh) + f(ah, bl)


def _gelu_tanh(x):
    return 0.5 * x * (1.0 + jnp.tanh(math.sqrt(2.0 / math.pi) * (x + 0.044715 * (x * x * x))))


def _sigmoid(x):
    return 0.5 * jnp.tanh(0.5 * x) + 0.5


def _rms(x, g):
    return x * lax.rsqrt(jnp.mean(x * x, axis=-1, keepdims=True) + RMS_EPS) * g


def _per_batch(x, fn):
    rows, d = x.shape
    return fn(x.reshape(rows // SUBLANES, SUBLANES, d)).reshape(rows, d)


def _params(sem, vmem=VMEM_LIMIT):
    return pltpu.CompilerParams(dimension_semantics=sem, vmem_limit_bytes=vmem)


def _full(shape, n_idx):
    zeros = (0,) * len(shape)
    return pl.BlockSpec(shape, lambda *_: zeros)


def _ada_body(cc_ref, w_ref, b_ref, o_ref):
    cc = cc_ref[...]
    s = cc * jax.nn.sigmoid(cc)
    o_ref[0] = _dot3(s, w_ref[0]) + b_ref[0]


def _ada(cc, ada_w, ada_b):
    depth, d, n = ada_w.shape
    tn = 1024
    return pl.pallas_call(
        _ada_body,
        grid=(depth, n // tn),
        in_specs=[
            pl.BlockSpec((MOD_ROWS, d), lambda l, j: (0, 0)),
            pl.BlockSpec((1, d, tn), lambda l, j: (l, 0, j)),
            pl.BlockSpec((1, 1, tn), lambda l, j: (l, 0, j)),
        ],
        out_specs=pl.BlockSpec((1, MOD_ROWS, tn), lambda l, j: (l, 0, j)),
        out_shape=jax.ShapeDtypeStruct((depth, MOD_ROWS, n), F32),
        compiler_params=_params(("arbitrary", "arbitrary")),
        name="ada_mod",
    )(cc, ada_w, ada_b.reshape(depth, 1, n))


def _inproj_body(x_ref, mod_ref, g_ref, w_ref, fcs_ref, u_ref, gl_ref, sa_ref, sb_ref, pq_ref, *, d, r, f):
    mod = mod_ref[0]
    h = _rms(x_ref[...], g_ref[...])
    h = _per_batch(h, lambda v: v * (1.0 + mod[:, d:2 * d])[None] + mod[:, 0:d][None]).astype(BF16)
    u_ref[...] = _dot(h, w_ref[:, 0:r]).astype(BF16)
    gl_ref[...] = _gelu_tanh(_dot(h, w_ref[:, r:2 * r])).astype(BF16)
    u4 = _dot(h, w_ref[:, 2 * r:2 * r + f]).astype(BF16)
    pq_ref[...] = _dot(u4, fcs_ref[...]).astype(BF16)
    s3 = 2 * r + f
    sa_ref[...] = jax.nn.sigmoid(_dot(h, w_ref[:, s3:s3 + d])).astype(BF16)
    sb_ref[...] = jax.nn.sigmoid(_dot(h, w_ref[:, s3 + d:s3 + 2 * d])).astype(BF16)


def _inproj(xa, mod, g, w_in, fcs, *, n_lat_rows, r):
    rows, d = xa.shape
    f = fcs.shape[0]
    n_lat = n_lat_rows // TR
    row = lambda width: pl.BlockSpec((TR, width), lambda i: (i, 0))
    return pl.pallas_call(
        functools.partial(_inproj_body, d=d, r=r, f=f),
        grid=(rows // TR,),
        in_specs=[
            row(d),
            pl.BlockSpec((1, SUBLANES, N_MOD * d), lambda i: (jnp.where(i < n_lat, 0, 1), 0, 0)),
            _full((1, d), 1), _full(w_in.shape, 1), _full(fcs.shape, 1),
        ],
        out_specs=[row(r), row(r), row(d), row(d), row(2 * f)],
        out_shape=[
            jax.ShapeDtypeStruct((rows, r), BF16),
            jax.ShapeDtypeStruct((rows, r), BF16),
            jax.ShapeDtypeStruct((rows, d), BF16),
            jax.ShapeDtypeStruct((rows, d), BF16),
            jax.ShapeDtypeStruct((rows, 2 * f), BF16),
        ],
        compiler_params=_params(("arbitrary",)),
        name="in_proj",
    )(xa, mod, g.reshape(1, d), w_in, fcs)


def _scan_body(*refs, reverse, merge, nb, n_lat_steps, ctx_len, order):
    if merge:
        (u_ref, up_ref, un_ref, cw_ref, cb_ref, wg_ref, bg_ref, lam_ref, hb_ref, gl_ref,
         o_ref, h_scr, a_scr, b_scr, uc_scr, hs_scr) = refs
    else:
        (u_ref, up_ref, un_ref, cw_ref, cb_ref, wg_ref, bg_ref, lam_ref,
         o_ref, h_scr, a_scr, b_scr, uc_scr, hs_scr) = refs
    nh = wg_ref.shape[0]
    rows = SUB * nb
    trows = TS * nb
    j = pl.program_id(0)

    @pl.when(j == 0)
    def _():
        h_scr[...] = jnp.zeros_like(h_scr)

    blk = order(j)
    n_steps = pl.num_programs(0)
    is_ctx = blk >= n_lat_steps
    keep_prev = jnp.where(is_ctx & (blk > n_lat_steps), 1.0, 0.0)
    keep_next = jnp.where(is_ctx & (blk < n_steps - 1), 1.0, 0.0)
    lat = jnp.where(is_ctx, 0.0, 1.0)
    halo = 2 * nb
    ext = jnp.concatenate([up_ref[...].astype(F32) * keep_prev, u_ref[...].astype(F32),
                           un_ref[...].astype(F32)[0:nb] * keep_next], axis=0)
    uc = cb_ref[...]
    for k, s in CONV_TAPS:
        lo = halo + s * nb
        uc = uc + cw_ref[k:k + 1, :] * ext[lo:lo + trows, :]
    uc_scr[...] = uc
    for p in range(GRID_W, TS, GRID_W):
        at = lambda q: ext[halo + q * nb:halo + (q + 1) * nb, :]
        rows_of = lambda q: pl.ds(q * nb, nb)
        uc_scr[rows_of(p), :] -= lat * (cw_ref[0:1, :] * at(p - 2) + cw_ref[1:2, :] * at(p - 1))
        uc_scr[rows_of(p + 1), :] -= lat * (cw_ref[0:1, :] * at(p - 1))
        uc_scr[rows_of(p - 1), :] -= lat * (cw_ref[3:4, :] * at(p))

    lam = lam_ref[...]
    decay = (-LRU_C) * (jnp.maximum(-lam, 0.0) + jnp.log1p(jnp.exp(-jnp.abs(lam))))

    subs = range(TS // SUB)
    for sub in (reversed(subs) if reverse else subs):
        r0 = sub * rows
        for g in range(nh):
            ug = uc_scr[r0:r0 + rows, g * LANES:(g + 1) * LANES]
            gates = _dot(ug.astype(BF16), wg_ref[g]) + bg_ref[g]
            log_a = decay[g] * _sigmoid(gates[:, :LANES])
            a = jnp.exp(log_a)
            z = -jnp.tanh(log_a) * (a * a + 1.0)
            mult = jnp.where(z > 0.0, z * lax.rsqrt(z), 0.0)
            a_scr[g] = a
            b_scr[g] = mult * (_sigmoid(gates[:, LANES:]) * ug)

        def step(k, hs):
            t = (SUB - 1 - k) if reverse else k
            off = pl.multiple_of(t * nb, nb)
            new = []
            for g in range(nh):
                hg = a_scr[g, pl.ds(off, nb), :] * hs[g] + b_scr[g, pl.ds(off, nb), :]
                hs_scr[pl.ds(r0 + off, nb), g * LANES:(g + 1) * LANES] = hg
                new.append(hg)
            return tuple(new)

        hs = lax.fori_loop(0, SUB, step, tuple(h_scr[g] for g in range(nh)), unroll=8)
        for g in range(nh):
            h_scr[g] = hs[g]

    if merge:
        o_ref[...] = ((hs_scr[...] + hb_ref[...].astype(F32)) * gl_ref[...].astype(F32)).astype(BF16)
    else:
        o_ref[...] = hs_scr[...].astype(BF16)


def _scan(u, conv_w, conv_b, wg, bg, lam, *, nb, n_lat_steps, ctx_len, reverse, hb=None, gl=None):
    rows_all, r = u.shape
    nh = wg.shape[0]
    trows = TS * nb
    n = rows_all // trows
    n_ctx = n - n_lat_steps
    merge = hb is not None
    if reverse:
        order = lambda j: jnp.where(j < n_ctx, n - 1 - j, n_lat_steps - 1 - (j - n_ctx))
    else:
        order = lambda j: jnp.where(j < n_ctx, n_lat_steps + j, j - n_ctx)
    hpb = trows // PACKED_ROWS
    tile = pl.BlockSpec((trows, r), lambda j: (order(j), 0))
    in_specs = [
        tile,
        pl.BlockSpec((PACKED_ROWS, r), lambda j: (jnp.maximum(order(j) * hpb - 1, 0), 0)),
        pl.BlockSpec((PACKED_ROWS, r), lambda j: (jnp.minimum((order(j) + 1) * hpb, n * hpb - 1), 0)),
        _full(conv_w.shape, 1), _full((1, r), 1), _full(wg.shape, 1), _full(bg.shape, 1), _full(lam.shape, 1),
    ]
    args = [u, u, u, conv_w, conv_b.reshape(1, r), wg, bg, lam]
    if merge:
        in_specs += [tile, tile]
        args += [hb, gl]
    return pl.pallas_call(
        functools.partial(_scan_body, reverse=reverse, merge=merge, nb=nb, n_lat_steps=n_lat_steps,
                          ctx_len=ctx_len, order=order),
        grid=(n,),
        in_specs=in_specs,
        out_specs=tile,
        out_shape=jax.ShapeDtypeStruct((rows_all, r), BF16),
        scratch_shapes=[
            pltpu.VMEM((nh, nb, LANES), F32),
            pltpu.VMEM((nh, SUB * nb, LANES), F32),
            pltpu.VMEM((nh, SUB * nb, LANES), F32),
            pltpu.VMEM((trows, r), F32),
            pltpu.VMEM((trows, r), F32),
        ],
        compiler_params=_params(("arbitrary",)),
        name="scan_fwd_merge" if merge else "scan_bwd",
    )(*args)


def _dft_body(ca_ref, sa_ref, cb_ref, sb_ref, p_ref, q_ref, *rest, n):
    o_ref, c_scr, s_scr = rest[-3:]

    @pl.when(pl.program_id(1) == 0)
    def _():
        cb = cb_ref[...]
        sb = sb_ref[...]
        for t1 in range(n // LANES):
            ca = ca_ref[:, t1:t1 + 1]
            sa = sa_ref[:, t1:t1 + 1]
            c_scr[:, t1 * LANES:(t1 + 1) * LANES] = (ca * cb - sa * sb).astype(BF16)
            s_scr[:, t1 * LANES:(t1 + 1) * LANES] = (sa * cb + ca * sb).astype(BF16)

    o_ref[...] = (_dot(c_scr[...], p_ref[...]) - _dot(s_scr[...], q_ref[...])).astype(BF16)


def _dft_tables(n):
    k = jnp.arange(n, dtype=jnp.int32)[:, None]
    t1 = jnp.arange(LANES, dtype=jnp.int32)[None, :]
    ang_a = ((k * ((t1 * LANES) % n)) % n).astype(F32) * (2.0 * math.pi / n)
    ang_b = ((k * t1) % n).astype(F32) * (2.0 * math.pi / n)
    scale = 1.0 / math.sqrt(n)
    return jnp.cos(ang_a), jnp.sin(ang_a), jnp.cos(ang_b) * scale, jnp.sin(ang_b) * scale


def _dft(pq, *, n, nb, f, row0, prev=None):
    s = pq.shape[0]
    tmk = min(n, 512)
    rb = row0 // n
    tab = pl.BlockSpec((tmk, LANES), lambda m, b: (m, 0))
    in_specs = [
        tab, tab, tab, tab,
        pl.BlockSpec((n, f), lambda m, b: (rb, 2 * b)),
        pl.BlockSpec((n, f), lambda m, b: (rb, 2 * b + 1)),
    ]
    args = list(_dft_tables(n)) + [pq, pq]
    aliases = {}
    if prev is not None:
        in_specs.append(pl.BlockSpec(memory_space=pl.ANY))
        args.append(prev)
        aliases = {6: 0}
    return pl.pallas_call(
        functools.partial(_dft_body, n=n),
        grid=(n // tmk, nb),
        in_specs=in_specs,
        out_specs=pl.BlockSpec((tmk, f), lambda m, b: (row0 // tmk + m, b)),
        out_shape=jax.ShapeDtypeStruct((s, nb * f), BF16),
        scratch_shapes=[pltpu.VMEM((tmk, n), BF16), pltpu.VMEM((tmk, n), BF16)],
        input_output_aliases=aliases,
        compiler_params=_params(("arbitrary", "arbitrary")),
        name="pos_dft",
    )(*args)


def _top2(vals):
    def first_max(vs):
        m = functools.reduce(jnp.maximum, vs)
        idx = jnp.full(m.shape, len(vs) - 1, jnp.int32)
        for k in range(len(vs) - 2, -1, -1):
            idx = jnp.where(vs[k] == m, k, idx)
        return m, idx
    m1, i1 = first_max(vals)
    rest = [jnp.where(i1 == k, -jnp.inf, v) for k, v in enumerate(vals)]
    m2, i2 = first_max(rest)
    return m1, i1, m2, i2


def _pick(idx, vals):
    out = vals[-1]
    for k in range(len(vals) - 2, -1, -1):
        out = jnp.where(idx == k, vals[k], out)
    return out


def _route_window(h2, rwt, rb, n_exp):
    aff = jax.nn.sigmoid(_dot3_nt(rwt, h2))
    sel = aff + rb
    per = n_exp // N_EXPERT_GROUPS
    srow = [sel[e:e + 1, :] for e in range(n_exp)]
    arow = [aff[e:e + 1, :] for e in range(n_exp)]
    scores = []
    for gi in range(N_EXPERT_GROUPS):
        m1, _, m2, _ = _top2(srow[gi * per:(gi + 1) * per])
        scores.append(m1 + m2)
    best = functools.reduce(jnp.maximum, scores)
    grp = jnp.full(best.shape, N_EXPERT_GROUPS - 1, jnp.int32)
    for gi in range(N_EXPERT_GROUPS - 2, -1, -1):
        grp = jnp.where(scores[gi] == best, gi, grp)
    v = [_pick(grp, [srow[gi * per + k] for gi in range(N_EXPERT_GROUPS)]) for k in range(per)]
    a = [_pick(grp, [arow[gi * per + k] for gi in range(N_EXPERT_GROUPS)]) for k in range(per)]
    _, i1, _, i2 = _top2(v)
    w1 = _pick(i1, a)
    w2 = _pick(i2, a)
    den = w1 + w2
    e1 = grp * per + i1
    e2 = grp * per + i2

    erow = lax.broadcasted_iota(jnp.int32, (n_exp, TM), 0)
    hit1 = erow == e1
    hit2 = erow == e2
    onehot = jnp.where(hit1 | hit2, 1.0, 0.0)
    cnt = jnp.sum(onehot, axis=1, keepdims=True)
    before = (lax.broadcasted_iota(jnp.int32, (TM, TM), 0)
              < lax.broadcasted_iota(jnp.int32, (TM, TM), 1)).astype(BF16)
    rank = _dot(onehot.astype(BF16), before)
    padded = ((cnt.astype(jnp.int32) + (SUBLANES - 1)) & (-SUBLANES)).astype(F32)
    lower = (lax.broadcasted_iota(jnp.int32, (n_exp, n_exp), 1)
             < lax.broadcasted_iota(jnp.int32, (n_exp, n_exp), 0)).astype(BF16)
    starts = _dot(lower, jnp.broadcast_to(padded, (n_exp, LANES)).astype(BF16))
    pos = starts[:, 0:1] + rank
    r1 = jnp.sum(jnp.where(hit1, pos, 0.0), axis=0, keepdims=True)
    r2 = jnp.sum(jnp.where(hit2, pos, 0.0), axis=0, keepdims=True)
    route = jnp.concatenate([r1, r2, w1 / den, w2 / den, jnp.zeros((LANES - 4, TM), F32)], axis=0)
    return route, jnp.broadcast_to(cnt, (n_exp, LANES))


def _mix_body(hg_ref, yf_ref, sa_ref, sb_ref, x_ref, mod_ref, wpr_ref, wpf_ref, wo_ref,
              gffn_ref, rwt_ref, rb_ref, xo_ref, hx_ref, rr_ref, rc_ref, cnt_ref, *, d, n_exp):
    mod = mod_ref[0]
    y_r = _dot(hg_ref[...], wpr_ref[...])
    y_f = _dot(yf_ref[...], wpf_ref[...])
    merged = sa_ref[...].astype(F32) * y_r + sb_ref[...].astype(F32) * y_f
    out = _dot(merged.astype(BF16), wo_ref[...])
    xn = x_ref[...] + _per_batch(out, lambda v: v * mod[:, 2 * d:3 * d][None])
    xo_ref[...] = xn
    h2 = _rms(xn, gffn_ref[...])
    h2 = _per_batch(h2, lambda v: v * (1.0 + mod[:, 4 * d:5 * d])[None] + mod[:, 3 * d:4 * d][None])
    hx_ref[...] = h2.astype(BF16)
    for k in range(TR // TM):
        route, cnt = _route_window(h2[k * TM:(k + 1) * TM, :], rwt_ref[...], rb_ref[...], n_exp)
        rr_ref[k] = route[0:SUBLANES, :]
        rc_ref[k * TM:(k + 1) * TM, :] = route.T
        cnt_ref[k] = cnt


def _mix(hg, yf, sa, sb, xa, mod, wpr, wpf, wo, gffn, rwt, rb, *, n_rows, n_lat_rows):
    rows_all, d = xa.shape
    f = wpf.shape[0]
    n_exp = rwt.shape[0]
    n_lat = n_lat_rows // TR
    wpt = TR // TM
    n_win = n_rows // TM
    row = lambda width: pl.BlockSpec((TR, width), lambda i: (i, 0))
    return pl.pallas_call(
        functools.partial(_mix_body, d=d, n_exp=n_exp),
        grid=(n_rows // TR,),
        in_specs=[
            row(d), row(f), row(d), row(d), row(d),
            pl.BlockSpec((1, SUBLANES, N_MOD * d), lambda i: (jnp.where(i < n_lat, 0, 1), 0, 0)),
            _full(wpr.shape, 1), _full(wpf.shape, 1), _full(wo.shape, 1),
            _full((1, d), 1), _full(rwt.shape, 1), _full((n_exp, 1), 1),
        ],
        out_specs=[
            row(d), row(d),
            pl.BlockSpec((wpt, SUBLANES, TM), lambda i: (i, 0, 0)),
            row(LANES),
            pl.BlockSpec((wpt, n_exp, LANES), lambda i: (i, 0, 0)),
        ],
        out_shape=[
            jax.ShapeDtypeStruct((rows_all, d), F32),
            jax.ShapeDtypeStruct((rows_all, d), BF16),
            jax.ShapeDtypeStruct((n_win, SUBLANES, TM), F32),
            jax.ShapeDtypeStruct((rows_all, LANES), F32),
            jax.ShapeDtypeStruct((n_win, n_exp, LANES), F32),
        ],
        compiler_params=_params(("arbitrary",)),
        name="mix_out",
    )(hg, yf, sa, sb, xa, mod, wpr, wpf, wo, gffn.reshape(1, d), rwt, rb.reshape(n_exp, 1))


def _local_rows(n_exp):
    return -(-(2 * TM + n_exp * (SUBLANES - 1)) // LANES) * LANES


def _run_copies(np_ref, goff_ref, win, *, n_exp, local, remote, sem, to_remote, wait):
    src = 0
    for e in range(n_exp):
        n = np_ref[win * n_exp + e]
        dst = goff_ref[win * n_exp + e]

        def one(c, carry, src=src, dst=dst):
            lo = local.at[pl.ds(pl.multiple_of(src + c * SUBLANES, SUBLANES), SUBLANES)]
            re = remote.at[pl.ds(pl.multiple_of(dst + c * SUBLANES, SUBLANES), SUBLANES)]
            cp = pltpu.make_async_copy(lo, re, sem) if to_remote else pltpu.make_async_copy(re, lo, sem)
            if wait:
                cp.wait()
            else:
                cp.start()
            return carry

        lax.fori_loop(0, lax.shift_right_logical(n, 3), one, 0)
        src = src + n


def _dispatch_body(np_ref, goff_ref, hx_ref, rr_ref, xs_ref, xloc, sem, *, n_exp, n_win, rows):
    w = pl.program_id(0)
    slot = lax.rem(w, 2)
    copies = functools.partial(_run_copies, np_ref, goff_ref, n_exp=n_exp, remote=xs_ref, to_remote=True)

    @pl.when(w >= 2)
    def _():
        copies(w - 2, local=xloc.at[slot], sem=sem.at[slot], wait=True)

    rr = rr_ref[0].astype(jnp.int32)
    riota = lax.broadcasted_iota(jnp.int32, (rows, TM), 0)
    p = jnp.where((riota == rr[0:1, :]) | (riota == rr[1:2, :]), 1.0, 0.0).astype(BF16)
    xloc[slot] = _dot(p, hx_ref[...])
    copies(w, local=xloc.at[slot], sem=sem.at[slot], wait=False)

    @pl.when(w == n_win - 1)
    def _():
        copies(w - 1, local=xloc.at[1 - slot], sem=sem.at[1 - slot], wait=True)
        copies(w, local=xloc.at[slot], sem=sem.at[slot], wait=True)


def _dispatch(npad, goff, hx, rr, *, n_exp, n_win, rows_max):
    d = hx.shape[1]
    rows = _local_rows(n_exp)
    assert n_win >= 2
    grid_spec = pltpu.PrefetchScalarGridSpec(
        num_scalar_prefetch=2,
        grid=(n_win,),
        in_specs=[
            pl.BlockSpec((TM, d), lambda w, a, g: (w, 0)),
            pl.BlockSpec((1, SUBLANES, TM), lambda w, a, g: (w, 0, 0)),
        ],
        out_specs=pl.BlockSpec(memory_space=pl.ANY),
        scratch_shapes=[pltpu.VMEM((2, rows, d), F32), pltpu.SemaphoreType.DMA((2,))],
    )
    return pl.pallas_call(
        functools.partial(_dispatch_body, n_exp=n_exp, n_win=n_win, rows=rows),
        grid_spec=grid_spec,
        out_shape=jax.ShapeDtypeStruct((rows_max, d), F32),
        compiler_params=_params(("arbitrary",)),
        name="moe_dispatch",
    )(npad, goff, hx, rr)


def _gmm_body(te_ref, nu_ref, xs_ref, w1_ref, w3_ref, w2_ref, ys_ref, w1b, w3b, w2b):
    t = pl.program_id(0)

    @pl.when((t == 0) | (te_ref[t] != te_ref[jnp.maximum(t - 1, 0)]))
    def _():
        w1b[...] = w1_ref[0, 0].astype(BF16)
        w3b[...] = w3_ref[0, 0].astype(BF16)
        w2b[...] = w2_ref[0, 0].astype(BF16)

    @pl.when(t < nu_ref[0])
    def _():
        x = xs_ref[...].astype(BF16)
        a = _dot(x, w1b[...])
        h = (a * jax.nn.sigmoid(a)) * _dot(x, w3b[...])
        ys_ref[...] = _dot(h.astype(BF16), w2b[...])


def _gmm(tile_exp, n_used, xs, w1, w3, w2, *, layer):
    rows_max, d = xs.shape
    de = w1.shape[3]
    row_tile = lambda t, te, nu: (jnp.minimum(t, nu[0] - 1), 0)
    wspec = lambda w: pl.BlockSpec((1, 1) + w.shape[2:], lambda t, te, nu: (layer, te[t], 0, 0))
    grid_spec = pltpu.PrefetchScalarGridSpec(
        num_scalar_prefetch=2,
        grid=(rows_max // MOE_TM,),
        in_specs=[pl.BlockSpec((MOE_TM, d), row_tile), wspec(w1), wspec(w3), wspec(w2)],
        out_specs=pl.BlockSpec((MOE_TM, d), row_tile),
        scratch_shapes=[pltpu.VMEM((d, de), BF16), pltpu.VMEM((d, de), BF16), pltpu.VMEM((de, d), BF16)],
    )
    return pl.pallas_call(
        _gmm_body,
        grid_spec=grid_spec,
        out_shape=jax.ShapeDtypeStruct((rows_max, d), F32),
        compiler_params=_params(("arbitrary",)),
        name="moe_experts",
    )(tile_exp, n_used, xs, w1, w3, w2)


def _combine_body(*refs, d, n_exp, n_win, rows, nb, final):
    if final:
        np_ref, goff_ref, ys_ref, rc_ref, xr_ref, mod_ref, gfin_ref, o_ref, yloc, sem, t_scr = refs
    else:
        np_ref, goff_ref, ys_ref, rc_ref, xr_ref, mod_ref, o_ref, yloc, sem = refs
    w = pl.program_id(0)
    slot = lax.rem(w, 2)
    copies = functools.partial(_run_copies, np_ref, goff_ref, n_exp=n_exp, remote=ys_ref, to_remote=False)

    @pl.when(w == 0)
    def _():
        copies(w, local=yloc.at[slot], sem=sem.at[slot], wait=False)

    @pl.when(w + 1 < n_win)
    def _():
        copies(w + 1, local=yloc.at[1 - slot], sem=sem.at[1 - slot], wait=False)

    copies(w, local=yloc.at[slot], sem=sem.at[slot], wait=True)

    total = np_ref[w * n_exp]
    for e in range(1, n_exp):
        total = total + np_ref[w * n_exp + e]
    y = jnp.where(lax.broadcasted_iota(jnp.int32, (rows, 1), 0) < total, yloc[slot], 0.0).astype(BF16)
    rc = rc_ref[...]
    ci = lax.broadcasted_iota(jnp.int32, (TM, rows), 1)
    pw = (jnp.where(ci == rc[:, 0:1].astype(jnp.int32), rc[:, 2:3], 0.0)
          + jnp.where(ci == rc[:, 1:2].astype(jnp.int32), rc[:, 3:4], 0.0))
    pw_hi, pw_lo = _split(pw)
    moe = _dot(pw_hi, y) + _dot(pw_lo, y)
    g2 = mod_ref[0][:, 5 * d:6 * d]
    xn = xr_ref[...] + _per_batch(moe, lambda v: v * g2[None])
    if final:
        xn = _rms(xn, gfin_ref[...])
        for g in range(d // LANES):
            t_scr[g] = xn[:, g * LANES:(g + 1) * LANES]
        for b in range(nb):
            o_ref[b] = jnp.concatenate(
                [t_scr[g, pl.ds(b, TM // nb, stride=nb), :] for g in range(d // LANES)], axis=-1)
    else:
        o_ref[...] = xn


def _combine(npad, goff, ys, rc, xres, mod, *, n_exp, n_win, n_lat_rows, nb, gfin=None, seq_out=None):
    d = xres.shape[1]
    rows = _local_rows(n_exp)
    final = gfin is not None
    n_lat = n_lat_rows // TM
    row = lambda width: pl.BlockSpec((TM, width), lambda w, a, g: (w, 0))
    in_specs = [pl.BlockSpec(memory_space=pl.ANY), row(LANES), row(d),
                pl.BlockSpec((1, SUBLANES, N_MOD * d), lambda w, a, g: (jnp.where(w < n_lat, 0, 1), 0, 0))]
    args = [npad, goff, ys, rc, xres, mod]
    scratch = [pltpu.VMEM((2, rows, d), F32), pltpu.SemaphoreType.DMA((2,))]
    if final:
        in_specs.append(pl.BlockSpec((1, d), lambda w, a, g: (0, 0)))
        args.append(gfin.reshape(1, d))
        out_shape = jax.ShapeDtypeStruct((nb, seq_out, d), F32)
        out_specs = pl.BlockSpec((nb, TM // nb, d), lambda w, a, g: (0, w, 0))
        scratch.append(pltpu.VMEM((d // LANES, TM, LANES), F32))
        aliases = {}
    else:
        out_shape = jax.ShapeDtypeStruct(xres.shape, F32)
        out_specs = row(d)
        aliases = {4: 0}
    grid_spec = pltpu.PrefetchScalarGridSpec(
        num_scalar_prefetch=2, grid=(n_win,), in_specs=in_specs, out_specs=out_specs, scratch_shapes=scratch)
    return pl.pallas_call(
        functools.partial(_combine_body, d=d, n_exp=n_exp, n_win=n_win, rows=rows, nb=nb, final=final),
        grid_spec=grid_spec,
        out_shape=out_shape,
        input_output_aliases=aliases,
        compiler_params=_params(("arbitrary",)),
        name="moe_combine",
    )(*args)


def _route_tables(cnt, *, n_exp, n_tiles):
    cnt = cnt[..., 0].astype(jnp.int32)
    npad = (cnt + (SUBLANES - 1)) & (-SUBLANES)
    seg = -(-jnp.sum(npad, axis=0) // MOE_TM) * MOE_TM
    ends = jnp.cumsum(seg)
    goff = (ends - seg)[None, :] + jnp.cumsum(npad, axis=0) - npad
    tile_ends = ends // MOE_TM
    n_used = tile_ends[-1]
    tile = jnp.minimum(jnp.arange(n_tiles, dtype=jnp.int32), n_used - 1)
    tile_exp = jnp.sum(tile[:, None] >= tile_ends[None, :], axis=1).astype(jnp.int32)
    return npad.reshape(-1), goff.reshape(-1).astype(jnp.int32), tile_exp, n_used.reshape(1).astype(jnp.int32)


def _channel_dft_tables(n):
    k = jnp.arange(n, dtype=jnp.int32)
    ang = ((k[:, None] * k[None, :]) % n).astype(F32) * (2.0 * math.pi / n)
    scale = 1.0 / math.sqrt(n)
    return jnp.cos(ang) * scale, jnp.sin(ang) * scale


def kernel(x, c, ctx, c_ctx, ada_w, ada_b, norm_mix_g, w_in, conv_w, conv_b, lru_w_a, lru_b_a, lru_w_x, lru_b_x, lru_lambda, w_proj_rnn, w_proj_fourier, w_out, norm_ffn_g, router_w, router_b, moe_w1, moe_w3, moe_w2, final_norm_g):
    nb, l, d = x.shape
    lc = ctx.shape[1]
    depth = ada_w.shape[0]
    r = conv_w.shape[2]
    f = w_proj_fourier.shape[1]
    nh, blk = lru_w_a.shape[2], lru_w_a.shape[3]
    assert nb == SUBLANES and blk == LANES and nb < MOD_ROWS
    assert l % TS == 0 and lc % TS == 0 and TS % GRID_W == 0 and (l * nb) % TR == 0 and (lc * nb) % TR == 0
    assert GRID_W & (GRID_W - 1) == 0 and lc & (lc - 1) == 0 and l % lc == 0 and lc <= 512
    s = l + lc
    n_lat_rows = l * nb

    to_rows = lambda v: jnp.transpose(v, (1, 0, 2)).reshape(v.shape[1] * nb, d)
    xa = jnp.concatenate([to_rows(x), to_rows(ctx)], axis=0)
    cc = jnp.zeros((MOD_ROWS, d), F32).at[:nb].set(c).at[nb].set(c_ctx)
    mod = _ada(cc, ada_w, ada_b)
    mod = jnp.stack([mod[:, :nb], jnp.broadcast_to(mod[:, nb:nb + 1], (depth, nb, N_MOD * d))], axis=1)

    cch, sch = _channel_dft_tables(f // FOURIER_GROUPS)
    eye = jnp.eye(FOURIER_GROUPS, dtype=F32)
    fcs = jnp.concatenate([jnp.kron(eye, cch), jnp.kron(eye, sch)], axis=1).astype(BF16)
    rwt = router_w.T
    n_exp = rwt.shape[0]

    out = None
    for li in range(depth):
        last = li == depth - 1
        u, gl, sa, sb, pq = _inproj(xa, mod[li], norm_mix_g[li], w_in[li].astype(BF16), fcs,
                                    n_lat_rows=n_lat_rows, r=r)
        wg = [jnp.concatenate([lru_w_a[li, k], lru_w_x[li, k]], axis=-1).astype(BF16) for k in range(2)]
        bg = [jnp.concatenate([lru_b_a[li, k], lru_b_x[li, k]], axis=-1).reshape(nh, 1, 2 * blk) for k in range(2)]
        lam = [lru_lambda[li, k].reshape(nh, 1, blk) for k in range(2)]
        scan = functools.partial(_scan, u, conv_w[li], conv_b[li], nb=nb, n_lat_steps=l // TS, ctx_len=lc)
        hb = scan(wg[1], bg[1], lam[1], reverse=True)
        hg = scan(wg[0], bg[0], lam[0], reverse=False, hb=hb, gl=gl)
        pq = pq.reshape(s, nb * 2 * f)
        yf = _dft(pq, n=l, nb=nb, f=f, row0=0)
        if not last:
            yf = _dft(pq, n=lc, nb=nb, f=f, row0=l, prev=yf)
        yf = yf.reshape(s * nb, f)
        n_rows = n_lat_rows if last else s * nb
        x1, hx, rr, rc, cnt = _mix(hg, yf, sa, sb, xa, mod[li], w_proj_rnn[li].astype(BF16),
                                   w_proj_fourier[li].astype(BF16), w_out[li].astype(BF16), norm_ffn_g[li],
                                   rwt, router_b, n_rows=n_rows, n_lat_rows=n_lat_rows)
        n_win = n_rows // TM
        rows_max = -(-(n_win * (2 * TM + n_exp * (SUBLANES - 1)) + n_exp * MOE_TM) // MOE_TM) * MOE_TM
        npad, goff, tile_exp, n_used = _route_tables(cnt, n_exp=n_exp, n_tiles=rows_max // MOE_TM)
        xs = _dispatch(npad, goff, hx, rr, n_exp=n_exp, n_win=n_win, rows_max=rows_max)
        ys = _gmm(tile_exp, n_used, xs, moe_w1, moe_w3, moe_w2, layer=li)
        comb = functools.partial(_combine, npad, goff, ys, rc, x1, mod[li], n_exp=n_exp, n_win=n_win,
                                 n_lat_rows=n_lat_rows, nb=nb)
        if last:
            out = comb(gfin=final_norm_g, seq_out=l)
        else:
            xa = comb()
    return out
```

```python
import functools
import math

import jax
import jax.numpy as jnp
from jax import lax
from jax.experimental import pallas as pl
from jax.experimental.pallas import tpu as pltpu

F32 = jnp.float32
BF16 = jnp.bfloat16

RMS_EPS = 1e-6
LRU_C = 8.0
GRID_W = 64
CONV_TAPS = ((0, -2), (1, -1), (2, 0), (3, 1))
N_MOD = 6
FOURIER_GROUPS = 4
N_EXPERT_GROUPS = 4
LANES = 128
SUBLANES = 8
PACKED_ROWS = 16
TR = 512
TM = 256
TS = 128
SUB = 64
MOE_TM = 512
MOD_ROWS = 16
VMEM_LIMIT = 56 * 1024 * 1024


def _dot(a, b):
    return jnp.dot(a, b, preferred_element_type=F32)


def _split(a):
    hi = a.astype(BF16)
    lo = (a - hi.astype(F32)).astype(BF16)
    return hi, lo


def _dot3(a, b):
    ah, al = _split(a)
    bh, bl = _split(b)
    return _dot(ah, bh) + _dot(al, bh) + _dot(ah, bl)


def _dot3_nt(a, b):
    dn = (((1,), (1,)), ((), ()))
    ah, al = _split(a)
    bh, bl = _split(b)
    f = lambda p, q: lax.dot_general(p, q, dn, preferred_element_type=F32)
    return f(ah, bh) + f(al, bh) + f(ah, bl)


def _gelu_tanh(x):
    return 0.5 * x * (1.0 + jnp.tanh(math.sqrt(2.0 / math.pi) * (x + 0.044715 * (x * x * x))))


def _sigmoid(x):
    return 0.5 * jnp.tanh(0.5 * x) + 0.5


def _rms(x, g):
    return x * lax.rsqrt(jnp.mean(x * x, axis=-1, keepdims=True) + RMS_EPS) * g


def _per_batch(x, fn):
    rows, d = x.shape
    return fn(x.reshape(rows // SUBLANES, SUBLANES, d)).reshape(rows, d)


def _params(sem, vmem=VMEM_LIMIT):
    return pltpu.CompilerParams(dimension_semantics=sem, vmem_limit_bytes=vmem)


def _full(shape, n_idx):
    zeros = (0,) * len(shape)
    return pl.BlockSpec(shape, lambda *_: zeros)


def _ada_body(cc_ref, w_ref, b_ref, o_ref):
    cc = cc_ref[...]
    s = cc * jax.nn.sigmoid(cc)
    o_ref[0] = _dot3(s, w_ref[0]) + b_ref[0]


def _ada(cc, ada_w, ada_b):
    depth, d, n = ada_w.shape
    tn = 1024
    return pl.pallas_call(
        _ada_body,
        grid=(depth, n // tn),
        in_specs=[
            pl.BlockSpec((MOD_ROWS, d), lambda l, j: (0, 0)),
            pl.BlockSpec((1, d, tn), lambda l, j: (l, 0, j)),
            pl.BlockSpec((1, 1, tn), lambda l, j: (l, 0, j)),
        ],
        out_specs=pl.BlockSpec((1, MOD_ROWS, tn), lambda l, j: (l, 0, j)),
        out_shape=jax.ShapeDtypeStruct((depth, MOD_ROWS, n), F32),
        compiler_params=_params(("arbitrary", "arbitrary")),
        name="ada_mod",
    )(cc, ada_w, ada_b.reshape(depth, 1, n))


def _to_time_major(blocks, scr, nb):
    npos, width = blocks[0].shape
    for b in range(nb):
        for g in range(width // LANES):
            scr[g, pl.ds(b, npos, stride=nb), :] = blocks[b][:, g * LANES:(g + 1) * LANES].astype(F32)
    return jnp.concatenate([scr[g] for g in range(width // LANES)], axis=-1)


def _to_batch_major(v, scr, nb):
    rows, width = v.shape
    for g in range(width // LANES):
        scr[g] = v[:, g * LANES:(g + 1) * LANES]
    return [jnp.concatenate([scr[g, pl.ds(b, rows // nb, stride=nb), :] for g in range(width // LANES)], axis=-1)
            for b in range(nb)]


def _inproj_body(*refs, d, r, f, nb, n_lat, first):
    if first:
        (x_ref, c_ref, mod_ref, g_ref, w_ref, fcs_ref,
         u_ref, gl_ref, sa_ref, sb_ref, pq_ref, xa_ref, t_scr) = refs

        @pl.when(pl.program_id(0) < n_lat)
        def _():
            xa_ref[...] = _to_time_major([x_ref[b] for b in range(nb)], t_scr, nb)

        @pl.when(pl.program_id(0) >= n_lat)
        def _():
            xa_ref[...] = _to_time_major([c_ref[b] for b in range(nb)], t_scr, nb)

        x = xa_ref[...]
    else:
        x_ref, mod_ref, g_ref, w_ref, fcs_ref, u_ref, gl_ref, sa_ref, sb_ref, pq_ref, t_scr = refs
        x = x_ref[...]
    mod = mod_ref[0]
    h = _rms(x, g_ref[...])
    h = _per_batch(h, lambda v: v * (1.0 + mod[:, d:2 * d])[None] + mod[:, 0:d][None]).astype(BF16)
    u_ref[...] = _dot(h, w_ref[:, 0:r]).astype(BF16)
    gl_ref[...] = _gelu_tanh(_dot(h, w_ref[:, r:2 * r])).astype(BF16)
    u4 = _dot(h, w_ref[:, 2 * r:2 * r + f]).astype(BF16)
    for b, blk in enumerate(_to_batch_major(_dot(u4, fcs_ref[...]), t_scr, nb)):
        pq_ref[:, b * 2 * f:(b + 1) * 2 * f] = blk.astype(BF16)
    s3 = 2 * r + f
    sa_ref[...] = jax.nn.sigmoid(_dot(h, w_ref[:, s3:s3 + d])).astype(BF16)
    sb_ref[...] = jax.nn.sigmoid(_dot(h, w_ref[:, s3 + d:s3 + 2 * d])).astype(BF16)


def _inproj(x, mod, g, w_in, fcs, *, n_lat_rows, r, nb, ctx=None):
    first = ctx is not None
    d = x.shape[-1]
    rows = (x.shape[1] + ctx.shape[1]) * nb if first else x.shape[0]
    f = fcs.shape[0]
    n_lat = n_lat_rows // TR
    npos = TR // nb
    row = lambda width: pl.BlockSpec((TR, width), lambda i: (i, 0))
    common = [
        pl.BlockSpec((1, SUBLANES, N_MOD * d), lambda i: (jnp.where(i < n_lat, 0, 1), 0, 0)),
        _full((1, d), 1), _full(w_in.shape, 1), _full(fcs.shape, 1),
    ]
    out_specs = [row(r), row(r), row(d), row(d), pl.BlockSpec((npos, nb * 2 * f), lambda i: (i, 0))]
    out_shape = [
        jax.ShapeDtypeStruct((rows, r), BF16),
        jax.ShapeDtypeStruct((rows, r), BF16),
        jax.ShapeDtypeStruct((rows, d), BF16),
        jax.ShapeDtypeStruct((rows, d), BF16),
        jax.ShapeDtypeStruct((rows // nb, nb * 2 * f), BF16),
    ]
    if first:
        in_specs = [
            pl.BlockSpec((nb, npos, d), lambda i: (0, jnp.minimum(i, n_lat - 1), 0)),
            pl.BlockSpec((nb, npos, d), lambda i: (0, jnp.maximum(i - n_lat, 0), 0)),
        ] + common
        args = [x, ctx]
        out_specs.append(row(d))
        out_shape.append(jax.ShapeDtypeStruct((rows, d), F32))
    else:
        in_specs = [row(d)] + common
        args = [x]
    return pl.pallas_call(
        functools.partial(_inproj_body, d=d, r=r, f=f, nb=nb, n_lat=n_lat, first=first),
        grid=(rows // TR,),
        in_specs=in_specs,
        out_specs=out_specs,
        out_shape=out_shape,
        scratch_shapes=[pltpu.VMEM((d // LANES, TR, LANES), F32)],
        compiler_params=_params(("arbitrary",)),
        name="in_proj",
    )(*args, mod, g.reshape(1, d), w_in, fcs)


def _scan_body(*refs, reverse, merge, nb, n_lat_steps, ctx_len, order):
    if merge:
        (u_ref, up_ref, un_ref, cw_ref, cb_ref, wg_ref, bg_ref, lam_ref, hb_ref, gl_ref,
         o_ref, h_scr, a_scr, b_scr, uc_scr, hs_scr) = refs
    else:
        (u_ref, up_ref, un_ref, cw_ref, cb_ref, wg_ref, bg_ref, lam_ref,
         o_ref, h_scr, a_scr, b_scr, uc_scr, hs_scr) = refs
    nh = wg_ref.shape[0]
    rows = SUB * nb
    trows = TS * nb
    j = pl.program_id(0)

    @pl.when(j == 0)
    def _():
        h_scr[...] = jnp.zeros_like(h_scr)

    blk = order(j)
    n_steps = pl.num_programs(0)
    is_ctx = blk >= n_lat_steps
    keep_prev = jnp.where(is_ctx & (blk > n_lat_steps), 1.0, 0.0)
    keep_next = jnp.where(is_ctx & (blk < n_steps - 1), 1.0, 0.0)
    lat = jnp.where(is_ctx, 0.0, 1.0)
    halo = 2 * nb
    ext = jnp.concatenate([up_ref[...].astype(F32) * keep_prev, u_ref[...].astype(F32),
                           un_ref[...].astype(F32)[0:nb] * keep_next], axis=0)
    uc = cb_ref[...]
    for k, s in CONV_TAPS:
        lo = halo + s * nb
        uc = uc + cw_ref[k:k + 1, :] * ext[lo:lo + trows, :]
    uc_scr[...] = uc
    for p in range(GRID_W, TS, GRID_W):
        at = lambda q: ext[halo + q * nb:halo + (q + 1) * nb, :]
        rows_of = lambda q: pl.ds(q * nb, nb)
        uc_scr[rows_of(p), :] -= lat * (cw_ref[0:1, :] * at(p - 2) + cw_ref[1:2, :] * at(p - 1))
        uc_scr[rows_of(p + 1), :] -= lat * (cw_ref[0:1, :] * at(p - 1))
        uc_scr[rows_of(p - 1), :] -= lat * (cw_ref[3:4, :] * at(p))

    lam = lam_ref[...]
    decay = (-LRU_C) * (jnp.maximum(-lam, 0.0) + jnp.log1p(jnp.exp(-jnp.abs(lam))))

    subs = range(TS // SUB)
    for sub in (reversed(subs) if reverse else subs):
        r0 = sub * rows
        for g in range(nh):
            ug = uc_scr[r0:r0 + rows, g * LANES:(g + 1) * LANES]
            gates = _dot(ug.astype(BF16), wg_ref[g]) + bg_ref[g]
            log_a = decay[g] * _sigmoid(gates[:, :LANES])
            a = jnp.exp(log_a)
            z = -jnp.tanh(log_a) * (a * a + 1.0)
            mult = jnp.where(z > 0.0, z * lax.rsqrt(z), 0.0)
            a_scr[g] = a
            b_scr[g] = mult * (_sigmoid(gates[:, LANES:]) * ug)

        def step(k, hs):
            t = (SUB - 1 - k) if reverse else k
            off = pl.multiple_of(t * nb, nb)
            new = []
            for g in range(nh):
                hg = a_scr[g, pl.ds(off, nb), :] * hs[g] + b_scr[g, pl.ds(off, nb), :]
                hs_scr[pl.ds(r0 + off, nb), g * LANES:(g + 1) * LANES] = hg
                new.append(hg)
            return tuple(new)

        hs = lax.fori_loop(0, SUB, step, tuple(h_scr[g] for g in range(nh)), unroll=8)
        for g in range(nh):
            h_scr[g] = hs[g]

    if merge:
        o_ref[...] = ((hs_scr[...] + hb_ref[...].astype(F32)) * gl_ref[...].astype(F32)).astype(BF16)
    else:
        o_ref[...] = hs_scr[...].astype(BF16)


def _scan(u, conv_w, conv_b, wg, bg, lam, *, nb, n_lat_steps, ctx_len, reverse, hb=None, gl=None):
    rows_all, r = u.shape
    nh = wg.shape[0]
    trows = TS * nb
    n = rows_all // trows
    n_ctx = n - n_lat_steps
    merge = hb is not None
    if reverse:
        order = lambda j: jnp.where(j < n_ctx, n - 1 - j, n_lat_steps - 1 - (j - n_ctx))
    else:
        order = lambda j: jnp.where(j < n_ctx, n_lat_steps + j, j - n_ctx)
    hpb = trows // PACKED_ROWS
    tile = pl.BlockSpec((trows, r), lambda j: (order(j), 0))
    in_specs = [
        tile,
        pl.BlockSpec((PACKED_ROWS, r), lambda j: (jnp.maximum(order(j) * hpb - 1, 0), 0)),
        pl.BlockSpec((PACKED_ROWS, r), lambda j: (jnp.minimum((order(j) + 1) * hpb, n * hpb - 1), 0)),
        _full(conv_w.shape, 1), _full((1, r), 1), _full(wg.shape, 1), _full(bg.shape, 1), _full(lam.shape, 1),
    ]
    args = [u, u, u, conv_w, conv_b.reshape(1, r), wg, bg, lam]
    if merge:
        in_specs += [tile, tile]
        args += [hb, gl]
    return pl.pallas_call(
        functools.partial(_scan_body, reverse=reverse, merge=merge, nb=nb, n_lat_steps=n_lat_steps,
                          ctx_len=ctx_len, order=order),
        grid=(n,),
        in_specs=in_specs,
        out_specs=tile,
        out_shape=jax.ShapeDtypeStruct((rows_all, r), BF16),
        scratch_shapes=[
            pltpu.VMEM((nh, nb, LANES), F32),
            pltpu.VMEM((nh, SUB * nb, LANES), F32),
            pltpu.VMEM((nh, SUB * nb, LANES), F32),
            pltpu.VMEM((trows, r), F32),
            pltpu.VMEM((trows, r), F32),
        ],
        compiler_params=_params(("arbitrary",)),
        name="scan_fwd_merge" if merge else "scan_bwd",
    )(*args)


def _dft_body(ca_ref, sa_ref, cb_ref, sb_ref, p_ref, q_ref, *rest, n):
    o_ref, c_scr, s_scr = rest[-3:]

    @pl.when(pl.program_id(1) == 0)
    def _():
        cb = cb_ref[...]
        sb = sb_ref[...]
        for t1 in range(n // LANES):
            ca = ca_ref[:, t1:t1 + 1]
            sa = sa_ref[:, t1:t1 + 1]
            c_scr[:, t1 * LANES:(t1 + 1) * LANES] = (ca * cb - sa * sb).astype(BF16)
            s_scr[:, t1 * LANES:(t1 + 1) * LANES] = (sa * cb + ca * sb).astype(BF16)

    o_ref[...] = (_dot(c_scr[...], p_ref[...]) - _dot(s_scr[...], q_ref[...])).astype(BF16)


def _dft_tables(n):
    k = jnp.arange(n, dtype=jnp.int32)[:, None]
    t1 = jnp.arange(LANES, dtype=jnp.int32)[None, :]
    ang_a = ((k * ((t1 * LANES) % n)) % n).astype(F32) * (2.0 * math.pi / n)
    ang_b = ((k * t1) % n).astype(F32) * (2.0 * math.pi / n)
    scale = 1.0 / math.sqrt(n)
    return jnp.cos(ang_a), jnp.sin(ang_a), jnp.cos(ang_b) * scale, jnp.sin(ang_b) * scale


def _dft(pq, *, n, nb, f, row0, prev=None):
    s = pq.shape[0]
    tmk = min(n, 512)
    rb = row0 // n
    tab = pl.BlockSpec((tmk, LANES), lambda m, b: (m, 0))
    in_specs = [
        tab, tab, tab, tab,
        pl.BlockSpec((n, f), lambda m, b: (rb, 2 * b)),
        pl.BlockSpec((n, f), lambda m, b: (rb, 2 * b + 1)),
    ]
    args = list(_dft_tables(n)) + [pq, pq]
    aliases = {}
    if prev is not None:
        in_specs.append(pl.BlockSpec(memory_space=pl.ANY))
        args.append(prev)
        aliases = {6: 0}
    return pl.pallas_call(
        functools.partial(_dft_body, n=n),
        grid=(n // tmk, nb),
        in_specs=in_specs,
        out_specs=pl.BlockSpec((tmk, f), lambda m, b: (row0 // tmk + m, b)),
        out_shape=jax.ShapeDtypeStruct((s, nb * f), BF16),
        scratch_shapes=[pltpu.VMEM((tmk, n), BF16), pltpu.VMEM((tmk, n), BF16)],
        input_output_aliases=aliases,
        compiler_params=_params(("arbitrary", "arbitrary")),
        name="pos_dft",
    )(*args)


def _top2(vals):
    def first_max(vs):
        m = functools.reduce(jnp.maximum, vs)
        idx = jnp.full(m.shape, len(vs) - 1, jnp.int32)
        for k in range(len(vs) - 2, -1, -1):
            idx = jnp.where(vs[k] == m, k, idx)
        return m, idx
    m1, i1 = first_max(vals)
    rest = [jnp.where(i1 == k, -jnp.inf, v) for k, v in enumerate(vals)]
    m2, i2 = first_max(rest)
    return m1, i1, m2, i2


def _pick(idx, vals):
    out = vals[-1]
    for k in range(len(vals) - 2, -1, -1):
        out = jnp.where(idx == k, vals[k], out)
    return out


def _route_window(h2, rw, rb, n_exp):
    aff = jax.nn.sigmoid(_dot3(h2, rw).T[0:n_exp, :])
    sel = aff + rb
    per = n_exp // N_EXPERT_GROUPS
    srow = [sel[e:e + 1, :] for e in range(n_exp)]
    arow = [aff[e:e + 1, :] for e in range(n_exp)]
    scores = []
    for gi in range(N_EXPERT_GROUPS):
        m1, _, m2, _ = _top2(srow[gi * per:(gi + 1) * per])
        scores.append(m1 + m2)
    best = functools.reduce(jnp.maximum, scores)
    grp = jnp.full(best.shape, N_EXPERT_GROUPS - 1, jnp.int32)
    for gi in range(N_EXPERT_GROUPS - 2, -1, -1):
        grp = jnp.where(scores[gi] == best, gi, grp)
    v = [_pick(grp, [srow[gi * per + k] for gi in range(N_EXPERT_GROUPS)]) for k in range(per)]
    a = [_pick(grp, [arow[gi * per + k] for gi in range(N_EXPERT_GROUPS)]) for k in range(per)]
    _, i1, _, i2 = _top2(v)
    w1 = _pick(i1, a)
    w2 = _pick(i2, a)
    den = w1 + w2
    e1 = grp * per + i1
    e2 = grp * per + i2

    erow = lax.broadcasted_iota(jnp.int32, (n_exp, TM), 0)
    hit1 = erow == e1
    hit2 = erow == e2
    onehot = jnp.where(hit1 | hit2, 1.0, 0.0)
    cnt = jnp.sum(onehot, axis=1, keepdims=True)
    before = (lax.broadcasted_iota(jnp.int32, (TM, TM), 0)
              < lax.broadcasted_iota(jnp.int32, (TM, TM), 1)).astype(BF16)
    rank = _dot(onehot.astype(BF16), before)
    padded = ((cnt.astype(jnp.int32) + (SUBLANES - 1)) & (-SUBLANES)).astype(F32)
    lower = (lax.broadcasted_iota(jnp.int32, (n_exp, n_exp), 1)
             < lax.broadcasted_iota(jnp.int32, (n_exp, n_exp), 0)).astype(BF16)
    starts = _dot(lower, jnp.broadcast_to(padded, (n_exp, LANES)).astype(BF16))
    pos = starts[:, 0:1] + rank
    r1 = jnp.sum(jnp.where(hit1, pos, 0.0), axis=0, keepdims=True)
    r2 = jnp.sum(jnp.where(hit2, pos, 0.0), axis=0, keepdims=True)
    route = jnp.concatenate([r1, r2, w1 / den, w2 / den, jnp.zeros((LANES - 4, TM), F32)], axis=0)
    return route, jnp.broadcast_to(cnt, (n_exp, LANES))


def _mix_body(hg_ref, yf_ref, sa_ref, sb_ref, x_ref, mod_ref, wpr_ref, wpf_ref, wo_ref,
              gffn_ref, rw_ref, rb_ref, xo_ref, hx_ref, rr_ref, rc_ref, cnt_ref, t_scr, *, d, n_exp, nb):
    mod = mod_ref[0]
    f = wpf_ref.shape[0]
    y_r = _dot(hg_ref[...], wpr_ref[...])
    yf = _to_time_major([yf_ref[:, b * f:(b + 1) * f] for b in range(nb)], t_scr, nb)
    y_f = _dot(yf.astype(BF16), wpf_ref[...])
    merged = sa_ref[...].astype(F32) * y_r + sb_ref[...].astype(F32) * y_f
    out = _dot(merged.astype(BF16), wo_ref[...])
    xn = x_ref[...] + _per_batch(out, lambda v: v * mod[:, 2 * d:3 * d][None])
    xo_ref[...] = xn
    h2 = _rms(xn, gffn_ref[...])
    h2 = _per_batch(h2, lambda v: v * (1.0 + mod[:, 4 * d:5 * d])[None] + mod[:, 3 * d:4 * d][None])
    hx_ref[...] = h2.astype(BF16)
    for k in range(TR // TM):
        route, cnt = _route_window(h2[k * TM:(k + 1) * TM, :], rw_ref[...], rb_ref[...], n_exp)
        rr_ref[k] = route[0:SUBLANES, :]
        rc_ref[k * TM:(k + 1) * TM, :] = route.T
        cnt_ref[k] = cnt


def _mix(hg, yf, sa, sb, xa, mod, wpr, wpf, wo, gffn, rw, rb, *, n_rows, n_lat_rows, nb):
    rows_all, d = xa.shape
    f = wpf.shape[0]
    n_exp = rb.shape[0]
    n_lat = n_lat_rows // TR
    wpt = TR // TM
    n_win = n_rows // TM
    row = lambda width: pl.BlockSpec((TR, width), lambda i: (i, 0))
    return pl.pallas_call(
        functools.partial(_mix_body, d=d, n_exp=n_exp, nb=nb),
        grid=(n_rows // TR,),
        in_specs=[
            row(d), pl.BlockSpec((TR // nb, nb * f), lambda i: (i, 0)), row(d), row(d), row(d),
            pl.BlockSpec((1, SUBLANES, N_MOD * d), lambda i: (jnp.where(i < n_lat, 0, 1), 0, 0)),
            _full(wpr.shape, 1), _full(wpf.shape, 1), _full(wo.shape, 1),
            _full((1, d), 1), _full(rw.shape, 1), _full((n_exp, 1), 1),
        ],
        scratch_shapes=[pltpu.VMEM((f // LANES, TR, LANES), F32)],
        out_specs=[
            row(d), row(d),
            pl.BlockSpec((wpt, SUBLANES, TM), lambda i: (i, 0, 0)),
            row(LANES),
            pl.BlockSpec((wpt, n_exp, LANES), lambda i: (i, 0, 0)),
        ],
        out_shape=[
            jax.ShapeDtypeStruct((rows_all, d), F32),
            jax.ShapeDtypeStruct((rows_all, d), BF16),
            jax.ShapeDtypeStruct((n_win, SUBLANES, TM), F32),
            jax.ShapeDtypeStruct((rows_all, LANES), F32),
            jax.ShapeDtypeStruct((n_win, n_exp, LANES), F32),
        ],
        compiler_params=_params(("arbitrary",)),
        name="mix_out",
    )(hg, yf, sa, sb, xa, mod, wpr, wpf, wo, gffn.reshape(1, d), rw, rb.reshape(n_exp, 1))


def _local_rows(n_exp):
    return -(-(2 * TM + n_exp * (SUBLANES - 1)) // LANES) * LANES


def _run_copy(local, remote, sem, lo_row, re_row, n_rows, to_remote):
    lo = local.at[pl.ds(lo_row, n_rows)]
    re = remote.at[pl.ds(re_row, n_rows)]
    return pltpu.make_async_copy(lo, re, sem) if to_remote else pltpu.make_async_copy(re, lo, sem)


def _run_copies(np_ref, goff_ref, win, *, n_exp, local, remote, sem, to_remote, wait):
    if wait:
        total = np_ref[win * n_exp]
        for e in range(1, n_exp):
            total = total + np_ref[win * n_exp + e]
        chunks = lax.shift_right_logical(total, 3)
        for bit in range((local.shape[0] // SUBLANES).bit_length()):
            @pl.when((lax.shift_right_logical(chunks, bit) & 1) == 1)
            def _():
                _run_copy(local, remote, sem, 0, 0, SUBLANES << bit, to_remote).wait()
        return
    src = 0
    for e in range(n_exp):
        n = np_ref[win * n_exp + e]
        dst = goff_ref[win * n_exp + e]

        def one(c, carry, src=src, dst=dst):
            _run_copy(local, remote, sem, pl.multiple_of(src + c * SUBLANES, SUBLANES),
                      pl.multiple_of(dst + c * SUBLANES, SUBLANES), SUBLANES, to_remote).start()
            return carry

        lax.fori_loop(0, lax.shift_right_logical(n, 3), one, 0)
        src = src + n


def _dispatch_body(np_ref, goff_ref, hx_ref, rr_ref, xs_ref, xloc, sem, *, n_exp, n_win, rows):
    w = pl.program_id(0)
    slot = lax.rem(w, 2)
    copies = functools.partial(_run_copies, np_ref, goff_ref, n_exp=n_exp, remote=xs_ref, to_remote=True)

    @pl.when(w >= 2)
    def _():
        copies(w - 2, local=xloc.at[slot], sem=sem.at[slot], wait=True)

    rr = rr_ref[0].astype(jnp.int32)
    riota = lax.broadcasted_iota(jnp.int32, (rows, TM), 0)
    p = jnp.where((riota == rr[0:1, :]) | (riota == rr[1:2, :]), 1.0, 0.0).astype(BF16)
    xloc[slot] = _dot(p, hx_ref[...])
    copies(w, local=xloc.at[slot], sem=sem.at[slot], wait=False)

    @pl.when(w == n_win - 1)
    def _():
        copies(w - 1, local=xloc.at[1 - slot], sem=sem.at[1 - slot], wait=True)
        copies(w, local=xloc.at[slot], sem=sem.at[slot], wait=True)


def _dispatch(npad, goff, hx, rr, *, n_exp, n_win, rows_max):
    d = hx.shape[1]
    rows = _local_rows(n_exp)
    assert n_win >= 2
    grid_spec = pltpu.PrefetchScalarGridSpec(
        num_scalar_prefetch=2,
        grid=(n_win,),
        in_specs=[
            pl.BlockSpec((TM, d), lambda w, a, g: (w, 0)),
            pl.BlockSpec((1, SUBLANES, TM), lambda w, a, g: (w, 0, 0)),
        ],
        out_specs=pl.BlockSpec(memory_space=pl.ANY),
        scratch_shapes=[pltpu.VMEM((2, rows, d), F32), pltpu.SemaphoreType.DMA((2,))],
    )
    return pl.pallas_call(
        functools.partial(_dispatch_body, n_exp=n_exp, n_win=n_win, rows=rows),
        grid_spec=grid_spec,
        out_shape=jax.ShapeDtypeStruct((rows_max, d), F32),
        compiler_params=_params(("arbitrary",)),
        name="moe_dispatch",
    )(npad, goff, hx, rr)


def _gmm_body(te_ref, nu_ref, xs_ref, w1_ref, w3_ref, w2_ref, ys_ref, w1b, w3b, w2b):
    t = pl.program_id(0)

    @pl.when((t == 0) | (te_ref[t] != te_ref[jnp.maximum(t - 1, 0)]))
    def _():
        w1b[...] = w1_ref[0, 0].astype(BF16)
        w3b[...] = w3_ref[0, 0].astype(BF16)
        w2b[...] = w2_ref[0, 0].astype(BF16)

    @pl.when(t < nu_ref[0])
    def _():
        x = xs_ref[...].astype(BF16)
        a = _dot(x, w1b[...])
        h = (a * jax.nn.sigmoid(a)) * _dot(x, w3b[...])
        ys_ref[...] = _dot(h.astype(BF16), w2b[...])


def _gmm(tile_exp, n_used, xs, w1, w3, w2, *, layer):
    rows_max, d = xs.shape
    de = w1.shape[3]
    row_tile = lambda t, te, nu: (jnp.minimum(t, nu[0] - 1), 0)
    wspec = lambda w: pl.BlockSpec((1, 1) + w.shape[2:], lambda t, te, nu: (layer, te[t], 0, 0))
    grid_spec = pltpu.PrefetchScalarGridSpec(
        num_scalar_prefetch=2,
        grid=(rows_max // MOE_TM,),
        in_specs=[pl.BlockSpec((MOE_TM, d), row_tile), wspec(w1), wspec(w3), wspec(w2)],
        out_specs=pl.BlockSpec((MOE_TM, d), row_tile),
        scratch_shapes=[pltpu.VMEM((d, de), BF16), pltpu.VMEM((d, de), BF16), pltpu.VMEM((de, d), BF16)],
    )
    return pl.pallas_call(
        _gmm_body,
        grid_spec=grid_spec,
        out_shape=jax.ShapeDtypeStruct((rows_max, d), F32),
        compiler_params=_params(("arbitrary",)),
        name="moe_experts",
    )(tile_exp, n_used, xs, w1, w3, w2)


def _combine_body(*refs, d, n_exp, n_win, rows, nb, final):
    if final:
        np_ref, goff_ref, ys_ref, rc_ref, xr_ref, mod_ref, gfin_ref, o_ref, yloc, sem, t_scr = refs
    else:
        np_ref, goff_ref, ys_ref, rc_ref, xr_ref, mod_ref, o_ref, yloc, sem = refs
    w = pl.program_id(0)
    slot = lax.rem(w, 2)
    copies = functools.partial(_run_copies, np_ref, goff_ref, n_exp=n_exp, remote=ys_ref, to_remote=False)

    @pl.when(w == 0)
    def _():
        copies(w, local=yloc.at[slot], sem=sem.at[slot], wait=False)

    @pl.when(w + 1 < n_win)
    def _():
        copies(w + 1, local=yloc.at[1 - slot], sem=sem.at[1 - slot], wait=False)

    copies(w, local=yloc.at[slot], sem=sem.at[slot], wait=True)

    total = np_ref[w * n_exp]
    for e in range(1, n_exp):
        total = total + np_ref[w * n_exp + e]
    y = jnp.where(lax.broadcasted_iota(jnp.int32, (rows, 1), 0) < total, yloc[slot], 0.0).astype(BF16)
    rc = rc_ref[...]
    ci = lax.broadcasted_iota(jnp.int32, (TM, rows), 1)
    pw = (jnp.where(ci == rc[:, 0:1].astype(jnp.int32), rc[:, 2:3], 0.0)
          + jnp.where(ci == rc[:, 1:2].astype(jnp.int32), rc[:, 3:4], 0.0))
    moe = _dot(pw.astype(BF16), y)
    g2 = mod_ref[0][:, 5 * d:6 * d]
    xn = xr_ref[...] + _per_batch(moe, lambda v: v * g2[None])
    if final:
        xn = _rms(xn, gfin_ref[...])
        for g in range(d // LANES):
            t_scr[g] = xn[:, g * LANES:(g + 1) * LANES]
        for b in range(nb):
            o_ref[b] = jnp.concatenate(
                [t_scr[g, pl.ds(b, TM // nb, stride=nb), :] for g in range(d // LANES)], axis=-1)
    else:
        o_ref[...] = xn


def _combine(npad, goff, ys, rc, xres, mod, *, n_exp, n_win, n_lat_rows, nb, gfin=None, seq_out=None):
    d = xres.shape[1]
    rows = _local_rows(n_exp)
    final = gfin is not None
    n_lat = n_lat_rows // TM
    row = lambda width: pl.BlockSpec((TM, width), lambda w, a, g: (w, 0))
    in_specs = [pl.BlockSpec(memory_space=pl.ANY), row(LANES), row(d),
                pl.BlockSpec((1, SUBLANES, N_MOD * d), lambda w, a, g: (jnp.where(w < n_lat, 0, 1), 0, 0))]
    args = [npad, goff, ys, rc, xres, mod]
    scratch = [pltpu.VMEM((2, rows, d), F32), pltpu.SemaphoreType.DMA((2,))]
    if final:
        in_specs.append(pl.BlockSpec((1, d), lambda w, a, g: (0, 0)))
        args.append(gfin.reshape(1, d))
        out_shape = jax.ShapeDtypeStruct((nb, seq_out, d), F32)
        out_specs = pl.BlockSpec((nb, TM // nb, d), lambda w, a, g: (0, w, 0))
        scratch.append(pltpu.VMEM((d // LANES, TM, LANES), F32))
        aliases = {}
    else:
        out_shape = jax.ShapeDtypeStruct(xres.shape, F32)
        out_specs = row(d)
        aliases = {4: 0}
    grid_spec = pltpu.PrefetchScalarGridSpec(
        num_scalar_prefetch=2, grid=(n_win,), in_specs=in_specs, out_specs=out_specs, scratch_shapes=scratch)
    return pl.pallas_call(
        functools.partial(_combine_body, d=d, n_exp=n_exp, n_win=n_win, rows=rows, nb=nb, final=final),
        grid_spec=grid_spec,
        out_shape=out_shape,
        input_output_aliases=aliases,
        compiler_params=_params(("arbitrary",)),
        name="moe_combine",
    )(*args)


def _route_tables(cnt, *, n_exp, n_tiles):
    cnt = cnt[..., 0].astype(jnp.int32)
    npad = (cnt + (SUBLANES - 1)) & (-SUBLANES)
    seg = -(-jnp.sum(npad, axis=0) // MOE_TM) * MOE_TM
    ends = jnp.cumsum(seg)
    goff = (ends - seg)[None, :] + jnp.cumsum(npad, axis=0) - npad
    tile_ends = ends // MOE_TM
    n_used = tile_ends[-1]
    tile = jnp.minimum(jnp.arange(n_tiles, dtype=jnp.int32), n_used - 1)
    tile_exp = jnp.sum(tile[:, None] >= tile_ends[None, :], axis=1).astype(jnp.int32)
    return npad.reshape(-1), goff.reshape(-1).astype(jnp.int32), tile_exp, n_used.reshape(1).astype(jnp.int32)


def _channel_dft_tables(n):
    k = jnp.arange(n, dtype=jnp.int32)
    ang = ((k[:, None] * k[None, :]) % n).astype(F32) * (2.0 * math.pi / n)
    scale = 1.0 / math.sqrt(n)
    return jnp.cos(ang) * scale, jnp.sin(ang) * scale


def kernel(x, c, ctx, c_ctx, ada_w, ada_b, norm_mix_g, w_in, conv_w, conv_b, lru_w_a, lru_b_a, lru_w_x, lru_b_x, lru_lambda, w_proj_rnn, w_proj_fourier, w_out, norm_ffn_g, router_w, router_b, moe_w1, moe_w3, moe_w2, final_norm_g):
    nb, l, d = x.shape
    lc = ctx.shape[1]
    depth = ada_w.shape[0]
    r = conv_w.shape[2]
    f = w_proj_fourier.shape[1]
    nh, blk = lru_w_a.shape[2], lru_w_a.shape[3]
    assert nb == SUBLANES and blk == LANES and nb < MOD_ROWS
    assert l % TS == 0 and lc % TS == 0 and TS % GRID_W == 0 and (l * nb) % TR == 0 and (lc * nb) % TR == 0
    assert GRID_W & (GRID_W - 1) == 0 and lc & (lc - 1) == 0 and l % lc == 0 and lc <= 512
    s = l + lc
    n_lat_rows = l * nb

    cc = jnp.zeros((MOD_ROWS, d), F32).at[:nb].set(c).at[nb].set(c_ctx)
    mod = _ada(cc, ada_w, ada_b)
    mod = jnp.stack([mod[:, :nb], jnp.broadcast_to(mod[:, nb:nb + 1], (depth, nb, N_MOD * d))], axis=1)

    cch, sch = _channel_dft_tables(f // FOURIER_GROUPS)
    eye = jnp.eye(FOURIER_GROUPS, dtype=F32)
    fcs = jnp.concatenate([jnp.kron(eye, cch), jnp.kron(eye, sch)], axis=1).astype(BF16)
    n_exp = router_w.shape[1]
    rw = jnp.pad(router_w, ((0, 0), (0, LANES - n_exp)))

    out = xa = None
    for li in range(depth):
        last = li == depth - 1
        inproj = functools.partial(_inproj, mod=mod[li], g=norm_mix_g[li], w_in=w_in[li].astype(BF16), fcs=fcs,
                                   n_lat_rows=n_lat_rows, r=r, nb=nb)
        if li == 0:
            u, gl, sa, sb, pq, xa = inproj(x, ctx=ctx)
        else:
            u, gl, sa, sb, pq = inproj(xa)
        wg = [jnp.concatenate([lru_w_a[li, k], lru_w_x[li, k]], axis=-1).astype(BF16) for k in range(2)]
        bg = [jnp.concatenate([lru_b_a[li, k], lru_b_x[li, k]], axis=-1).reshape(nh, 1, 2 * blk) for k in range(2)]
        lam = [lru_lambda[li, k].reshape(nh, 1, blk) for k in range(2)]
        scan = functools.partial(_scan, u, conv_w[li], conv_b[li], nb=nb, n_lat_steps=l // TS, ctx_len=lc)
        hb = scan(wg[1], bg[1], lam[1], reverse=True)
        hg = scan(wg[0], bg[0], lam[0], reverse=False, hb=hb, gl=gl)
        yf = _dft(pq, n=l, nb=nb, f=f, row0=0)
        if not last:
            yf = _dft(pq, n=lc, nb=nb, f=f, row0=l, prev=yf)
        n_rows = n_lat_rows if last else s * nb
        x1, hx, rr, rc, cnt = _mix(hg, yf, sa, sb, xa, mod[li], w_proj_rnn[li].astype(BF16),
                                   w_proj_fourier[li].astype(BF16), w_out[li].astype(BF16), norm_ffn_g[li],
                                   rw, router_b, n_rows=n_rows, n_lat_rows=n_lat_rows, nb=nb)
        n_win = n_rows // TM
        rows_max = -(-(n_win * (2 * TM + n_exp * (SUBLANES - 1)) + n_exp * MOE_TM) // MOE_TM) * MOE_TM
        npad, goff, tile_exp, n_used = _route_tables(cnt, n_exp=n_exp, n_tiles=rows_max // MOE_TM)
        xs = _dispatch(npad, goff, hx, rr, n_exp=n_exp, n_win=n_win, rows_max=rows_max)
        ys = _gmm(tile_exp, n_used, xs, moe_w1, moe_w3, moe_w2, layer=li)
        comb = functools.partial(_combine, npad, goff, ys, rc, x1, mod[li], n_exp=n_exp, n_win=n_win,
                                 n_lat_rows=n_lat_rows, nb=nb)
        if last:
            out = comb(gfin=final_norm_g, seq_out=l)
        else:
            xa = comb()
    return out
```

```python
import functools
import math

import jax
import jax.numpy as jnp
from jax import lax
from jax.experimental import pallas as pl
from jax.experimental.pallas import tpu as pltpu

F32 = jnp.float32
BF16 = jnp.bfloat16

RMS_EPS = 1e-6
LRU_C = 8.0
GRID_W = 64
CONV_TAPS = ((0, -2), (1, -1), (2, 0), (3, 1))
N_MOD = 6
FOURIER_GROUPS = 4
N_EXPERT_GROUPS = 4
LANES = 128
SUBLANES = 8
PACKED_ROWS = 16
TR = 512
TM = 256
TS = 128
SUB = 64
MOE_TM = 512
MOD_ROWS = 16
VMEM_LIMIT = 56 * 1024 * 1024


def _dot(a, b):
    return jnp.dot(a, b, preferred_element_type=F32)


def _split(a):
    hi = a.astype(BF16)
    lo = (a - hi.astype(F32)).astype(BF16)
    return hi, lo


def _dot3(a, b):
    ah, al = _split(a)
    bh, bl = _split(b)
    return _dot(ah, bh) + _dot(al, bh) + _dot(ah, bl)


def _dot3_nt(a, b):
    dn = (((1,), (1,)), ((), ()))
    ah, al = _split(a)
    bh, bl = _split(b)
    f = lambda p, q: lax.dot_general(p, q, dn, preferred_element_type=F32)
    return f(ah, bh) + f(al, bh) + f(ah, bl)


def _gelu_tanh(x):
    return 0.5 * x * (1.0 + jnp.tanh(math.sqrt(2.0 / math.pi) * (x + 0.044715 * (x * x * x))))


def _sigmoid(x):
    return 0.5 * jnp.tanh(0.5 * x) + 0.5


def _rms(x, g):
    return x * lax.rsqrt(jnp.mean(x * x, axis=-1, keepdims=True) + RMS_EPS) * g


def _per_batch(x, fn):
    rows, d = x.shape
    return fn(x.reshape(rows // SUBLANES, SUBLANES, d)).reshape(rows, d)


def _params(sem, vmem=VMEM_LIMIT):
    return pltpu.CompilerParams(dimension_semantics=sem, vmem_limit_bytes=vmem)


def _full(shape, n_idx):
    zeros = (0,) * len(shape)
    return pl.BlockSpec(shape, lambda *_: zeros)


def _ada_body(cc_ref, w_ref, b_ref, o_ref):
    cc = cc_ref[...]
    s = cc * jax.nn.sigmoid(cc)
    o_ref[0] = _dot3(s, w_ref[0]) + b_ref[0]


def _ada(cc, ada_w, ada_b):
    depth, d, n = ada_w.shape
    tn = 1024
    return pl.pallas_call(
        _ada_body,
        grid=(depth, n // tn),
        in_specs=[
            pl.BlockSpec((MOD_ROWS, d), lambda l, j: (0, 0)),
            pl.BlockSpec((1, d, tn), lambda l, j: (l, 0, j)),
            pl.BlockSpec((1, 1, tn), lambda l, j: (l, 0, j)),
        ],
        out_specs=pl.BlockSpec((1, MOD_ROWS, tn), lambda l, j: (l, 0, j)),
        out_shape=jax.ShapeDtypeStruct((depth, MOD_ROWS, n), F32),
        compiler_params=_params(("arbitrary", "arbitrary")),
        name="ada_mod",
    )(cc, ada_w, ada_b.reshape(depth, 1, n))


def _to_time_major(blocks, scr, nb):
    npos, width = blocks[0].shape
    for b in range(nb):
        for g in range(width // LANES):
            scr[g, pl.ds(b, npos, stride=nb), :] = blocks[b][:, g * LANES:(g + 1) * LANES].astype(F32)
    return jnp.concatenate([scr[g] for g in range(width // LANES)], axis=-1)


def _to_batch_major(v, scr, nb):
    rows, width = v.shape
    for g in range(width // LANES):
        scr[g] = v[:, g * LANES:(g + 1) * LANES]
    return [jnp.concatenate([scr[g, pl.ds(b, rows // nb, stride=nb), :] for g in range(width // LANES)], axis=-1)
            for b in range(nb)]


def _inproj_body(*refs, d, r, f, nb, n_lat, first):
    if first:
        (x_ref, c_ref, mod_ref, g_ref, w_ref, fcs_ref,
         u_ref, gl_ref, sa_ref, sb_ref, pq_ref, xa_ref, t_scr) = refs

        @pl.when(pl.program_id(0) < n_lat)
        def _():
            xa_ref[...] = _to_time_major([x_ref[b] for b in range(nb)], t_scr, nb)

        @pl.when(pl.program_id(0) >= n_lat)
        def _():
            xa_ref[...] = _to_time_major([c_ref[b] for b in range(nb)], t_scr, nb)

        x = xa_ref[...]
    else:
        x_ref, mod_ref, g_ref, w_ref, fcs_ref, u_ref, gl_ref, sa_ref, sb_ref, pq_ref, t_scr = refs
        x = x_ref[...]
    mod = mod_ref[0]
    h = _rms(x, g_ref[...])
    h = _per_batch(h, lambda v: v * (1.0 + mod[:, d:2 * d])[None] + mod[:, 0:d][None]).astype(BF16)
    u_ref[...] = _dot(h, w_ref[:, 0:r]).astype(BF16)
    gl_ref[...] = _gelu_tanh(_dot(h, w_ref[:, r:2 * r])).astype(BF16)
    u4 = _dot(h, w_ref[:, 2 * r:2 * r + f]).astype(BF16)
    for b, blk in enumerate(_to_batch_major(_dot(u4, fcs_ref[...]), t_scr, nb)):
        pq_ref[:, b * 2 * f:(b + 1) * 2 * f] = blk.astype(BF16)
    s3 = 2 * r + f
    sa_ref[...] = jax.nn.sigmoid(_dot(h, w_ref[:, s3:s3 + d])).astype(BF16)
    sb_ref[...] = jax.nn.sigmoid(_dot(h, w_ref[:, s3 + d:s3 + 2 * d])).astype(BF16)


def _inproj(x, mod, g, w_in, fcs, *, n_lat_rows, r, nb, ctx=None):
    first = ctx is not None
    d = x.shape[-1]
    rows = (x.shape[1] + ctx.shape[1]) * nb if first else x.shape[0]
    f = fcs.shape[0]
    n_lat = n_lat_rows // TR
    npos = TR // nb
    row = lambda width: pl.BlockSpec((TR, width), lambda i: (i, 0))
    common = [
        pl.BlockSpec((1, SUBLANES, N_MOD * d), lambda i: (jnp.where(i < n_lat, 0, 1), 0, 0)),
        _full((1, d), 1), _full(w_in.shape, 1), _full(fcs.shape, 1),
    ]
    out_specs = [row(r), row(r), row(d), row(d), pl.BlockSpec((npos, nb * 2 * f), lambda i: (i, 0))]
    out_shape = [
        jax.ShapeDtypeStruct((rows, r), BF16),
        jax.ShapeDtypeStruct((rows, r), BF16),
        jax.ShapeDtypeStruct((rows, d), BF16),
        jax.ShapeDtypeStruct((rows, d), BF16),
        jax.ShapeDtypeStruct((rows // nb, nb * 2 * f), BF16),
    ]
    if first:
        in_specs = [
            pl.BlockSpec((nb, npos, d), lambda i: (0, jnp.minimum(i, n_lat - 1), 0)),
            pl.BlockSpec((nb, npos, d), lambda i: (0, jnp.maximum(i - n_lat, 0), 0)),
        ] + common
        args = [x, ctx]
        out_specs.append(row(d))
        out_shape.append(jax.ShapeDtypeStruct((rows, d), F32))
    else:
        in_specs = [row(d)] + common
        args = [x]
    return pl.pallas_call(
        functools.partial(_inproj_body, d=d, r=r, f=f, nb=nb, n_lat=n_lat, first=first),
        grid=(rows // TR,),
        in_specs=in_specs,
        out_specs=out_specs,
        out_shape=out_shape,
        scratch_shapes=[pltpu.VMEM((d // LANES, TR, LANES), F32)],
        compiler_params=_params(("arbitrary",)),
        name="in_proj",
    )(*args, mod, g.reshape(1, d), w_in, fcs)


def _scan_body(*refs, reverse, merge, nb, n_lat_steps, ctx_len, order):
    if merge:
        (u_ref, up_ref, un_ref, cw_ref, cb_ref, wg_ref, bg_ref, lam_ref, hb_ref, gl_ref,
         o_ref, h_scr, a_scr, b_scr, uc_scr, hs_scr) = refs
    else:
        (u_ref, up_ref, un_ref, cw_ref, cb_ref, wg_ref, bg_ref, lam_ref,
         o_ref, h_scr, a_scr, b_scr, uc_scr, hs_scr) = refs
    nh = wg_ref.shape[0]
    rows = SUB * nb
    trows = TS * nb
    j = pl.program_id(0)

    @pl.when(j == 0)
    def _():
        h_scr[...] = jnp.zeros_like(h_scr)

    blk = order(j)
    n_steps = pl.num_programs(0)
    is_ctx = blk >= n_lat_steps
    keep_prev = jnp.where(is_ctx & (blk > n_lat_steps), 1.0, 0.0)
    keep_next = jnp.where(is_ctx & (blk < n_steps - 1), 1.0, 0.0)
    lat = jnp.where(is_ctx, 0.0, 1.0)
    halo = 2 * nb
    ext = jnp.concatenate([up_ref[...].astype(F32) * keep_prev, u_ref[...].astype(F32),
                           un_ref[...].astype(F32)[0:nb] * keep_next], axis=0)
    uc = cb_ref[...]
    for k, s in CONV_TAPS:
        lo = halo + s * nb
        uc = uc + cw_ref[k:k + 1, :] * ext[lo:lo + trows, :]
    uc_scr[...] = uc
    for p in range(GRID_W, TS, GRID_W):
        at = lambda q: ext[halo + q * nb:halo + (q + 1) * nb, :]
        rows_of = lambda q: pl.ds(q * nb, nb)
        uc_scr[rows_of(p), :] -= lat * (cw_ref[0:1, :] * at(p - 2) + cw_ref[1:2, :] * at(p - 1))
        uc_scr[rows_of(p + 1), :] -= lat * (cw_ref[0:1, :] * at(p - 1))
        uc_scr[rows_of(p - 1), :] -= lat * (cw_ref[3:4, :] * at(p))

    lam = lam_ref[...]
    decay = (-LRU_C) * (jnp.maximum(-lam, 0.0) + jnp.log1p(jnp.exp(-jnp.abs(lam))))

    subs = range(TS // SUB)
    for sub in (reversed(subs) if reverse else subs):
        r0 = sub * rows
        for g in range(nh):
            ug = uc_scr[r0:r0 + rows, g * LANES:(g + 1) * LANES]
            gates = _dot(ug.astype(BF16), wg_ref[g]) + bg_ref[g]
            log_a = decay[g] * _sigmoid(gates[:, :LANES])
            a = jnp.exp(log_a)
            z = -jnp.tanh(log_a) * (a * a + 1.0)
            mult = jnp.where(z > 0.0, z * lax.rsqrt(z), 0.0)
            a_scr[g] = a
            b_scr[g] = mult * (_sigmoid(gates[:, LANES:]) * ug)

        def step(k, hs):
            t = (SUB - 1 - k) if reverse else k
            off = pl.multiple_of(t * nb, nb)
            new = []
            for g in range(nh):
                hg = a_scr[g, pl.ds(off, nb), :] * hs[g] + b_scr[g, pl.ds(off, nb), :]
                hs_scr[pl.ds(r0 + off, nb), g * LANES:(g + 1) * LANES] = hg
                new.append(hg)
            return tuple(new)

        hs = lax.fori_loop(0, SUB, step, tuple(h_scr[g] for g in range(nh)), unroll=8)
        for g in range(nh):
            h_scr[g] = hs[g]

    if merge:
        o_ref[...] = ((hs_scr[...] + hb_ref[...].astype(F32)) * gl_ref[...].astype(F32)).astype(BF16)
    else:
        o_ref[...] = hs_scr[...].astype(BF16)


def _scan(u, conv_w, conv_b, wg, bg, lam, *, nb, n_lat_steps, ctx_len, reverse, hb=None, gl=None):
    rows_all, r = u.shape
    nh = wg.shape[0]
    trows = TS * nb
    n = rows_all // trows
    n_ctx = n - n_lat_steps
    merge = hb is not None
    if reverse:
        order = lambda j: jnp.where(j < n_ctx, n - 1 - j, n_lat_steps - 1 - (j - n_ctx))
    else:
        order = lambda j: jnp.where(j < n_ctx, n_lat_steps + j, j - n_ctx)
    hpb = trows // PACKED_ROWS
    tile = pl.BlockSpec((trows, r), lambda j: (order(j), 0))
    in_specs = [
        tile,
        pl.BlockSpec((PACKED_ROWS, r), lambda j: (jnp.maximum(order(j) * hpb - 1, 0), 0)),
        pl.BlockSpec((PACKED_ROWS, r), lambda j: (jnp.minimum((order(j) + 1) * hpb, n * hpb - 1), 0)),
        _full(conv_w.shape, 1), _full((1, r), 1), _full(wg.shape, 1), _full(bg.shape, 1), _full(lam.shape, 1),
    ]
    args = [u, u, u, conv_w, conv_b.reshape(1, r), wg, bg, lam]
    if merge:
        in_specs += [tile, tile]
        args += [hb, gl]
    return pl.pallas_call(
        functools.partial(_scan_body, reverse=reverse, merge=merge, nb=nb, n_lat_steps=n_lat_steps,
                          ctx_len=ctx_len, order=order),
        grid=(n,),
        in_specs=in_specs,
        out_specs=tile,
        out_shape=jax.ShapeDtypeStruct((rows_all, r), BF16),
        scratch_shapes=[
            pltpu.VMEM((nh, nb, LANES), F32),
            pltpu.VMEM((nh, SUB * nb, LANES), F32),
            pltpu.VMEM((nh, SUB * nb, LANES), F32),
            pltpu.VMEM((trows, r), F32),
            pltpu.VMEM((trows, r), F32),
        ],
        compiler_params=_params(("arbitrary",)),
        name="scan_fwd_merge" if merge else "scan_bwd",
    )(*args)


FOLD_TILE = 256


def _fold_body(a_ref, b1_ref, b2_ref, o_ref, *, f):
    m = pl.program_id(0)
    i = lax.broadcasted_iota(jnp.int32, (FOLD_TILE, 2 * FOLD_TILE), 0)
    col = lax.broadcasted_iota(jnp.int32, (FOLD_TILE, 2 * FOLD_TILE), 1)
    sel = (col == jnp.where(i == 0, FOLD_TILE, FOLD_TILE - i)) & ((i > 0) | (m > 0))
    mirrored = _dot(jnp.where(sel, 1.0, 0.0).astype(BF16), jnp.concatenate([b1_ref[...], b2_ref[...]], axis=0))
    a = a_ref[...].astype(F32)
    o_ref[:, 0:f] = (a[:, 0:f] + mirrored[:, 0:f]).astype(BF16)
    o_ref[:, f:2 * f] = (a[:, f:2 * f] - mirrored[:, f:2 * f]).astype(BF16)


def _fold(pq, *, n, nb, f):
    nt = n // FOLD_TILE
    col = lambda idx: pl.BlockSpec((FOLD_TILE, 2 * f), lambda m, b: (idx(m), b))
    return pl.pallas_call(
        functools.partial(_fold_body, f=f),
        grid=(nt // 2, nb),
        in_specs=[col(lambda m: m), col(lambda m: nt - 1 - m), col(lambda m: jnp.minimum(nt - m, nt - 1))],
        out_specs=col(lambda m: m),
        out_shape=jax.ShapeDtypeStruct((n // 2, nb * 2 * f), BF16),
        compiler_params=_params(("arbitrary", "arbitrary")),
        name="dft_fold",
    )(pq, pq, pq)


def _dft_body(ca_ref, sa_ref, cb_ref, sb_ref, p_ref, q_ref, *rest, kdim, folded, scale):
    o_ref, c_scr, s_scr = rest[-3:]

    @pl.when(pl.program_id(1) == 0)
    def _():
        cb = cb_ref[...]
        sb = sb_ref[...]
        for t1 in range(kdim // LANES):
            ca = ca_ref[:, t1:t1 + 1]
            sa = sa_ref[:, t1:t1 + 1]
            c_scr[:, t1 * LANES:(t1 + 1) * LANES] = (ca * cb - sa * sb).astype(BF16)
            s_scr[:, t1 * LANES:(t1 + 1) * LANES] = (sa * cb + ca * sb).astype(BF16)

    y = _dot(c_scr[...], p_ref[...]) - _dot(s_scr[...], q_ref[...])
    if folded:
        mid_ref = rest[0]
        tmk = o_ref.shape[0]
        k = pl.program_id(0) * tmk + lax.broadcasted_iota(jnp.int32, (tmk, 1), 0)
        sign = (1 - 2 * (k & 1)).astype(F32)
        y = y + sign * (scale * mid_ref[...].astype(F32)[0:1, :])
    o_ref[...] = y.astype(BF16)


def _dft_tables(n):
    k = jnp.arange(n, dtype=jnp.int32)[:, None]
    t1 = jnp.arange(LANES, dtype=jnp.int32)[None, :]
    ang_a = ((k * ((t1 * LANES) % n)) % n).astype(F32) * (2.0 * math.pi / n)
    ang_b = ((k * t1) % n).astype(F32) * (2.0 * math.pi / n)
    scale = 1.0 / math.sqrt(n)
    return jnp.cos(ang_a), jnp.sin(ang_a), jnp.cos(ang_b) * scale, jnp.sin(ang_b) * scale


def _dft(pq, *, n, nb, f, row0, prev=None, folded=None):
    s = pq.shape[0]
    tmk = min(n, 512)
    kdim = n // 2 if folded is not None else n
    tab = pl.BlockSpec((tmk, LANES), lambda m, b: (m, 0))
    if folded is not None:
        assert row0 == 0
        src, rb = folded, 0
    else:
        src, rb = pq, row0 // n
    in_specs = [
        tab, tab, tab, tab,
        pl.BlockSpec((kdim, f), lambda m, b: (rb, 2 * b)),
        pl.BlockSpec((kdim, f), lambda m, b: (rb, 2 * b + 1)),
    ]
    args = list(_dft_tables(n)) + [src, src]
    if folded is not None:
        in_specs.append(pl.BlockSpec((PACKED_ROWS, f), lambda m, b: (kdim // PACKED_ROWS, 2 * b)))
        args.append(pq)
    aliases = {}
    if prev is not None:
        in_specs.append(pl.BlockSpec(memory_space=pl.ANY))
        args.append(prev)
        aliases = {len(args) - 1: 0}
    return pl.pallas_call(
        functools.partial(_dft_body, kdim=kdim, folded=folded is not None, scale=1.0 / math.sqrt(n)),
        grid=(n // tmk, nb),
        in_specs=in_specs,
        out_specs=pl.BlockSpec((tmk, f), lambda m, b: (row0 // tmk + m, b)),
        out_shape=jax.ShapeDtypeStruct((s, nb * f), BF16),
        scratch_shapes=[pltpu.VMEM((tmk, kdim), BF16), pltpu.VMEM((tmk, kdim), BF16)],
        input_output_aliases=aliases,
        compiler_params=_params(("arbitrary", "arbitrary")),
        name="pos_dft",
    )(*args)


def _top2(vals):
    def first_max(vs):
        m = functools.reduce(jnp.maximum, vs)
        idx = jnp.full(m.shape, len(vs) - 1, jnp.int32)
        for k in range(len(vs) - 2, -1, -1):
            idx = jnp.where(vs[k] == m, k, idx)
        return m, idx
    m1, i1 = first_max(vals)
    rest = [jnp.where(i1 == k, -jnp.inf, v) for k, v in enumerate(vals)]
    m2, i2 = first_max(rest)
    return m1, i1, m2, i2


def _pick(idx, vals):
    out = vals[-1]
    for k in range(len(vals) - 2, -1, -1):
        out = jnp.where(idx == k, vals[k], out)
    return out


def _route_window(h2, rw, rb, n_exp):
    aff = jax.nn.sigmoid(_dot3(h2, rw).T[0:n_exp, :])
    sel = aff + rb
    per = n_exp // N_EXPERT_GROUPS
    srow = [sel[e:e + 1, :] for e in range(n_exp)]
    arow = [aff[e:e + 1, :] for e in range(n_exp)]
    scores = []
    for gi in range(N_EXPERT_GROUPS):
        m1, _, m2, _ = _top2(srow[gi * per:(gi + 1) * per])
        scores.append(m1 + m2)
    best = functools.reduce(jnp.maximum, scores)
    grp = jnp.full(best.shape, N_EXPERT_GROUPS - 1, jnp.int32)
    for gi in range(N_EXPERT_GROUPS - 2, -1, -1):
        grp = jnp.where(scores[gi] == best, gi, grp)
    v = [_pick(grp, [srow[gi * per + k] for gi in range(N_EXPERT_GROUPS)]) for k in range(per)]
    a = [_pick(grp, [arow[gi * per + k] for gi in range(N_EXPERT_GROUPS)]) for k in range(per)]
    _, i1, _, i2 = _top2(v)
    w1 = _pick(i1, a)
    w2 = _pick(i2, a)
    den = w1 + w2
    e1 = grp * per + i1
    e2 = grp * per + i2

    erow = lax.broadcasted_iota(jnp.int32, (n_exp, TM), 0)
    hit1 = erow == e1
    hit2 = erow == e2
    onehot = jnp.where(hit1 | hit2, 1.0, 0.0)
    cnt = jnp.sum(onehot, axis=1, keepdims=True)
    before = (lax.broadcasted_iota(jnp.int32, (TM, TM), 0)
              < lax.broadcasted_iota(jnp.int32, (TM, TM), 1)).astype(BF16)
    rank = _dot(onehot.astype(BF16), before)
    padded = ((cnt.astype(jnp.int32) + (SUBLANES - 1)) & (-SUBLANES)).astype(F32)
    lower = (lax.broadcasted_iota(jnp.int32, (n_exp, n_exp), 1)
             < lax.broadcasted_iota(jnp.int32, (n_exp, n_exp), 0)).astype(BF16)
    starts = _dot(lower, jnp.broadcast_to(padded, (n_exp, LANES)).astype(BF16))
    pos = starts[:, 0:1] + rank
    r1 = jnp.sum(jnp.where(hit1, pos, 0.0), axis=0, keepdims=True)
    r2 = jnp.sum(jnp.where(hit2, pos, 0.0), axis=0, keepdims=True)
    route = jnp.concatenate([r1, r2, w1 / den, w2 / den, jnp.zeros((LANES - 4, TM), F32)], axis=0)
    return route, jnp.broadcast_to(cnt, (n_exp, LANES))


def _mix_body(hg_ref, yf_ref, sa_ref, sb_ref, x_ref, mod_ref, wpr_ref, wpf_ref, wo_ref,
              gffn_ref, rw_ref, rb_ref, xo_ref, hx_ref, rr_ref, rc_ref, cnt_ref, t_scr, *, d, n_exp, nb):
    mod = mod_ref[0]
    f = wpf_ref.shape[0]
    y_r = _dot(hg_ref[...], wpr_ref[...])
    yf = _to_time_major([yf_ref[:, b * f:(b + 1) * f] for b in range(nb)], t_scr, nb)
    y_f = _dot(yf.astype(BF16), wpf_ref[...])
    merged = sa_ref[...].astype(F32) * y_r + sb_ref[...].astype(F32) * y_f
    out = _dot(merged.astype(BF16), wo_ref[...])
    xn = x_ref[...] + _per_batch(out, lambda v: v * mod[:, 2 * d:3 * d][None])
    xo_ref[...] = xn
    h2 = _rms(xn, gffn_ref[...])
    h2 = _per_batch(h2, lambda v: v * (1.0 + mod[:, 4 * d:5 * d])[None] + mod[:, 3 * d:4 * d][None])
    hx_ref[...] = h2.astype(BF16)
    for k in range(TR // TM):
        route, cnt = _route_window(h2[k * TM:(k + 1) * TM, :], rw_ref[...], rb_ref[...], n_exp)
        rr_ref[k] = route[0:SUBLANES, :]
        rc_ref[k * TM:(k + 1) * TM, :] = route.T
        cnt_ref[k] = cnt


def _mix(hg, yf, sa, sb, xa, mod, wpr, wpf, wo, gffn, rw, rb, *, n_rows, n_lat_rows, nb):
    rows_all, d = xa.shape
    f = wpf.shape[0]
    n_exp = rb.shape[0]
    n_lat = n_lat_rows // TR
    wpt = TR // TM
    n_win = n_rows // TM
    row = lambda width: pl.BlockSpec((TR, width), lambda i: (i, 0))
    return pl.pallas_call(
        functools.partial(_mix_body, d=d, n_exp=n_exp, nb=nb),
        grid=(n_rows // TR,),
        in_specs=[
            row(d), pl.BlockSpec((TR // nb, nb * f), lambda i: (i, 0)), row(d), row(d), row(d),
            pl.BlockSpec((1, SUBLANES, N_MOD * d), lambda i: (jnp.where(i < n_lat, 0, 1), 0, 0)),
            _full(wpr.shape, 1), _full(wpf.shape, 1), _full(wo.shape, 1),
            _full((1, d), 1), _full(rw.shape, 1), _full((n_exp, 1), 1),
        ],
        scratch_shapes=[pltpu.VMEM((f // LANES, TR, LANES), F32)],
        out_specs=[
            row(d), row(d),
            pl.BlockSpec((wpt, SUBLANES, TM), lambda i: (i, 0, 0)),
            row(LANES),
            pl.BlockSpec((wpt, n_exp, LANES), lambda i: (i, 0, 0)),
        ],
        out_shape=[
            jax.ShapeDtypeStruct((rows_all, d), F32),
            jax.ShapeDtypeStruct((rows_all, d), BF16),
            jax.ShapeDtypeStruct((n_win, SUBLANES, TM), F32),
            jax.ShapeDtypeStruct((rows_all, LANES), F32),
            jax.ShapeDtypeStruct((n_win, n_exp, LANES), F32),
        ],
        compiler_params=_params(("arbitrary",)),
        name="mix_out",
    )(hg, yf, sa, sb, xa, mod, wpr, wpf, wo, gffn.reshape(1, d), rw, rb.reshape(n_exp, 1))


def _local_rows(n_exp):
    return -(-(2 * TM + n_exp * (SUBLANES - 1)) // LANES) * LANES


def _pack_bf16_pairs(x):
    half = x.shape[1] // 2
    bits = lax.bitcast_convert_type(x, jnp.uint32)
    return (bits[:, :half] & jnp.uint32(0xFFFF0000)) | lax.shift_right_logical(bits[:, half:], jnp.uint32(16))


def _unpack_bf16_pairs(u):
    hi = lax.bitcast_convert_type(u & jnp.uint32(0xFFFF0000), F32)
    lo = lax.bitcast_convert_type(lax.shift_left(u, jnp.uint32(16)), F32)
    return jnp.concatenate([hi, lo], axis=-1)


def _run_copy(local, remote, sem, lo_row, re_row, n_rows, to_remote):
    lo = local.at[pl.ds(lo_row, n_rows)]
    re = remote.at[pl.ds(re_row, n_rows)]
    return pltpu.make_async_copy(lo, re, sem) if to_remote else pltpu.make_async_copy(re, lo, sem)


def _run_copies(np_ref, goff_ref, win, *, n_exp, local, remote, sem, to_remote, wait):
    if wait:
        total = np_ref[win * n_exp]
        for e in range(1, n_exp):
            total = total + np_ref[win * n_exp + e]
        chunks = lax.shift_right_logical(total, 3)
        for bit in range((local.shape[0] // SUBLANES).bit_length()):
            @pl.when((lax.shift_right_logical(chunks, bit) & 1) == 1)
            def _():
                _run_copy(local, remote, sem, 0, 0, SUBLANES << bit, to_remote).wait()
        return
    src = 0
    for e in range(n_exp):
        n = np_ref[win * n_exp + e]
        dst = goff_ref[win * n_exp + e]

        def one(c, carry, src=src, dst=dst):
            _run_copy(local, remote, sem, pl.multiple_of(src + c * SUBLANES, SUBLANES),
                      pl.multiple_of(dst + c * SUBLANES, SUBLANES), SUBLANES, to_remote).start()
            return carry

        lax.fori_loop(0, lax.shift_right_logical(n, 3), one, 0)
        src = src + n


def _dispatch_body(np_ref, goff_ref, hx_ref, rr_ref, xs_ref, xloc, sem, *, n_exp, n_win, rows):
    w = pl.program_id(0)
    slot = lax.rem(w, 2)
    copies = functools.partial(_run_copies, np_ref, goff_ref, n_exp=n_exp, remote=xs_ref, to_remote=True)

    @pl.when(w >= 2)
    def _():
        copies(w - 2, local=xloc.at[slot], sem=sem.at[slot], wait=True)

    rr = rr_ref[0].astype(jnp.int32)
    riota = lax.broadcasted_iota(jnp.int32, (rows, TM), 0)
    p = jnp.where((riota == rr[0:1, :]) | (riota == rr[1:2, :]), 1.0, 0.0).astype(BF16)
    xloc[slot] = _pack_bf16_pairs(_dot(p, hx_ref[...]))
    copies(w, local=xloc.at[slot], sem=sem.at[slot], wait=False)

    @pl.when(w == n_win - 1)
    def _():
        copies(w - 1, local=xloc.at[1 - slot], sem=sem.at[1 - slot], wait=True)
        copies(w, local=xloc.at[slot], sem=sem.at[slot], wait=True)


def _dispatch(npad, goff, hx, rr, *, n_exp, n_win, rows_max):
    d = hx.shape[1]
    rows = _local_rows(n_exp)
    assert n_win >= 2
    grid_spec = pltpu.PrefetchScalarGridSpec(
        num_scalar_prefetch=2,
        grid=(n_win,),
        in_specs=[
            pl.BlockSpec((TM, d), lambda w, a, g: (w, 0)),
            pl.BlockSpec((1, SUBLANES, TM), lambda w, a, g: (w, 0, 0)),
        ],
        out_specs=pl.BlockSpec(memory_space=pl.ANY),
        scratch_shapes=[pltpu.VMEM((2, rows, d // 2), jnp.uint32), pltpu.SemaphoreType.DMA((2,))],
    )
    return pl.pallas_call(
        functools.partial(_dispatch_body, n_exp=n_exp, n_win=n_win, rows=rows),
        grid_spec=grid_spec,
        out_shape=jax.ShapeDtypeStruct((rows_max, d // 2), jnp.uint32),
        compiler_params=_params(("arbitrary",)),
        name="moe_dispatch",
    )(npad, goff, hx, rr)


def _gmm_body(te_ref, nu_ref, xs_ref, w1_ref, w3_ref, w2_ref, ys_ref, w1b, w3b, w2b):
    t = pl.program_id(0)

    @pl.when((t == 0) | (te_ref[t] != te_ref[jnp.maximum(t - 1, 0)]))
    def _():
        w1b[...] = w1_ref[0, 0].astype(BF16)
        w3b[...] = w3_ref[0, 0].astype(BF16)
        w2b[...] = w2_ref[0, 0].astype(BF16)

    @pl.when(t < nu_ref[0])
    def _():
        x = _unpack_bf16_pairs(xs_ref[...]).astype(BF16)
        a = _dot(x, w1b[...])
        h = (a * jax.nn.sigmoid(a)) * _dot(x, w3b[...])
        y = _dot(h.astype(BF16), w2b[...])
        ys_ref[...] = _pack_bf16_pairs(y.astype(BF16).astype(F32))


def _gmm(tile_exp, n_used, xs, w1, w3, w2, *, layer):
    rows_max = xs.shape[0]
    d, de = w1.shape[2], w1.shape[3]
    row_tile = lambda t, te, nu: (jnp.minimum(t, nu[0] - 1), 0)
    wspec = lambda w: pl.BlockSpec((1, 1) + w.shape[2:], lambda t, te, nu: (layer, te[t], 0, 0))
    grid_spec = pltpu.PrefetchScalarGridSpec(
        num_scalar_prefetch=2,
        grid=(rows_max // MOE_TM,),
        in_specs=[pl.BlockSpec((MOE_TM, d // 2), row_tile), wspec(w1), wspec(w3), wspec(w2)],
        out_specs=pl.BlockSpec((MOE_TM, d // 2), row_tile),
        scratch_shapes=[pltpu.VMEM((d, de), BF16), pltpu.VMEM((d, de), BF16), pltpu.VMEM((de, d), BF16)],
    )
    return pl.pallas_call(
        _gmm_body,
        grid_spec=grid_spec,
        out_shape=jax.ShapeDtypeStruct((rows_max, d // 2), jnp.uint32),
        compiler_params=_params(("arbitrary",)),
        name="moe_experts",
    )(tile_exp, n_used, xs, w1, w3, w2)


def _combine_body(*refs, d, n_exp, n_win, rows, nb, final):
    if final:
        np_ref, goff_ref, ys_ref, rc_ref, xr_ref, mod_ref, gfin_ref, o_ref, yloc, sem, t_scr = refs
    else:
        np_ref, goff_ref, ys_ref, rc_ref, xr_ref, mod_ref, o_ref, yloc, sem = refs
    w = pl.program_id(0)
    slot = lax.rem(w, 2)
    copies = functools.partial(_run_copies, np_ref, goff_ref, n_exp=n_exp, remote=ys_ref, to_remote=False)

    @pl.when(w == 0)
    def _():
        copies(w, local=yloc.at[slot], sem=sem.at[slot], wait=False)

    @pl.when(w + 1 < n_win)
    def _():
        copies(w + 1, local=yloc.at[1 - slot], sem=sem.at[1 - slot], wait=False)

    copies(w, local=yloc.at[slot], sem=sem.at[slot], wait=True)

    total = np_ref[w * n_exp]
    for e in range(1, n_exp):
        total = total + np_ref[w * n_exp + e]
    keep = lax.broadcasted_iota(jnp.int32, (rows, 1), 0) < total
    y = _unpack_bf16_pairs(jnp.where(keep, yloc[slot], jnp.uint32(0))).astype(BF16)
    rc = rc_ref[...]
    ci = lax.broadcasted_iota(jnp.int32, (TM, rows), 1)
    pw = (jnp.where(ci == rc[:, 0:1].astype(jnp.int32), rc[:, 2:3], 0.0)
          + jnp.where(ci == rc[:, 1:2].astype(jnp.int32), rc[:, 3:4], 0.0))
    moe = _dot(pw.astype(BF16), y)
    g2 = mod_ref[0][:, 5 * d:6 * d]
    xn = xr_ref[...] + _per_batch(moe, lambda v: v * g2[None])
    if final:
        xn = _rms(xn, gfin_ref[...])
        for g in range(d // LANES):
            t_scr[g] = xn[:, g * LANES:(g + 1) * LANES]
        for b in range(nb):
            o_ref[b] = jnp.concatenate(
                [t_scr[g, pl.ds(b, TM // nb, stride=nb), :] for g in range(d // LANES)], axis=-1)
    else:
        o_ref[...] = xn


def _combine(npad, goff, ys, rc, xres, mod, *, n_exp, n_win, n_lat_rows, nb, gfin=None, seq_out=None):
    d = xres.shape[1]
    rows = _local_rows(n_exp)
    final = gfin is not None
    n_lat = n_lat_rows // TM
    row = lambda width: pl.BlockSpec((TM, width), lambda w, a, g: (w, 0))
    in_specs = [pl.BlockSpec(memory_space=pl.ANY), row(LANES), row(d),
                pl.BlockSpec((1, SUBLANES, N_MOD * d), lambda w, a, g: (jnp.where(w < n_lat, 0, 1), 0, 0))]
    args = [npad, goff, ys, rc, xres, mod]
    scratch = [pltpu.VMEM((2, rows, d // 2), jnp.uint32), pltpu.SemaphoreType.DMA((2,))]
    if final:
        in_specs.append(pl.BlockSpec((1, d), lambda w, a, g: (0, 0)))
        args.append(gfin.reshape(1, d))
        out_shape = jax.ShapeDtypeStruct((nb, seq_out, d), F32)
        out_specs = pl.BlockSpec((nb, TM // nb, d), lambda w, a, g: (0, w, 0))
        scratch.append(pltpu.VMEM((d // LANES, TM, LANES), F32))
        aliases = {}
    else:
        out_shape = jax.ShapeDtypeStruct(xres.shape, F32)
        out_specs = row(d)
        aliases = {4: 0}
    grid_spec = pltpu.PrefetchScalarGridSpec(
        num_scalar_prefetch=2, grid=(n_win,), in_specs=in_specs, out_specs=out_specs, scratch_shapes=scratch)
    return pl.pallas_call(
        functools.partial(_combine_body, d=d, n_exp=n_exp, n_win=n_win, rows=rows, nb=nb, final=final),
        grid_spec=grid_spec,
        out_shape=out_shape,
        input_output_aliases=aliases,
        compiler_params=_params(("arbitrary",)),
        name="moe_combine",
    )(*args)


def _route_tables(cnt, *, n_exp, n_tiles):
    cnt = cnt[..., 0].astype(jnp.int32)
    npad = (cnt + (SUBLANES - 1)) & (-SUBLANES)
    seg = -(-jnp.sum(npad, axis=0) // MOE_TM) * MOE_TM
    ends = jnp.cumsum(seg)
    goff = (ends - seg)[None, :] + jnp.cumsum(npad, axis=0) - npad
    tile_ends = ends // MOE_TM
    n_used = tile_ends[-1]
    tile = jnp.minimum(jnp.arange(n_tiles, dtype=jnp.int32), n_used - 1)
    tile_exp = jnp.sum(tile[:, None] >= tile_ends[None, :], axis=1).astype(jnp.int32)
    return npad.reshape(-1), goff.reshape(-1).astype(jnp.int32), tile_exp, n_used.reshape(1).astype(jnp.int32)


def _channel_dft_tables(n):
    k = jnp.arange(n, dtype=jnp.int32)
    ang = ((k[:, None] * k[None, :]) % n).astype(F32) * (2.0 * math.pi / n)
    scale = 1.0 / math.sqrt(n)
    return jnp.cos(ang) * scale, jnp.sin(ang) * scale


def kernel(x, c, ctx, c_ctx, ada_w, ada_b, norm_mix_g, w_in, conv_w, conv_b, lru_w_a, lru_b_a, lru_w_x, lru_b_x, lru_lambda, w_proj_rnn, w_proj_fourier, w_out, norm_ffn_g, router_w, router_b, moe_w1, moe_w3, moe_w2, final_norm_g):
    nb, l, d = x.shape
    lc = ctx.shape[1]
    depth = ada_w.shape[0]
    r = conv_w.shape[2]
    f = w_proj_fourier.shape[1]
    nh, blk = lru_w_a.shape[2], lru_w_a.shape[3]
    assert nb == SUBLANES and blk == LANES and nb < MOD_ROWS
    assert l % TS == 0 and lc % TS == 0 and TS % GRID_W == 0 and (l * nb) % TR == 0 and (lc * nb) % TR == 0
    assert GRID_W & (GRID_W - 1) == 0 and lc & (lc - 1) == 0 and l % lc == 0 and lc <= 512
    s = l + lc
    n_lat_rows = l * nb

    cc = jnp.zeros((MOD_ROWS, d), F32).at[:nb].set(c).at[nb].set(c_ctx)
    mod = _ada(cc, ada_w, ada_b)
    mod = jnp.stack([mod[:, :nb], jnp.broadcast_to(mod[:, nb:nb + 1], (depth, nb, N_MOD * d))], axis=1)

    cch, sch = _channel_dft_tables(f // FOURIER_GROUPS)
    eye = jnp.eye(FOURIER_GROUPS, dtype=F32)
    fcs = jnp.concatenate([jnp.kron(eye, cch), jnp.kron(eye, sch)], axis=1).astype(BF16)
    n_exp = router_w.shape[1]
    rw = jnp.pad(router_w, ((0, 0), (0, LANES - n_exp)))

    out = xa = None
    for li in range(depth):
        last = li == depth - 1
        inproj = functools.partial(_inproj, mod=mod[li], g=norm_mix_g[li], w_in=w_in[li].astype(BF16), fcs=fcs,
                                   n_lat_rows=n_lat_rows, r=r, nb=nb)
        if li == 0:
            u, gl, sa, sb, pq, xa = inproj(x, ctx=ctx)
        else:
            u, gl, sa, sb, pq = inproj(xa)
        wg = [jnp.concatenate([lru_w_a[li, k], lru_w_x[li, k]], axis=-1).astype(BF16) for k in range(2)]
        bg = [jnp.concatenate([lru_b_a[li, k], lru_b_x[li, k]], axis=-1).reshape(nh, 1, 2 * blk) for k in range(2)]
        lam = [lru_lambda[li, k].reshape(nh, 1, blk) for k in range(2)]
        scan = functools.partial(_scan, u, conv_w[li], conv_b[li], nb=nb, n_lat_steps=l // TS, ctx_len=lc)
        hb = scan(wg[1], bg[1], lam[1], reverse=True)
        hg = scan(wg[0], bg[0], lam[0], reverse=False, hb=hb, gl=gl)
        yf = _dft(pq, n=l, nb=nb, f=f, row0=0, folded=_fold(pq, n=l, nb=nb, f=f))
        if not last:
            yf = _dft(pq, n=lc, nb=nb, f=f, row0=l, prev=yf)
        n_rows = n_lat_rows if last else s * nb
        x1, hx, rr, rc, cnt = _mix(hg, yf, sa, sb, xa, mod[li], w_proj_rnn[li].astype(BF16),
                                   w_proj_fourier[li].astype(BF16), w_out[li].astype(BF16), norm_ffn_g[li],
                                   rw, router_b, n_rows=n_rows, n_lat_rows=n_lat_rows, nb=nb)
        n_win = n_rows // TM
        rows_max = -(-(n_win * (2 * TM + n_exp * (SUBLANES - 1)) + n_exp * MOE_TM) // MOE_TM) * MOE_TM
        npad, goff, tile_exp, n_used = _route_tables(cnt, n_exp=n_exp, n_tiles=rows_max // MOE_TM)
        xs = _dispatch(npad, goff, hx, rr, n_exp=n_exp, n_win=n_win, rows_max=rows_max)
        ys = _gmm(tile_exp, n_used, xs, moe_w1, moe_w3, moe_w2, layer=li)
        comb = functools.partial(_combine, npad, goff, ys, rc, x1, mod[li], n_exp=n_exp, n_win=n_win,
                                 n_lat_rows=n_lat_rows, nb=nb)
        if last:
            out = comb(gfin=final_norm_g, seq_out=l)
        else:
            xa = comb()
    return out
```

```python
import functools
import math

import jax
import jax.numpy as jnp
from jax import lax
from jax.experimental import pallas as pl
from jax.experimental.pallas import tpu as pltpu

F32 = jnp.float32
BF16 = jnp.bfloat16

RMS_EPS = 1e-6
LRU_C = 8.0
GRID_W = 64
CONV_TAPS = ((0, -2), (1, -1), (2, 0), (3, 1))
N_MOD = 6
FOURIER_GROUPS = 4
N_EXPERT_GROUPS = 4
LANES = 128
SUBLANES = 8
PACKED_ROWS = 16
TR = 512
TM = 256
TS = 128
SUB = 64
MOE_TM = 512
MOD_ROWS = 16
VMEM_LIMIT = 56 * 1024 * 1024


def _dot(a, b):
    return jnp.dot(a, b, preferred_element_type=F32)


def _split(a):
    hi = a.astype(BF16)
    lo = (a - hi.astype(F32)).astype(BF16)
    return hi, lo


def _dot3(a, b):
    ah, al = _split(a)
    bh, bl = _split(b)
    return _dot(ah, bh) + _dot(al, bh) + _dot(ah, bl)


def _dot3_nt(a, b):
    dn = (((1,), (1,)), ((), ()))
    ah, al = _split(a)
    bh, bl = _split(b)
    f = lambda p, q: lax.dot_general(p, q, dn, preferred_element_type=F32)
    return f(ah, bh) + f(al, bh) + f(ah, bl)


def _gelu_tanh(x):
    return 0.5 * x * (1.0 + jnp.tanh(math.sqrt(2.0 / math.pi) * (x + 0.044715 * (x * x * x))))


def _sigmoid(x):
    return 0.5 * jnp.tanh(0.5 * x) + 0.5


def _rms(x, g):
    return x * lax.rsqrt(jnp.mean(x * x, axis=-1, keepdims=True) + RMS_EPS) * g


def _per_batch(x, fn):
    rows, d = x.shape
    return fn(x.reshape(rows // SUBLANES, SUBLANES, d)).reshape(rows, d)


def _params(sem, vmem=VMEM_LIMIT):
    return pltpu.CompilerParams(dimension_semantics=sem, vmem_limit_bytes=vmem)


def _full(shape, n_idx):
    zeros = (0,) * len(shape)
    return pl.BlockSpec(shape, lambda *_: zeros)


def _ada_body(cc_ref, w_ref, b_ref, o_ref):
    cc = cc_ref[...]
    s = cc * jax.nn.sigmoid(cc)
    o_ref[0] = _dot3(s, w_ref[0]) + b_ref[0]


def _ada(cc, ada_w, ada_b):
    depth, d, n = ada_w.shape
    tn = 1024
    return pl.pallas_call(
        _ada_body,
        grid=(depth, n // tn),
        in_specs=[
            pl.BlockSpec((MOD_ROWS, d), lambda l, j: (0, 0)),
            pl.BlockSpec((1, d, tn), lambda l, j: (l, 0, j)),
            pl.BlockSpec((1, 1, tn), lambda l, j: (l, 0, j)),
        ],
        out_specs=pl.BlockSpec((1, MOD_ROWS, tn), lambda l, j: (l, 0, j)),
        out_shape=jax.ShapeDtypeStruct((depth, MOD_ROWS, n), F32),
        compiler_params=_params(("arbitrary", "arbitrary")),
        name="ada_mod",
    )(cc, ada_w, ada_b.reshape(depth, 1, n))


def _to_time_major(blocks, scr, nb):
    npos, width = blocks[0].shape
    for b in range(nb):
        for g in range(width // LANES):
            scr[g, pl.ds(b, npos, stride=nb), :] = blocks[b][:, g * LANES:(g + 1) * LANES].astype(F32)
    return jnp.concatenate([scr[g] for g in range(width // LANES)], axis=-1)


def _to_batch_major(v, scr, nb):
    rows, width = v.shape
    for g in range(width // LANES):
        scr[g] = v[:, g * LANES:(g + 1) * LANES]
    return [jnp.concatenate([scr[g, pl.ds(b, rows // nb, stride=nb), :] for g in range(width // LANES)], axis=-1)
            for b in range(nb)]


def _inproj_body(*refs, d, r, f, nb, n_lat, first):
    if first:
        (x_ref, c_ref, mod_ref, g_ref, w_ref, fcs_ref,
         u_ref, gl_ref, sa_ref, sb_ref, pq_ref, xa_ref, t_scr) = refs

        @pl.when(pl.program_id(0) < n_lat)
        def _():
            xa_ref[...] = _to_time_major([x_ref[b] for b in range(nb)], t_scr, nb)

        @pl.when(pl.program_id(0) >= n_lat)
        def _():
            xa_ref[...] = _to_time_major([c_ref[b] for b in range(nb)], t_scr, nb)

        x = xa_ref[...]
    else:
        x_ref, mod_ref, g_ref, w_ref, fcs_ref, u_ref, gl_ref, sa_ref, sb_ref, pq_ref, t_scr = refs
        x = x_ref[...]
    mod = mod_ref[0]
    h = _rms(x, g_ref[...])
    h = _per_batch(h, lambda v: v * (1.0 + mod[:, d:2 * d])[None] + mod[:, 0:d][None]).astype(BF16)
    u_ref[...] = _dot(h, w_ref[:, 0:r]).astype(BF16)
    gl_ref[...] = _gelu_tanh(_dot(h, w_ref[:, r:2 * r])).astype(BF16)
    u4 = _dot(h, w_ref[:, 2 * r:2 * r + f]).astype(BF16)
    for b, blk in enumerate(_to_batch_major(_dot(u4, fcs_ref[...]), t_scr, nb)):
        pq_ref[:, b * 2 * f:(b + 1) * 2 * f] = blk.astype(BF16)
    s3 = 2 * r + f
    sa_ref[...] = jax.nn.sigmoid(_dot(h, w_ref[:, s3:s3 + d])).astype(BF16)
    sb_ref[...] = jax.nn.sigmoid(_dot(h, w_ref[:, s3 + d:s3 + 2 * d])).astype(BF16)


def _inproj(x, mod, g, w_in, fcs, *, n_lat_rows, r, nb, ctx=None):
    first = ctx is not None
    d = x.shape[-1]
    rows = (x.shape[1] + ctx.shape[1]) * nb if first else x.shape[0]
    f = fcs.shape[0]
    n_lat = n_lat_rows // TR
    npos = TR // nb
    row = lambda width: pl.BlockSpec((TR, width), lambda i: (i, 0))
    common = [
        pl.BlockSpec((1, SUBLANES, N_MOD * d), lambda i: (jnp.where(i < n_lat, 0, 1), 0, 0)),
        _full((1, d), 1), _full(w_in.shape, 1), _full(fcs.shape, 1),
    ]
    out_specs = [row(r), row(r), row(d), row(d), pl.BlockSpec((npos, nb * 2 * f), lambda i: (i, 0))]
    out_shape = [
        jax.ShapeDtypeStruct((rows, r), BF16),
        jax.ShapeDtypeStruct((rows, r), BF16),
        jax.ShapeDtypeStruct((rows, d), BF16),
        jax.ShapeDtypeStruct((rows, d), BF16),
        jax.ShapeDtypeStruct((rows // nb, nb * 2 * f), BF16),
    ]
    if first:
        in_specs = [
            pl.BlockSpec((nb, npos, d), lambda i: (0, jnp.minimum(i, n_lat - 1), 0)),
            pl.BlockSpec((nb, npos, d), lambda i: (0, jnp.maximum(i - n_lat, 0), 0)),
        ] + common
        args = [x, ctx]
        out_specs.append(row(d))
        out_shape.append(jax.ShapeDtypeStruct((rows, d), F32))
    else:
        in_specs = [row(d)] + common
        args = [x]
    return pl.pallas_call(
        functools.partial(_inproj_body, d=d, r=r, f=f, nb=nb, n_lat=n_lat, first=first),
        grid=(rows // TR,),
        in_specs=in_specs,
        out_specs=out_specs,
        out_shape=out_shape,
        scratch_shapes=[pltpu.VMEM((d // LANES, TR, LANES), F32)],
        compiler_params=_params(("arbitrary",)),
        name="in_proj",
    )(*args, mod, g.reshape(1, d), w_in, fcs)


def _scan_body(*refs, reverse, merge, nb, n_lat_steps, ctx_len, order):
    if merge:
        (u_ref, up_ref, un_ref, cw_ref, cb_ref, wg_ref, bg_ref, lam_ref, hb_ref, gl_ref,
         o_ref, h_scr, a_scr, b_scr, uc_scr, hs_scr) = refs
    else:
        (u_ref, up_ref, un_ref, cw_ref, cb_ref, wg_ref, bg_ref, lam_ref,
         o_ref, h_scr, a_scr, b_scr, uc_scr, hs_scr) = refs
    nh = wg_ref.shape[0]
    rows = SUB * nb
    trows = TS * nb
    j = pl.program_id(0)

    @pl.when(j == 0)
    def _():
        h_scr[...] = jnp.zeros_like(h_scr)

    blk = order(j)
    n_steps = pl.num_programs(0)
    is_ctx = blk >= n_lat_steps
    keep_prev = jnp.where(is_ctx & (blk > n_lat_steps), 1.0, 0.0)
    keep_next = jnp.where(is_ctx & (blk < n_steps - 1), 1.0, 0.0)
    lat = jnp.where(is_ctx, 0.0, 1.0)
    halo = 2 * nb
    ext = jnp.concatenate([up_ref[...].astype(F32) * keep_prev, u_ref[...].astype(F32),
                           un_ref[...].astype(F32)[0:nb] * keep_next], axis=0)
    uc = cb_ref[...]
    for k, s in CONV_TAPS:
        lo = halo + s * nb
        uc = uc + cw_ref[k:k + 1, :] * ext[lo:lo + trows, :]
    uc_scr[...] = uc
    for p in range(GRID_W, TS, GRID_W):
        at = lambda q: ext[halo + q * nb:halo + (q + 1) * nb, :]
        rows_of = lambda q: pl.ds(q * nb, nb)
        uc_scr[rows_of(p), :] -= lat * (cw_ref[0:1, :] * at(p - 2) + cw_ref[1:2, :] * at(p - 1))
        uc_scr[rows_of(p + 1), :] -= lat * (cw_ref[0:1, :] * at(p - 1))
        uc_scr[rows_of(p - 1), :] -= lat * (cw_ref[3:4, :] * at(p))

    lam = lam_ref[...]
    decay = (-LRU_C) * (jnp.maximum(-lam, 0.0) + jnp.log1p(jnp.exp(-jnp.abs(lam))))

    subs = range(TS // SUB)
    for sub in (reversed(subs) if reverse else subs):
        r0 = sub * rows
        for g in range(nh):
            ug = uc_scr[r0:r0 + rows, g * LANES:(g + 1) * LANES]
            gates = _dot(ug.astype(BF16), wg_ref[g]) + bg_ref[g]
            log_a = decay[g] * _sigmoid(gates[:, :LANES])
            a = jnp.exp(log_a)
            z = -jnp.tanh(log_a) * (a * a + 1.0)
            mult = jnp.where(z > 0.0, z * lax.rsqrt(z), 0.0)
            a_scr[g] = a
            b_scr[g] = mult * (_sigmoid(gates[:, LANES:]) * ug)

        def step(k, hs):
            t = (SUB - 1 - k) if reverse else k
            off = pl.multiple_of(t * nb, nb)
            new = []
            for g in range(nh):
                hg = a_scr[g, pl.ds(off, nb), :] * hs[g] + b_scr[g, pl.ds(off, nb), :]
                hs_scr[pl.ds(r0 + off, nb), g * LANES:(g + 1) * LANES] = hg
                new.append(hg)
            return tuple(new)

        hs = lax.fori_loop(0, SUB, step, tuple(h_scr[g] for g in range(nh)), unroll=8)
        for g in range(nh):
            h_scr[g] = hs[g]

    if merge:
        o_ref[...] = ((hs_scr[...] + hb_ref[...].astype(F32)) * gl_ref[...].astype(F32)).astype(BF16)
    else:
        o_ref[...] = hs_scr[...].astype(BF16)


def _scan(u, conv_w, conv_b, wg, bg, lam, *, nb, n_lat_steps, ctx_len, reverse, hb=None, gl=None):
    rows_all, r = u.shape
    nh = wg.shape[0]
    trows = TS * nb
    n = rows_all // trows
    n_ctx = n - n_lat_steps
    merge = hb is not None
    if reverse:
        order = lambda j: jnp.where(j < n_ctx, n - 1 - j, n_lat_steps - 1 - (j - n_ctx))
    else:
        order = lambda j: jnp.where(j < n_ctx, n_lat_steps + j, j - n_ctx)
    hpb = trows // PACKED_ROWS
    tile = pl.BlockSpec((trows, r), lambda j: (order(j), 0))
    in_specs = [
        tile,
        pl.BlockSpec((PACKED_ROWS, r), lambda j: (jnp.maximum(order(j) * hpb - 1, 0), 0)),
        pl.BlockSpec((PACKED_ROWS, r), lambda j: (jnp.minimum((order(j) + 1) * hpb, n * hpb - 1), 0)),
        _full(conv_w.shape, 1), _full((1, r), 1), _full(wg.shape, 1), _full(bg.shape, 1), _full(lam.shape, 1),
    ]
    args = [u, u, u, conv_w, conv_b.reshape(1, r), wg, bg, lam]
    if merge:
        in_specs += [tile, tile]
        args += [hb, gl]
    return pl.pallas_call(
        functools.partial(_scan_body, reverse=reverse, merge=merge, nb=nb, n_lat_steps=n_lat_steps,
                          ctx_len=ctx_len, order=order),
        grid=(n,),
        in_specs=in_specs,
        out_specs=tile,
        out_shape=jax.ShapeDtypeStruct((rows_all, r), BF16),
        scratch_shapes=[
            pltpu.VMEM((nh, nb, LANES), F32),
            pltpu.VMEM((nh, SUB * nb, LANES), F32),
            pltpu.VMEM((nh, SUB * nb, LANES), F32),
            pltpu.VMEM((trows, r), F32),
            pltpu.VMEM((trows, r), F32),
        ],
        compiler_params=_params(("arbitrary",)),
        name="scan_fwd_merge" if merge else "scan_bwd",
    )(*args)


FOLD_TILE = 256


def _fold_body(a_ref, b1_ref, b2_ref, o_ref, *, f):
    m = pl.program_id(0)
    i = lax.broadcasted_iota(jnp.int32, (FOLD_TILE, 2 * FOLD_TILE), 0)
    col = lax.broadcasted_iota(jnp.int32, (FOLD_TILE, 2 * FOLD_TILE), 1)
    sel = (col == jnp.where(i == 0, FOLD_TILE, FOLD_TILE - i)) & ((i > 0) | (m > 0))
    mirrored = _dot(jnp.where(sel, 1.0, 0.0).astype(BF16), jnp.concatenate([b1_ref[...], b2_ref[...]], axis=0))
    a = a_ref[...].astype(F32)
    o_ref[:, 0:f] = (a[:, 0:f] + mirrored[:, 0:f]).astype(BF16)
    o_ref[:, f:2 * f] = (a[:, f:2 * f] - mirrored[:, f:2 * f]).astype(BF16)


def _fold(pq, *, n, nb, f):
    nt = n // FOLD_TILE
    col = lambda idx: pl.BlockSpec((FOLD_TILE, 2 * f), lambda m, b: (idx(m), b))
    return pl.pallas_call(
        functools.partial(_fold_body, f=f),
        grid=(nt // 2, nb),
        in_specs=[col(lambda m: m), col(lambda m: nt - 1 - m), col(lambda m: jnp.minimum(nt - m, nt - 1))],
        out_specs=col(lambda m: m),
        out_shape=jax.ShapeDtypeStruct((n // 2, nb * 2 * f), BF16),
        compiler_params=_params(("arbitrary", "arbitrary")),
        name="dft_fold",
    )(pq, pq, pq)


def _dft_body(ca_ref, sa_ref, cb_ref, sb_ref, p_ref, q_ref, *rest, kdim, folded, scale):
    o_ref, c_scr, s_scr = rest[-3:]

    @pl.when(pl.program_id(1) == 0)
    def _():
        cb = cb_ref[...]
        sb = sb_ref[...]
        for t1 in range(kdim // LANES):
            ca = ca_ref[:, t1:t1 + 1]
            sa = sa_ref[:, t1:t1 + 1]
            c_scr[:, t1 * LANES:(t1 + 1) * LANES] = (ca * cb - sa * sb).astype(BF16)
            s_scr[:, t1 * LANES:(t1 + 1) * LANES] = (sa * cb + ca * sb).astype(BF16)

    y = _dot(c_scr[...], p_ref[...]) - _dot(s_scr[...], q_ref[...])
    if folded:
        mid_ref = rest[0]
        tmk = o_ref.shape[0]
        k = pl.program_id(0) * tmk + lax.broadcasted_iota(jnp.int32, (tmk, 1), 0)
        sign = (1 - 2 * (k & 1)).astype(F32)
        y = y + sign * (scale * mid_ref[...].astype(F32)[0:1, :])
    o_ref[...] = y.astype(BF16)


def _dft_tables(n):
    k = jnp.arange(n, dtype=jnp.int32)[:, None]
    t1 = jnp.arange(LANES, dtype=jnp.int32)[None, :]
    ang_a = ((k * ((t1 * LANES) % n)) % n).astype(F32) * (2.0 * math.pi / n)
    ang_b = ((k * t1) % n).astype(F32) * (2.0 * math.pi / n)
    scale = 1.0 / math.sqrt(n)
    return jnp.cos(ang_a), jnp.sin(ang_a), jnp.cos(ang_b) * scale, jnp.sin(ang_b) * scale


def _dft(pq, *, n, nb, f, row0, prev=None, folded=None):
    s = pq.shape[0]
    tmk = min(n, 512)
    kdim = n // 2 if folded is not None else n
    tab = pl.BlockSpec((tmk, LANES), lambda m, b: (m, 0))
    if folded is not None:
        assert row0 == 0
        src, rb = folded, 0
    else:
        src, rb = pq, row0 // n
    in_specs = [
        tab, tab, tab, tab,
        pl.BlockSpec((kdim, f), lambda m, b: (rb, 2 * b)),
        pl.BlockSpec((kdim, f), lambda m, b: (rb, 2 * b + 1)),
    ]
    args = list(_dft_tables(n)) + [src, src]
    if folded is not None:
        in_specs.append(pl.BlockSpec((PACKED_ROWS, f), lambda m, b: (kdim // PACKED_ROWS, 2 * b)))
        args.append(pq)
    aliases = {}
    if prev is not None:
        in_specs.append(pl.BlockSpec(memory_space=pl.ANY))
        args.append(prev)
        aliases = {len(args) - 1: 0}
    return pl.pallas_call(
        functools.partial(_dft_body, kdim=kdim, folded=folded is not None, scale=1.0 / math.sqrt(n)),
        grid=(n // tmk, nb),
        in_specs=in_specs,
        out_specs=pl.BlockSpec((tmk, f), lambda m, b: (row0 // tmk + m, b)),
        out_shape=jax.ShapeDtypeStruct((s, nb * f), BF16),
        scratch_shapes=[pltpu.VMEM((tmk, kdim), BF16), pltpu.VMEM((tmk, kdim), BF16)],
        input_output_aliases=aliases,
        compiler_params=_params(("arbitrary", "arbitrary")),
        name="pos_dft",
    )(*args)


def _top2(vals):
    def first_max(vs):
        m = functools.reduce(jnp.maximum, vs)
        idx = jnp.full(m.shape, len(vs) - 1, jnp.int32)
        for k in range(len(vs) - 2, -1, -1):
            idx = jnp.where(vs[k] == m, k, idx)
        return m, idx
    m1, i1 = first_max(vals)
    rest = [jnp.where(i1 == k, -jnp.inf, v) for k, v in enumerate(vals)]
    m2, i2 = first_max(rest)
    return m1, i1, m2, i2


def _pick(idx, vals):
    out = vals[-1]
    for k in range(len(vals) - 2, -1, -1):
        out = jnp.where(idx == k, vals[k], out)
    return out


def _route_window(h2, rw, rb, n_exp):
    aff = jax.nn.sigmoid(_dot3(h2, rw).T[0:n_exp, :])
    sel = aff + rb
    per = n_exp // N_EXPERT_GROUPS
    srow = [sel[e:e + 1, :] for e in range(n_exp)]
    arow = [aff[e:e + 1, :] for e in range(n_exp)]
    scores = []
    for gi in range(N_EXPERT_GROUPS):
        m1, _, m2, _ = _top2(srow[gi * per:(gi + 1) * per])
        scores.append(m1 + m2)
    best = functools.reduce(jnp.maximum, scores)
    grp = jnp.full(best.shape, N_EXPERT_GROUPS - 1, jnp.int32)
    for gi in range(N_EXPERT_GROUPS - 2, -1, -1):
        grp = jnp.where(scores[gi] == best, gi, grp)
    v = [_pick(grp, [srow[gi * per + k] for gi in range(N_EXPERT_GROUPS)]) for k in range(per)]
    a = [_pick(grp, [arow[gi * per + k] for gi in range(N_EXPERT_GROUPS)]) for k in range(per)]
    _, i1, _, i2 = _top2(v)
    w1 = _pick(i1, a)
    w2 = _pick(i2, a)
    den = w1 + w2
    e1 = grp * per + i1
    e2 = grp * per + i2

    erow = lax.broadcasted_iota(jnp.int32, (n_exp, TM), 0)
    hit1 = erow == e1
    hit2 = erow == e2
    onehot = jnp.where(hit1 | hit2, 1.0, 0.0)
    cnt = jnp.sum(onehot, axis=1, keepdims=True)
    before = (lax.broadcasted_iota(jnp.int32, (TM, TM), 0)
              < lax.broadcasted_iota(jnp.int32, (TM, TM), 1)).astype(BF16)
    rank = _dot(onehot.astype(BF16), before)
    padded = ((cnt.astype(jnp.int32) + (SUBLANES - 1)) & (-SUBLANES)).astype(F32)
    lower = (lax.broadcasted_iota(jnp.int32, (n_exp, n_exp), 1)
             < lax.broadcasted_iota(jnp.int32, (n_exp, n_exp), 0)).astype(BF16)
    starts = _dot(lower, jnp.broadcast_to(padded, (n_exp, LANES)).astype(BF16))
    pos = starts[:, 0:1] + rank
    r1 = jnp.sum(jnp.where(hit1, pos, 0.0), axis=0, keepdims=True)
    r2 = jnp.sum(jnp.where(hit2, pos, 0.0), axis=0, keepdims=True)
    route = jnp.concatenate([r1, r2, w1 / den, w2 / den, jnp.zeros((LANES - 4, TM), F32)], axis=0)
    return route, jnp.broadcast_to(cnt, (n_exp, LANES))


def _mix_body(hg_ref, yf_ref, sa_ref, sb_ref, x_ref, mod_ref, wpr_ref, wpf_ref, wo_ref,
              gffn_ref, rw_ref, rb_ref, xo_ref, xl_ref, rc_ref, cnt_ref, t_scr, *, d, n_exp, nb):
    mod = mod_ref[0]
    f = wpf_ref.shape[0]
    y_r = _dot(hg_ref[...], wpr_ref[...])
    yf = _to_time_major([yf_ref[:, b * f:(b + 1) * f] for b in range(nb)], t_scr, nb)
    y_f = _dot(yf.astype(BF16), wpf_ref[...])
    merged = sa_ref[...].astype(F32) * y_r + sb_ref[...].astype(F32) * y_f
    out = _dot(merged.astype(BF16), wo_ref[...])
    xn = x_ref[...] + _per_batch(out, lambda v: v * mod[:, 2 * d:3 * d][None])
    xo_ref[...] = xn
    h2 = _rms(xn, gffn_ref[...])
    h2 = _per_batch(h2, lambda v: v * (1.0 + mod[:, 4 * d:5 * d])[None] + mod[:, 3 * d:4 * d][None])
    rows = _local_rows(n_exp)
    riota = lax.broadcasted_iota(jnp.int32, (rows, TM), 0)
    for k in range(TR // TM):
        hw = h2[k * TM:(k + 1) * TM, :]
        route, cnt = _route_window(hw, rw_ref[...], rb_ref[...], n_exp)
        rc_ref[k * TM:(k + 1) * TM, :] = route.T
        cnt_ref[k] = cnt
        r12 = route[0:2, :].astype(jnp.int32)
        p = jnp.where((riota == r12[0:1, :]) | (riota == r12[1:2, :]), 1.0, 0.0).astype(BF16)
        xl_ref[k * rows:(k + 1) * rows, :] = _pack_bf16_pairs(_dot(p, hw.astype(BF16)))


def _mix(hg, yf, sa, sb, xa, mod, wpr, wpf, wo, gffn, rw, rb, *, n_rows, n_lat_rows, nb):
    rows_all, d = xa.shape
    f = wpf.shape[0]
    n_exp = rb.shape[0]
    n_lat = n_lat_rows // TR
    wpt = TR // TM
    n_win = n_rows // TM
    lrows = _local_rows(n_exp)
    row = lambda width: pl.BlockSpec((TR, width), lambda i: (i, 0))
    return pl.pallas_call(
        functools.partial(_mix_body, d=d, n_exp=n_exp, nb=nb),
        grid=(n_rows // TR,),
        in_specs=[
            row(d), pl.BlockSpec((TR // nb, nb * f), lambda i: (i, 0)), row(d), row(d), row(d),
            pl.BlockSpec((1, SUBLANES, N_MOD * d), lambda i: (jnp.where(i < n_lat, 0, 1), 0, 0)),
            _full(wpr.shape, 1), _full(wpf.shape, 1), _full(wo.shape, 1),
            _full((1, d), 1), _full(rw.shape, 1), _full((n_exp, 1), 1),
        ],
        scratch_shapes=[pltpu.VMEM((f // LANES, TR, LANES), F32)],
        out_specs=[
            row(d),
            pl.BlockSpec((wpt * lrows, d // 2), lambda i: (i, 0)),
            row(LANES),
            pl.BlockSpec((wpt, n_exp, LANES), lambda i: (i, 0, 0)),
        ],
        out_shape=[
            jax.ShapeDtypeStruct((rows_all, d), F32),
            jax.ShapeDtypeStruct((n_win * lrows, d // 2), jnp.uint32),
            jax.ShapeDtypeStruct((rows_all, LANES), F32),
            jax.ShapeDtypeStruct((n_win, n_exp, LANES), F32),
        ],
        compiler_params=_params(("arbitrary",)),
        name="mix_out",
    )(hg, yf, sa, sb, xa, mod, wpr, wpf, wo, gffn.reshape(1, d), rw, rb.reshape(n_exp, 1))


def _local_rows(n_exp):
    return -(-(2 * TM + n_exp * (SUBLANES - 1)) // LANES) * LANES


def _pack_bf16_pairs(x):
    half = x.shape[1] // 2
    bits = lax.bitcast_convert_type(x, jnp.uint32)
    return (bits[:, :half] & jnp.uint32(0xFFFF0000)) | lax.shift_right_logical(bits[:, half:], jnp.uint32(16))


def _unpack_bf16_pairs(u):
    hi = lax.bitcast_convert_type(u & jnp.uint32(0xFFFF0000), F32)
    lo = lax.bitcast_convert_type(lax.shift_left(u, jnp.uint32(16)), F32)
    return jnp.concatenate([hi, lo], axis=-1)


CPT = MOE_TM // SUBLANES


def _group_copies(table_ref, hbm, buf, sem, *, gather):
    for j in range(CPT):
        far = hbm.at[pl.ds(pl.multiple_of(table_ref[0, 0, j] * SUBLANES, SUBLANES), SUBLANES)]
        near = buf.at[pl.ds(j * SUBLANES, SUBLANES)]
        (pltpu.make_async_copy(far, near, sem) if gather else pltpu.make_async_copy(near, far, sem)).start()


def _tile_wait(hbm, buf, sem, *, gather):
    far = hbm.at[pl.ds(0, MOE_TM)]
    (pltpu.make_async_copy(far, buf, sem) if gather else pltpu.make_async_copy(buf, far, sem)).wait()


def _gmm_body(te_ref, nu_ref, src_ref, nxt_ref, dst_ref, xl_ref, w1_ref, w3_ref, w2_ref, yl_ref,
              w1b, w3b, w2b, xbuf, ybuf, gsem, ssem):
    t = pl.program_id(0)
    n_used = nu_ref[0]
    slot = lax.rem(t, 2)
    other = 1 - slot

    @pl.when((t == 0) | (te_ref[t] != te_ref[jnp.maximum(t - 1, 0)]))
    def _():
        w1b[...] = w1_ref[0, 0].astype(BF16)
        w3b[...] = w3_ref[0, 0].astype(BF16)
        w2b[...] = w2_ref[0, 0].astype(BF16)

    @pl.when(t < n_used)
    def _():
        @pl.when(t == 0)
        def _():
            _group_copies(src_ref, xl_ref, xbuf.at[slot], gsem.at[slot], gather=True)

        @pl.when(t + 1 < n_used)
        def _():
            _group_copies(nxt_ref, xl_ref, xbuf.at[other], gsem.at[other], gather=True)

        _tile_wait(xl_ref, xbuf.at[slot], gsem.at[slot], gather=True)
        x = _unpack_bf16_pairs(xbuf[slot]).astype(BF16)
        a = _dot(x, w1b[...])
        h = (a * jax.nn.sigmoid(a)) * _dot(x, w3b[...])
        y = _dot(h.astype(BF16), w2b[...])

        @pl.when(t >= 2)
        def _():
            _tile_wait(yl_ref, ybuf.at[slot], ssem.at[slot], gather=False)

        ybuf[slot] = _pack_bf16_pairs(y.astype(BF16).astype(F32))
        _group_copies(dst_ref, yl_ref, ybuf.at[slot], ssem.at[slot], gather=False)

        @pl.when(t == n_used - 1)
        def _():
            @pl.when(t >= 1)
            def _():
                _tile_wait(yl_ref, ybuf.at[other], ssem.at[other], gather=False)

            _tile_wait(yl_ref, ybuf.at[slot], ssem.at[slot], gather=False)


def _gmm(tile_exp, n_used, src, dst, xl, w1, w3, w2, *, layer, out_rows):
    n_tiles = src.shape[0]
    d, de = w1.shape[2], w1.shape[3]
    wspec = lambda w: pl.BlockSpec((1, 1) + w.shape[2:], lambda t, te, nu: (layer, te[t], 0, 0))
    table = lambda idx: pl.BlockSpec((1, 1, CPT), lambda t, te, nu: (idx(t, nu), 0, 0), memory_space=pltpu.SMEM)
    cur = lambda t, nu: jnp.minimum(t, nu[0] - 1)
    grid_spec = pltpu.PrefetchScalarGridSpec(
        num_scalar_prefetch=2,
        grid=(n_tiles,),
        in_specs=[table(cur), table(lambda t, nu: jnp.minimum(t + 1, nu[0] - 1)), table(cur),
                  pl.BlockSpec(memory_space=pl.ANY), wspec(w1), wspec(w3), wspec(w2)],
        out_specs=pl.BlockSpec(memory_space=pl.ANY),
        scratch_shapes=[
            pltpu.VMEM((d, de), BF16), pltpu.VMEM((d, de), BF16), pltpu.VMEM((de, d), BF16),
            pltpu.VMEM((2, MOE_TM, d // 2), jnp.uint32), pltpu.VMEM((2, MOE_TM, d // 2), jnp.uint32),
            pltpu.SemaphoreType.DMA((2,)), pltpu.SemaphoreType.DMA((2,)),
        ],
    )
    return pl.pallas_call(
        _gmm_body,
        grid_spec=grid_spec,
        out_shape=jax.ShapeDtypeStruct((out_rows, d // 2), jnp.uint32),
        compiler_params=_params(("arbitrary",)),
        name="moe_experts",
    )(tile_exp, n_used, src, src, dst, xl, w1, w3, w2)


def _combine_body(*refs, d, n_exp, n_win, rows, nb, final):
    if final:
        tot_ref, yl_ref, rc_ref, xr_ref, mod_ref, gfin_ref, o_ref, t_scr = refs
    else:
        tot_ref, yl_ref, rc_ref, xr_ref, mod_ref, o_ref = refs
    keep = lax.broadcasted_iota(jnp.int32, (rows, 1), 0) < tot_ref[pl.program_id(0)]
    y = _unpack_bf16_pairs(jnp.where(keep, yl_ref[...], jnp.uint32(0))).astype(BF16)
    rc = rc_ref[...]
    ci = lax.broadcasted_iota(jnp.int32, (TM, rows), 1)
    pw = (jnp.where(ci == rc[:, 0:1].astype(jnp.int32), rc[:, 2:3], 0.0)
          + jnp.where(ci == rc[:, 1:2].astype(jnp.int32), rc[:, 3:4], 0.0))
    moe = _dot(pw.astype(BF16), y)
    g2 = mod_ref[0][:, 5 * d:6 * d]
    xn = xr_ref[...] + _per_batch(moe, lambda v: v * g2[None])
    if final:
        xn = _rms(xn, gfin_ref[...])
        for g in range(d // LANES):
            t_scr[g] = xn[:, g * LANES:(g + 1) * LANES]
        for b in range(nb):
            o_ref[b] = jnp.concatenate(
                [t_scr[g, pl.ds(b, TM // nb, stride=nb), :] for g in range(d // LANES)], axis=-1)
    else:
        o_ref[...] = xn


def _combine(totals, yl, rc, xres, mod, *, n_exp, n_win, n_lat_rows, nb, gfin=None, seq_out=None):
    d = xres.shape[1]
    rows = _local_rows(n_exp)
    final = gfin is not None
    n_lat = n_lat_rows // TM
    row = lambda width: pl.BlockSpec((TM, width), lambda w, a: (w, 0))
    in_specs = [pl.BlockSpec((rows, d // 2), lambda w, a: (w, 0)), row(LANES), row(d),
                pl.BlockSpec((1, SUBLANES, N_MOD * d), lambda w, a: (jnp.where(w < n_lat, 0, 1), 0, 0))]
    args = [totals, yl, rc, xres, mod]
    scratch = []
    if final:
        in_specs.append(pl.BlockSpec((1, d), lambda w, a: (0, 0)))
        args.append(gfin.reshape(1, d))
        out_shape = jax.ShapeDtypeStruct((nb, seq_out, d), F32)
        out_specs = pl.BlockSpec((nb, TM // nb, d), lambda w, a: (0, w, 0))
        scratch.append(pltpu.VMEM((d // LANES, TM, LANES), F32))
        aliases = {}
    else:
        out_shape = jax.ShapeDtypeStruct(xres.shape, F32)
        out_specs = row(d)
        aliases = {3: 0}
    grid_spec = pltpu.PrefetchScalarGridSpec(
        num_scalar_prefetch=1, grid=(n_win,), in_specs=in_specs, out_specs=out_specs, scratch_shapes=scratch)
    return pl.pallas_call(
        functools.partial(_combine_body, d=d, n_exp=n_exp, n_win=n_win, rows=rows, nb=nb, final=final),
        grid_spec=grid_spec,
        out_shape=out_shape,
        input_output_aliases=aliases,
        compiler_params=_params(("arbitrary",)),
        name="moe_combine",
    )(*args)


def _route_tables(cnt, *, n_exp, n_tiles):
    cnt = cnt[..., 0].astype(jnp.int32)
    n_win = cnt.shape[0]
    lrows = _local_rows(n_exp)
    npad = (cnt + (SUBLANES - 1)) & (-SUBLANES)
    seg = -(-jnp.sum(npad, axis=0) // MOE_TM) * MOE_TM
    ends = jnp.cumsum(seg)
    goff = (ends - seg)[None, :] + jnp.cumsum(npad, axis=0) - npad
    loff = jnp.cumsum(npad, axis=1) - npad
    tile_ends = ends // MOE_TM
    n_used = tile_ends[-1]
    tile = jnp.minimum(jnp.arange(n_tiles, dtype=jnp.int32), n_used - 1)
    tile_exp = jnp.sum(tile[:, None] >= tile_ends[None, :], axis=1).astype(jnp.int32)

    starts = (goff.T // SUBLANES).reshape(-1)
    lens = (npad.T // SUBLANES).reshape(-1)
    local = ((jnp.arange(n_win, dtype=jnp.int32)[:, None] * lrows + loff).T // SUBLANES).reshape(-1)
    group = jnp.arange(n_tiles * CPT, dtype=jnp.int32)
    run = jnp.searchsorted(starts, group, side="right").astype(jnp.int32) - 1
    valid = group < starts[run] + lens[run]
    src = jnp.where(valid, local[run] + group - starts[run], 0)
    dst = jnp.where(valid, src, n_win * lrows // SUBLANES + group)
    shape = (n_tiles, 1, CPT)
    return (tile_exp, n_used.reshape(1).astype(jnp.int32), src.reshape(shape).astype(jnp.int32),
            dst.reshape(shape).astype(jnp.int32), jnp.sum(npad, axis=1).astype(jnp.int32))


def _channel_dft_tables(n):
    k = jnp.arange(n, dtype=jnp.int32)
    ang = ((k[:, None] * k[None, :]) % n).astype(F32) * (2.0 * math.pi / n)
    scale = 1.0 / math.sqrt(n)
    return jnp.cos(ang) * scale, jnp.sin(ang) * scale


def kernel(x, c, ctx, c_ctx, ada_w, ada_b, norm_mix_g, w_in, conv_w, conv_b, lru_w_a, lru_b_a, lru_w_x, lru_b_x, lru_lambda, w_proj_rnn, w_proj_fourier, w_out, norm_ffn_g, router_w, router_b, moe_w1, moe_w3, moe_w2, final_norm_g):
    nb, l, d = x.shape
    lc = ctx.shape[1]
    depth = ada_w.shape[0]
    r = conv_w.shape[2]
    f = w_proj_fourier.shape[1]
    nh, blk = lru_w_a.shape[2], lru_w_a.shape[3]
    assert nb == SUBLANES and blk == LANES and nb < MOD_ROWS
    assert l % TS == 0 and lc % TS == 0 and TS % GRID_W == 0 and (l * nb) % TR == 0 and (lc * nb) % TR == 0
    assert GRID_W & (GRID_W - 1) == 0 and lc & (lc - 1) == 0 and l % lc == 0 and lc <= 512
    s = l + lc
    n_lat_rows = l * nb

    cc = jnp.zeros((MOD_ROWS, d), F32).at[:nb].set(c).at[nb].set(c_ctx)
    mod = _ada(cc, ada_w, ada_b)
    mod = jnp.stack([mod[:, :nb], jnp.broadcast_to(mod[:, nb:nb + 1], (depth, nb, N_MOD * d))], axis=1)

    cch, sch = _channel_dft_tables(f // FOURIER_GROUPS)
    eye = jnp.eye(FOURIER_GROUPS, dtype=F32)
    fcs = jnp.concatenate([jnp.kron(eye, cch), jnp.kron(eye, sch)], axis=1).astype(BF16)
    n_exp = router_w.shape[1]
    rw = jnp.pad(router_w, ((0, 0), (0, LANES - n_exp)))

    out = xa = None
    for li in range(depth):
        last = li == depth - 1
        inproj = functools.partial(_inproj, mod=mod[li], g=norm_mix_g[li], w_in=w_in[li].astype(BF16), fcs=fcs,
                                   n_lat_rows=n_lat_rows, r=r, nb=nb)
        if li == 0:
            u, gl, sa, sb, pq, xa = inproj(x, ctx=ctx)
        else:
            u, gl, sa, sb, pq = inproj(xa)
        wg = [jnp.concatenate([lru_w_a[li, k], lru_w_x[li, k]], axis=-1).astype(BF16) for k in range(2)]
        bg = [jnp.concatenate([lru_b_a[li, k], lru_b_x[li, k]], axis=-1).reshape(nh, 1, 2 * blk) for k in range(2)]
        lam = [lru_lambda[li, k].reshape(nh, 1, blk) for k in range(2)]
        scan = functools.partial(_scan, u, conv_w[li], conv_b[li], nb=nb, n_lat_steps=l // TS, ctx_len=lc)
        hb = scan(wg[1], bg[1], lam[1], reverse=True)
        hg = scan(wg[0], bg[0], lam[0], reverse=False, hb=hb, gl=gl)
        yf = _dft(pq, n=l, nb=nb, f=f, row0=0, folded=_fold(pq, n=l, nb=nb, f=f))
        if not last:
            yf = _dft(pq, n=lc, nb=nb, f=f, row0=l, prev=yf)
        n_rows = n_lat_rows if last else s * nb
        x1, xl, rc, cnt = _mix(hg, yf, sa, sb, xa, mod[li], w_proj_rnn[li].astype(BF16),
                               w_proj_fourier[li].astype(BF16), w_out[li].astype(BF16), norm_ffn_g[li],
                               rw, router_b, n_rows=n_rows, n_lat_rows=n_lat_rows, nb=nb)
        n_win = n_rows // TM
        n_tiles = -(-(n_win * (2 * TM + n_exp * (SUBLANES - 1)) + n_exp * MOE_TM) // MOE_TM)
        tile_exp, n_used, src, dst, totals = _route_tables(cnt, n_exp=n_exp, n_tiles=n_tiles)
        yl = _gmm(tile_exp, n_used, src, dst, xl, moe_w1, moe_w3, moe_w2, layer=li,
                  out_rows=xl.shape[0] + n_tiles * MOE_TM)
        comb = functools.partial(_combine, totals, yl, rc, x1, mod[li], n_exp=n_exp, n_win=n_win,
                                 n_lat_rows=n_lat_rows, nb=nb)
        if last:
            out = comb(gfin=final_norm_g, seq_out=l)
        else:
            xa = comb()
    return out
```

```python
import functools
import math

import jax
import jax.numpy as jnp
from jax import lax
from jax.experimental import pallas as pl
from jax.experimental.pallas import tpu as pltpu

F32 = jnp.float32
BF16 = jnp.bfloat16

RMS_EPS = 1e-6
LRU_C = 8.0
GRID_W = 64
CONV_TAPS = ((0, -2), (1, -1), (2, 0), (3, 1))
N_MOD = 6
FOURIER_GROUPS = 4
N_EXPERT_GROUPS = 4
LANES = 128
SUBLANES = 8
PACKED_ROWS = 16
TR = 512
TM = 256
TS = 128
SUB = 64
MOE_TM = 512
MOD_ROWS = 16
VMEM_LIMIT = 56 * 1024 * 1024


def _dot(a, b):
    return jnp.dot(a, b, preferred_element_type=F32)


def _split(a):
    hi = a.astype(BF16)
    lo = (a - hi.astype(F32)).astype(BF16)
    return hi, lo


def _dot3(a, b):
    ah, al = _split(a)
    bh, bl = _split(b)
    return _dot(ah, bh) + _dot(al, bh) + _dot(ah, bl)


def _dot3_nt(a, b):
    dn = (((1,), (1,)), ((), ()))
    ah, al = _split(a)
    bh, bl = _split(b)
    f = lambda p, q: lax.dot_general(p, q, dn, preferred_element_type=F32)
    return f(ah, bh) + f(al, bh) + f(ah, bl)


def _gelu_tanh(x):
    return 0.5 * x * (1.0 + jnp.tanh(math.sqrt(2.0 / math.pi) * (x + 0.044715 * (x * x * x))))


def _sigmoid(x):
    return 0.5 * jnp.tanh(0.5 * x) + 0.5


def _rms(x, g):
    return x * lax.rsqrt(jnp.mean(x * x, axis=-1, keepdims=True) + RMS_EPS) * g


def _per_batch(x, fn):
    rows, d = x.shape
    return fn(x.reshape(rows // SUBLANES, SUBLANES, d)).reshape(rows, d)


def _params(sem, vmem=VMEM_LIMIT):
    return pltpu.CompilerParams(dimension_semantics=sem, vmem_limit_bytes=vmem)


def _full(shape, n_idx):
    zeros = (0,) * len(shape)
    return pl.BlockSpec(shape, lambda *_: zeros)


def _ada_body(cc_ref, w_ref, b_ref, o_ref):
    cc = cc_ref[...]
    s = cc * jax.nn.sigmoid(cc)
    o_ref[0] = _dot3(s, w_ref[0]) + b_ref[0]


def _ada(cc, ada_w, ada_b):
    depth, d, n = ada_w.shape
    tn = 1024
    return pl.pallas_call(
        _ada_body,
        grid=(depth, n // tn),
        in_specs=[
            pl.BlockSpec((MOD_ROWS, d), lambda l, j: (0, 0)),
            pl.BlockSpec((1, d, tn), lambda l, j: (l, 0, j)),
            pl.BlockSpec((1, 1, tn), lambda l, j: (l, 0, j)),
        ],
        out_specs=pl.BlockSpec((1, MOD_ROWS, tn), lambda l, j: (l, 0, j)),
        out_shape=jax.ShapeDtypeStruct((depth, MOD_ROWS, n), F32),
        compiler_params=_params(("arbitrary", "arbitrary")),
        name="ada_mod",
    )(cc, ada_w, ada_b.reshape(depth, 1, n))


def _to_time_major(blocks, scr, nb):
    npos, width = blocks[0].shape
    for b in range(nb):
        for g in range(width // LANES):
            scr[g, pl.ds(b, npos, stride=nb), :] = blocks[b][:, g * LANES:(g + 1) * LANES].astype(F32)
    return jnp.concatenate([scr[g] for g in range(width // LANES)], axis=-1)


def _to_batch_major(v, scr, nb):
    rows, width = v.shape
    for g in range(width // LANES):
        scr[g] = v[:, g * LANES:(g + 1) * LANES]
    return [jnp.concatenate([scr[g, pl.ds(b, rows // nb, stride=nb), :] for g in range(width // LANES)], axis=-1)
            for b in range(nb)]


def _inproj_body(*refs, d, r, f, nb, n_lat, first):
    if first:
        (x_ref, c_ref, mod_ref, g_ref, w_ref, fcs_ref,
         u_ref, gl_ref, sa_ref, sb_ref, pq_ref, xa_ref, t_scr) = refs

        @pl.when(pl.program_id(0) < n_lat)
        def _():
            xa_ref[...] = _to_time_major([x_ref[b] for b in range(nb)], t_scr, nb)

        @pl.when(pl.program_id(0) >= n_lat)
        def _():
            xa_ref[...] = _to_time_major([c_ref[b] for b in range(nb)], t_scr, nb)

        x = xa_ref[...]
    else:
        x_ref, mod_ref, g_ref, w_ref, fcs_ref, u_ref, gl_ref, sa_ref, sb_ref, pq_ref, t_scr = refs
        x = x_ref[...]
    mod = mod_ref[0]
    h = _rms(x, g_ref[...])
    h = _per_batch(h, lambda v: v * (1.0 + mod[:, d:2 * d])[None] + mod[:, 0:d][None]).astype(BF16)
    u_ref[...] = _dot(h, w_ref[:, 0:r]).astype(BF16)
    gl_ref[...] = _gelu_tanh(_dot(h, w_ref[:, r:2 * r])).astype(BF16)
    u4 = _dot(h, w_ref[:, 2 * r:2 * r + f]).astype(BF16)
    for b, blk in enumerate(_to_batch_major(_dot(u4, fcs_ref[...]), t_scr, nb)):
        pq_ref[:, b * 2 * f:(b + 1) * 2 * f] = blk.astype(BF16)
    s3 = 2 * r + f
    sa_ref[...] = jax.nn.sigmoid(_dot(h, w_ref[:, s3:s3 + d])).astype(BF16)
    sb_ref[...] = jax.nn.sigmoid(_dot(h, w_ref[:, s3 + d:s3 + 2 * d])).astype(BF16)


def _inproj(x, mod, g, w_in, fcs, *, n_lat_rows, r, nb, ctx=None):
    first = ctx is not None
    d = x.shape[-1]
    rows = (x.shape[1] + ctx.shape[1]) * nb if first else x.shape[0]
    f = fcs.shape[0]
    n_lat = n_lat_rows // TR
    npos = TR // nb
    row = lambda width: pl.BlockSpec((TR, width), lambda i: (i, 0))
    common = [
        pl.BlockSpec((1, SUBLANES, N_MOD * d), lambda i: (jnp.where(i < n_lat, 0, 1), 0, 0)),
        _full((1, d), 1), _full(w_in.shape, 1), _full(fcs.shape, 1),
    ]
    out_specs = [row(r), row(r), row(d), row(d), pl.BlockSpec((npos, nb * 2 * f), lambda i: (i, 0))]
    out_shape = [
        jax.ShapeDtypeStruct((rows, r), BF16),
        jax.ShapeDtypeStruct((rows, r), BF16),
        jax.ShapeDtypeStruct((rows, d), BF16),
        jax.ShapeDtypeStruct((rows, d), BF16),
        jax.ShapeDtypeStruct((rows // nb, nb * 2 * f), BF16),
    ]
    if first:
        in_specs = [
            pl.BlockSpec((nb, npos, d), lambda i: (0, jnp.minimum(i, n_lat - 1), 0)),
            pl.BlockSpec((nb, npos, d), lambda i: (0, jnp.maximum(i - n_lat, 0), 0)),
        ] + common
        args = [x, ctx]
        out_specs.append(row(d))
        out_shape.append(jax.ShapeDtypeStruct((rows, d), F32))
    else:
        in_specs = [row(d)] + common
        args = [x]
    return pl.pallas_call(
        functools.partial(_inproj_body, d=d, r=r, f=f, nb=nb, n_lat=n_lat, first=first),
        grid=(rows // TR,),
        in_specs=in_specs,
        out_specs=out_specs,
        out_shape=out_shape,
        scratch_shapes=[pltpu.VMEM((d // LANES, TR, LANES), F32)],
        compiler_params=_params(("arbitrary",)),
        name="in_proj",
    )(*args, mod, g.reshape(1, d), w_in, fcs)


def _scan_body(*refs, reverse, merge, nb, n_lat_steps, ctx_len, order):
    if merge:
        (u_ref, up_ref, un_ref, cw_ref, cb_ref, wg_ref, bg_ref, lam_ref, hb_ref, gl_ref,
         o_ref, h_scr, a_scr, b_scr, uc_scr, hs_scr) = refs
    else:
        (u_ref, up_ref, un_ref, cw_ref, cb_ref, wg_ref, bg_ref, lam_ref,
         o_ref, h_scr, a_scr, b_scr, uc_scr, hs_scr) = refs
    nh = wg_ref.shape[0]
    rows = SUB * nb
    trows = TS * nb
    j = pl.program_id(0)

    @pl.when(j == 0)
    def _():
        h_scr[...] = jnp.zeros_like(h_scr)

    blk = order(j)
    n_steps = pl.num_programs(0)
    is_ctx = blk >= n_lat_steps
    keep_prev = jnp.where(is_ctx & (blk > n_lat_steps), 1.0, 0.0)
    keep_next = jnp.where(is_ctx & (blk < n_steps - 1), 1.0, 0.0)
    lat = jnp.where(is_ctx, 0.0, 1.0)
    halo = 2 * nb
    ext = jnp.concatenate([up_ref[...].astype(F32) * keep_prev, u_ref[...].astype(F32),
                           un_ref[...].astype(F32)[0:nb] * keep_next], axis=0)
    uc = cb_ref[...]
    for k, s in CONV_TAPS:
        lo = halo + s * nb
        uc = uc + cw_ref[k:k + 1, :] * ext[lo:lo + trows, :]
    uc_scr[...] = uc
    for p in range(GRID_W, TS, GRID_W):
        at = lambda q: ext[halo + q * nb:halo + (q + 1) * nb, :]
        rows_of = lambda q: pl.ds(q * nb, nb)
        uc_scr[rows_of(p), :] -= lat * (cw_ref[0:1, :] * at(p - 2) + cw_ref[1:2, :] * at(p - 1))
        uc_scr[rows_of(p + 1), :] -= lat * (cw_ref[0:1, :] * at(p - 1))
        uc_scr[rows_of(p - 1), :] -= lat * (cw_ref[3:4, :] * at(p))

    lam = lam_ref[...]
    decay = (-LRU_C) * (jnp.maximum(-lam, 0.0) + jnp.log1p(jnp.exp(-jnp.abs(lam))))

    subs = range(TS // SUB)
    for sub in (reversed(subs) if reverse else subs):
        r0 = sub * rows
        for g in range(nh):
            ug = uc_scr[r0:r0 + rows, g * LANES:(g + 1) * LANES]
            gates = _dot(ug.astype(BF16), wg_ref[g]) + bg_ref[g]
            log_a = decay[g] * _sigmoid(gates[:, :LANES])
            a = jnp.exp(log_a)
            z = -jnp.tanh(log_a) * (a * a + 1.0)
            mult = jnp.where(z > 0.0, z * lax.rsqrt(z), 0.0)
            a_scr[g] = a
            b_scr[g] = mult * (_sigmoid(gates[:, LANES:]) * ug)

        def step(k, hs):
            t = (SUB - 1 - k) if reverse else k
            off = pl.multiple_of(t * nb, nb)
            new = []
            for g in range(nh):
                hg = a_scr[g, pl.ds(off, nb), :] * hs[g] + b_scr[g, pl.ds(off, nb), :]
                hs_scr[pl.ds(r0 + off, nb), g * LANES:(g + 1) * LANES] = hg
                new.append(hg)
            return tuple(new)

        hs = lax.fori_loop(0, SUB, step, tuple(h_scr[g] for g in range(nh)), unroll=8)
        for g in range(nh):
            h_scr[g] = hs[g]

    if merge:
        o_ref[...] = ((hs_scr[...] + hb_ref[...].astype(F32)) * gl_ref[...].astype(F32)).astype(BF16)
    else:
        o_ref[...] = hs_scr[...].astype(BF16)


def _scan(u, conv_w, conv_b, wg, bg, lam, *, nb, n_lat_steps, ctx_len, reverse, hb=None, gl=None):
    rows_all, r = u.shape
    nh = wg.shape[0]
    trows = TS * nb
    n = rows_all // trows
    n_ctx = n - n_lat_steps
    merge = hb is not None
    if reverse:
        order = lambda j: jnp.where(j < n_ctx, n - 1 - j, n_lat_steps - 1 - (j - n_ctx))
    else:
        order = lambda j: jnp.where(j < n_ctx, n_lat_steps + j, j - n_ctx)
    hpb = trows // PACKED_ROWS
    tile = pl.BlockSpec((trows, r), lambda j: (order(j), 0))
    in_specs = [
        tile,
        pl.BlockSpec((PACKED_ROWS, r), lambda j: (jnp.maximum(order(j) * hpb - 1, 0), 0)),
        pl.BlockSpec((PACKED_ROWS, r), lambda j: (jnp.minimum((order(j) + 1) * hpb, n * hpb - 1), 0)),
        _full(conv_w.shape, 1), _full((1, r), 1), _full(wg.shape, 1), _full(bg.shape, 1), _full(lam.shape, 1),
    ]
    args = [u, u, u, conv_w, conv_b.reshape(1, r), wg, bg, lam]
    if merge:
        in_specs += [tile, tile]
        args += [hb, gl]
    return pl.pallas_call(
        functools.partial(_scan_body, reverse=reverse, merge=merge, nb=nb, n_lat_steps=n_lat_steps,
                          ctx_len=ctx_len, order=order),
        grid=(n,),
        in_specs=in_specs,
        out_specs=tile,
        out_shape=jax.ShapeDtypeStruct((rows_all, r), BF16),
        scratch_shapes=[
            pltpu.VMEM((nh, nb, LANES), F32),
            pltpu.VMEM((nh, SUB * nb, LANES), F32),
            pltpu.VMEM((nh, SUB * nb, LANES), F32),
            pltpu.VMEM((trows, r), F32),
            pltpu.VMEM((trows, r), F32),
        ],
        compiler_params=_params(("arbitrary",)),
        name="scan_fwd_merge" if merge else "scan_bwd",
    )(*args)


FOLD_TILE = 256


def _fold_body(a_ref, b1_ref, b2_ref, o_ref, *, f):
    m = pl.program_id(0)
    i = lax.broadcasted_iota(jnp.int32, (FOLD_TILE, 2 * FOLD_TILE), 0)
    col = lax.broadcasted_iota(jnp.int32, (FOLD_TILE, 2 * FOLD_TILE), 1)
    sel = (col == jnp.where(i == 0, FOLD_TILE, FOLD_TILE - i)) & ((i > 0) | (m > 0))
    mirrored = _dot(jnp.where(sel, 1.0, 0.0).astype(BF16), jnp.concatenate([b1_ref[...], b2_ref[...]], axis=0))
    a = a_ref[...].astype(F32)
    o_ref[:, 0:f] = (a[:, 0:f] + mirrored[:, 0:f]).astype(BF16)
    o_ref[:, f:2 * f] = (a[:, f:2 * f] - mirrored[:, f:2 * f]).astype(BF16)


def _fold(pq, *, n, nb, f):
    nt = n // FOLD_TILE
    col = lambda idx: pl.BlockSpec((FOLD_TILE, 2 * f), lambda m, b: (idx(m), b))
    return pl.pallas_call(
        functools.partial(_fold_body, f=f),
        grid=(nt // 2, nb),
        in_specs=[col(lambda m: m), col(lambda m: nt - 1 - m), col(lambda m: jnp.minimum(nt - m, nt - 1))],
        out_specs=col(lambda m: m),
        out_shape=jax.ShapeDtypeStruct((n // 2, nb * 2 * f), BF16),
        compiler_params=_params(("arbitrary", "arbitrary")),
        name="dft_fold",
    )(pq, pq, pq)


def _dft_body(ca_ref, sa_ref, cb_ref, sb_ref, p_ref, q_ref, *rest, kdim, folded, scale):
    o_ref, c_scr, s_scr = rest[-3:]

    @pl.when(pl.program_id(1) == 0)
    def _():
        cb = cb_ref[...]
        sb = sb_ref[...]
        for t1 in range(kdim // LANES):
            ca = ca_ref[:, t1:t1 + 1]
            sa = sa_ref[:, t1:t1 + 1]
            c_scr[:, t1 * LANES:(t1 + 1) * LANES] = (ca * cb - sa * sb).astype(BF16)
            s_scr[:, t1 * LANES:(t1 + 1) * LANES] = (sa * cb + ca * sb).astype(BF16)

    y = _dot(c_scr[...], p_ref[...]) - _dot(s_scr[...], q_ref[...])
    if folded:
        mid_ref = rest[0]
        tmk = o_ref.shape[0]
        k = pl.program_id(0) * tmk + lax.broadcasted_iota(jnp.int32, (tmk, 1), 0)
        sign = (1 - 2 * (k & 1)).astype(F32)
        y = y + sign * (scale * mid_ref[...].astype(F32)[0:1, :])
    o_ref[...] = y.astype(BF16)


def _dft_tables(n):
    k = jnp.arange(n, dtype=jnp.int32)[:, None]
    t1 = jnp.arange(LANES, dtype=jnp.int32)[None, :]
    ang_a = ((k * ((t1 * LANES) % n)) % n).astype(F32) * (2.0 * math.pi / n)
    ang_b = ((k * t1) % n).astype(F32) * (2.0 * math.pi / n)
    scale = 1.0 / math.sqrt(n)
    return jnp.cos(ang_a), jnp.sin(ang_a), jnp.cos(ang_b) * scale, jnp.sin(ang_b) * scale


def _dft(pq, *, n, nb, f, row0, prev=None, folded=None):
    s = pq.shape[0]
    tmk = min(n, 512)
    kdim = n // 2 if folded is not None else n
    tab = pl.BlockSpec((tmk, LANES), lambda m, b: (m, 0))
    if folded is not None:
        assert row0 == 0
        src, rb = folded, 0
    else:
        src, rb = pq, row0 // n
    in_specs = [
        tab, tab, tab, tab,
        pl.BlockSpec((kdim, f), lambda m, b: (rb, 2 * b)),
        pl.BlockSpec((kdim, f), lambda m, b: (rb, 2 * b + 1)),
    ]
    args = list(_dft_tables(n)) + [src, src]
    if folded is not None:
        in_specs.append(pl.BlockSpec((PACKED_ROWS, f), lambda m, b: (kdim // PACKED_ROWS, 2 * b)))
        args.append(pq)
    aliases = {}
    if prev is not None:
        in_specs.append(pl.BlockSpec(memory_space=pl.ANY))
        args.append(prev)
        aliases = {len(args) - 1: 0}
    return pl.pallas_call(
        functools.partial(_dft_body, kdim=kdim, folded=folded is not None, scale=1.0 / math.sqrt(n)),
        grid=(n // tmk, nb),
        in_specs=in_specs,
        out_specs=pl.BlockSpec((tmk, f), lambda m, b: (row0 // tmk + m, b)),
        out_shape=jax.ShapeDtypeStruct((s, nb * f), BF16),
        scratch_shapes=[pltpu.VMEM((tmk, kdim), BF16), pltpu.VMEM((tmk, kdim), BF16)],
        input_output_aliases=aliases,
        compiler_params=_params(("arbitrary", "arbitrary")),
        name="pos_dft",
    )(*args)


def _top2(vals):
    def first_max(vs):
        m = functools.reduce(jnp.maximum, vs)
        idx = jnp.full(m.shape, len(vs) - 1, jnp.int32)
        for k in range(len(vs) - 2, -1, -1):
            idx = jnp.where(vs[k] == m, k, idx)
        return m, idx
    m1, i1 = first_max(vals)
    rest = [jnp.where(i1 == k, -jnp.inf, v) for k, v in enumerate(vals)]
    m2, i2 = first_max(rest)
    return m1, i1, m2, i2


def _pick(idx, vals):
    out = vals[-1]
    for k in range(len(vals) - 2, -1, -1):
        out = jnp.where(idx == k, vals[k], out)
    return out


def _route_window(h2, rw, rb, n_exp):
    aff = jax.nn.sigmoid(_dot3(h2, rw).T[0:n_exp, :])
    sel = aff + rb
    per = n_exp // N_EXPERT_GROUPS
    srow = [sel[e:e + 1, :] for e in range(n_exp)]
    arow = [aff[e:e + 1, :] for e in range(n_exp)]
    scores = []
    for gi in range(N_EXPERT_GROUPS):
        m1, _, m2, _ = _top2(srow[gi * per:(gi + 1) * per])
        scores.append(m1 + m2)
    best = functools.reduce(jnp.maximum, scores)
    grp = jnp.full(best.shape, N_EXPERT_GROUPS - 1, jnp.int32)
    for gi in range(N_EXPERT_GROUPS - 2, -1, -1):
        grp = jnp.where(scores[gi] == best, gi, grp)
    v = [_pick(grp, [srow[gi * per + k] for gi in range(N_EXPERT_GROUPS)]) for k in range(per)]
    a = [_pick(grp, [arow[gi * per + k] for gi in range(N_EXPERT_GROUPS)]) for k in range(per)]
    _, i1, _, i2 = _top2(v)
    w1 = _pick(i1, a)
    w2 = _pick(i2, a)
    den = w1 + w2
    e1 = grp * per + i1
    e2 = grp * per + i2

    erow = lax.broadcasted_iota(jnp.int32, (n_exp, TM), 0)
    hit1 = erow == e1
    hit2 = erow == e2
    onehot = jnp.where(hit1 | hit2, 1.0, 0.0)
    cnt = jnp.sum(onehot, axis=1, keepdims=True)
    before = (lax.broadcasted_iota(jnp.int32, (TM, TM), 0)
              < lax.broadcasted_iota(jnp.int32, (TM, TM), 1)).astype(BF16)
    rank = _dot(onehot.astype(BF16), before)
    padded = ((cnt.astype(jnp.int32) + (SUBLANES - 1)) & (-SUBLANES)).astype(F32)
    lower = (lax.broadcasted_iota(jnp.int32, (n_exp, n_exp), 1)
             < lax.broadcasted_iota(jnp.int32, (n_exp, n_exp), 0)).astype(BF16)
    starts = _dot(lower, jnp.broadcast_to(padded, (n_exp, LANES)).astype(BF16))
    pos = starts[:, 0:1] + rank
    r1 = jnp.sum(jnp.where(hit1, pos, 0.0), axis=0, keepdims=True)
    r2 = jnp.sum(jnp.where(hit2, pos, 0.0), axis=0, keepdims=True)
    route = jnp.concatenate([r1, r2, w1 / den, w2 / den, jnp.zeros((LANES - 4, TM), F32)], axis=0)
    return route, jnp.broadcast_to(cnt, (n_exp, LANES))


def _mix_body(hg_ref, yf_ref, sa_ref, sb_ref, x_ref, mod_ref, wpr_ref, wpf_ref, wo_ref,
              gffn_ref, rw_ref, rb_ref, xo_ref, xl_ref, rc_ref, cnt_ref, t_scr, *, d, n_exp, nb):
    mod = mod_ref[0]
    f = wpf_ref.shape[0]
    y_r = _dot(hg_ref[...], wpr_ref[...])
    yf = _to_time_major([yf_ref[:, b * f:(b + 1) * f] for b in range(nb)], t_scr, nb)
    y_f = _dot(yf.astype(BF16), wpf_ref[...])
    merged = sa_ref[...].astype(F32) * y_r + sb_ref[...].astype(F32) * y_f
    out = _dot(merged.astype(BF16), wo_ref[...])
    xn = x_ref[...] + _per_batch(out, lambda v: v * mod[:, 2 * d:3 * d][None])
    xo_ref[...] = xn
    h2 = _rms(xn, gffn_ref[...])
    h2 = _per_batch(h2, lambda v: v * (1.0 + mod[:, 4 * d:5 * d])[None] + mod[:, 3 * d:4 * d][None])
    rows = _local_rows(n_exp)
    riota = lax.broadcasted_iota(jnp.int32, (rows, TM), 0)
    for k in range(TR // TM):
        hw = h2[k * TM:(k + 1) * TM, :]
        route, cnt = _route_window(hw, rw_ref[...], rb_ref[...], n_exp)
        rc_ref[k * TM:(k + 1) * TM, :] = route.T
        cnt_ref[k] = cnt
        r12 = route[0:2, :].astype(jnp.int32)
        p = jnp.where((riota == r12[0:1, :]) | (riota == r12[1:2, :]), 1.0, 0.0).astype(BF16)
        xl_ref[k * rows:(k + 1) * rows, :] = _pack_bf16_pairs(_dot(p, hw.astype(BF16)))


def _mix(hg, yf, sa, sb, xa, mod, wpr, wpf, wo, gffn, rw, rb, *, n_rows, n_lat_rows, nb):
    rows_all, d = xa.shape
    f = wpf.shape[0]
    n_exp = rb.shape[0]
    n_lat = n_lat_rows // TR
    wpt = TR // TM
    n_win = n_rows // TM
    lrows = _local_rows(n_exp)
    row = lambda width: pl.BlockSpec((TR, width), lambda i: (i, 0))
    return pl.pallas_call(
        functools.partial(_mix_body, d=d, n_exp=n_exp, nb=nb),
        grid=(n_rows // TR,),
        in_specs=[
            row(d), pl.BlockSpec((TR // nb, nb * f), lambda i: (i, 0)), row(d), row(d), row(d),
            pl.BlockSpec((1, SUBLANES, N_MOD * d), lambda i: (jnp.where(i < n_lat, 0, 1), 0, 0)),
            _full(wpr.shape, 1), _full(wpf.shape, 1), _full(wo.shape, 1),
            _full((1, d), 1), _full(rw.shape, 1), _full((n_exp, 1), 1),
        ],
        scratch_shapes=[pltpu.VMEM((f // LANES, TR, LANES), F32)],
        out_specs=[
            row(d),
            pl.BlockSpec((wpt * lrows, d // 2), lambda i: (i, 0)),
            row(LANES),
            pl.BlockSpec((wpt, n_exp, LANES), lambda i: (i, 0, 0)),
        ],
        out_shape=[
            jax.ShapeDtypeStruct((rows_all, d), F32),
            jax.ShapeDtypeStruct((n_win * lrows, d // 2), jnp.uint32),
            jax.ShapeDtypeStruct((rows_all, LANES), F32),
            jax.ShapeDtypeStruct((n_win, n_exp, LANES), F32),
        ],
        compiler_params=_params(("arbitrary",)),
        name="mix_out",
    )(hg, yf, sa, sb, xa, mod, wpr, wpf, wo, gffn.reshape(1, d), rw, rb.reshape(n_exp, 1))


def _local_rows(n_exp):
    return -(-(2 * TM + n_exp * (SUBLANES - 1)) // LANES) * LANES


def _pack_bf16_pairs(x):
    half = x.shape[1] // 2
    bits = lax.bitcast_convert_type(x, jnp.uint32)
    return (bits[:, :half] & jnp.uint32(0xFFFF0000)) | lax.shift_right_logical(bits[:, half:], jnp.uint32(16))


def _unpack_bf16_pairs(u):
    hi = lax.bitcast_convert_type(u & jnp.uint32(0xFFFF0000), F32)
    lo = lax.bitcast_convert_type(lax.shift_left(u, jnp.uint32(16)), F32)
    return jnp.concatenate([hi, lo], axis=-1)


CPT = MOE_TM // SUBLANES


def _group_copies(table_ref, hbm, buf, sem, *, gather):
    for j in range(CPT):
        far = hbm.at[pl.ds(pl.multiple_of(table_ref[0, 0, j] * SUBLANES, SUBLANES), SUBLANES)]
        near = buf.at[pl.ds(j * SUBLANES, SUBLANES)]
        (pltpu.make_async_copy(far, near, sem) if gather else pltpu.make_async_copy(near, far, sem)).start()


def _tile_wait(hbm, buf, sem, *, gather):
    far = hbm.at[pl.ds(0, MOE_TM)]
    (pltpu.make_async_copy(far, buf, sem) if gather else pltpu.make_async_copy(buf, far, sem)).wait()


def _gmm_body(te_ref, nu_ref, src_ref, nxt_ref, dst_ref, xl_ref, w1_ref, w3_ref, w2_ref, yl_ref,
              w1b, w3b, w2b, xbuf, ybuf, gsem, ssem):
    t = pl.program_id(0)
    n_used = nu_ref[0]
    slot = lax.rem(t, 2)
    other = 1 - slot

    @pl.when((t == 0) | (te_ref[t] != te_ref[jnp.maximum(t - 1, 0)]))
    def _():
        w1b[...] = w1_ref[0, 0].astype(BF16)
        w3b[...] = w3_ref[0, 0].astype(BF16)
        w2b[...] = w2_ref[0, 0].astype(BF16)

    @pl.when(t < n_used)
    def _():
        @pl.when(t == 0)
        def _():
            _group_copies(src_ref, xl_ref, xbuf.at[slot], gsem.at[slot], gather=True)

        @pl.when(t + 1 < n_used)
        def _():
            _group_copies(nxt_ref, xl_ref, xbuf.at[other], gsem.at[other], gather=True)

        _tile_wait(xl_ref, xbuf.at[slot], gsem.at[slot], gather=True)
        x = _unpack_bf16_pairs(xbuf[slot]).astype(BF16)
        a = _dot(x, w1b[...])
        h = (a * jax.nn.sigmoid(a)) * _dot(x, w3b[...])
        y = _dot(h.astype(BF16), w2b[...])

        @pl.when(t >= 2)
        def _():
            _tile_wait(yl_ref, ybuf.at[slot], ssem.at[slot], gather=False)

        ybuf[slot] = _pack_bf16_pairs(y.astype(BF16).astype(F32))
        _group_copies(dst_ref, yl_ref, ybuf.at[slot], ssem.at[slot], gather=False)

        @pl.when(t == n_used - 1)
        def _():
            @pl.when(t >= 1)
            def _():
                _tile_wait(yl_ref, ybuf.at[other], ssem.at[other], gather=False)

            _tile_wait(yl_ref, ybuf.at[slot], ssem.at[slot], gather=False)


def _gmm(tile_exp, n_used, src, dst, xl, w1, w3, w2, *, layer, out_rows):
    n_tiles = src.shape[0]
    d, de = w1.shape[2], w1.shape[3]
    wspec = lambda w: pl.BlockSpec((1, 1) + w.shape[2:], lambda t, te, nu: (layer, te[t], 0, 0))
    table = lambda idx: pl.BlockSpec((1, 1, CPT), lambda t, te, nu: (idx(t, nu), 0, 0), memory_space=pltpu.SMEM)
    cur = lambda t, nu: jnp.minimum(t, nu[0] - 1)
    grid_spec = pltpu.PrefetchScalarGridSpec(
        num_scalar_prefetch=2,
        grid=(n_tiles,),
        in_specs=[table(cur), table(lambda t, nu: jnp.minimum(t + 1, nu[0] - 1)), table(cur),
                  pl.BlockSpec(memory_space=pl.ANY), wspec(w1), wspec(w3), wspec(w2)],
        out_specs=pl.BlockSpec(memory_space=pl.ANY),
        scratch_shapes=[
            pltpu.VMEM((d, de), BF16), pltpu.VMEM((d, de), BF16), pltpu.VMEM((de, d), BF16),
            pltpu.VMEM((2, MOE_TM, d // 2), jnp.uint32), pltpu.VMEM((2, MOE_TM, d // 2), jnp.uint32),
            pltpu.SemaphoreType.DMA((2,)), pltpu.SemaphoreType.DMA((2,)),
        ],
    )
    return pl.pallas_call(
        _gmm_body,
        grid_spec=grid_spec,
        out_shape=jax.ShapeDtypeStruct((out_rows, d // 2), jnp.uint32),
        compiler_params=_params(("arbitrary",)),
        name="moe_experts",
    )(tile_exp, n_used, src, src, dst, xl, w1, w3, w2)


def _combine_body(*refs, d, n_exp, n_win, rows, nb, final):
    if final:
        tot_ref, yl_ref, rc_ref, xr_ref, mod_ref, gfin_ref, o_ref, t_scr = refs
    else:
        tot_ref, yl_ref, rc_ref, xr_ref, mod_ref, o_ref = refs
    keep = lax.broadcasted_iota(jnp.int32, (rows, 1), 0) < tot_ref[pl.program_id(0)]
    y = _unpack_bf16_pairs(jnp.where(keep, yl_ref[...], jnp.uint32(0))).astype(BF16)
    rc = rc_ref[...]
    ci = lax.broadcasted_iota(jnp.int32, (TM, rows), 1)
    pw = (jnp.where(ci == rc[:, 0:1].astype(jnp.int32), rc[:, 2:3], 0.0)
          + jnp.where(ci == rc[:, 1:2].astype(jnp.int32), rc[:, 3:4], 0.0))
    moe = _dot(pw.astype(BF16), y)
    g2 = mod_ref[0][:, 5 * d:6 * d]
    xn = xr_ref[...] + _per_batch(moe, lambda v: v * g2[None])
    if final:
        xn = _rms(xn, gfin_ref[...])
        for g in range(d // LANES):
            t_scr[g] = xn[:, g * LANES:(g + 1) * LANES]
        for b in range(nb):
            o_ref[b] = jnp.concatenate(
                [t_scr[g, pl.ds(b, TM // nb, stride=nb), :] for g in range(d // LANES)], axis=-1)
    else:
        o_ref[...] = xn


def _combine(totals, yl, rc, xres, mod, *, n_exp, n_win, n_lat_rows, nb, gfin=None, seq_out=None):
    d = xres.shape[1]
    rows = _local_rows(n_exp)
    final = gfin is not None
    n_lat = n_lat_rows // TM
    row = lambda width: pl.BlockSpec((TM, width), lambda w, a: (w, 0))
    in_specs = [pl.BlockSpec((rows, d // 2), lambda w, a: (w, 0)), row(LANES), row(d),
                pl.BlockSpec((1, SUBLANES, N_MOD * d), lambda w, a: (jnp.where(w < n_lat, 0, 1), 0, 0))]
    args = [totals, yl, rc, xres, mod]
    scratch = []
    if final:
        in_specs.append(pl.BlockSpec((1, d), lambda w, a: (0, 0)))
        args.append(gfin.reshape(1, d))
        out_shape = jax.ShapeDtypeStruct((nb, seq_out, d), F32)
        out_specs = pl.BlockSpec((nb, TM // nb, d), lambda w, a: (0, w, 0))
        scratch.append(pltpu.VMEM((d // LANES, TM, LANES), F32))
        aliases = {}
    else:
        out_shape = jax.ShapeDtypeStruct(xres.shape, F32)
        out_specs = row(d)
        aliases = {3: 0}
    grid_spec = pltpu.PrefetchScalarGridSpec(
        num_scalar_prefetch=1, grid=(n_win,), in_specs=in_specs, out_specs=out_specs, scratch_shapes=scratch)
    return pl.pallas_call(
        functools.partial(_combine_body, d=d, n_exp=n_exp, n_win=n_win, rows=rows, nb=nb, final=final),
        grid_spec=grid_spec,
        out_shape=out_shape,
        input_output_aliases=aliases,
        compiler_params=_params(("arbitrary",)),
        name="moe_combine",
    )(*args)


def _route_tables(cnt, *, n_exp, n_tiles):
    cnt = cnt[..., 0].astype(jnp.int32)
    n_win = cnt.shape[0]
    lrows = _local_rows(n_exp)
    npad = (cnt + (SUBLANES - 1)) & (-SUBLANES)
    seg = -(-jnp.sum(npad, axis=0) // MOE_TM) * MOE_TM
    ends = jnp.cumsum(seg)
    goff = (ends - seg)[None, :] + jnp.cumsum(npad, axis=0) - npad
    loff = jnp.cumsum(npad, axis=1) - npad
    tile_ends = ends // MOE_TM
    n_used = tile_ends[-1]
    tile = jnp.minimum(jnp.arange(n_tiles, dtype=jnp.int32), n_used - 1)
    tile_exp = jnp.sum(tile[:, None] >= tile_ends[None, :], axis=1).astype(jnp.int32)

    starts = (goff.T // SUBLANES).reshape(-1)
    lens = (npad.T // SUBLANES).reshape(-1)
    local = ((jnp.arange(n_win, dtype=jnp.int32)[:, None] * lrows + loff).T // SUBLANES).reshape(-1)
    group = jnp.arange(n_tiles * CPT, dtype=jnp.int32)
    began = starts[None, :] <= group[:, None]
    last = lambda v: jnp.sum(jnp.where(began, jnp.diff(v, prepend=0)[None, :], 0), axis=1)
    valid = group < last(starts + lens)
    src = jnp.where(valid, group + last(local - starts), 0)
    dst = jnp.where(valid, src, n_win * lrows // SUBLANES + group)
    shape = (n_tiles, 1, CPT)
    return (tile_exp, n_used.reshape(1).astype(jnp.int32), src.reshape(shape).astype(jnp.int32),
            dst.reshape(shape).astype(jnp.int32), jnp.sum(npad, axis=1).astype(jnp.int32))


def _channel_dft_tables(n):
    k = jnp.arange(n, dtype=jnp.int32)
    ang = ((k[:, None] * k[None, :]) % n).astype(F32) * (2.0 * math.pi / n)
    scale = 1.0 / math.sqrt(n)
    return jnp.cos(ang) * scale, jnp.sin(ang) * scale


def kernel(x, c, ctx, c_ctx, ada_w, ada_b, norm_mix_g, w_in, conv_w, conv_b, lru_w_a, lru_b_a, lru_w_x, lru_b_x, lru_lambda, w_proj_rnn, w_proj_fourier, w_out, norm_ffn_g, router_w, router_b, moe_w1, moe_w3, moe_w2, final_norm_g):
    nb, l, d = x.shape
    lc = ctx.shape[1]
    depth = ada_w.shape[0]
    r = conv_w.shape[2]
    f = w_proj_fourier.shape[1]
    nh, blk = lru_w_a.shape[2], lru_w_a.shape[3]
    assert nb == SUBLANES and blk == LANES and nb < MOD_ROWS
    assert l % TS == 0 and lc % TS == 0 and TS % GRID_W == 0 and (l * nb) % TR == 0 and (lc * nb) % TR == 0
    assert GRID_W & (GRID_W - 1) == 0 and lc & (lc - 1) == 0 and l % lc == 0 and lc <= 512
    s = l + lc
    n_lat_rows = l * nb

    cc = jnp.zeros((MOD_ROWS, d), F32).at[:nb].set(c).at[nb].set(c_ctx)
    mod = _ada(cc, ada_w, ada_b)
    mod = jnp.stack([mod[:, :nb], jnp.broadcast_to(mod[:, nb:nb + 1], (depth, nb, N_MOD * d))], axis=1)

    cch, sch = _channel_dft_tables(f // FOURIER_GROUPS)
    eye = jnp.eye(FOURIER_GROUPS, dtype=F32)
    fcs = jnp.concatenate([jnp.kron(eye, cch), jnp.kron(eye, sch)], axis=1).astype(BF16)
    n_exp = router_w.shape[1]
    rw = jnp.pad(router_w, ((0, 0), (0, LANES - n_exp)))

    out = xa = None
    for li in range(depth):
        last = li == depth - 1
        inproj = functools.partial(_inproj, mod=mod[li], g=norm_mix_g[li], w_in=w_in[li].astype(BF16), fcs=fcs,
                                   n_lat_rows=n_lat_rows, r=r, nb=nb)
        if li == 0:
            u, gl, sa, sb, pq, xa = inproj(x, ctx=ctx)
        else:
            u, gl, sa, sb, pq = inproj(xa)
        wg = [jnp.concatenate([lru_w_a[li, k], lru_w_x[li, k]], axis=-1).astype(BF16) for k in range(2)]
        bg = [jnp.concatenate([lru_b_a[li, k], lru_b_x[li, k]], axis=-1).reshape(nh, 1, 2 * blk) for k in range(2)]
        lam = [lru_lambda[li, k].reshape(nh, 1, blk) for k in range(2)]
        scan = functools.partial(_scan, u, conv_w[li], conv_b[li], nb=nb, n_lat_steps=l // TS, ctx_len=lc)
        hb = scan(wg[1], bg[1], lam[1], reverse=True)
        hg = scan(wg[0], bg[0], lam[0], reverse=False, hb=hb, gl=gl)
        yf = _dft(pq, n=l, nb=nb, f=f, row0=0, folded=_fold(pq, n=l, nb=nb, f=f))
        if not last:
            yf = _dft(pq, n=lc, nb=nb, f=f, row0=l, prev=yf)
        n_rows = n_lat_rows if last else s * nb
        x1, xl, rc, cnt = _mix(hg, yf, sa, sb, xa, mod[li], w_proj_rnn[li].astype(BF16),
                               w_proj_fourier[li].astype(BF16), w_out[li].astype(BF16), norm_ffn_g[li],
                               rw, router_b, n_rows=n_rows, n_lat_rows=n_lat_rows, nb=nb)
        n_win = n_rows // TM
        n_tiles = -(-(n_win * (2 * TM + n_exp * (SUBLANES - 1)) + n_exp * MOE_TM) // MOE_TM)
        tile_exp, n_used, src, dst, totals = _route_tables(cnt, n_exp=n_exp, n_tiles=n_tiles)
        yl = _gmm(tile_exp, n_used, src, dst, xl, moe_w1, moe_w3, moe_w2, layer=li,
                  out_rows=xl.shape[0] + n_tiles * MOE_TM)
        comb = functools.partial(_combine, totals, yl, rc, x1, mod[li], n_exp=n_exp, n_win=n_win,
                                 n_lat_rows=n_lat_rows, nb=nb)
        if last:
            out = comb(gfin=final_norm_g, seq_out=l)
        else:
            xa = comb()
    return out
```

```python
import functools
import math

import jax
import jax.numpy as jnp
from jax import lax
from jax.experimental import pallas as pl
from jax.experimental.pallas import tpu as pltpu

F32 = jnp.float32
BF16 = jnp.bfloat16

RMS_EPS = 1e-6
LRU_C = 8.0
GRID_W = 64
CONV_TAPS = ((0, -2), (1, -1), (2, 0), (3, 1))
N_MOD = 6
FOURIER_GROUPS = 4
N_EXPERT_GROUPS = 4
LANES = 128
SUBLANES = 8
PACKED_ROWS = 16
TR = 512
TM = 256
TS = 128
SUB = 64
MOE_TM = 512
MOD_ROWS = 16
VMEM_LIMIT = 56 * 1024 * 1024


def _dot(a, b):
    return jnp.dot(a, b, preferred_element_type=F32)


def _split(a):
    hi = a.astype(BF16)
    lo = (a - hi.astype(F32)).astype(BF16)
    return hi, lo


def _dot3(a, b):
    ah, al = _split(a)
    bh, bl = _split(b)
    return _dot(ah, bh) + _dot(al, bh) + _dot(ah, bl)


def _dot3_nt(a, b):
    dn = (((1,), (1,)), ((), ()))
    ah, al = _split(a)
    bh, bl = _split(b)
    f = lambda p, q: lax.dot_general(p, q, dn, preferred_element_type=F32)
    return f(ah, bh) + f(al, bh) + f(ah, bl)


def _gelu_tanh(x):
    return 0.5 * x * (1.0 + jnp.tanh(math.sqrt(2.0 / math.pi) * (x + 0.044715 * (x * x * x))))


def _sigmoid(x):
    return 0.5 * jnp.tanh(0.5 * x) + 0.5


def _rms(x, g):
    return x * lax.rsqrt(jnp.mean(x * x, axis=-1, keepdims=True) + RMS_EPS) * g


def _per_batch(x, fn):
    rows, d = x.shape
    return fn(x.reshape(rows // SUBLANES, SUBLANES, d)).reshape(rows, d)


def _params(sem, vmem=VMEM_LIMIT):
    return pltpu.CompilerParams(dimension_semantics=sem, vmem_limit_bytes=vmem)


def _full(shape, n_idx):
    zeros = (0,) * len(shape)
    return pl.BlockSpec(shape, lambda *_: zeros)


def _ada_body(cc_ref, w_ref, b_ref, o_ref):
    cc = cc_ref[...]
    s = cc * jax.nn.sigmoid(cc)
    o_ref[0] = _dot3(s, w_ref[0]) + b_ref[0]


def _ada(cc, ada_w, ada_b):
    depth, d, n = ada_w.shape
    tn = 1024
    return pl.pallas_call(
        _ada_body,
        grid=(depth, n // tn),
        in_specs=[
            pl.BlockSpec((MOD_ROWS, d), lambda l, j: (0, 0)),
            pl.BlockSpec((1, d, tn), lambda l, j: (l, 0, j)),
            pl.BlockSpec((1, 1, tn), lambda l, j: (l, 0, j)),
        ],
        out_specs=pl.BlockSpec((1, MOD_ROWS, tn), lambda l, j: (l, 0, j)),
        out_shape=jax.ShapeDtypeStruct((depth, MOD_ROWS, n), F32),
        compiler_params=_params(("arbitrary", "arbitrary")),
        name="ada_mod",
    )(cc, ada_w, ada_b.reshape(depth, 1, n))


def _to_time_major(blocks, scr, nb):
    npos, width = blocks[0].shape
    for b in range(nb):
        for g in range(width // LANES):
            scr[g, pl.ds(b, npos, stride=nb), :] = blocks[b][:, g * LANES:(g + 1) * LANES].astype(F32)
    return jnp.concatenate([scr[g] for g in range(width // LANES)], axis=-1)


def _to_batch_major(v, scr, nb):
    rows, width = v.shape
    for g in range(width // LANES):
        scr[g] = v[:, g * LANES:(g + 1) * LANES]
    return [jnp.concatenate([scr[g, pl.ds(b, rows // nb, stride=nb), :] for g in range(width // LANES)], axis=-1)
            for b in range(nb)]


def _inproj_body(*refs, d, r, f, nb, n_lat, first):
    if first:
        (x_ref, c_ref, mod_ref, g_ref, w_ref, fcs_ref,
         u_ref, gl_ref, sa_ref, sb_ref, pq_ref, xa_ref, t_scr) = refs

        @pl.when(pl.program_id(0) < n_lat)
        def _():
            xa_ref[...] = _to_time_major([x_ref[b] for b in range(nb)], t_scr, nb)

        @pl.when(pl.program_id(0) >= n_lat)
        def _():
            xa_ref[...] = _to_time_major([c_ref[b] for b in range(nb)], t_scr, nb)

        x = xa_ref[...]
    else:
        x_ref, mod_ref, g_ref, w_ref, fcs_ref, u_ref, gl_ref, sa_ref, sb_ref, pq_ref, t_scr = refs
        x = x_ref[...]
    mod = mod_ref[0]
    h = _rms(x, g_ref[...])
    h = _per_batch(h, lambda v: v * (1.0 + mod[:, d:2 * d])[None] + mod[:, 0:d][None]).astype(BF16)
    u_ref[...] = _dot(h, w_ref[:, 0:r]).astype(BF16)
    gl_ref[...] = _gelu_tanh(_dot(h, w_ref[:, r:2 * r])).astype(BF16)
    u4 = _dot(h, w_ref[:, 2 * r:2 * r + f]).astype(BF16)
    for b, blk in enumerate(_to_batch_major(_dot(u4, fcs_ref[...]), t_scr, nb)):
        pq_ref[:, b * 2 * f:(b + 1) * 2 * f] = blk.astype(BF16)
    s3 = 2 * r + f
    sa_ref[...] = jax.nn.sigmoid(_dot(h, w_ref[:, s3:s3 + d])).astype(BF16)
    sb_ref[...] = jax.nn.sigmoid(_dot(h, w_ref[:, s3 + d:s3 + 2 * d])).astype(BF16)


def _inproj(x, mod, g, w_in, fcs, *, n_lat_rows, r, nb, ctx=None):
    first = ctx is not None
    d = x.shape[-1]
    rows = (x.shape[1] + ctx.shape[1]) * nb if first else x.shape[0]
    f = fcs.shape[0]
    n_lat = n_lat_rows // TR
    npos = TR // nb
    row = lambda width: pl.BlockSpec((TR, width), lambda i: (i, 0))
    common = [
        pl.BlockSpec((1, SUBLANES, N_MOD * d), lambda i: (jnp.where(i < n_lat, 0, 1), 0, 0)),
        _full((1, d), 1), _full(w_in.shape, 1), _full(fcs.shape, 1),
    ]
    out_specs = [row(r), row(r), row(d), row(d), pl.BlockSpec((npos, nb * 2 * f), lambda i: (i, 0))]
    out_shape = [
        jax.ShapeDtypeStruct((rows, r), BF16),
        jax.ShapeDtypeStruct((rows, r), BF16),
        jax.ShapeDtypeStruct((rows, d), BF16),
        jax.ShapeDtypeStruct((rows, d), BF16),
        jax.ShapeDtypeStruct((rows // nb, nb * 2 * f), BF16),
    ]
    if first:
        in_specs = [
            pl.BlockSpec((nb, npos, d), lambda i: (0, jnp.minimum(i, n_lat - 1), 0)),
            pl.BlockSpec((nb, npos, d), lambda i: (0, jnp.maximum(i - n_lat, 0), 0)),
        ] + common
        args = [x, ctx]
        out_specs.append(row(d))
        out_shape.append(jax.ShapeDtypeStruct((rows, d), F32))
    else:
        in_specs = [row(d)] + common
        args = [x]
    return pl.pallas_call(
        functools.partial(_inproj_body, d=d, r=r, f=f, nb=nb, n_lat=n_lat, first=first),
        grid=(rows // TR,),
        in_specs=in_specs,
        out_specs=out_specs,
        out_shape=out_shape,
        scratch_shapes=[pltpu.VMEM((d // LANES, TR, LANES), F32)],
        compiler_params=_params(("arbitrary",)),
        name="in_proj",
    )(*args, mod, g.reshape(1, d), w_in, fcs)


def _scan_body(*refs, reverse, merge, nb, n_lat_steps, ctx_len, order):
    if merge:
        (u_ref, up_ref, un_ref, cw_ref, cb_ref, wg_ref, bg_ref, lam_ref, hb_ref, gl_ref,
         o_ref, h_scr, a_scr, b_scr, uc_scr, hs_scr) = refs
    else:
        (u_ref, up_ref, un_ref, cw_ref, cb_ref, wg_ref, bg_ref, lam_ref,
         o_ref, h_scr, a_scr, b_scr, uc_scr, hs_scr) = refs
    nh = wg_ref.shape[0]
    rows = SUB * nb
    trows = TS * nb
    j = pl.program_id(0)

    @pl.when(j == 0)
    def _():
        h_scr[...] = jnp.zeros_like(h_scr)

    blk = order(j)
    n_steps = pl.num_programs(0)
    is_ctx = blk >= n_lat_steps
    keep_prev = jnp.where(is_ctx & (blk > n_lat_steps), 1.0, 0.0)
    keep_next = jnp.where(is_ctx & (blk < n_steps - 1), 1.0, 0.0)
    lat = jnp.where(is_ctx, 0.0, 1.0)
    halo = 2 * nb
    ext = jnp.concatenate([up_ref[...].astype(F32) * keep_prev, u_ref[...].astype(F32),
                           un_ref[...].astype(F32)[0:nb] * keep_next], axis=0)
    uc = cb_ref[...]
    for k, s in CONV_TAPS:
        lo = halo + s * nb
        uc = uc + cw_ref[k:k + 1, :] * ext[lo:lo + trows, :]
    uc_scr[...] = uc
    for p in range(GRID_W, TS, GRID_W):
        at = lambda q: ext[halo + q * nb:halo + (q + 1) * nb, :]
        rows_of = lambda q: pl.ds(q * nb, nb)
        uc_scr[rows_of(p), :] -= lat * (cw_ref[0:1, :] * at(p - 2) + cw_ref[1:2, :] * at(p - 1))
        uc_scr[rows_of(p + 1), :] -= lat * (cw_ref[0:1, :] * at(p - 1))
        uc_scr[rows_of(p - 1), :] -= lat * (cw_ref[3:4, :] * at(p))

    lam = lam_ref[...]
    half_decay = (-0.5 * LRU_C) * (jnp.maximum(-lam, 0.0) + jnp.log1p(jnp.exp(-jnp.abs(lam))))

    subs = range(TS // SUB)
    for sub in (reversed(subs) if reverse else subs):
        r0 = sub * rows
        for g in range(nh):
            ug = uc_scr[r0:r0 + rows, g * LANES:(g + 1) * LANES]
            gates = _dot(ug.astype(BF16), wg_ref[g]) + bg_ref[g]
            log_a = half_decay[g] * jnp.tanh(0.5 * gates[:, :LANES]) + half_decay[g]
            a = jnp.exp(log_a)
            z = (-1.0 - a * a) * jnp.tanh(log_a)
            mult = jnp.where(z > 0.0, z * lax.rsqrt(z), 0.0)
            a_scr[g] = a
            b_scr[g] = mult * (_sigmoid(gates[:, LANES:]) * ug)

        def step(k, hs):
            t = (SUB - 1 - k) if reverse else k
            off = pl.multiple_of(t * nb, nb)
            new = []
            for g in range(nh):
                hg = a_scr[g, pl.ds(off, nb), :] * hs[g] + b_scr[g, pl.ds(off, nb), :]
                hs_scr[pl.ds(r0 + off, nb), g * LANES:(g + 1) * LANES] = hg
                new.append(hg)
            return tuple(new)

        hs = lax.fori_loop(0, SUB, step, tuple(h_scr[g] for g in range(nh)), unroll=8)
        for g in range(nh):
            h_scr[g] = hs[g]

    if merge:
        o_ref[...] = ((hs_scr[...] + hb_ref[...].astype(F32)) * gl_ref[...].astype(F32)).astype(BF16)
    else:
        o_ref[...] = hs_scr[...].astype(BF16)


def _scan(u, conv_w, conv_b, wg, bg, lam, *, nb, n_lat_steps, ctx_len, reverse, hb=None, gl=None):
    rows_all, r = u.shape
    nh = wg.shape[0]
    trows = TS * nb
    n = rows_all // trows
    n_ctx = n - n_lat_steps
    merge = hb is not None
    if reverse:
        order = lambda j: jnp.where(j < n_ctx, n - 1 - j, n_lat_steps - 1 - (j - n_ctx))
    else:
        order = lambda j: jnp.where(j < n_ctx, n_lat_steps + j, j - n_ctx)
    hpb = trows // PACKED_ROWS
    tile = pl.BlockSpec((trows, r), lambda j: (order(j), 0))
    in_specs = [
        tile,
        pl.BlockSpec((PACKED_ROWS, r), lambda j: (jnp.maximum(order(j) * hpb - 1, 0), 0)),
        pl.BlockSpec((PACKED_ROWS, r), lambda j: (jnp.minimum((order(j) + 1) * hpb, n * hpb - 1), 0)),
        _full(conv_w.shape, 1), _full((1, r), 1), _full(wg.shape, 1), _full(bg.shape, 1), _full(lam.shape, 1),
    ]
    args = [u, u, u, conv_w, conv_b.reshape(1, r), wg, bg, lam]
    if merge:
        in_specs += [tile, tile]
        args += [hb, gl]
    return pl.pallas_call(
        functools.partial(_scan_body, reverse=reverse, merge=merge, nb=nb, n_lat_steps=n_lat_steps,
                          ctx_len=ctx_len, order=order),
        grid=(n,),
        in_specs=in_specs,
        out_specs=tile,
        out_shape=jax.ShapeDtypeStruct((rows_all, r), BF16),
        scratch_shapes=[
            pltpu.VMEM((nh, nb, LANES), F32),
            pltpu.VMEM((nh, SUB * nb, LANES), F32),
            pltpu.VMEM((nh, SUB * nb, LANES), F32),
            pltpu.VMEM((trows, r), F32),
            pltpu.VMEM((trows, r), F32),
        ],
        compiler_params=_params(("arbitrary",)),
        name="scan_fwd_merge" if merge else "scan_bwd",
    )(*args)


FOLD_TILE = 256


def _fold_body(a_ref, b1_ref, b2_ref, o_ref, *, f, nb):
    m = pl.program_id(0)
    i = lax.broadcasted_iota(jnp.int32, (FOLD_TILE, 2 * FOLD_TILE), 0)
    col = lax.broadcasted_iota(jnp.int32, (FOLD_TILE, 2 * FOLD_TILE), 1)
    sel = (col == jnp.where(i == 0, FOLD_TILE, FOLD_TILE - i)) & ((i > 0) | (m > 0))
    pad = jnp.zeros((FOLD_TILE - PACKED_ROWS, b1_ref.shape[1]), BF16)
    mirrored = _dot(jnp.where(sel, 1.0, 0.0).astype(BF16),
                    jnp.concatenate([b1_ref[...], b2_ref[...], pad], axis=0))
    for b in range(nb):
        p = slice(b * 2 * f, b * 2 * f + f)
        q = slice(b * 2 * f + f, (b + 1) * 2 * f)
        o_ref[:, p] = (a_ref[:, p].astype(F32) + mirrored[:, p]).astype(BF16)
        o_ref[:, q] = (a_ref[:, q].astype(F32) - mirrored[:, q]).astype(BF16)


def _fold(pq, *, n, nb, f):
    nt = n // FOLD_TILE
    hpt = FOLD_TILE // PACKED_ROWS
    width = nb * 2 * f
    tile = lambda idx: pl.BlockSpec((FOLD_TILE, width), lambda m: (idx(m), 0))
    return pl.pallas_call(
        functools.partial(_fold_body, f=f, nb=nb),
        grid=(nt // 2,),
        in_specs=[tile(lambda m: m), tile(lambda m: nt - 1 - m),
                  pl.BlockSpec((PACKED_ROWS, width), lambda m: (jnp.minimum(nt - m, nt - 1) * hpt, 0))],
        out_specs=tile(lambda m: m),
        out_shape=jax.ShapeDtypeStruct((n // 2, width), BF16),
        compiler_params=_params(("arbitrary",)),
        name="dft_fold",
    )(pq, pq, pq)


def _dft_body(ca_ref, sa_ref, cb_ref, sb_ref, p_ref, q_ref, *rest, kdim, folded, scale):
    o_ref, c_scr, s_scr = rest[-3:]

    @pl.when(pl.program_id(1) == 0)
    def _():
        cb = cb_ref[...]
        sb = sb_ref[...]
        for t1 in range(kdim // LANES):
            ca = ca_ref[:, t1:t1 + 1]
            sa = sa_ref[:, t1:t1 + 1]
            c_scr[:, t1 * LANES:(t1 + 1) * LANES] = (ca * cb - sa * sb).astype(BF16)
            s_scr[:, t1 * LANES:(t1 + 1) * LANES] = (sa * cb + ca * sb).astype(BF16)

    y = _dot(c_scr[...], p_ref[...]) - _dot(s_scr[...], q_ref[...])
    if folded:
        mid_ref = rest[0]
        tmk = o_ref.shape[0]
        k = pl.program_id(0) * tmk + lax.broadcasted_iota(jnp.int32, (tmk, 1), 0)
        sign = (1 - 2 * (k & 1)).astype(F32)
        y = y + sign * (scale * mid_ref[...].astype(F32)[0:1, :])
    o_ref[...] = y.astype(BF16)


def _dft_tables(n):
    k = jnp.arange(n, dtype=jnp.int32)[:, None]
    t1 = jnp.arange(LANES, dtype=jnp.int32)[None, :]
    ang_a = ((k * ((t1 * LANES) % n)) % n).astype(F32) * (2.0 * math.pi / n)
    ang_b = ((k * t1) % n).astype(F32) * (2.0 * math.pi / n)
    scale = 1.0 / math.sqrt(n)
    return jnp.cos(ang_a), jnp.sin(ang_a), jnp.cos(ang_b) * scale, jnp.sin(ang_b) * scale


def _dft(pq, *, n, nb, f, row0, prev=None, folded=None):
    s = pq.shape[0]
    tmk = min(n, 512)
    kdim = n // 2 if folded is not None else n
    tab = pl.BlockSpec((tmk, LANES), lambda m, b: (m, 0))
    if folded is not None:
        assert row0 == 0
        src, rb = folded, 0
    else:
        src, rb = pq, row0 // n
    in_specs = [
        tab, tab, tab, tab,
        pl.BlockSpec((kdim, f), lambda m, b: (rb, 2 * b)),
        pl.BlockSpec((kdim, f), lambda m, b: (rb, 2 * b + 1)),
    ]
    args = list(_dft_tables(n)) + [src, src]
    if folded is not None:
        in_specs.append(pl.BlockSpec((PACKED_ROWS, f), lambda m, b: (kdim // PACKED_ROWS, 2 * b)))
        args.append(pq)
    aliases = {}
    if prev is not None:
        in_specs.append(pl.BlockSpec(memory_space=pl.ANY))
        args.append(prev)
        aliases = {len(args) - 1: 0}
    return pl.pallas_call(
        functools.partial(_dft_body, kdim=kdim, folded=folded is not None, scale=1.0 / math.sqrt(n)),
        grid=(n // tmk, nb),
        in_specs=in_specs,
        out_specs=pl.BlockSpec((tmk, f), lambda m, b: (row0 // tmk + m, b)),
        out_shape=jax.ShapeDtypeStruct((s, nb * f), BF16),
        scratch_shapes=[pltpu.VMEM((tmk, kdim), BF16), pltpu.VMEM((tmk, kdim), BF16)],
        input_output_aliases=aliases,
        compiler_params=_params(("arbitrary", "arbitrary")),
        name="pos_dft",
    )(*args)


def _top2(vals):
    def first_max(vs):
        m = functools.reduce(jnp.maximum, vs)
        idx = jnp.full(m.shape, len(vs) - 1, jnp.int32)
        for k in range(len(vs) - 2, -1, -1):
            idx = jnp.where(vs[k] == m, k, idx)
        return m, idx
    m1, i1 = first_max(vals)
    rest = [jnp.where(i1 == k, -jnp.inf, v) for k, v in enumerate(vals)]
    m2, i2 = first_max(rest)
    return m1, i1, m2, i2


def _pick(idx, vals):
    out = vals[-1]
    for k in range(len(vals) - 2, -1, -1):
        out = jnp.where(idx == k, vals[k], out)
    return out


def _route_window(h2, rw, rb, n_exp):
    aff = jax.nn.sigmoid(_dot3(h2, rw).T[0:n_exp, :])
    sel = aff + rb
    per = n_exp // N_EXPERT_GROUPS
    srow = [sel[e:e + 1, :] for e in range(n_exp)]
    arow = [aff[e:e + 1, :] for e in range(n_exp)]
    scores = []
    for gi in range(N_EXPERT_GROUPS):
        m1, _, m2, _ = _top2(srow[gi * per:(gi + 1) * per])
        scores.append(m1 + m2)
    best = functools.reduce(jnp.maximum, scores)
    grp = jnp.full(best.shape, N_EXPERT_GROUPS - 1, jnp.int32)
    for gi in range(N_EXPERT_GROUPS - 2, -1, -1):
        grp = jnp.where(scores[gi] == best, gi, grp)
    v = [_pick(grp, [srow[gi * per + k] for gi in range(N_EXPERT_GROUPS)]) for k in range(per)]
    a = [_pick(grp, [arow[gi * per + k] for gi in range(N_EXPERT_GROUPS)]) for k in range(per)]
    _, i1, _, i2 = _top2(v)
    w1 = _pick(i1, a)
    w2 = _pick(i2, a)
    den = w1 + w2
    e1 = grp * per + i1
    e2 = grp * per + i2

    erow = lax.broadcasted_iota(jnp.int32, (n_exp, TM), 0)
    hit1 = erow == e1
    hit2 = erow == e2
    onehot = jnp.where(hit1 | hit2, 1.0, 0.0)
    cnt = jnp.sum(onehot, axis=1, keepdims=True)
    before = (lax.broadcasted_iota(jnp.int32, (TM, TM), 0)
              < lax.broadcasted_iota(jnp.int32, (TM, TM), 1)).astype(BF16)
    rank = _dot(onehot.astype(BF16), before)
    padded = ((cnt.astype(jnp.int32) + (SUBLANES - 1)) & (-SUBLANES)).astype(F32)
    lower = (lax.broadcasted_iota(jnp.int32, (n_exp, n_exp), 1)
             < lax.broadcasted_iota(jnp.int32, (n_exp, n_exp), 0)).astype(BF16)
    starts = _dot(lower, jnp.broadcast_to(padded, (n_exp, LANES)).astype(BF16))
    pos = starts[:, 0:1] + rank
    r1 = jnp.sum(jnp.where(hit1, pos, 0.0), axis=0, keepdims=True)
    r2 = jnp.sum(jnp.where(hit2, pos, 0.0), axis=0, keepdims=True)
    route = jnp.concatenate([r1, r2, w1 / den, w2 / den, jnp.zeros((LANES - 4, TM), F32)], axis=0)
    return route, jnp.broadcast_to(cnt, (n_exp, LANES))


def _mix_body(hg_ref, yf_ref, sa_ref, sb_ref, x_ref, mod_ref, wpr_ref, wpf_ref, wo_ref,
              gffn_ref, rw_ref, rb_ref, xo_ref, xl_ref, rc_ref, cnt_ref, t_scr, *, d, n_exp, nb):
    mod = mod_ref[0]
    f = wpf_ref.shape[0]
    y_r = _dot(hg_ref[...], wpr_ref[...])
    yf = _to_time_major([yf_ref[:, b * f:(b + 1) * f] for b in range(nb)], t_scr, nb)
    y_f = _dot(yf.astype(BF16), wpf_ref[...])
    merged = sa_ref[...].astype(F32) * y_r + sb_ref[...].astype(F32) * y_f
    out = _dot(merged.astype(BF16), wo_ref[...])
    xn = x_ref[...] + _per_batch(out, lambda v: v * mod[:, 2 * d:3 * d][None])
    xo_ref[...] = xn
    h2 = _rms(xn, gffn_ref[...])
    h2 = _per_batch(h2, lambda v: v * (1.0 + mod[:, 4 * d:5 * d])[None] + mod[:, 3 * d:4 * d][None])
    rows = _local_rows(n_exp)
    riota = lax.broadcasted_iota(jnp.int32, (rows, TM), 0)
    for k in range(TR // TM):
        hw = h2[k * TM:(k + 1) * TM, :]
        route, cnt = _route_window(hw, rw_ref[...], rb_ref[...], n_exp)
        rc_ref[k * TM:(k + 1) * TM, :] = route.T
        cnt_ref[k] = cnt
        r12 = route[0:2, :].astype(jnp.int32)
        p = jnp.where((riota == r12[0:1, :]) | (riota == r12[1:2, :]), 1.0, 0.0).astype(BF16)
        xl_ref[k * rows:(k + 1) * rows, :] = _pack_bf16_pairs(_dot(p, hw.astype(BF16)))


def _mix(hg, yf, sa, sb, xa, mod, wpr, wpf, wo, gffn, rw, rb, *, n_rows, n_lat_rows, nb):
    rows_all, d = xa.shape
    f = wpf.shape[0]
    n_exp = rb.shape[0]
    n_lat = n_lat_rows // TR
    wpt = TR // TM
    n_win = n_rows // TM
    lrows = _local_rows(n_exp)
    row = lambda width: pl.BlockSpec((TR, width), lambda i: (i, 0))
    return pl.pallas_call(
        functools.partial(_mix_body, d=d, n_exp=n_exp, nb=nb),
        grid=(n_rows // TR,),
        in_specs=[
            row(d), pl.BlockSpec((TR // nb, nb * f), lambda i: (i, 0)), row(d), row(d), row(d),
            pl.BlockSpec((1, SUBLANES, N_MOD * d), lambda i: (jnp.where(i < n_lat, 0, 1), 0, 0)),
            _full(wpr.shape, 1), _full(wpf.shape, 1), _full(wo.shape, 1),
            _full((1, d), 1), _full(rw.shape, 1), _full((n_exp, 1), 1),
        ],
        scratch_shapes=[pltpu.VMEM((f // LANES, TR, LANES), F32)],
        out_specs=[
            row(d),
            pl.BlockSpec((wpt * lrows, d // 2), lambda i: (i, 0)),
            row(LANES),
            pl.BlockSpec((wpt, n_exp, LANES), lambda i: (i, 0, 0)),
        ],
        out_shape=[
            jax.ShapeDtypeStruct((rows_all, d), F32),
            jax.ShapeDtypeStruct((n_win * lrows, d // 2), jnp.uint32),
            jax.ShapeDtypeStruct((rows_all, LANES), F32),
            jax.ShapeDtypeStruct((n_win, n_exp, LANES), F32),
        ],
        compiler_params=_params(("arbitrary",)),
        name="mix_out",
    )(hg, yf, sa, sb, xa, mod, wpr, wpf, wo, gffn.reshape(1, d), rw, rb.reshape(n_exp, 1))


def _local_rows(n_exp):
    return -(-(2 * TM + n_exp * (SUBLANES - 1)) // LANES) * LANES


def _pack_bf16_pairs(x):
    half = x.shape[1] // 2
    bits = lax.bitcast_convert_type(x, jnp.uint32)
    return (bits[:, :half] & jnp.uint32(0xFFFF0000)) | lax.shift_right_logical(bits[:, half:], jnp.uint32(16))


def _unpack_bf16_pairs(u):
    hi = lax.bitcast_convert_type(u & jnp.uint32(0xFFFF0000), F32)
    lo = lax.bitcast_convert_type(lax.shift_left(u, jnp.uint32(16)), F32)
    return jnp.concatenate([hi, lo], axis=-1)


CPT = MOE_TM // SUBLANES


def _group_copies(table_ref, hbm, buf, sem, *, gather):
    for j in range(CPT):
        far = hbm.at[pl.ds(pl.multiple_of(table_ref[0, 0, j] * SUBLANES, SUBLANES), SUBLANES)]
        near = buf.at[pl.ds(j * SUBLANES, SUBLANES)]
        (pltpu.make_async_copy(far, near, sem) if gather else pltpu.make_async_copy(near, far, sem)).start()


def _tile_wait(hbm, buf, sem, *, gather):
    far = hbm.at[pl.ds(0, MOE_TM)]
    (pltpu.make_async_copy(far, buf, sem) if gather else pltpu.make_async_copy(buf, far, sem)).wait()


def _gmm_body(te_ref, nu_ref, src_ref, nxt_ref, dst_ref, xl_ref, w1_ref, w3_ref, w2_ref, yl_ref,
              w1b, w3b, w2b, xbuf, ybuf, gsem, ssem):
    t = pl.program_id(0)
    n_used = nu_ref[0]
    slot = lax.rem(t, 2)
    other = 1 - slot

    @pl.when((t == 0) | (te_ref[t] != te_ref[jnp.maximum(t - 1, 0)]))
    def _():
        w1b[...] = w1_ref[0, 0].astype(BF16)
        w3b[...] = w3_ref[0, 0].astype(BF16)
        w2b[...] = w2_ref[0, 0].astype(BF16)

    @pl.when(t < n_used)
    def _():
        @pl.when(t == 0)
        def _():
            _group_copies(src_ref, xl_ref, xbuf.at[slot], gsem.at[slot], gather=True)

        _tile_wait(xl_ref, xbuf.at[slot], gsem.at[slot], gather=True)
        _group_copies(nxt_ref, xl_ref, xbuf.at[other], gsem.at[other], gather=True)
        x = _unpack_bf16_pairs(xbuf[slot]).astype(BF16)
        a = _dot(x, w1b[...])
        h = (a * jax.nn.sigmoid(a)) * _dot(x, w3b[...])
        y = _dot(h.astype(BF16), w2b[...])

        @pl.when(t >= 2)
        def _():
            _tile_wait(yl_ref, ybuf.at[slot], ssem.at[slot], gather=False)

        ybuf[slot] = _pack_bf16_pairs(y.astype(BF16).astype(F32))
        _group_copies(dst_ref, yl_ref, ybuf.at[slot], ssem.at[slot], gather=False)

        @pl.when(t == n_used - 1)
        def _():
            _tile_wait(xl_ref, xbuf.at[other], gsem.at[other], gather=True)

            @pl.when(t >= 1)
            def _():
                _tile_wait(yl_ref, ybuf.at[other], ssem.at[other], gather=False)

            _tile_wait(yl_ref, ybuf.at[slot], ssem.at[slot], gather=False)


def _gmm(tile_exp, n_used, src, dst, xl, w1, w3, w2, *, layer, out_rows):
    n_tiles = src.shape[0]
    d, de = w1.shape[2], w1.shape[3]
    wspec = lambda w: pl.BlockSpec((1, 1) + w.shape[2:], lambda t, te, nu: (layer, te[t], 0, 0))
    table = lambda idx: pl.BlockSpec((1, 1, CPT), lambda t, te, nu: (idx(t, nu), 0, 0), memory_space=pltpu.SMEM)
    cur = lambda t, nu: jnp.minimum(t, nu[0] - 1)
    grid_spec = pltpu.PrefetchScalarGridSpec(
        num_scalar_prefetch=2,
        grid=(n_tiles,),
        in_specs=[table(cur), table(lambda t, nu: jnp.minimum(t + 1, nu[0] - 1)), table(cur),
                  pl.BlockSpec(memory_space=pl.ANY), wspec(w1), wspec(w3), wspec(w2)],
        out_specs=pl.BlockSpec(memory_space=pl.ANY),
        scratch_shapes=[
            pltpu.VMEM((d, de), BF16), pltpu.VMEM((d, de), BF16), pltpu.VMEM((de, d), BF16),
            pltpu.VMEM((2, MOE_TM, d // 2), jnp.uint32), pltpu.VMEM((2, MOE_TM, d // 2), jnp.uint32),
            pltpu.SemaphoreType.DMA((2,)), pltpu.SemaphoreType.DMA((2,)),
        ],
    )
    return pl.pallas_call(
        _gmm_body,
        grid_spec=grid_spec,
        out_shape=jax.ShapeDtypeStruct((out_rows, d // 2), jnp.uint32),
        compiler_params=_params(("arbitrary",)),
        name="moe_experts",
    )(tile_exp, n_used, src, src, dst, xl, w1, w3, w2)


def _combine_body(*refs, d, n_exp, n_win, rows, nb, final):
    if final:
        tot_ref, yl_ref, rc_ref, xr_ref, mod_ref, gfin_ref, o_ref, t_scr = refs
    else:
        tot_ref, yl_ref, rc_ref, xr_ref, mod_ref, o_ref = refs
    keep = lax.broadcasted_iota(jnp.int32, (rows, 1), 0) < tot_ref[pl.program_id(0)]
    y = _unpack_bf16_pairs(jnp.where(keep, yl_ref[...], jnp.uint32(0))).astype(BF16)
    rc = rc_ref[...]
    ci = lax.broadcasted_iota(jnp.int32, (TM, rows), 1)
    pw = (jnp.where(ci == rc[:, 0:1].astype(jnp.int32), rc[:, 2:3], 0.0)
          + jnp.where(ci == rc[:, 1:2].astype(jnp.int32), rc[:, 3:4], 0.0))
    moe = _dot(pw.astype(BF16), y)
    g2 = mod_ref[0][:, 5 * d:6 * d]
    xn = xr_ref[...] + _per_batch(moe, lambda v: v * g2[None])
    if final:
        xn = _rms(xn, gfin_ref[...])
        for g in range(d // LANES):
            t_scr[g] = xn[:, g * LANES:(g + 1) * LANES]
        for b in range(nb):
            o_ref[b] = jnp.concatenate(
                [t_scr[g, pl.ds(b, TM // nb, stride=nb), :] for g in range(d // LANES)], axis=-1)
    else:
        o_ref[...] = xn


def _combine(totals, yl, rc, xres, mod, *, n_exp, n_win, n_lat_rows, nb, gfin=None, seq_out=None):
    d = xres.shape[1]
    rows = _local_rows(n_exp)
    final = gfin is not None
    n_lat = n_lat_rows // TM
    row = lambda width: pl.BlockSpec((TM, width), lambda w, a: (w, 0))
    in_specs = [pl.BlockSpec((rows, d // 2), lambda w, a: (w, 0)), row(LANES), row(d),
                pl.BlockSpec((1, SUBLANES, N_MOD * d), lambda w, a: (jnp.where(w < n_lat, 0, 1), 0, 0))]
    args = [totals, yl, rc, xres, mod]
    scratch = []
    if final:
        in_specs.append(pl.BlockSpec((1, d), lambda w, a: (0, 0)))
        args.append(gfin.reshape(1, d))
        out_shape = jax.ShapeDtypeStruct((nb, seq_out, d), F32)
        out_specs = pl.BlockSpec((nb, TM // nb, d), lambda w, a: (0, w, 0))
        scratch.append(pltpu.VMEM((d // LANES, TM, LANES), F32))
        aliases = {}
    else:
        out_shape = jax.ShapeDtypeStruct(xres.shape, F32)
        out_specs = row(d)
        aliases = {3: 0}
    grid_spec = pltpu.PrefetchScalarGridSpec(
        num_scalar_prefetch=1, grid=(n_win,), in_specs=in_specs, out_specs=out_specs, scratch_shapes=scratch)
    return pl.pallas_call(
        functools.partial(_combine_body, d=d, n_exp=n_exp, n_win=n_win, rows=rows, nb=nb, final=final),
        grid_spec=grid_spec,
        out_shape=out_shape,
        input_output_aliases=aliases,
        compiler_params=_params(("arbitrary",)),
        name="moe_combine",
    )(*args)


def _route_tables(cnt, *, n_exp, n_tiles):
    cnt = cnt[..., 0].astype(jnp.int32)
    n_win = cnt.shape[0]
    lrows = _local_rows(n_exp)
    npad = (cnt + (SUBLANES - 1)) & (-SUBLANES)
    seg = -(-jnp.sum(npad, axis=0) // MOE_TM) * MOE_TM
    ends = jnp.cumsum(seg)
    goff = (ends - seg)[None, :] + jnp.cumsum(npad, axis=0) - npad
    loff = jnp.cumsum(npad, axis=1) - npad
    tile_ends = ends // MOE_TM
    n_used = tile_ends[-1]
    tile = jnp.minimum(jnp.arange(n_tiles, dtype=jnp.int32), n_used - 1)
    tile_exp = jnp.sum(tile[:, None] >= tile_ends[None, :], axis=1).astype(jnp.int32)

    starts = (goff.T // SUBLANES).reshape(-1)
    lens = (npad.T // SUBLANES).reshape(-1)
    local = ((jnp.arange(n_win, dtype=jnp.int32)[:, None] * lrows + loff).T // SUBLANES).reshape(-1)
    group = jnp.arange(n_tiles * CPT, dtype=jnp.int32)
    began = starts[None, :] <= group[:, None]
    last = lambda v: jnp.sum(jnp.where(began, jnp.diff(v, prepend=0)[None, :], 0), axis=1)
    valid = group < last(starts + lens)
    src = jnp.where(valid, group + last(local - starts), 0)
    dst = jnp.where(valid, src, n_win * lrows // SUBLANES + group)
    shape = (n_tiles, 1, CPT)
    return (tile_exp, n_used.reshape(1).astype(jnp.int32), src.reshape(shape).astype(jnp.int32),
            dst.reshape(shape).astype(jnp.int32), jnp.sum(npad, axis=1).astype(jnp.int32))


def _channel_dft_tables(n):
    k = jnp.arange(n, dtype=jnp.int32)
    ang = ((k[:, None] * k[None, :]) % n).astype(F32) * (2.0 * math.pi / n)
    scale = 1.0 / math.sqrt(n)
    return jnp.cos(ang) * scale, jnp.sin(ang) * scale


def kernel(x, c, ctx, c_ctx, ada_w, ada_b, norm_mix_g, w_in, conv_w, conv_b, lru_w_a, lru_b_a, lru_w_x, lru_b_x, lru_lambda, w_proj_rnn, w_proj_fourier, w_out, norm_ffn_g, router_w, router_b, moe_w1, moe_w3, moe_w2, final_norm_g):
    nb, l, d = x.shape
    lc = ctx.shape[1]
    depth = ada_w.shape[0]
    r = conv_w.shape[2]
    f = w_proj_fourier.shape[1]
    nh, blk = lru_w_a.shape[2], lru_w_a.shape[3]
    assert nb == SUBLANES and blk == LANES and nb < MOD_ROWS
    assert l % TS == 0 and lc % TS == 0 and TS % GRID_W == 0 and (l * nb) % TR == 0 and (lc * nb) % TR == 0
    assert GRID_W & (GRID_W - 1) == 0 and lc & (lc - 1) == 0 and l % lc == 0 and lc <= 512
    s = l + lc
    n_lat_rows = l * nb

    cc = jnp.zeros((MOD_ROWS, d), F32).at[:nb].set(c).at[nb].set(c_ctx)
    mod = _ada(cc, ada_w, ada_b)
    mod = jnp.stack([mod[:, :nb], jnp.broadcast_to(mod[:, nb:nb + 1], (depth, nb, N_MOD * d))], axis=1)

    cch, sch = _channel_dft_tables(f // FOURIER_GROUPS)
    eye = jnp.eye(FOURIER_GROUPS, dtype=F32)
    fcs = jnp.concatenate([jnp.kron(eye, cch), jnp.kron(eye, sch)], axis=1).astype(BF16)
    n_exp = router_w.shape[1]
    rw = jnp.pad(router_w, ((0, 0), (0, LANES - n_exp)))

    out = xa = None
    for li in range(depth):
        last = li == depth - 1
        inproj = functools.partial(_inproj, mod=mod[li], g=norm_mix_g[li], w_in=w_in[li].astype(BF16), fcs=fcs,
                                   n_lat_rows=n_lat_rows, r=r, nb=nb)
        if li == 0:
            u, gl, sa, sb, pq, xa = inproj(x, ctx=ctx)
        else:
            u, gl, sa, sb, pq = inproj(xa)
        wg = [jnp.concatenate([lru_w_a[li, k], lru_w_x[li, k]], axis=-1).astype(BF16) for k in range(2)]
        bg = [jnp.concatenate([lru_b_a[li, k], lru_b_x[li, k]], axis=-1).reshape(nh, 1, 2 * blk) for k in range(2)]
        lam = [lru_lambda[li, k].reshape(nh, 1, blk) for k in range(2)]
        scan = functools.partial(_scan, u, conv_w[li], conv_b[li], nb=nb, n_lat_steps=l // TS, ctx_len=lc)
        hb = scan(wg[1], bg[1], lam[1], reverse=True)
        hg = scan(wg[0], bg[0], lam[0], reverse=False, hb=hb, gl=gl)
        yf = _dft(pq, n=l, nb=nb, f=f, row0=0, folded=_fold(pq, n=l, nb=nb, f=f))
        if not last:
            yf = _dft(pq, n=lc, nb=nb, f=f, row0=l, prev=yf)
        n_rows = n_lat_rows if last else s * nb
        x1, xl, rc, cnt = _mix(hg, yf, sa, sb, xa, mod[li], w_proj_rnn[li].astype(BF16),
                               w_proj_fourier[li].astype(BF16), w_out[li].astype(BF16), norm_ffn_g[li],
                               rw, router_b, n_rows=n_rows, n_lat_rows=n_lat_rows, nb=nb)
        n_win = n_rows // TM
        n_tiles = -(-(n_win * (2 * TM + n_exp * (SUBLANES - 1)) + n_exp * MOE_TM) // MOE_TM)
        tile_exp, n_used, src, dst, totals = _route_tables(cnt, n_exp=n_exp, n_tiles=n_tiles)
        yl = _gmm(tile_exp, n_used, src, dst, xl, moe_w1, moe_w3, moe_w2, layer=li,
                  out_rows=xl.shape[0] + n_tiles * MOE_TM)
        comb = functools.partial(_combine, totals, yl, rc, x1, mod[li], n_exp=n_exp, n_win=n_win,
                                 n_lat_rows=n_lat_rows, nb=nb)
        if last:
            out = comb(gfin=final_norm_g, seq_out=l)
        else:
            xa = comb()
    return out
```

```python
import functools
import math

import jax
import jax.numpy as jnp
from jax import lax
from jax.experimental import pallas as pl
from jax.experimental.pallas import tpu as pltpu

F32 = jnp.float32
BF16 = jnp.bfloat16

RMS_EPS = 1e-6
LRU_C = 8.0
GRID_W = 64
CONV_TAPS = ((0, -2), (1, -1), (2, 0), (3, 1))
N_MOD = 6
FOURIER_GROUPS = 4
N_EXPERT_GROUPS = 4
LANES = 128
SUBLANES = 8
PACKED_ROWS = 16
TR = 512
TM = 256
TS = 128
SUB = 64
MOE_TM = 512
MOD_ROWS = 16
VMEM_LIMIT = 56 * 1024 * 1024


def _dot(a, b):
    return jnp.dot(a, b, preferred_element_type=F32)


def _split(a):
    hi = a.astype(BF16)
    lo = (a - hi.astype(F32)).astype(BF16)
    return hi, lo


def _dot3(a, b):
    ah, al = _split(a)
    bh, bl = _split(b)
    return _dot(ah, bh) + _dot(al, bh) + _dot(ah, bl)


def _gelu_tanh(x):
    return 0.5 * x * (1.0 + jnp.tanh(math.sqrt(2.0 / math.pi) * (x + 0.044715 * (x * x * x))))


def _sigmoid(x):
    return 0.5 * jnp.tanh(0.5 * x) + 0.5


def _rms(x, g):
    return x * lax.rsqrt(jnp.mean(x * x, axis=-1, keepdims=True) + RMS_EPS) * g


def _per_batch(x, fn):
    rows, d = x.shape
    return fn(x.reshape(rows // SUBLANES, SUBLANES, d)).reshape(rows, d)


def _params(sem, vmem=VMEM_LIMIT):
    return pltpu.CompilerParams(dimension_semantics=sem, vmem_limit_bytes=vmem)


def _full(shape):
    zeros = (0,) * len(shape)
    return pl.BlockSpec(shape, lambda *_: zeros)


def _ada_body(cc_ref, w_ref, b_ref, o_ref):
    cc = cc_ref[...]
    s = cc * jax.nn.sigmoid(cc)
    o_ref[0] = _dot3(s, w_ref[0]) + b_ref[0]


def _ada(cc, ada_w, ada_b):
    depth, d, n = ada_w.shape
    tn = 1024
    return pl.pallas_call(
        _ada_body,
        grid=(depth, n // tn),
        in_specs=[
            pl.BlockSpec((MOD_ROWS, d), lambda l, j: (0, 0)),
            pl.BlockSpec((1, d, tn), lambda l, j: (l, 0, j)),
            pl.BlockSpec((1, 1, tn), lambda l, j: (l, 0, j)),
        ],
        out_specs=pl.BlockSpec((1, MOD_ROWS, tn), lambda l, j: (l, 0, j)),
        out_shape=jax.ShapeDtypeStruct((depth, MOD_ROWS, n), F32),
        compiler_params=_params(("arbitrary", "arbitrary")),
        name="ada_mod",
    )(cc, ada_w, ada_b.reshape(depth, 1, n))


def _to_time_major(blocks, scr, nb):
    npos, width = blocks[0].shape
    for b in range(nb):
        for g in range(width // LANES):
            scr[g, pl.ds(b, npos, stride=nb), :] = blocks[b][:, g * LANES:(g + 1) * LANES].astype(F32)
    return jnp.concatenate([scr[g] for g in range(width // LANES)], axis=-1)


def _to_batch_major(v, scr, nb):
    rows, width = v.shape
    for g in range(width // LANES):
        scr[g] = v[:, g * LANES:(g + 1) * LANES]
    return [jnp.concatenate([scr[g, pl.ds(b, rows // nb, stride=nb), :] for g in range(width // LANES)], axis=-1)
            for b in range(nb)]


def _inproj_body(*refs, d, r, f, nb, n_lat, first):
    if first:
        (x_ref, c_ref, mod_ref, g_ref, w_ref, fcs_ref,
         u_ref, gl_ref, sa_ref, sb_ref, pq_ref, xa_ref, t_scr) = refs

        @pl.when(pl.program_id(0) < n_lat)
        def _():
            xa_ref[...] = _to_time_major([x_ref[b] for b in range(nb)], t_scr, nb)

        @pl.when(pl.program_id(0) >= n_lat)
        def _():
            xa_ref[...] = _to_time_major([c_ref[b] for b in range(nb)], t_scr, nb)

        x = xa_ref[...]
    else:
        x_ref, mod_ref, g_ref, w_ref, fcs_ref, u_ref, gl_ref, sa_ref, sb_ref, pq_ref, t_scr = refs
        x = x_ref[...]
    mod = mod_ref[0]
    h = _rms(x, g_ref[...])
    h = _per_batch(h, lambda v: v * (1.0 + mod[:, d:2 * d])[None] + mod[:, 0:d][None]).astype(BF16)
    u_ref[...] = _dot(h, w_ref[:, 0:r]).astype(BF16)
    gl_ref[...] = _gelu_tanh(_dot(h, w_ref[:, r:2 * r])).astype(BF16)
    u4 = _dot(h, w_ref[:, 2 * r:2 * r + f]).astype(BF16)
    for b, blk in enumerate(_to_batch_major(_dot(u4, fcs_ref[...]), t_scr, nb)):
        pq_ref[:, b * 2 * f:(b + 1) * 2 * f] = blk.astype(BF16)
    s3 = 2 * r + f
    sa_ref[...] = jax.nn.sigmoid(_dot(h, w_ref[:, s3:s3 + d])).astype(BF16)
    sb_ref[...] = jax.nn.sigmoid(_dot(h, w_ref[:, s3 + d:s3 + 2 * d])).astype(BF16)


def _inproj(x, mod, g, w_in, fcs, *, n_lat_rows, r, nb, ctx=None):
    first = ctx is not None
    d = x.shape[-1]
    rows = (x.shape[1] + ctx.shape[1]) * nb if first else x.shape[0]
    f = fcs.shape[0]
    n_lat = n_lat_rows // TR
    npos = TR // nb
    row = lambda width: pl.BlockSpec((TR, width), lambda i: (i, 0))
    common = [
        pl.BlockSpec((1, SUBLANES, N_MOD * d), lambda i: (jnp.where(i < n_lat, 0, 1), 0, 0)),
        _full((1, d)), _full(w_in.shape), _full(fcs.shape),
    ]
    out_specs = [row(r), row(r), row(d), row(d), pl.BlockSpec((npos, nb * 2 * f), lambda i: (i, 0))]
    out_shape = [
        jax.ShapeDtypeStruct((rows, r), BF16),
        jax.ShapeDtypeStruct((rows, r), BF16),
        jax.ShapeDtypeStruct((rows, d), BF16),
        jax.ShapeDtypeStruct((rows, d), BF16),
        jax.ShapeDtypeStruct((rows // nb, nb * 2 * f), BF16),
    ]
    if first:
        in_specs = [
            pl.BlockSpec((nb, npos, d), lambda i: (0, jnp.minimum(i, n_lat - 1), 0)),
            pl.BlockSpec((nb, npos, d), lambda i: (0, jnp.maximum(i - n_lat, 0), 0)),
        ] + common
        args = [x, ctx]
        out_specs.append(row(d))
        out_shape.append(jax.ShapeDtypeStruct((rows, d), F32))
    else:
        in_specs = [row(d)] + common
        args = [x]
    return pl.pallas_call(
        functools.partial(_inproj_body, d=d, r=r, f=f, nb=nb, n_lat=n_lat, first=first),
        grid=(rows // TR,),
        in_specs=in_specs,
        out_specs=out_specs,
        out_shape=out_shape,
        scratch_shapes=[pltpu.VMEM((d // LANES, TR, LANES), F32)],
        compiler_params=_params(("arbitrary",)),
        name="in_proj",
    )(*args, mod, g.reshape(1, d), w_in, fcs)


def _scan_body(*refs, reverse, merge, nb, n_lat_steps, ctx_len, order):
    if merge:
        (u_ref, up_ref, un_ref, cw_ref, cb_ref, wg_ref, bg_ref, lam_ref, hb_ref, gl_ref,
         o_ref, h_scr, a_scr, b_scr, uc_scr, hs_scr) = refs
    else:
        (u_ref, up_ref, un_ref, cw_ref, cb_ref, wg_ref, bg_ref, lam_ref,
         o_ref, h_scr, a_scr, b_scr, uc_scr, hs_scr) = refs
    nh = wg_ref.shape[0]
    rows = SUB * nb
    trows = TS * nb
    j = pl.program_id(0)

    @pl.when(j == 0)
    def _():
        h_scr[...] = jnp.zeros_like(h_scr)

    blk = order(j)
    n_steps = pl.num_programs(0)
    is_ctx = blk >= n_lat_steps
    keep_prev = jnp.where(is_ctx & (blk > n_lat_steps), 1.0, 0.0)
    keep_next = jnp.where(is_ctx & (blk < n_steps - 1), 1.0, 0.0)
    lat = jnp.where(is_ctx, 0.0, 1.0)
    halo = 2 * nb
    ext = jnp.concatenate([up_ref[...].astype(F32) * keep_prev, u_ref[...].astype(F32),
                           un_ref[...].astype(F32)[0:nb] * keep_next], axis=0)
    uc = cb_ref[...]
    for k, s in CONV_TAPS:
        lo = halo + s * nb
        uc = uc + cw_ref[k:k + 1, :] * ext[lo:lo + trows, :]
    uc_scr[...] = uc
    for p in range(GRID_W, TS, GRID_W):
        at = lambda q: ext[halo + q * nb:halo + (q + 1) * nb, :]
        rows_of = lambda q: pl.ds(q * nb, nb)
        uc_scr[rows_of(p), :] -= lat * (cw_ref[0:1, :] * at(p - 2) + cw_ref[1:2, :] * at(p - 1))
        uc_scr[rows_of(p + 1), :] -= lat * (cw_ref[0:1, :] * at(p - 1))
        uc_scr[rows_of(p - 1), :] -= lat * (cw_ref[3:4, :] * at(p))

    lam = lam_ref[...]
    half_decay = (-0.5 * LRU_C) * (jnp.maximum(-lam, 0.0) + jnp.log1p(jnp.exp(-jnp.abs(lam))))

    subs = range(TS // SUB)
    for sub in (reversed(subs) if reverse else subs):
        r0 = sub * rows
        for g in range(nh):
            ug = uc_scr[r0:r0 + rows, g * LANES:(g + 1) * LANES]
            gates = _dot(ug.astype(BF16), wg_ref[g]) + bg_ref[g]
            log_a = half_decay[g] * jnp.tanh(0.5 * gates[:, :LANES]) + half_decay[g]
            a = jnp.exp(log_a)
            z = (-1.0 - a * a) * jnp.tanh(log_a)
            mult = jnp.where(z > 0.0, z * lax.rsqrt(z), 0.0)
            a_scr[g] = a
            b_scr[g] = mult * (_sigmoid(gates[:, LANES:]) * ug)

        def step(k, hs):
            t = (SUB - 1 - k) if reverse else k
            off = pl.multiple_of(t * nb, nb)
            new = []
            for g in range(nh):
                hg = a_scr[g, pl.ds(off, nb), :] * hs[g] + b_scr[g, pl.ds(off, nb), :]
                hs_scr[pl.ds(r0 + off, nb), g * LANES:(g + 1) * LANES] = hg
                new.append(hg)
            return tuple(new)

        hs = lax.fori_loop(0, SUB, step, tuple(h_scr[g] for g in range(nh)), unroll=8)
        for g in range(nh):
            h_scr[g] = hs[g]

    if merge:
        o_ref[...] = ((hs_scr[...] + hb_ref[...].astype(F32)) * gl_ref[...].astype(F32)).astype(BF16)
    else:
        o_ref[...] = hs_scr[...].astype(BF16)


def _scan(u, conv_w, conv_b, wg, bg, lam, *, nb, n_lat_steps, ctx_len, reverse, hb=None, gl=None):
    rows_all, r = u.shape
    nh = wg.shape[0]
    trows = TS * nb
    n = rows_all // trows
    n_ctx = n - n_lat_steps
    merge = hb is not None
    if reverse:
        order = lambda j: jnp.where(j < n_ctx, n - 1 - j, n_lat_steps - 1 - (j - n_ctx))
    else:
        order = lambda j: jnp.where(j < n_ctx, n_lat_steps + j, j - n_ctx)
    hpb = trows // PACKED_ROWS
    tile = pl.BlockSpec((trows, r), lambda j: (order(j), 0))
    in_specs = [
        tile,
        pl.BlockSpec((PACKED_ROWS, r), lambda j: (jnp.maximum(order(j) * hpb - 1, 0), 0)),
        pl.BlockSpec((PACKED_ROWS, r), lambda j: (jnp.minimum((order(j) + 1) * hpb, n * hpb - 1), 0)),
        _full(conv_w.shape), _full((1, r)), _full(wg.shape), _full(bg.shape), _full(lam.shape),
    ]
    args = [u, u, u, conv_w, conv_b.reshape(1, r), wg, bg, lam]
    if merge:
        in_specs += [tile, tile]
        args += [hb, gl]
    return pl.pallas_call(
        functools.partial(_scan_body, reverse=reverse, merge=merge, nb=nb, n_lat_steps=n_lat_steps,
                          ctx_len=ctx_len, order=order),
        grid=(n,),
        in_specs=in_specs,
        out_specs=tile,
        out_shape=jax.ShapeDtypeStruct((rows_all, r), BF16),
        scratch_shapes=[
            pltpu.VMEM((nh, nb, LANES), F32),
            pltpu.VMEM((nh, SUB * nb, LANES), F32),
            pltpu.VMEM((nh, SUB * nb, LANES), F32),
            pltpu.VMEM((trows, r), F32),
            pltpu.VMEM((trows, r), F32),
        ],
        compiler_params=_params(("arbitrary",)),
        name="scan_fwd_merge" if merge else "scan_bwd",
    )(*args)


FOLD_TILE = 256


def _fold_body(a_ref, b1_ref, b2_ref, o_ref, *, f, nb):
    m = pl.program_id(0)
    i = lax.broadcasted_iota(jnp.int32, (FOLD_TILE, 2 * FOLD_TILE), 0)
    col = lax.broadcasted_iota(jnp.int32, (FOLD_TILE, 2 * FOLD_TILE), 1)
    sel = (col == jnp.where(i == 0, FOLD_TILE, FOLD_TILE - i)) & ((i > 0) | (m > 0))
    pad = jnp.zeros((FOLD_TILE - PACKED_ROWS, b1_ref.shape[1]), BF16)
    mirrored = _dot(jnp.where(sel, 1.0, 0.0).astype(BF16),
                    jnp.concatenate([b1_ref[...], b2_ref[...], pad], axis=0))
    for b in range(nb):
        p = slice(b * 2 * f, b * 2 * f + f)
        q = slice(b * 2 * f + f, (b + 1) * 2 * f)
        o_ref[:, p] = (a_ref[:, p].astype(F32) + mirrored[:, p]).astype(BF16)
        o_ref[:, q] = (a_ref[:, q].astype(F32) - mirrored[:, q]).astype(BF16)


def _fold(pq, *, n, nb, f):
    nt = n // FOLD_TILE
    hpt = FOLD_TILE // PACKED_ROWS
    width = nb * 2 * f
    tile = lambda idx: pl.BlockSpec((FOLD_TILE, width), lambda m: (idx(m), 0))
    return pl.pallas_call(
        functools.partial(_fold_body, f=f, nb=nb),
        grid=(nt // 2,),
        in_specs=[tile(lambda m: m), tile(lambda m: nt - 1 - m),
                  pl.BlockSpec((PACKED_ROWS, width), lambda m: (jnp.minimum(nt - m, nt - 1) * hpt, 0))],
        out_specs=tile(lambda m: m),
        out_shape=jax.ShapeDtypeStruct((n // 2, width), BF16),
        compiler_params=_params(("arbitrary",)),
        name="dft_fold",
    )(pq, pq, pq)


def _dft_body(ca_ref, sa_ref, cb_ref, sb_ref, p_ref, q_ref, *rest, kdim, folded, scale):
    o_ref, c_scr, s_scr = rest[-3:]

    @pl.when(pl.program_id(1) == 0)
    def _():
        cb = cb_ref[...]
        sb = sb_ref[...]
        for t1 in range(kdim // LANES):
            ca = ca_ref[:, t1:t1 + 1]
            sa = sa_ref[:, t1:t1 + 1]
            c_scr[:, t1 * LANES:(t1 + 1) * LANES] = (ca * cb - sa * sb).astype(BF16)
            s_scr[:, t1 * LANES:(t1 + 1) * LANES] = (sa * cb + ca * sb).astype(BF16)

    y = _dot(c_scr[...], p_ref[...]) - _dot(s_scr[...], q_ref[...])
    if folded:
        mid_ref = rest[0]
        tmk = o_ref.shape[0]
        k = pl.program_id(0) * tmk + lax.broadcasted_iota(jnp.int32, (tmk, 1), 0)
        sign = (1 - 2 * (k & 1)).astype(F32)
        y = y + sign * (scale * mid_ref[...].astype(F32)[0:1, :])
    o_ref[...] = y.astype(BF16)


def _dft_tables(n):
    k = jnp.arange(n, dtype=jnp.int32)[:, None]
    t1 = jnp.arange(LANES, dtype=jnp.int32)[None, :]
    ang_a = ((k * ((t1 * LANES) % n)) % n).astype(F32) * (2.0 * math.pi / n)
    ang_b = ((k * t1) % n).astype(F32) * (2.0 * math.pi / n)
    scale = 1.0 / math.sqrt(n)
    return jnp.cos(ang_a), jnp.sin(ang_a), jnp.cos(ang_b) * scale, jnp.sin(ang_b) * scale


def _dft(pq, *, n, nb, f, row0, out_rows, prev=None, folded=None):
    tmk = min(n, 512)
    kdim = n // 2 if folded is not None else n
    tab = pl.BlockSpec((tmk, LANES), lambda m, b: (m, 0))
    if folded is not None:
        assert row0 == 0
        src, rb = folded, 0
    else:
        src, rb = pq, row0 // n
    in_specs = [
        tab, tab, tab, tab,
        pl.BlockSpec((kdim, f), lambda m, b: (rb, 2 * b)),
        pl.BlockSpec((kdim, f), lambda m, b: (rb, 2 * b + 1)),
    ]
    args = list(_dft_tables(n)) + [src, src]
    if folded is not None:
        in_specs.append(pl.BlockSpec((PACKED_ROWS, f), lambda m, b: (kdim // PACKED_ROWS, 2 * b)))
        args.append(pq)
    aliases = {}
    if prev is not None:
        in_specs.append(pl.BlockSpec(memory_space=pl.ANY))
        args.append(prev)
        aliases = {len(args) - 1: 0}
    return pl.pallas_call(
        functools.partial(_dft_body, kdim=kdim, folded=folded is not None, scale=1.0 / math.sqrt(n)),
        grid=(n // tmk, nb),
        in_specs=in_specs,
        out_specs=pl.BlockSpec((tmk, f), lambda m, b: (row0 // tmk + m, b)),
        out_shape=jax.ShapeDtypeStruct((out_rows, nb * f), BF16),
        scratch_shapes=[pltpu.VMEM((tmk, kdim), BF16), pltpu.VMEM((tmk, kdim), BF16)],
        input_output_aliases=aliases,
        compiler_params=_params(("arbitrary", "arbitrary")),
        name="pos_dft",
    )(*args)


def _top2(vals):
    def first_max(vs):
        m = functools.reduce(jnp.maximum, vs)
        idx = jnp.full(m.shape, len(vs) - 1, jnp.int32)
        for k in range(len(vs) - 2, -1, -1):
            idx = jnp.where(vs[k] == m, k, idx)
        return m, idx
    m1, i1 = first_max(vals)
    rest = [jnp.where(i1 == k, -jnp.inf, v) for k, v in enumerate(vals)]
    m2, i2 = first_max(rest)
    return m1, i1, m2, i2


def _pick(idx, vals):
    out = vals[-1]
    for k in range(len(vals) - 2, -1, -1):
        out = jnp.where(idx == k, vals[k], out)
    return out


def _route_window(h2, rw, rb, n_exp):
    aff = jax.nn.sigmoid(_dot3(h2, rw).T[0:n_exp, :])
    sel = aff + rb
    per = n_exp // N_EXPERT_GROUPS
    srow = [sel[e:e + 1, :] for e in range(n_exp)]
    arow = [aff[e:e + 1, :] for e in range(n_exp)]
    scores = []
    for gi in range(N_EXPERT_GROUPS):
        m1, _, m2, _ = _top2(srow[gi * per:(gi + 1) * per])
        scores.append(m1 + m2)
    best = functools.reduce(jnp.maximum, scores)
    grp = jnp.full(best.shape, N_EXPERT_GROUPS - 1, jnp.int32)
    for gi in range(N_EXPERT_GROUPS - 2, -1, -1):
        grp = jnp.where(scores[gi] == best, gi, grp)
    v = [_pick(grp, [srow[gi * per + k] for gi in range(N_EXPERT_GROUPS)]) for k in range(per)]
    a = [_pick(grp, [arow[gi * per + k] for gi in range(N_EXPERT_GROUPS)]) for k in range(per)]
    _, i1, _, i2 = _top2(v)
    w1 = _pick(i1, a)
    w2 = _pick(i2, a)
    den = w1 + w2
    e1 = grp * per + i1
    e2 = grp * per + i2

    erow = lax.broadcasted_iota(jnp.int32, (n_exp, TM), 0)
    hit1 = erow == e1
    hit2 = erow == e2
    onehot = jnp.where(hit1 | hit2, 1.0, 0.0)
    cnt = jnp.sum(onehot, axis=1, keepdims=True)
    before = (lax.broadcasted_iota(jnp.int32, (TM, TM), 0)
              < lax.broadcasted_iota(jnp.int32, (TM, TM), 1)).astype(BF16)
    rank = _dot(onehot.astype(BF16), before)
    padded = ((cnt.astype(jnp.int32) + (SUBLANES - 1)) & (-SUBLANES)).astype(F32)
    lower = (lax.broadcasted_iota(jnp.int32, (n_exp, n_exp), 1)
             < lax.broadcasted_iota(jnp.int32, (n_exp, n_exp), 0)).astype(BF16)
    starts = _dot(lower, jnp.broadcast_to(padded, (n_exp, LANES)).astype(BF16))
    pos = starts[:, 0:1] + rank
    r1 = jnp.sum(jnp.where(hit1, pos, 0.0), axis=0, keepdims=True)
    r2 = jnp.sum(jnp.where(hit2, pos, 0.0), axis=0, keepdims=True)
    route = jnp.concatenate([r1, r2, w1 / den, w2 / den, jnp.zeros((LANES - 4, TM), F32)], axis=0)
    return route, jnp.broadcast_to(cnt, (n_exp, LANES))


def _mix_body(hg_ref, yf_ref, sa_ref, sb_ref, x_ref, mod_ref, wpr_ref, wpf_ref, wo_ref,
              gffn_ref, rw_ref, rb_ref, xo_ref, xl_ref, rc_ref, cnt_ref, t_scr, *, d, n_exp, nb):
    mod = mod_ref[0]
    f = wpf_ref.shape[0]
    y_r = _dot(hg_ref[...], wpr_ref[...])
    yf = _to_time_major([yf_ref[:, b * f:(b + 1) * f] for b in range(nb)], t_scr, nb)
    y_f = _dot(yf.astype(BF16), wpf_ref[...])
    merged = sa_ref[...].astype(F32) * y_r + sb_ref[...].astype(F32) * y_f
    out = _dot(merged.astype(BF16), wo_ref[...])
    xn = x_ref[...] + _per_batch(out, lambda v: v * mod[:, 2 * d:3 * d][None])
    xo_ref[...] = xn
    h2 = _rms(xn, gffn_ref[...])
    h2 = _per_batch(h2, lambda v: v * (1.0 + mod[:, 4 * d:5 * d])[None] + mod[:, 3 * d:4 * d][None])
    rows = _local_rows(n_exp)
    riota = lax.broadcasted_iota(jnp.int32, (rows, TM), 0)
    for k in range(TR // TM):
        hw = h2[k * TM:(k + 1) * TM, :]
        route, cnt = _route_window(hw, rw_ref[...], rb_ref[...], n_exp)
        rc_ref[k * TM:(k + 1) * TM, :] = route.T
        cnt_ref[k] = cnt
        r12 = route[0:2, :].astype(jnp.int32)
        p = jnp.where((riota == r12[0:1, :]) | (riota == r12[1:2, :]), 1.0, 0.0).astype(BF16)
        xl_ref[k * rows:(k + 1) * rows, :] = _pack_bf16_pairs(_dot(p, hw.astype(BF16)))


def _mix(hg, yf, sa, sb, xa, mod, wpr, wpf, wo, gffn, rw, rb, *, n_rows, n_lat_rows, nb):
    d = xa.shape[1]
    f = wpf.shape[0]
    n_exp = rb.shape[0]
    n_lat = n_lat_rows // TR
    wpt = TR // TM
    n_win = n_rows // TM
    lrows = _local_rows(n_exp)
    row = lambda width: pl.BlockSpec((TR, width), lambda i: (i, 0))
    return pl.pallas_call(
        functools.partial(_mix_body, d=d, n_exp=n_exp, nb=nb),
        grid=(n_rows // TR,),
        in_specs=[
            row(d), pl.BlockSpec((TR // nb, nb * f), lambda i: (i, 0)), row(d), row(d), row(d),
            pl.BlockSpec((1, SUBLANES, N_MOD * d), lambda i: (jnp.where(i < n_lat, 0, 1), 0, 0)),
            _full(wpr.shape), _full(wpf.shape), _full(wo.shape),
            _full((1, d)), _full(rw.shape), _full((n_exp, 1)),
        ],
        scratch_shapes=[pltpu.VMEM((f // LANES, TR, LANES), F32)],
        out_specs=[
            row(d),
            pl.BlockSpec((wpt * lrows, d // 2), lambda i: (i, 0)),
            row(LANES),
            pl.BlockSpec((wpt, n_exp, LANES), lambda i: (i, 0, 0)),
        ],
        out_shape=[
            jax.ShapeDtypeStruct((n_rows, d), F32),
            jax.ShapeDtypeStruct((n_win * lrows, d // 2), jnp.uint32),
            jax.ShapeDtypeStruct((n_rows, LANES), F32),
            jax.ShapeDtypeStruct((n_win, n_exp, LANES), F32),
        ],
        compiler_params=_params(("arbitrary",)),
        name="mix_out",
    )(hg, yf, sa, sb, xa, mod, wpr, wpf, wo, gffn.reshape(1, d), rw, rb.reshape(n_exp, 1))


def _local_rows(n_exp):
    return -(-(2 * TM + n_exp * (SUBLANES - 1)) // LANES) * LANES


def _pack_bf16_pairs(x):
    half = x.shape[1] // 2
    bits = lax.bitcast_convert_type(x, jnp.uint32)
    return (bits[:, :half] & jnp.uint32(0xFFFF0000)) | lax.shift_right_logical(bits[:, half:], jnp.uint32(16))


def _unpack_bf16_pairs(u):
    hi = lax.bitcast_convert_type(u & jnp.uint32(0xFFFF0000), F32)
    lo = lax.bitcast_convert_type(lax.shift_left(u, jnp.uint32(16)), F32)
    return jnp.concatenate([hi, lo], axis=-1)


CPT = MOE_TM // SUBLANES


def _group_copies(table_ref, hbm, buf, sem, *, gather):
    for j in range(CPT):
        far = hbm.at[pl.ds(pl.multiple_of(table_ref[0, 0, j] * SUBLANES, SUBLANES), SUBLANES)]
        near = buf.at[pl.ds(j * SUBLANES, SUBLANES)]
        (pltpu.make_async_copy(far, near, sem) if gather else pltpu.make_async_copy(near, far, sem)).start()


def _tile_wait(hbm, buf, sem, *, gather):
    far = hbm.at[pl.ds(0, MOE_TM)]
    (pltpu.make_async_copy(far, buf, sem) if gather else pltpu.make_async_copy(buf, far, sem)).wait()


def _gmm_body(te_ref, nu_ref, src_ref, nxt_ref, dst_ref, xl_ref, w1_ref, w3_ref, w2_ref, yl_ref,
              w1b, w3b, w2b, xbuf, ybuf, gsem, ssem):
    t = pl.program_id(0)
    n_used = nu_ref[0]
    slot = lax.rem(t, 2)
    other = 1 - slot

    @pl.when((t == 0) | (te_ref[t] != te_ref[jnp.maximum(t - 1, 0)]))
    def _():
        w1b[...] = w1_ref[0, 0].astype(BF16)
        w3b[...] = w3_ref[0, 0].astype(BF16)
        w2b[...] = w2_ref[0, 0].astype(BF16)

    @pl.when(t < n_used)
    def _():
        @pl.when(t == 0)
        def _():
            _group_copies(src_ref, xl_ref, xbuf.at[slot], gsem.at[slot], gather=True)

        _tile_wait(xl_ref, xbuf.at[slot], gsem.at[slot], gather=True)
        _group_copies(nxt_ref, xl_ref, xbuf.at[other], gsem.at[other], gather=True)
        x = _unpack_bf16_pairs(xbuf[slot]).astype(BF16)
        a = _dot(x, w1b[...])
        h = (a * jax.nn.sigmoid(a)) * _dot(x, w3b[...])
        y = _dot(h.astype(BF16), w2b[...])

        @pl.when(t >= 2)
        def _():
            _tile_wait(yl_ref, ybuf.at[slot], ssem.at[slot], gather=False)

        ybuf[slot] = _pack_bf16_pairs(y.astype(BF16).astype(F32))
        _group_copies(dst_ref, yl_ref, ybuf.at[slot], ssem.at[slot], gather=False)

        @pl.when(t == n_used - 1)
        def _():
            _tile_wait(xl_ref, xbuf.at[other], gsem.at[other], gather=True)

            @pl.when(t >= 1)
            def _():
                _tile_wait(yl_ref, ybuf.at[other], ssem.at[other], gather=False)

            _tile_wait(yl_ref, ybuf.at[slot], ssem.at[slot], gather=False)


def _gmm(tile_exp, n_used, src, dst, xl, w1, w3, w2, *, layer, out_rows):
    n_tiles = src.shape[0]
    d, de = w1.shape[2], w1.shape[3]
    wspec = lambda w: pl.BlockSpec((1, 1) + w.shape[2:], lambda t, te, nu: (layer, te[t], 0, 0))
    table = lambda idx: pl.BlockSpec((1, 1, CPT), lambda t, te, nu: (idx(t, nu), 0, 0), memory_space=pltpu.SMEM)
    cur = lambda t, nu: jnp.minimum(t, nu[0] - 1)
    grid_spec = pltpu.PrefetchScalarGridSpec(
        num_scalar_prefetch=2,
        grid=(n_tiles,),
        in_specs=[table(cur), table(lambda t, nu: jnp.minimum(t + 1, nu[0] - 1)), table(cur),
                  pl.BlockSpec(memory_space=pl.ANY), wspec(w1), wspec(w3), wspec(w2)],
        out_specs=pl.BlockSpec(memory_space=pl.ANY),
        scratch_shapes=[
            pltpu.VMEM((d, de), BF16), pltpu.VMEM((d, de), BF16), pltpu.VMEM((de, d), BF16),
            pltpu.VMEM((2, MOE_TM, d // 2), jnp.uint32), pltpu.VMEM((2, MOE_TM, d // 2), jnp.uint32),
            pltpu.SemaphoreType.DMA((2,)), pltpu.SemaphoreType.DMA((2,)),
        ],
    )
    return pl.pallas_call(
        _gmm_body,
        grid_spec=grid_spec,
        out_shape=jax.ShapeDtypeStruct((out_rows, d // 2), jnp.uint32),
        compiler_params=_params(("arbitrary",)),
        name="moe_experts",
    )(tile_exp, n_used, src, src, dst, xl, w1, w3, w2)


def _combine_body(*refs, d, rows, nb, final):
    if final:
        tot_ref, yl_ref, rc_ref, xr_ref, mod_ref, gfin_ref, o_ref, t_scr = refs
    else:
        tot_ref, yl_ref, rc_ref, xr_ref, mod_ref, o_ref = refs
    wpt = TR // TM
    ci = lax.broadcasted_iota(jnp.int32, (TM, rows), 1)
    parts = []
    for k in range(wpt):
        keep = lax.broadcasted_iota(jnp.int32, (rows, 1), 0) < tot_ref[pl.program_id(0) * wpt + k]
        y = _unpack_bf16_pairs(jnp.where(keep, yl_ref[k * rows:(k + 1) * rows, :], jnp.uint32(0))).astype(BF16)
        rc = rc_ref[k * TM:(k + 1) * TM, :]
        pw = (jnp.where(ci == rc[:, 0:1].astype(jnp.int32), rc[:, 2:3], 0.0)
              + jnp.where(ci == rc[:, 1:2].astype(jnp.int32), rc[:, 3:4], 0.0))
        parts.append(_dot(pw.astype(BF16), y))
    moe = jnp.concatenate(parts, axis=0)
    g2 = mod_ref[0][:, 5 * d:6 * d]
    xn = xr_ref[...] + _per_batch(moe, lambda v: v * g2[None])
    if final:
        xn = _rms(xn, gfin_ref[...])
        for b, blk in enumerate(_to_batch_major(xn, t_scr, nb)):
            o_ref[b] = blk
    else:
        o_ref[...] = xn


def _combine(totals, yl, rc, xres, mod, *, n_exp, n_win, n_lat_rows, nb, gfin=None, seq_out=None):
    d = xres.shape[1]
    rows = _local_rows(n_exp)
    final = gfin is not None
    n_lat = n_lat_rows // TR
    wpt = TR // TM
    row = lambda width: pl.BlockSpec((TR, width), lambda w, a: (w, 0))
    in_specs = [pl.BlockSpec((wpt * rows, d // 2), lambda w, a: (w, 0)), row(LANES), row(d),
                pl.BlockSpec((1, SUBLANES, N_MOD * d), lambda w, a: (jnp.where(w < n_lat, 0, 1), 0, 0))]
    args = [totals, yl, rc, xres, mod]
    scratch = []
    if final:
        in_specs.append(pl.BlockSpec((1, d), lambda w, a: (0, 0)))
        args.append(gfin.reshape(1, d))
        out_shape = jax.ShapeDtypeStruct((nb, seq_out, d), F32)
        out_specs = pl.BlockSpec((nb, TR // nb, d), lambda w, a: (0, w, 0))
        scratch.append(pltpu.VMEM((d // LANES, TR, LANES), F32))
        aliases = {}
    else:
        out_shape = jax.ShapeDtypeStruct(xres.shape, F32)
        out_specs = row(d)
        aliases = {3: 0}
    grid_spec = pltpu.PrefetchScalarGridSpec(
        num_scalar_prefetch=1, grid=(n_win // wpt,), in_specs=in_specs, out_specs=out_specs,
        scratch_shapes=scratch)
    return pl.pallas_call(
        functools.partial(_combine_body, d=d, rows=rows, nb=nb, final=final),
        grid_spec=grid_spec,
        out_shape=out_shape,
        input_output_aliases=aliases,
        compiler_params=_params(("arbitrary",)),
        name="moe_combine",
    )(*args)


def _route_tables(cnt, *, n_exp, n_tiles):
    cnt = cnt[..., 0].astype(jnp.int32)
    n_win = cnt.shape[0]
    lrows = _local_rows(n_exp)
    npad = (cnt + (SUBLANES - 1)) & (-SUBLANES)
    seg = -(-jnp.sum(npad, axis=0) // MOE_TM) * MOE_TM
    ends = jnp.cumsum(seg)
    goff = (ends - seg)[None, :] + jnp.cumsum(npad, axis=0) - npad
    loff = jnp.cumsum(npad, axis=1) - npad
    tile_ends = ends // MOE_TM
    n_used = tile_ends[-1]
    tile = jnp.minimum(jnp.arange(n_tiles, dtype=jnp.int32), n_used - 1)
    tile_exp = jnp.sum(tile[:, None] >= tile_ends[None, :], axis=1).astype(jnp.int32)

    starts = (goff.T // SUBLANES).reshape(-1)
    lens = (npad.T // SUBLANES).reshape(-1)
    local = ((jnp.arange(n_win, dtype=jnp.int32)[:, None] * lrows + loff).T // SUBLANES).reshape(-1)
    group = jnp.arange(n_tiles * CPT, dtype=jnp.int32)
    began = starts[None, :] <= group[:, None]
    last = lambda v: jnp.sum(jnp.where(began, jnp.diff(v, prepend=0)[None, :], 0), axis=1)
    valid = group < last(starts + lens)
    src = jnp.where(valid, group + last(local - starts), 0)
    dst = jnp.where(valid, src, n_win * lrows // SUBLANES + group)
    shape = (n_tiles, 1, CPT)
    return (tile_exp, n_used.reshape(1).astype(jnp.int32), src.reshape(shape).astype(jnp.int32),
            dst.reshape(shape).astype(jnp.int32), jnp.sum(npad, axis=1).astype(jnp.int32))


def _channel_dft_tables(n):
    k = jnp.arange(n, dtype=jnp.int32)
    ang = ((k[:, None] * k[None, :]) % n).astype(F32) * (2.0 * math.pi / n)
    scale = 1.0 / math.sqrt(n)
    return jnp.cos(ang) * scale, jnp.sin(ang) * scale


def kernel(x, c, ctx, c_ctx, ada_w, ada_b, norm_mix_g, w_in, conv_w, conv_b, lru_w_a, lru_b_a, lru_w_x, lru_b_x, lru_lambda, w_proj_rnn, w_proj_fourier, w_out, norm_ffn_g, router_w, router_b, moe_w1, moe_w3, moe_w2, final_norm_g):
    nb, l, d = x.shape
    lc = ctx.shape[1]
    depth = ada_w.shape[0]
    r = conv_w.shape[2]
    f = w_proj_fourier.shape[1]
    nh, blk = lru_w_a.shape[2], lru_w_a.shape[3]
    assert nb == SUBLANES and blk == LANES and nb < MOD_ROWS
    assert l % TS == 0 and lc % TS == 0 and TS % GRID_W == 0 and (l * nb) % TR == 0 and (lc * nb) % TR == 0
    assert GRID_W & (GRID_W - 1) == 0 and l % lc == 0 and l % (2 * FOLD_TILE) == 0 and l // LANES <= LANES
    s = l + lc
    n_lat_rows = l * nb

    cc = jnp.zeros((MOD_ROWS, d), F32).at[:nb].set(c).at[nb].set(c_ctx)
    mod = _ada(cc, ada_w, ada_b)
    mod = jnp.stack([mod[:, :nb], jnp.broadcast_to(mod[:, nb:nb + 1], (depth, nb, N_MOD * d))], axis=1)

    cch, sch = _channel_dft_tables(f // FOURIER_GROUPS)
    eye = jnp.eye(FOURIER_GROUPS, dtype=F32)
    fcs = jnp.concatenate([jnp.kron(eye, cch), jnp.kron(eye, sch)], axis=1).astype(BF16)
    n_exp = router_w.shape[1]
    rw = jnp.pad(router_w, ((0, 0), (0, LANES - n_exp)))

    out = xa = None
    for li in range(depth):
        last = li == depth - 1
        inproj = functools.partial(_inproj, mod=mod[li], g=norm_mix_g[li], w_in=w_in[li].astype(BF16), fcs=fcs,
                                   n_lat_rows=n_lat_rows, r=r, nb=nb)
        if li == 0:
            u, gl, sa, sb, pq, xa = inproj(x, ctx=ctx)
        else:
            u, gl, sa, sb, pq = inproj(xa)
        wg = [jnp.concatenate([lru_w_a[li, k], lru_w_x[li, k]], axis=-1).astype(BF16) for k in range(2)]
        bg = [jnp.concatenate([lru_b_a[li, k], lru_b_x[li, k]], axis=-1).reshape(nh, 1, 2 * blk) for k in range(2)]
        lam = [lru_lambda[li, k].reshape(nh, 1, blk) for k in range(2)]
        scan = functools.partial(_scan, u, conv_w[li], conv_b[li], nb=nb, n_lat_steps=l // TS, ctx_len=lc)
        hb = scan(wg[1], bg[1], lam[1], reverse=True)
        hg = scan(wg[0], bg[0], lam[0], reverse=False, hb=hb, gl=gl)
        n_rows = n_lat_rows if last else s * nb
        yf = _dft(pq, n=l, nb=nb, f=f, row0=0, out_rows=n_rows // nb, folded=_fold(pq, n=l, nb=nb, f=f))
        if not last:
            yf = _dft(pq, n=lc, nb=nb, f=f, row0=l, out_rows=n_rows // nb, prev=yf)
        x1, xl, rc, cnt = _mix(hg, yf, sa, sb, xa, mod[li], w_proj_rnn[li].astype(BF16),
                               w_proj_fourier[li].astype(BF16), w_out[li].astype(BF16), norm_ffn_g[li],
                               rw, router_b, n_rows=n_rows, n_lat_rows=n_lat_rows, nb=nb)
        n_win = n_rows // TM
        n_tiles = -(-(n_win * (2 * TM + n_exp * (SUBLANES - 1)) + n_exp * MOE_TM) // MOE_TM)
        tile_exp, n_used, src, dst, totals = _route_tables(cnt, n_exp=n_exp, n_tiles=n_tiles)
        yl = _gmm(tile_exp, n_used, src, dst, xl, moe_w1, moe_w3, moe_w2, layer=li,
                  out_rows=xl.shape[0] + n_tiles * MOE_TM)
        comb = functools.partial(_combine, totals, yl, rc, x1, mod[li], n_exp=n_exp, n_win=n_win,
                                 n_lat_rows=n_lat_rows, nb=nb)
        if last:
            out = comb(gfin=final_norm_g, seq_out=l)
        else:
            xa = comb()
    return out
```

```python
import functools
import math

import jax
import jax.numpy as jnp
from jax import lax
from jax.experimental import pallas as pl
from jax.experimental.pallas import tpu as pltpu

F32 = jnp.float32
BF16 = jnp.bfloat16

RMS_EPS = 1e-6
LRU_C = 8.0
GRID_W = 64
CONV_TAPS = ((0, -2), (1, -1), (2, 0), (3, 1))
N_MOD = 6
FOURIER_GROUPS = 4
N_EXPERT_GROUPS = 4
LANES = 128
SUBLANES = 8
PACKED_ROWS = 16
TR = 512
TM = 256
TS = 128
SUB = 64
MOE_TM = 512
MOD_ROWS = 16
VMEM_LIMIT = 56 * 1024 * 1024


def _dot(a, b):
    return jnp.dot(a, b, preferred_element_type=F32)


def _split(a):
    hi = a.astype(BF16)
    lo = (a - hi.astype(F32)).astype(BF16)
    return hi, lo


def _dot3(a, b):
    ah, al = _split(a)
    bh, bl = _split(b)
    return _dot(ah, bh) + _dot(al, bh) + _dot(ah, bl)


def _gelu_tanh(x):
    return 0.5 * x * (1.0 + jnp.tanh(math.sqrt(2.0 / math.pi) * (x + 0.044715 * (x * x * x))))


def _sigmoid(x):
    return 0.5 * jnp.tanh(0.5 * x) + 0.5


def _rms(x, g):
    return x * lax.rsqrt(jnp.mean(x * x, axis=-1, keepdims=True) + RMS_EPS) * g


def _per_batch(x, fn):
    rows, d = x.shape
    return fn(x.reshape(rows // SUBLANES, SUBLANES, d)).reshape(rows, d)


def _params(sem, vmem=VMEM_LIMIT):
    return pltpu.CompilerParams(dimension_semantics=sem, vmem_limit_bytes=vmem)


def _full(shape):
    zeros = (0,) * len(shape)
    return pl.BlockSpec(shape, lambda *_: zeros)


def _ada_body(cc_ref, w_ref, b_ref, o_ref):
    cc = cc_ref[...]
    s = cc * jax.nn.sigmoid(cc)
    o_ref[0] = _dot3(s, w_ref[0]) + b_ref[0]


def _ada(cc, ada_w, ada_b):
    depth, d, n = ada_w.shape
    tn = 1024
    return pl.pallas_call(
        _ada_body,
        grid=(depth, n // tn),
        in_specs=[
            pl.BlockSpec((MOD_ROWS, d), lambda l, j: (0, 0)),
            pl.BlockSpec((1, d, tn), lambda l, j: (l, 0, j)),
            pl.BlockSpec((1, 1, tn), lambda l, j: (l, 0, j)),
        ],
        out_specs=pl.BlockSpec((1, MOD_ROWS, tn), lambda l, j: (l, 0, j)),
        out_shape=jax.ShapeDtypeStruct((depth, MOD_ROWS, n), F32),
        compiler_params=_params(("arbitrary", "arbitrary")),
        name="ada_mod",
    )(cc, ada_w, ada_b.reshape(depth, 1, n))


def _to_time_major(blocks, scr, nb):
    npos, width = blocks[0].shape
    for b in range(nb):
        for g in range(width // LANES):
            scr[g, pl.ds(b, npos, stride=nb), :] = blocks[b][:, g * LANES:(g + 1) * LANES].astype(F32)
    return jnp.concatenate([scr[g] for g in range(width // LANES)], axis=-1)


def _to_batch_major(v, scr, nb):
    rows, width = v.shape
    for g in range(width // LANES):
        scr[g] = v[:, g * LANES:(g + 1) * LANES]
    return [jnp.concatenate([scr[g, pl.ds(b, rows // nb, stride=nb), :] for g in range(width // LANES)], axis=-1)
            for b in range(nb)]


def _inproj_body(*refs, d, r, f, nb, n_lat, first):
    if first:
        (x_ref, c_ref, mod_ref, g_ref, w_ref, fcs_ref,
         u_ref, gl_ref, sa_ref, sb_ref, pq_ref, xa_ref, t_scr) = refs

        @pl.when(pl.program_id(0) < n_lat)
        def _():
            xa_ref[...] = _to_time_major([x_ref[b] for b in range(nb)], t_scr, nb)

        @pl.when(pl.program_id(0) >= n_lat)
        def _():
            xa_ref[...] = _to_time_major([c_ref[b] for b in range(nb)], t_scr, nb)

        x = xa_ref[...]
    else:
        x_ref, mod_ref, g_ref, w_ref, fcs_ref, u_ref, gl_ref, sa_ref, sb_ref, pq_ref, t_scr = refs
        x = x_ref[...]
    mod = mod_ref[0]
    h = _rms(x, g_ref[...])
    h = _per_batch(h, lambda v: v * (1.0 + mod[:, d:2 * d])[None] + mod[:, 0:d][None]).astype(BF16)
    u_ref[...] = _dot(h, w_ref[:, 0:r]).astype(BF16)
    gl_ref[...] = _gelu_tanh(_dot(h, w_ref[:, r:2 * r])).astype(BF16)
    u4 = _dot(h, w_ref[:, 2 * r:2 * r + f]).astype(BF16)
    for b, blk in enumerate(_to_batch_major(_dot(u4, fcs_ref[...]), t_scr, nb)):
        pq_ref[:, b * 2 * f:(b + 1) * 2 * f] = blk.astype(BF16)
    s3 = 2 * r + f
    sa_ref[...] = jax.nn.sigmoid(_dot(h, w_ref[:, s3:s3 + d])).astype(BF16)
    sb_ref[...] = jax.nn.sigmoid(_dot(h, w_ref[:, s3 + d:s3 + 2 * d])).astype(BF16)


def _inproj(x, mod, g, w_in, fcs, *, n_lat_rows, r, nb, ctx=None):
    first = ctx is not None
    d = x.shape[-1]
    rows = (x.shape[1] + ctx.shape[1]) * nb if first else x.shape[0]
    f = fcs.shape[0]
    n_lat = n_lat_rows // TR
    npos = TR // nb
    row = lambda width: pl.BlockSpec((TR, width), lambda i: (i, 0))
    common = [
        pl.BlockSpec((1, SUBLANES, N_MOD * d), lambda i: (jnp.where(i < n_lat, 0, 1), 0, 0)),
        _full((1, d)), _full(w_in.shape), _full(fcs.shape),
    ]
    out_specs = [row(r), row(r), row(d), row(d), pl.BlockSpec((npos, nb * 2 * f), lambda i: (i, 0))]
    out_shape = [
        jax.ShapeDtypeStruct((rows, r), BF16),
        jax.ShapeDtypeStruct((rows, r), BF16),
        jax.ShapeDtypeStruct((rows, d), BF16),
        jax.ShapeDtypeStruct((rows, d), BF16),
        jax.ShapeDtypeStruct((rows // nb, nb * 2 * f), BF16),
    ]
    if first:
        in_specs = [
            pl.BlockSpec((nb, npos, d), lambda i: (0, jnp.minimum(i, n_lat - 1), 0)),
            pl.BlockSpec((nb, npos, d), lambda i: (0, jnp.maximum(i - n_lat, 0), 0)),
        ] + common
        args = [x, ctx]
        out_specs.append(row(d))
        out_shape.append(jax.ShapeDtypeStruct((rows, d), F32))
    else:
        in_specs = [row(d)] + common
        args = [x]
    return pl.pallas_call(
        functools.partial(_inproj_body, d=d, r=r, f=f, nb=nb, n_lat=n_lat, first=first),
        grid=(rows // TR,),
        in_specs=in_specs,
        out_specs=out_specs,
        out_shape=out_shape,
        scratch_shapes=[pltpu.VMEM((d // LANES, TR, LANES), F32)],
        compiler_params=_params(("arbitrary",)),
        name="in_proj",
    )(*args, mod, g.reshape(1, d), w_in, fcs)


def _conv_tile(u_ref, up_ref, un_ref, cw_ref, cb_ref, uc_scr, blk, *, nb, n_lat_steps):
    trows = TS * nb
    n_steps = pl.num_programs(0)
    is_ctx = blk >= n_lat_steps
    keep_prev = jnp.where(is_ctx & (blk > n_lat_steps), 1.0, 0.0)
    keep_next = jnp.where(is_ctx & (blk < n_steps - 1), 1.0, 0.0)
    lat = jnp.where(is_ctx, 0.0, 1.0)
    halo = 2 * nb
    ext = jnp.concatenate([up_ref[...].astype(F32) * keep_prev, u_ref[...].astype(F32),
                           un_ref[...].astype(F32)[0:nb] * keep_next], axis=0)
    uc = cb_ref[...]
    for k, s in CONV_TAPS:
        lo = halo + s * nb
        uc = uc + cw_ref[k:k + 1, :] * ext[lo:lo + trows, :]
    uc_scr[...] = uc
    for p in range(GRID_W, TS, GRID_W):
        at = lambda q: ext[halo + q * nb:halo + (q + 1) * nb, :]
        rows_of = lambda q: pl.ds(q * nb, nb)
        uc_scr[rows_of(p), :] -= lat * (cw_ref[0:1, :] * at(p - 2) + cw_ref[1:2, :] * at(p - 1))
        uc_scr[rows_of(p + 1), :] -= lat * (cw_ref[0:1, :] * at(p - 1))
        uc_scr[rows_of(p - 1), :] -= lat * (cw_ref[3:4, :] * at(p))


def _scan_body(*refs, reverse, merge, nb, n_lat_steps, order):
    if merge:
        uc_ref, wg_ref, bg_ref, lam_ref, hb_ref, gl_ref, o_ref, h_scr, a_scr, b_scr, uc_scr, hs_scr = refs
    else:
        (u_ref, up_ref, un_ref, cw_ref, cb_ref, wg_ref, bg_ref, lam_ref,
         o_ref, uco_ref, h_scr, a_scr, b_scr, uc_scr, hs_scr) = refs
    nh = wg_ref.shape[0]
    rows = SUB * nb
    j = pl.program_id(0)

    @pl.when(j == 0)
    def _():
        h_scr[...] = jnp.zeros_like(h_scr)

    if merge:
        uc_scr[...] = uc_ref[...].astype(F32)
    else:
        _conv_tile(u_ref, up_ref, un_ref, cw_ref, cb_ref, uc_scr, order(j), nb=nb, n_lat_steps=n_lat_steps)
        uco_ref[...] = uc_scr[...].astype(BF16)

    lam = lam_ref[...]
    half_decay = (-0.5 * LRU_C) * (jnp.maximum(-lam, 0.0) + jnp.log1p(jnp.exp(-jnp.abs(lam))))

    subs = range(TS // SUB)
    for sub in (reversed(subs) if reverse else subs):
        r0 = sub * rows
        for g in range(nh):
            ug = uc_scr[r0:r0 + rows, g * LANES:(g + 1) * LANES]
            gates = _dot(ug.astype(BF16), wg_ref[g]) + bg_ref[g]
            log_a = half_decay[g] * jnp.tanh(0.5 * gates[:, :LANES]) + half_decay[g]
            a = jnp.exp(log_a)
            z = (-1.0 - a * a) * jnp.tanh(log_a)
            mult = jnp.where(z > 0.0, z * lax.rsqrt(z), 0.0)
            a_scr[g] = a
            b_scr[g] = mult * (_sigmoid(gates[:, LANES:]) * ug)

        def step(k, hs):
            t = (SUB - 1 - k) if reverse else k
            off = pl.multiple_of(t * nb, nb)
            new = []
            for g in range(nh):
                hg = a_scr[g, pl.ds(off, nb), :] * hs[g] + b_scr[g, pl.ds(off, nb), :]
                hs_scr[pl.ds(r0 + off, nb), g * LANES:(g + 1) * LANES] = hg
                new.append(hg)
            return tuple(new)

        hs = lax.fori_loop(0, SUB, step, tuple(h_scr[g] for g in range(nh)), unroll=8)
        for g in range(nh):
            h_scr[g] = hs[g]

    if merge:
        o_ref[...] = ((hs_scr[...] + hb_ref[...].astype(F32)) * gl_ref[...].astype(F32)).astype(BF16)
    else:
        o_ref[...] = hs_scr[...].astype(BF16)


def _scan(u, wg, bg, lam, *, nb, n_lat_steps, conv=None, hb=None, gl=None):
    rows_all, r = u.shape
    nh = wg.shape[0]
    trows = TS * nb
    n = rows_all // trows
    n_ctx = n - n_lat_steps
    merge = hb is not None
    reverse = not merge
    if reverse:
        order = lambda j: jnp.where(j < n_ctx, n - 1 - j, n_lat_steps - 1 - (j - n_ctx))
    else:
        order = lambda j: jnp.where(j < n_ctx, n_lat_steps + j, j - n_ctx)
    hpb = trows // PACKED_ROWS
    tile = pl.BlockSpec((trows, r), lambda j: (order(j), 0))
    gate_specs = [_full(wg.shape), _full(bg.shape), _full(lam.shape)]
    out = jax.ShapeDtypeStruct((rows_all, r), BF16)
    if merge:
        in_specs = [tile] + gate_specs + [tile, tile]
        args = [u, wg, bg, lam, hb, gl]
        out_specs, out_shape = tile, out
    else:
        conv_w, conv_b = conv
        in_specs = [
            tile,
            pl.BlockSpec((PACKED_ROWS, r), lambda j: (jnp.maximum(order(j) * hpb - 1, 0), 0)),
            pl.BlockSpec((PACKED_ROWS, r), lambda j: (jnp.minimum((order(j) + 1) * hpb, n * hpb - 1), 0)),
            _full(conv_w.shape), _full((1, r)),
        ] + gate_specs
        args = [u, u, u, conv_w, conv_b.reshape(1, r), wg, bg, lam]
        out_specs, out_shape = [tile, tile], [out, out]
    return pl.pallas_call(
        functools.partial(_scan_body, reverse=reverse, merge=merge, nb=nb, n_lat_steps=n_lat_steps,
                          order=order),
        grid=(n,),
        in_specs=in_specs,
        out_specs=out_specs,
        out_shape=out_shape,
        scratch_shapes=[
            pltpu.VMEM((nh, nb, LANES), F32),
            pltpu.VMEM((nh, SUB * nb, LANES), F32),
            pltpu.VMEM((nh, SUB * nb, LANES), F32),
            pltpu.VMEM((trows, r), F32),
            pltpu.VMEM((trows, r), F32),
        ],
        compiler_params=_params(("arbitrary",)),
        name="scan_fwd_merge" if merge else "scan_bwd",
    )(*args)


FOLD_TILE = 256


def _fold_body(a_ref, b1_ref, b2_ref, o_ref, *, f, nb):
    m = pl.program_id(0)
    i = lax.broadcasted_iota(jnp.int32, (FOLD_TILE, 2 * FOLD_TILE), 0)
    col = lax.broadcasted_iota(jnp.int32, (FOLD_TILE, 2 * FOLD_TILE), 1)
    sel = (col == jnp.where(i == 0, FOLD_TILE, FOLD_TILE - i)) & ((i > 0) | (m > 0))
    pad = jnp.zeros((FOLD_TILE - PACKED_ROWS, b1_ref.shape[1]), BF16)
    mirrored = _dot(jnp.where(sel, 1.0, 0.0).astype(BF16),
                    jnp.concatenate([b1_ref[...], b2_ref[...], pad], axis=0))
    for b in range(nb):
        p = slice(b * 2 * f, b * 2 * f + f)
        q = slice(b * 2 * f + f, (b + 1) * 2 * f)
        o_ref[:, p] = (a_ref[:, p].astype(F32) + mirrored[:, p]).astype(BF16)
        o_ref[:, q] = (a_ref[:, q].astype(F32) - mirrored[:, q]).astype(BF16)


def _fold(pq, *, n, nb, f):
    nt = n // FOLD_TILE
    hpt = FOLD_TILE // PACKED_ROWS
    width = nb * 2 * f
    tile = lambda idx: pl.BlockSpec((FOLD_TILE, width), lambda m: (idx(m), 0))
    return pl.pallas_call(
        functools.partial(_fold_body, f=f, nb=nb),
        grid=(nt // 2,),
        in_specs=[tile(lambda m: m), tile(lambda m: nt - 1 - m),
                  pl.BlockSpec((PACKED_ROWS, width), lambda m: (jnp.minimum(nt - m, nt - 1) * hpt, 0))],
        out_specs=tile(lambda m: m),
        out_shape=jax.ShapeDtypeStruct((n // 2, width), BF16),
        compiler_params=_params(("arbitrary",)),
        name="dft_fold",
    )(pq, pq, pq)


def _dft_body(ca_ref, sa_ref, cb_ref, sb_ref, p_ref, q_ref, *rest, kdim, folded, scale):
    o_ref, c_scr, s_scr = rest[-3:]

    @pl.when(pl.program_id(1) == 0)
    def _():
        cb = cb_ref[...]
        sb = sb_ref[...]
        for t1 in range(kdim // LANES):
            ca = ca_ref[:, t1:t1 + 1]
            sa = sa_ref[:, t1:t1 + 1]
            c_scr[:, t1 * LANES:(t1 + 1) * LANES] = (ca * cb - sa * sb).astype(BF16)
            s_scr[:, t1 * LANES:(t1 + 1) * LANES] = (sa * cb + ca * sb).astype(BF16)

    y = _dot(c_scr[...], p_ref[...]) - _dot(s_scr[...], q_ref[...])
    if folded:
        mid_ref = rest[0]
        tmk = o_ref.shape[0]
        k = pl.program_id(0) * tmk + lax.broadcasted_iota(jnp.int32, (tmk, 1), 0)
        sign = (1 - 2 * (k & 1)).astype(F32)
        y = y + sign * (scale * mid_ref[...].astype(F32)[0:1, :])
    o_ref[...] = y.astype(BF16)


def _dft_tables(n):
    k = jnp.arange(n, dtype=jnp.int32)[:, None]
    t1 = jnp.arange(LANES, dtype=jnp.int32)[None, :]
    ang_a = ((k * ((t1 * LANES) % n)) % n).astype(F32) * (2.0 * math.pi / n)
    ang_b = ((k * t1) % n).astype(F32) * (2.0 * math.pi / n)
    scale = 1.0 / math.sqrt(n)
    return jnp.cos(ang_a), jnp.sin(ang_a), jnp.cos(ang_b) * scale, jnp.sin(ang_b) * scale


def _dft(pq, *, n, nb, f, row0, out_rows, prev=None, folded=None):
    tmk = min(n, 512)
    kdim = n // 2 if folded is not None else n
    tab = pl.BlockSpec((tmk, LANES), lambda m, b: (m, 0))
    if folded is not None:
        assert row0 == 0
        src, rb = folded, 0
    else:
        src, rb = pq, row0 // n
    in_specs = [
        tab, tab, tab, tab,
        pl.BlockSpec((kdim, f), lambda m, b: (rb, 2 * b)),
        pl.BlockSpec((kdim, f), lambda m, b: (rb, 2 * b + 1)),
    ]
    args = list(_dft_tables(n)) + [src, src]
    if folded is not None:
        in_specs.append(pl.BlockSpec((PACKED_ROWS, f), lambda m, b: (kdim // PACKED_ROWS, 2 * b)))
        args.append(pq)
    aliases = {}
    if prev is not None:
        in_specs.append(pl.BlockSpec(memory_space=pl.ANY))
        args.append(prev)
        aliases = {len(args) - 1: 0}
    return pl.pallas_call(
        functools.partial(_dft_body, kdim=kdim, folded=folded is not None, scale=1.0 / math.sqrt(n)),
        grid=(n // tmk, nb),
        in_specs=in_specs,
        out_specs=pl.BlockSpec((tmk, f), lambda m, b: (row0 // tmk + m, b)),
        out_shape=jax.ShapeDtypeStruct((out_rows, nb * f), BF16),
        scratch_shapes=[pltpu.VMEM((tmk, kdim), BF16), pltpu.VMEM((tmk, kdim), BF16)],
        input_output_aliases=aliases,
        compiler_params=_params(("arbitrary", "arbitrary")),
        name="pos_dft",
    )(*args)


def _top2(vals):
    def first_max(vs):
        m = functools.reduce(jnp.maximum, vs)
        idx = jnp.full(m.shape, len(vs) - 1, jnp.int32)
        for k in range(len(vs) - 2, -1, -1):
            idx = jnp.where(vs[k] == m, k, idx)
        return m, idx
    m1, i1 = first_max(vals)
    rest = [jnp.where(i1 == k, -jnp.inf, v) for k, v in enumerate(vals)]
    m2, i2 = first_max(rest)
    return m1, i1, m2, i2


def _pick(idx, vals):
    out = vals[-1]
    for k in range(len(vals) - 2, -1, -1):
        out = jnp.where(idx == k, vals[k], out)
    return out


def _route_window(h2, rw, rb, n_exp):
    aff = jax.nn.sigmoid(_dot3(h2, rw).T[0:n_exp, :])
    sel = aff + rb
    per = n_exp // N_EXPERT_GROUPS
    srow = [sel[e:e + 1, :] for e in range(n_exp)]
    arow = [aff[e:e + 1, :] for e in range(n_exp)]
    scores = []
    for gi in range(N_EXPERT_GROUPS):
        m1, _, m2, _ = _top2(srow[gi * per:(gi + 1) * per])
        scores.append(m1 + m2)
    best = functools.reduce(jnp.maximum, scores)
    grp = jnp.full(best.shape, N_EXPERT_GROUPS - 1, jnp.int32)
    for gi in range(N_EXPERT_GROUPS - 2, -1, -1):
        grp = jnp.where(scores[gi] == best, gi, grp)
    v = [_pick(grp, [srow[gi * per + k] for gi in range(N_EXPERT_GROUPS)]) for k in range(per)]
    a = [_pick(grp, [arow[gi * per + k] for gi in range(N_EXPERT_GROUPS)]) for k in range(per)]
    _, i1, _, i2 = _top2(v)
    w1 = _pick(i1, a)
    w2 = _pick(i2, a)
    den = w1 + w2
    e1 = grp * per + i1
    e2 = grp * per + i2

    erow = lax.broadcasted_iota(jnp.int32, (n_exp, TM), 0)
    hit1 = erow == e1
    hit2 = erow == e2
    onehot = jnp.where(hit1 | hit2, 1.0, 0.0)
    cnt = jnp.sum(onehot, axis=1, keepdims=True)
    before = (lax.broadcasted_iota(jnp.int32, (TM, TM), 0)
              < lax.broadcasted_iota(jnp.int32, (TM, TM), 1)).astype(BF16)
    rank = _dot(onehot.astype(BF16), before)
    padded = ((cnt.astype(jnp.int32) + (SUBLANES - 1)) & (-SUBLANES)).astype(F32)
    lower = (lax.broadcasted_iota(jnp.int32, (n_exp, n_exp), 1)
             < lax.broadcasted_iota(jnp.int32, (n_exp, n_exp), 0)).astype(BF16)
    starts = _dot(lower, jnp.broadcast_to(padded, (n_exp, LANES)).astype(BF16))
    pos = starts[:, 0:1] + rank
    r1 = jnp.sum(jnp.where(hit1, pos, 0.0), axis=0, keepdims=True)
    r2 = jnp.sum(jnp.where(hit2, pos, 0.0), axis=0, keepdims=True)
    route = jnp.concatenate([r1, r2, w1 / den, w2 / den, jnp.zeros((LANES - 4, TM), F32)], axis=0)
    return route, jnp.broadcast_to(cnt, (n_exp, LANES))


def _mix_body(hg_ref, yf_ref, sa_ref, sb_ref, x_ref, mod_ref, wpr_ref, wpf_ref, wo_ref,
              gffn_ref, rw_ref, rb_ref, xo_ref, xl_ref, rc_ref, cnt_ref, t_scr, *, d, n_exp, nb):
    mod = mod_ref[0]
    f = wpf_ref.shape[0]
    y_r = _dot(hg_ref[...], wpr_ref[...])
    yf = _to_time_major([yf_ref[:, b * f:(b + 1) * f] for b in range(nb)], t_scr, nb)
    y_f = _dot(yf.astype(BF16), wpf_ref[...])
    merged = sa_ref[...].astype(F32) * y_r + sb_ref[...].astype(F32) * y_f
    out = _dot(merged.astype(BF16), wo_ref[...])
    xn = x_ref[...] + _per_batch(out, lambda v: v * mod[:, 2 * d:3 * d][None])
    xo_ref[...] = xn
    h2 = _rms(xn, gffn_ref[...])
    h2 = _per_batch(h2, lambda v: v * (1.0 + mod[:, 4 * d:5 * d])[None] + mod[:, 3 * d:4 * d][None])
    rows = _local_rows(n_exp)
    riota = lax.broadcasted_iota(jnp.int32, (rows, TM), 0)
    for k in range(TR // TM):
        hw = h2[k * TM:(k + 1) * TM, :]
        route, cnt = _route_window(hw, rw_ref[...], rb_ref[...], n_exp)
        rc_ref[k * TM:(k + 1) * TM, :] = route.T
        cnt_ref[k] = cnt
        r12 = route[0:2, :].astype(jnp.int32)
        p = jnp.where((riota == r12[0:1, :]) | (riota == r12[1:2, :]), 1.0, 0.0).astype(BF16)
        xl_ref[k * rows:(k + 1) * rows, :] = _pack_bf16_pairs(_dot(p, hw.astype(BF16)))


def _mix(hg, yf, sa, sb, xa, mod, wpr, wpf, wo, gffn, rw, rb, *, n_rows, n_lat_rows, nb):
    d = xa.shape[1]
    f = wpf.shape[0]
    n_exp = rb.shape[0]
    n_lat = n_lat_rows // TR
    wpt = TR // TM
    n_win = n_rows // TM
    lrows = _local_rows(n_exp)
    row = lambda width: pl.BlockSpec((TR, width), lambda i: (i, 0))
    return pl.pallas_call(
        functools.partial(_mix_body, d=d, n_exp=n_exp, nb=nb),
        grid=(n_rows // TR,),
        in_specs=[
            row(d), pl.BlockSpec((TR // nb, nb * f), lambda i: (i, 0)), row(d), row(d), row(d),
            pl.BlockSpec((1, SUBLANES, N_MOD * d), lambda i: (jnp.where(i < n_lat, 0, 1), 0, 0)),
            _full(wpr.shape), _full(wpf.shape), _full(wo.shape),
            _full((1, d)), _full(rw.shape), _full((n_exp, 1)),
        ],
        scratch_shapes=[pltpu.VMEM((f // LANES, TR, LANES), F32)],
        out_specs=[
            row(d),
            pl.BlockSpec((wpt * lrows, d // 2), lambda i: (i, 0)),
            row(LANES),
            pl.BlockSpec((wpt, n_exp, LANES), lambda i: (i, 0, 0)),
        ],
        out_shape=[
            jax.ShapeDtypeStruct((n_rows, d), F32),
            jax.ShapeDtypeStruct((n_win * lrows, d // 2), jnp.uint32),
            jax.ShapeDtypeStruct((n_rows, LANES), F32),
            jax.ShapeDtypeStruct((n_win, n_exp, LANES), F32),
        ],
        compiler_params=_params(("arbitrary",)),
        name="mix_out",
    )(hg, yf, sa, sb, xa, mod, wpr, wpf, wo, gffn.reshape(1, d), rw, rb.reshape(n_exp, 1))


def _local_rows(n_exp):
    return -(-(2 * TM + n_exp * (SUBLANES - 1)) // LANES) * LANES


def _pack_bf16_pairs(x):
    half = x.shape[1] // 2
    bits = lax.bitcast_convert_type(x, jnp.uint32)
    return (bits[:, :half] & jnp.uint32(0xFFFF0000)) | lax.shift_right_logical(bits[:, half:], jnp.uint32(16))


def _unpack_bf16_pairs(u):
    hi = lax.bitcast_convert_type(u & jnp.uint32(0xFFFF0000), F32)
    lo = lax.bitcast_convert_type(lax.shift_left(u, jnp.uint32(16)), F32)
    return jnp.concatenate([hi, lo], axis=-1)


CPT = MOE_TM // SUBLANES


def _group_copies(table_ref, hbm, buf, sem, *, gather):
    for j in range(CPT):
        far = hbm.at[pl.ds(pl.multiple_of(table_ref[0, 0, j] * SUBLANES, SUBLANES), SUBLANES)]
        near = buf.at[pl.ds(j * SUBLANES, SUBLANES)]
        (pltpu.make_async_copy(far, near, sem) if gather else pltpu.make_async_copy(near, far, sem)).start()


def _tile_wait(hbm, buf, sem, *, gather):
    far = hbm.at[pl.ds(0, MOE_TM)]
    (pltpu.make_async_copy(far, buf, sem) if gather else pltpu.make_async_copy(buf, far, sem)).wait()


def _gmm_body(te_ref, nu_ref, src_ref, nxt_ref, dst_ref, xl_ref, w1_ref, w3_ref, w2_ref, yl_ref,
              w1b, w3b, w2b, xbuf, ybuf, gsem, ssem):
    t = pl.program_id(0)
    n_used = nu_ref[0]
    slot = lax.rem(t, 2)
    other = 1 - slot

    @pl.when((t == 0) | (te_ref[t] != te_ref[jnp.maximum(t - 1, 0)]))
    def _():
        w1b[...] = w1_ref[0, 0].astype(BF16)
        w3b[...] = w3_ref[0, 0].astype(BF16)
        w2b[...] = w2_ref[0, 0].astype(BF16)

    @pl.when(t < n_used)
    def _():
        @pl.when(t == 0)
        def _():
            _group_copies(src_ref, xl_ref, xbuf.at[slot], gsem.at[slot], gather=True)

        _tile_wait(xl_ref, xbuf.at[slot], gsem.at[slot], gather=True)
        _group_copies(nxt_ref, xl_ref, xbuf.at[other], gsem.at[other], gather=True)
        x = _unpack_bf16_pairs(xbuf[slot]).astype(BF16)
        a = _dot(x, w1b[...])
        h = (a * jax.nn.sigmoid(a)) * _dot(x, w3b[...])
        y = _dot(h.astype(BF16), w2b[...])

        @pl.when(t >= 2)
        def _():
            _tile_wait(yl_ref, ybuf.at[slot], ssem.at[slot], gather=False)

        ybuf[slot] = _pack_bf16_pairs(y.astype(BF16).astype(F32))
        _group_copies(dst_ref, yl_ref, ybuf.at[slot], ssem.at[slot], gather=False)

        @pl.when(t == n_used - 1)
        def _():
            _tile_wait(xl_ref, xbuf.at[other], gsem.at[other], gather=True)

            @pl.when(t >= 1)
            def _():
                _tile_wait(yl_ref, ybuf.at[other], ssem.at[other], gather=False)

            _tile_wait(yl_ref, ybuf.at[slot], ssem.at[slot], gather=False)


def _gmm(tile_exp, n_used, src, dst, xl, w1, w3, w2, *, layer, out_rows):
    n_tiles = src.shape[0]
    d, de = w1.shape[2], w1.shape[3]
    wspec = lambda w: pl.BlockSpec((1, 1) + w.shape[2:], lambda t, te, nu: (layer, te[t], 0, 0))
    table = lambda idx: pl.BlockSpec((1, 1, CPT), lambda t, te, nu: (idx(t, nu), 0, 0), memory_space=pltpu.SMEM)
    cur = lambda t, nu: jnp.minimum(t, nu[0] - 1)
    grid_spec = pltpu.PrefetchScalarGridSpec(
        num_scalar_prefetch=2,
        grid=(n_tiles,),
        in_specs=[table(cur), table(lambda t, nu: jnp.minimum(t + 1, nu[0] - 1)), table(cur),
                  pl.BlockSpec(memory_space=pl.ANY), wspec(w1), wspec(w3), wspec(w2)],
        out_specs=pl.BlockSpec(memory_space=pl.ANY),
        scratch_shapes=[
            pltpu.VMEM((d, de), BF16), pltpu.VMEM((d, de), BF16), pltpu.VMEM((de, d), BF16),
            pltpu.VMEM((2, MOE_TM, d // 2), jnp.uint32), pltpu.VMEM((2, MOE_TM, d // 2), jnp.uint32),
            pltpu.SemaphoreType.DMA((2,)), pltpu.SemaphoreType.DMA((2,)),
        ],
    )
    return pl.pallas_call(
        _gmm_body,
        grid_spec=grid_spec,
        out_shape=jax.ShapeDtypeStruct((out_rows, d // 2), jnp.uint32),
        compiler_params=_params(("arbitrary",)),
        name="moe_experts",
    )(tile_exp, n_used, src, src, dst, xl, w1, w3, w2)


def _combine_body(*refs, d, rows, nb, final):
    if final:
        tot_ref, yl_ref, rc_ref, xr_ref, mod_ref, gfin_ref, o_ref, t_scr = refs
    else:
        tot_ref, yl_ref, rc_ref, xr_ref, mod_ref, o_ref = refs
    wpt = TR // TM
    ci = lax.broadcasted_iota(jnp.int32, (TM, rows), 1)
    parts = []
    for k in range(wpt):
        keep = lax.broadcasted_iota(jnp.int32, (rows, 1), 0) < tot_ref[pl.program_id(0) * wpt + k]
        y = _unpack_bf16_pairs(jnp.where(keep, yl_ref[k * rows:(k + 1) * rows, :], jnp.uint32(0))).astype(BF16)
        rc = rc_ref[k * TM:(k + 1) * TM, :]
        pw = (jnp.where(ci == rc[:, 0:1].astype(jnp.int32), rc[:, 2:3], 0.0)
              + jnp.where(ci == rc[:, 1:2].astype(jnp.int32), rc[:, 3:4], 0.0))
        parts.append(_dot(pw.astype(BF16), y))
    moe = jnp.concatenate(parts, axis=0)
    g2 = mod_ref[0][:, 5 * d:6 * d]
    xn = xr_ref[...] + _per_batch(moe, lambda v: v * g2[None])
    if final:
        xn = _rms(xn, gfin_ref[...])
        for b, blk in enumerate(_to_batch_major(xn, t_scr, nb)):
            o_ref[b] = blk
    else:
        o_ref[...] = xn


def _combine(totals, yl, rc, xres, mod, *, n_exp, n_win, n_lat_rows, nb, gfin=None, seq_out=None):
    d = xres.shape[1]
    rows = _local_rows(n_exp)
    final = gfin is not None
    n_lat = n_lat_rows // TR
    wpt = TR // TM
    row = lambda width: pl.BlockSpec((TR, width), lambda w, a: (w, 0))
    in_specs = [pl.BlockSpec((wpt * rows, d // 2), lambda w, a: (w, 0)), row(LANES), row(d),
                pl.BlockSpec((1, SUBLANES, N_MOD * d), lambda w, a: (jnp.where(w < n_lat, 0, 1), 0, 0))]
    args = [totals, yl, rc, xres, mod]
    scratch = []
    if final:
        in_specs.append(pl.BlockSpec((1, d), lambda w, a: (0, 0)))
        args.append(gfin.reshape(1, d))
        out_shape = jax.ShapeDtypeStruct((nb, seq_out, d), F32)
        out_specs = pl.BlockSpec((nb, TR // nb, d), lambda w, a: (0, w, 0))
        scratch.append(pltpu.VMEM((d // LANES, TR, LANES), F32))
        aliases = {}
    else:
        out_shape = jax.ShapeDtypeStruct(xres.shape, F32)
        out_specs = row(d)
        aliases = {3: 0}
    grid_spec = pltpu.PrefetchScalarGridSpec(
        num_scalar_prefetch=1, grid=(n_win // wpt,), in_specs=in_specs, out_specs=out_specs,
        scratch_shapes=scratch)
    return pl.pallas_call(
        functools.partial(_combine_body, d=d, rows=rows, nb=nb, final=final),
        grid_spec=grid_spec,
        out_shape=out_shape,
        input_output_aliases=aliases,
        compiler_params=_params(("arbitrary",)),
        name="moe_combine",
    )(*args)


def _route_tables(cnt, *, n_exp, n_tiles):
    cnt = cnt[..., 0].astype(jnp.int32)
    n_win = cnt.shape[0]
    lrows = _local_rows(n_exp)
    npad = (cnt + (SUBLANES - 1)) & (-SUBLANES)
    seg = -(-jnp.sum(npad, axis=0) // MOE_TM) * MOE_TM
    ends = jnp.cumsum(seg)
    goff = (ends - seg)[None, :] + jnp.cumsum(npad, axis=0) - npad
    loff = jnp.cumsum(npad, axis=1) - npad
    tile_ends = ends // MOE_TM
    n_used = tile_ends[-1]
    tile = jnp.minimum(jnp.arange(n_tiles, dtype=jnp.int32), n_used - 1)
    tile_exp = jnp.sum(tile[:, None] >= tile_ends[None, :], axis=1).astype(jnp.int32)

    starts = (goff.T // SUBLANES).reshape(-1)
    lens = (npad.T // SUBLANES).reshape(-1)
    local = ((jnp.arange(n_win, dtype=jnp.int32)[:, None] * lrows + loff).T // SUBLANES).reshape(-1)
    group = jnp.arange(n_tiles * CPT, dtype=jnp.int32)
    began = starts[None, :] <= group[:, None]
    last = lambda v: jnp.sum(jnp.where(began, jnp.diff(v, prepend=0)[None, :], 0), axis=1)
    valid = group < last(starts + lens)
    src = jnp.where(valid, group + last(local - starts), 0)
    dst = jnp.where(valid, src, n_win * lrows // SUBLANES + group)
    shape = (n_tiles, 1, CPT)
    return (tile_exp, n_used.reshape(1).astype(jnp.int32), src.reshape(shape).astype(jnp.int32),
            dst.reshape(shape).astype(jnp.int32), jnp.sum(npad, axis=1).astype(jnp.int32))


def _channel_dft_tables(n):
    k = jnp.arange(n, dtype=jnp.int32)
    ang = ((k[:, None] * k[None, :]) % n).astype(F32) * (2.0 * math.pi / n)
    scale = 1.0 / math.sqrt(n)
    return jnp.cos(ang) * scale, jnp.sin(ang) * scale


def kernel(x, c, ctx, c_ctx, ada_w, ada_b, norm_mix_g, w_in, conv_w, conv_b, lru_w_a, lru_b_a, lru_w_x, lru_b_x, lru_lambda, w_proj_rnn, w_proj_fourier, w_out, norm_ffn_g, router_w, router_b, moe_w1, moe_w3, moe_w2, final_norm_g):
    nb, l, d = x.shape
    lc = ctx.shape[1]
    depth = ada_w.shape[0]
    r = conv_w.shape[2]
    f = w_proj_fourier.shape[1]
    nh, blk = lru_w_a.shape[2], lru_w_a.shape[3]
    assert nb == SUBLANES and blk == LANES and nb < MOD_ROWS
    assert l % TS == 0 and lc % TS == 0 and TS % GRID_W == 0 and (l * nb) % TR == 0 and (lc * nb) % TR == 0
    assert GRID_W & (GRID_W - 1) == 0 and l % lc == 0 and l % (2 * FOLD_TILE) == 0 and l // LANES <= LANES
    s = l + lc
    n_lat_rows = l * nb

    cc = jnp.zeros((MOD_ROWS, d), F32).at[:nb].set(c).at[nb].set(c_ctx)
    mod = _ada(cc, ada_w, ada_b)
    mod = jnp.stack([mod[:, :nb], jnp.broadcast_to(mod[:, nb:nb + 1], (depth, nb, N_MOD * d))], axis=1)

    cch, sch = _channel_dft_tables(f // FOURIER_GROUPS)
    eye = jnp.eye(FOURIER_GROUPS, dtype=F32)
    fcs = jnp.concatenate([jnp.kron(eye, cch), jnp.kron(eye, sch)], axis=1).astype(BF16)
    n_exp = router_w.shape[1]
    rw = jnp.pad(router_w, ((0, 0), (0, LANES - n_exp)))

    out = xa = None
    for li in range(depth):
        last = li == depth - 1
        inproj = functools.partial(_inproj, mod=mod[li], g=norm_mix_g[li], w_in=w_in[li].astype(BF16), fcs=fcs,
                                   n_lat_rows=n_lat_rows, r=r, nb=nb)
        if li == 0:
            u, gl, sa, sb, pq, xa = inproj(x, ctx=ctx)
        else:
            u, gl, sa, sb, pq = inproj(xa)
        wg = [jnp.concatenate([lru_w_a[li, k], lru_w_x[li, k]], axis=-1).astype(BF16) for k in range(2)]
        bg = [jnp.concatenate([lru_b_a[li, k], lru_b_x[li, k]], axis=-1).reshape(nh, 1, 2 * blk) for k in range(2)]
        lam = [lru_lambda[li, k].reshape(nh, 1, blk) for k in range(2)]
        hb, uc = _scan(u, wg[1], bg[1], lam[1], nb=nb, n_lat_steps=l // TS, conv=(conv_w[li], conv_b[li]))
        hg = _scan(uc, wg[0], bg[0], lam[0], nb=nb, n_lat_steps=l // TS, hb=hb, gl=gl)
        n_rows = n_lat_rows if last else s * nb
        yf = _dft(pq, n=l, nb=nb, f=f, row0=0, out_rows=n_rows // nb, folded=_fold(pq, n=l, nb=nb, f=f))
        if not last:
            yf = _dft(pq, n=lc, nb=nb, f=f, row0=l, out_rows=n_rows // nb, prev=yf)
        x1, xl, rc, cnt = _mix(hg, yf, sa, sb, xa, mod[li], w_proj_rnn[li].astype(BF16),
                               w_proj_fourier[li].astype(BF16), w_out[li].astype(BF16), norm_ffn_g[li],
                               rw, router_b, n_rows=n_rows, n_lat_rows=n_lat_rows, nb=nb)
        n_win = n_rows // TM
        n_tiles = -(-(n_win * (2 * TM + n_exp * (SUBLANES - 1)) + n_exp * MOE_TM) // MOE_TM)
        tile_exp, n_used, src, dst, totals = _route_tables(cnt, n_exp=n_exp, n_tiles=n_tiles)
        yl = _gmm(tile_exp, n_used, src, dst, xl, moe_w1, moe_w3, moe_w2, layer=li,
                  out_rows=xl.shape[0] + n_tiles * MOE_TM)
        comb = functools.partial(_combine, totals, yl, rc, x1, mod[li], n_exp=n_exp, n_win=n_win,
                                 n_lat_rows=n_lat_rows, nb=nb)
        if last:
            out = comb(gfin=final_norm_g, seq_out=l)
        else:
            xa = comb()
    return out
```

```python
import functools
import math

import jax
import jax.numpy as jnp
from jax import lax
from jax.experimental import pallas as pl
from jax.experimental.pallas import tpu as pltpu

F32 = jnp.float32
BF16 = jnp.bfloat16

RMS_EPS = 1e-6
LRU_C = 8.0
GRID_W = 64
CONV_TAPS = ((0, -2), (1, -1), (2, 0), (3, 1))
N_MOD = 6
FOURIER_GROUPS = 4
N_EXPERT_GROUPS = 4
LANES = 128
SUBLANES = 8
PACKED_ROWS = 16
TR = 512
TM = 256
TS = 128
SUB = 64
MOE_TM = 512
MOD_ROWS = 16
VMEM_LIMIT = 56 * 1024 * 1024


def _dot(a, b):
    return jnp.dot(a, b, preferred_element_type=F32)


def _split(a):
    hi = a.astype(BF16)
    lo = (a - hi.astype(F32)).astype(BF16)
    return hi, lo


def _dot3(a, b):
    ah, al = _split(a)
    bh, bl = _split(b)
    return _dot(ah, bh) + _dot(al, bh) + _dot(ah, bl)


def _gelu_tanh(x):
    return 0.5 * x * (1.0 + jnp.tanh(math.sqrt(2.0 / math.pi) * (x + 0.044715 * (x * x * x))))


def _sigmoid(x):
    return 0.5 * jnp.tanh(0.5 * x) + 0.5


def _rms(x, g):
    return x * lax.rsqrt(jnp.mean(x * x, axis=-1, keepdims=True) + RMS_EPS) * g


def _per_batch(x, fn):
    rows, d = x.shape
    return fn(x.reshape(rows // SUBLANES, SUBLANES, d)).reshape(rows, d)


def _params(sem, vmem=VMEM_LIMIT):
    return pltpu.CompilerParams(dimension_semantics=sem, vmem_limit_bytes=vmem)


def _full(shape):
    zeros = (0,) * len(shape)
    return pl.BlockSpec(shape, lambda *_: zeros)


def _ada_body(cc_ref, w_ref, b_ref, o_ref):
    cc = cc_ref[...]
    s = cc * jax.nn.sigmoid(cc)
    o_ref[0] = _dot3(s, w_ref[0]) + b_ref[0]


def _ada(cc, ada_w, ada_b):
    depth, d, n = ada_w.shape
    tn = 1024
    return pl.pallas_call(
        _ada_body,
        grid=(depth, n // tn),
        in_specs=[
            pl.BlockSpec((MOD_ROWS, d), lambda l, j: (0, 0)),
            pl.BlockSpec((1, d, tn), lambda l, j: (l, 0, j)),
            pl.BlockSpec((1, 1, tn), lambda l, j: (l, 0, j)),
        ],
        out_specs=pl.BlockSpec((1, MOD_ROWS, tn), lambda l, j: (l, 0, j)),
        out_shape=jax.ShapeDtypeStruct((depth, MOD_ROWS, n), F32),
        compiler_params=_params(("arbitrary", "arbitrary")),
        name="ada_mod",
    )(cc, ada_w, ada_b.reshape(depth, 1, n))


def _to_time_major(blocks, scr, nb):
    npos, width = blocks[0].shape
    for b in range(nb):
        for g in range(width // LANES):
            scr[g, pl.ds(b, npos, stride=nb), :] = blocks[b][:, g * LANES:(g + 1) * LANES].astype(F32)
    return jnp.concatenate([scr[g] for g in range(width // LANES)], axis=-1)


def _to_batch_major(v, scr, nb):
    rows, width = v.shape
    for g in range(width // LANES):
        scr[g] = v[:, g * LANES:(g + 1) * LANES]
    return [jnp.concatenate([scr[g, pl.ds(b, rows // nb, stride=nb), :] for g in range(width // LANES)], axis=-1)
            for b in range(nb)]


def _inproj_body(*refs, d, r, f, nb, n_lat, first):
    if first:
        (x_ref, c_ref, mod_ref, g_ref, w_ref, fcs_ref,
         u_ref, gl_ref, sa_ref, sb_ref, pq_ref, xa_ref, t_scr) = refs

        @pl.when(pl.program_id(0) < n_lat)
        def _():
            xa_ref[...] = _to_time_major([x_ref[b] for b in range(nb)], t_scr, nb)

        @pl.when(pl.program_id(0) >= n_lat)
        def _():
            xa_ref[...] = _to_time_major([c_ref[b] for b in range(nb)], t_scr, nb)

        x = xa_ref[...]
    else:
        x_ref, mod_ref, g_ref, w_ref, fcs_ref, u_ref, gl_ref, sa_ref, sb_ref, pq_ref, t_scr = refs
        x = x_ref[...]
    mod = mod_ref[0]
    h = _rms(x, g_ref[...])
    h = _per_batch(h, lambda v: v * (1.0 + mod[:, d:2 * d])[None] + mod[:, 0:d][None]).astype(BF16)
    u_ref[...] = _dot(h, w_ref[:, 0:r]).astype(BF16)
    gl_ref[...] = _gelu_tanh(_dot(h, w_ref[:, r:2 * r])).astype(BF16)
    u4 = _dot(h, w_ref[:, 2 * r:2 * r + f]).astype(BF16)
    fg = fcs_ref.shape[0]
    pq_g = [_dot(u4[:, k * fg:(k + 1) * fg], fcs_ref[...]) for k in range(f // fg)]
    pq = jnp.concatenate([v[:, :fg] for v in pq_g] + [v[:, fg:] for v in pq_g], axis=-1)
    for b, blk in enumerate(_to_batch_major(pq, t_scr, nb)):
        pq_ref[:, b * 2 * f:(b + 1) * 2 * f] = blk.astype(BF16)
    s3 = 2 * r + f
    sa_ref[...] = jax.nn.sigmoid(_dot(h, w_ref[:, s3:s3 + d])).astype(BF16)
    sb_ref[...] = jax.nn.sigmoid(_dot(h, w_ref[:, s3 + d:s3 + 2 * d])).astype(BF16)


def _inproj(x, mod, g, w_in, fcs, *, n_lat_rows, r, f, nb, ctx=None):
    first = ctx is not None
    d = x.shape[-1]
    rows = (x.shape[1] + ctx.shape[1]) * nb if first else x.shape[0]
    n_lat = n_lat_rows // TR
    npos = TR // nb
    row = lambda width: pl.BlockSpec((TR, width), lambda i: (i, 0))
    common = [
        pl.BlockSpec((1, SUBLANES, N_MOD * d), lambda i: (jnp.where(i < n_lat, 0, 1), 0, 0)),
        _full((1, d)), _full(w_in.shape), _full(fcs.shape),
    ]
    out_specs = [row(r), row(r), row(d), row(d), pl.BlockSpec((npos, nb * 2 * f), lambda i: (i, 0))]
    out_shape = [
        jax.ShapeDtypeStruct((rows, r), BF16),
        jax.ShapeDtypeStruct((rows, r), BF16),
        jax.ShapeDtypeStruct((rows, d), BF16),
        jax.ShapeDtypeStruct((rows, d), BF16),
        jax.ShapeDtypeStruct((rows // nb, nb * 2 * f), BF16),
    ]
    if first:
        in_specs = [
            pl.BlockSpec((nb, npos, d), lambda i: (0, jnp.minimum(i, n_lat - 1), 0)),
            pl.BlockSpec((nb, npos, d), lambda i: (0, jnp.maximum(i - n_lat, 0), 0)),
        ] + common
        args = [x, ctx]
        out_specs.append(row(d))
        out_shape.append(jax.ShapeDtypeStruct((rows, d), F32))
    else:
        in_specs = [row(d)] + common
        args = [x]
    return pl.pallas_call(
        functools.partial(_inproj_body, d=d, r=r, f=f, nb=nb, n_lat=n_lat, first=first),
        grid=(rows // TR,),
        in_specs=in_specs,
        out_specs=out_specs,
        out_shape=out_shape,
        scratch_shapes=[pltpu.VMEM((d // LANES, TR, LANES), F32)],
        compiler_params=_params(("arbitrary",)),
        name="in_proj",
    )(*args, mod, g.reshape(1, d), w_in, fcs)


def _conv_tile(u_ref, up_ref, un_ref, cw_ref, cb_ref, uc_scr, blk, *, nb, n_lat_steps):
    trows = TS * nb
    n_steps = pl.num_programs(0)
    is_ctx = blk >= n_lat_steps
    keep_prev = jnp.where(is_ctx & (blk > n_lat_steps), 1.0, 0.0)
    keep_next = jnp.where(is_ctx & (blk < n_steps - 1), 1.0, 0.0)
    lat = jnp.where(is_ctx, 0.0, 1.0)
    halo = 2 * nb
    ext = jnp.concatenate([up_ref[...].astype(F32) * keep_prev, u_ref[...].astype(F32),
                           un_ref[...].astype(F32)[0:nb] * keep_next], axis=0)
    uc = cb_ref[...]
    for k, s in CONV_TAPS:
        lo = halo + s * nb
        uc = uc + cw_ref[k:k + 1, :] * ext[lo:lo + trows, :]
    uc_scr[...] = uc
    for p in range(GRID_W, TS, GRID_W):
        at = lambda q: ext[halo + q * nb:halo + (q + 1) * nb, :]
        rows_of = lambda q: pl.ds(q * nb, nb)
        uc_scr[rows_of(p), :] -= lat * (cw_ref[0:1, :] * at(p - 2) + cw_ref[1:2, :] * at(p - 1))
        uc_scr[rows_of(p + 1), :] -= lat * (cw_ref[0:1, :] * at(p - 1))
        uc_scr[rows_of(p - 1), :] -= lat * (cw_ref[3:4, :] * at(p))


def _scan_body(*refs, reverse, merge, nb, n_lat_steps, order):
    if merge:
        uc_ref, wg_ref, bg_ref, lam_ref, hb_ref, gl_ref, o_ref, h_scr, a_scr, b_scr, uc_scr, hs_scr = refs
    else:
        (u_ref, up_ref, un_ref, cw_ref, cb_ref, wg_ref, bg_ref, lam_ref,
         o_ref, uco_ref, h_scr, a_scr, b_scr, uc_scr, hs_scr) = refs
    nh = wg_ref.shape[0]
    rows = SUB * nb
    j = pl.program_id(0)

    @pl.when(j == 0)
    def _():
        h_scr[...] = jnp.zeros_like(h_scr)

    if merge:
        uc_scr[...] = uc_ref[...].astype(F32)
    else:
        _conv_tile(u_ref, up_ref, un_ref, cw_ref, cb_ref, uc_scr, order(j), nb=nb, n_lat_steps=n_lat_steps)
        uco_ref[...] = uc_scr[...].astype(BF16)

    lam = lam_ref[...]
    half_decay = (-0.5 * LRU_C) * (jnp.maximum(-lam, 0.0) + jnp.log1p(jnp.exp(-jnp.abs(lam))))

    subs = range(TS // SUB)
    for sub in (reversed(subs) if reverse else subs):
        r0 = sub * rows
        for g in range(nh):
            ug = uc_scr[r0:r0 + rows, g * LANES:(g + 1) * LANES]
            gates = _dot(ug.astype(BF16), wg_ref[g]) + bg_ref[g]
            log_a = half_decay[g] * jnp.tanh(0.5 * gates[:, :LANES]) + half_decay[g]
            a = jnp.exp(log_a)
            z = (-1.0 - a * a) * jnp.tanh(log_a)
            mult = jnp.where(z > 0.0, z * lax.rsqrt(z), 0.0)
            a_scr[g] = a
            b_scr[g] = mult * (_sigmoid(gates[:, LANES:]) * ug)

        def step(k, hs):
            t = (SUB - 1 - k) if reverse else k
            off = pl.multiple_of(t * nb, nb)
            new = []
            for g in range(nh):
                hg = a_scr[g, pl.ds(off, nb), :] * hs[g] + b_scr[g, pl.ds(off, nb), :]
                hs_scr[pl.ds(r0 + off, nb), g * LANES:(g + 1) * LANES] = hg
                new.append(hg)
            return tuple(new)

        hs = lax.fori_loop(0, SUB, step, tuple(h_scr[g] for g in range(nh)), unroll=8)
        for g in range(nh):
            h_scr[g] = hs[g]

    if merge:
        o_ref[...] = ((hs_scr[...] + hb_ref[...].astype(F32)) * gl_ref[...].astype(F32)).astype(BF16)
    else:
        o_ref[...] = hs_scr[...].astype(BF16)


def _scan(u, wg, bg, lam, *, nb, n_lat_steps, conv=None, hb=None, gl=None):
    rows_all, r = u.shape
    nh = wg.shape[0]
    trows = TS * nb
    n = rows_all // trows
    n_ctx = n - n_lat_steps
    merge = hb is not None
    reverse = not merge
    if reverse:
        order = lambda j: jnp.where(j < n_ctx, n - 1 - j, n_lat_steps - 1 - (j - n_ctx))
    else:
        order = lambda j: jnp.where(j < n_ctx, n_lat_steps + j, j - n_ctx)
    hpb = trows // PACKED_ROWS
    tile = pl.BlockSpec((trows, r), lambda j: (order(j), 0))
    gate_specs = [_full(wg.shape), _full(bg.shape), _full(lam.shape)]
    out = jax.ShapeDtypeStruct((rows_all, r), BF16)
    if merge:
        in_specs = [tile] + gate_specs + [tile, tile]
        args = [u, wg, bg, lam, hb, gl]
        out_specs, out_shape = tile, out
    else:
        conv_w, conv_b = conv
        in_specs = [
            tile,
            pl.BlockSpec((PACKED_ROWS, r), lambda j: (jnp.maximum(order(j) * hpb - 1, 0), 0)),
            pl.BlockSpec((PACKED_ROWS, r), lambda j: (jnp.minimum((order(j) + 1) * hpb, n * hpb - 1), 0)),
            _full(conv_w.shape), _full((1, r)),
        ] + gate_specs
        args = [u, u, u, conv_w, conv_b.reshape(1, r), wg, bg, lam]
        out_specs, out_shape = [tile, tile], [out, out]
    return pl.pallas_call(
        functools.partial(_scan_body, reverse=reverse, merge=merge, nb=nb, n_lat_steps=n_lat_steps,
                          order=order),
        grid=(n,),
        in_specs=in_specs,
        out_specs=out_specs,
        out_shape=out_shape,
        scratch_shapes=[
            pltpu.VMEM((nh, nb, LANES), F32),
            pltpu.VMEM((nh, SUB * nb, LANES), F32),
            pltpu.VMEM((nh, SUB * nb, LANES), F32),
            pltpu.VMEM((trows, r), F32),
            pltpu.VMEM((trows, r), F32),
        ],
        compiler_params=_params(("arbitrary",)),
        name="scan_fwd_merge" if merge else "scan_bwd",
    )(*args)


FOLD_TILE = 256


def _fold_body(a_ref, b1_ref, b2_ref, o_ref, *, f, nb):
    m = pl.program_id(0)
    i = lax.broadcasted_iota(jnp.int32, (FOLD_TILE, 2 * FOLD_TILE), 0)
    col = lax.broadcasted_iota(jnp.int32, (FOLD_TILE, 2 * FOLD_TILE), 1)
    sel = (col == jnp.where(i == 0, FOLD_TILE, FOLD_TILE - i)) & ((i > 0) | (m > 0))
    pad = jnp.zeros((FOLD_TILE - PACKED_ROWS, b1_ref.shape[1]), BF16)
    mirrored = _dot(jnp.where(sel, 1.0, 0.0).astype(BF16),
                    jnp.concatenate([b1_ref[...], b2_ref[...], pad], axis=0))
    for b in range(nb):
        p = slice(b * 2 * f, b * 2 * f + f)
        q = slice(b * 2 * f + f, (b + 1) * 2 * f)
        o_ref[:, p] = (a_ref[:, p].astype(F32) + mirrored[:, p]).astype(BF16)
        o_ref[:, q] = (a_ref[:, q].astype(F32) - mirrored[:, q]).astype(BF16)


def _fold(pq, *, n, nb, f):
    nt = n // FOLD_TILE
    hpt = FOLD_TILE // PACKED_ROWS
    width = nb * 2 * f
    tile = lambda idx: pl.BlockSpec((FOLD_TILE, width), lambda m: (idx(m), 0))
    return pl.pallas_call(
        functools.partial(_fold_body, f=f, nb=nb),
        grid=(nt // 2,),
        in_specs=[tile(lambda m: m), tile(lambda m: nt - 1 - m),
                  pl.BlockSpec((PACKED_ROWS, width), lambda m: (jnp.minimum(nt - m, nt - 1) * hpt, 0))],
        out_specs=tile(lambda m: m),
        out_shape=jax.ShapeDtypeStruct((n // 2, width), BF16),
        compiler_params=_params(("arbitrary",)),
        name="dft_fold",
    )(pq, pq, pq)


def _dft_body(ca_ref, sa_ref, cb_ref, sb_ref, p_ref, q_ref, *rest, kdim, folded, scale):
    o_ref, c_scr, s_scr = rest[-3:]

    @pl.when(pl.program_id(1) == 0)
    def _():
        cb = cb_ref[...]
        sb = sb_ref[...]
        for t1 in range(kdim // LANES):
            ca = ca_ref[:, t1:t1 + 1]
            sa = sa_ref[:, t1:t1 + 1]
            c_scr[:, t1 * LANES:(t1 + 1) * LANES] = (ca * cb - sa * sb).astype(BF16)
            s_scr[:, t1 * LANES:(t1 + 1) * LANES] = (sa * cb + ca * sb).astype(BF16)

    y = _dot(c_scr[...], p_ref[...]) - _dot(s_scr[...], q_ref[...])
    if folded:
        mid_ref = rest[0]
        tmk = o_ref.shape[0]
        k = pl.program_id(0) * tmk + lax.broadcasted_iota(jnp.int32, (tmk, 1), 0)
        sign = (1 - 2 * (k & 1)).astype(F32)
        y = y + sign * (scale * mid_ref[...].astype(F32)[0:1, :])
    o_ref[...] = y.astype(BF16)


def _dft_tables(n):
    k = jnp.arange(n, dtype=jnp.int32)[:, None]
    t1 = jnp.arange(LANES, dtype=jnp.int32)[None, :]
    ang_a = ((k * ((t1 * LANES) % n)) % n).astype(F32) * (2.0 * math.pi / n)
    ang_b = ((k * t1) % n).astype(F32) * (2.0 * math.pi / n)
    scale = 1.0 / math.sqrt(n)
    return jnp.cos(ang_a), jnp.sin(ang_a), jnp.cos(ang_b) * scale, jnp.sin(ang_b) * scale


def _dft(pq, *, n, nb, f, row0, out_rows, prev=None, folded=None):
    tmk = min(n, 512)
    kdim = n // 2 if folded is not None else n
    tab = pl.BlockSpec((tmk, LANES), lambda m, b: (m, 0))
    if folded is not None:
        assert row0 == 0
        src, rb = folded, 0
    else:
        src, rb = pq, row0 // n
    in_specs = [
        tab, tab, tab, tab,
        pl.BlockSpec((kdim, f), lambda m, b: (rb, 2 * b)),
        pl.BlockSpec((kdim, f), lambda m, b: (rb, 2 * b + 1)),
    ]
    args = list(_dft_tables(n)) + [src, src]
    if folded is not None:
        in_specs.append(pl.BlockSpec((PACKED_ROWS, f), lambda m, b: (kdim // PACKED_ROWS, 2 * b)))
        args.append(pq)
    aliases = {}
    if prev is not None:
        in_specs.append(pl.BlockSpec(memory_space=pl.ANY))
        args.append(prev)
        aliases = {len(args) - 1: 0}
    return pl.pallas_call(
        functools.partial(_dft_body, kdim=kdim, folded=folded is not None, scale=1.0 / math.sqrt(n)),
        grid=(n // tmk, nb),
        in_specs=in_specs,
        out_specs=pl.BlockSpec((tmk, f), lambda m, b: (row0 // tmk + m, b)),
        out_shape=jax.ShapeDtypeStruct((out_rows, nb * f), BF16),
        scratch_shapes=[pltpu.VMEM((tmk, kdim), BF16), pltpu.VMEM((tmk, kdim), BF16)],
        input_output_aliases=aliases,
        compiler_params=_params(("arbitrary", "arbitrary")),
        name="pos_dft",
    )(*args)


def _top2(vals):
    def first_max(vs):
        m = functools.reduce(jnp.maximum, vs)
        idx = jnp.full(m.shape, len(vs) - 1, jnp.int32)
        for k in range(len(vs) - 2, -1, -1):
            idx = jnp.where(vs[k] == m, k, idx)
        return m, idx
    m1, i1 = first_max(vals)
    rest = [jnp.where(i1 == k, -jnp.inf, v) for k, v in enumerate(vals)]
    m2, i2 = first_max(rest)
    return m1, i1, m2, i2


def _pick(idx, vals):
    out = vals[-1]
    for k in range(len(vals) - 2, -1, -1):
        out = jnp.where(idx == k, vals[k], out)
    return out


def _route_window(h2, rw, rb, n_exp):
    hh, hl = _split(h2)
    both = _dot(hh, rw)
    logits = both[:, :LANES] + both[:, LANES:] + _dot(hl, rw[:, :LANES])
    aff = jax.nn.sigmoid(logits.T[0:n_exp, :])
    sel = aff + rb
    per = n_exp // N_EXPERT_GROUPS
    srow = [sel[e:e + 1, :] for e in range(n_exp)]
    arow = [aff[e:e + 1, :] for e in range(n_exp)]
    scores = []
    for gi in range(N_EXPERT_GROUPS):
        m1, _, m2, _ = _top2(srow[gi * per:(gi + 1) * per])
        scores.append(m1 + m2)
    best = functools.reduce(jnp.maximum, scores)
    grp = jnp.full(best.shape, N_EXPERT_GROUPS - 1, jnp.int32)
    for gi in range(N_EXPERT_GROUPS - 2, -1, -1):
        grp = jnp.where(scores[gi] == best, gi, grp)
    v = [_pick(grp, [srow[gi * per + k] for gi in range(N_EXPERT_GROUPS)]) for k in range(per)]
    a = [_pick(grp, [arow[gi * per + k] for gi in range(N_EXPERT_GROUPS)]) for k in range(per)]
    _, i1, _, i2 = _top2(v)
    w1 = _pick(i1, a)
    w2 = _pick(i2, a)
    den = w1 + w2
    e1 = grp * per + i1
    e2 = grp * per + i2

    erow = lax.broadcasted_iota(jnp.int32, (n_exp, TM), 0)
    hit1 = erow == e1
    hit2 = erow == e2
    onehot = jnp.where(hit1 | hit2, 1.0, 0.0)
    cnt = jnp.sum(onehot, axis=1, keepdims=True)
    before = (lax.broadcasted_iota(jnp.int32, (TM, TM), 0)
              < lax.broadcasted_iota(jnp.int32, (TM, TM), 1)).astype(BF16)
    rank = _dot(onehot.astype(BF16), before)
    padded = ((cnt.astype(jnp.int32) + (SUBLANES - 1)) & (-SUBLANES)).astype(F32)
    lower = (lax.broadcasted_iota(jnp.int32, (n_exp, n_exp), 1)
             < lax.broadcasted_iota(jnp.int32, (n_exp, n_exp), 0)).astype(BF16)
    starts = _dot(lower, jnp.broadcast_to(padded, (n_exp, LANES)).astype(BF16))
    pos = starts[:, 0:1] + rank
    r1 = jnp.sum(jnp.where(hit1, pos, 0.0), axis=0, keepdims=True)
    r2 = jnp.sum(jnp.where(hit2, pos, 0.0), axis=0, keepdims=True)
    route = jnp.concatenate([r1, r2, w1 / den, w2 / den, jnp.zeros((LANES - 4, TM), F32)], axis=0)
    return route, jnp.broadcast_to(cnt, (n_exp, LANES))


def _mix_body(hg_ref, yf_ref, sa_ref, sb_ref, x_ref, mod_ref, wpr_ref, wpf_ref, wo_ref,
              gffn_ref, rw_ref, rb_ref, xo_ref, xl_ref, rc_ref, cnt_ref, t_scr, *, d, n_exp, nb):
    mod = mod_ref[0]
    f = wpf_ref.shape[0]
    y_r = _dot(hg_ref[...], wpr_ref[...])
    yf = _to_time_major([yf_ref[:, b * f:(b + 1) * f] for b in range(nb)], t_scr, nb)
    y_f = _dot(yf.astype(BF16), wpf_ref[...])
    merged = sa_ref[...].astype(F32) * y_r + sb_ref[...].astype(F32) * y_f
    out = _dot(merged.astype(BF16), wo_ref[...])
    xn = x_ref[...] + _per_batch(out, lambda v: v * mod[:, 2 * d:3 * d][None])
    xo_ref[...] = xn
    h2 = _rms(xn, gffn_ref[...])
    h2 = _per_batch(h2, lambda v: v * (1.0 + mod[:, 4 * d:5 * d])[None] + mod[:, 3 * d:4 * d][None])
    rows = _local_rows(n_exp)
    riota = lax.broadcasted_iota(jnp.int32, (rows, TM), 0)
    for k in range(TR // TM):
        hw = h2[k * TM:(k + 1) * TM, :]
        route, cnt = _route_window(hw, rw_ref[...], rb_ref[...], n_exp)
        rc_ref[k * TM:(k + 1) * TM, :] = route.T
        cnt_ref[k] = cnt
        r12 = route[0:2, :].astype(jnp.int32)
        p = jnp.where((riota == r12[0:1, :]) | (riota == r12[1:2, :]), 1.0, 0.0).astype(BF16)
        xl_ref[k * rows:(k + 1) * rows, :] = _pack_bf16_pairs(_dot(p, hw.astype(BF16)))


def _mix(hg, yf, sa, sb, xa, mod, wpr, wpf, wo, gffn, rw, rb, *, n_rows, n_lat_rows, nb):
    d = xa.shape[1]
    f = wpf.shape[0]
    n_exp = rb.shape[0]
    n_lat = n_lat_rows // TR
    wpt = TR // TM
    n_win = n_rows // TM
    lrows = _local_rows(n_exp)
    row = lambda width: pl.BlockSpec((TR, width), lambda i: (i, 0))
    return pl.pallas_call(
        functools.partial(_mix_body, d=d, n_exp=n_exp, nb=nb),
        grid=(n_rows // TR,),
        in_specs=[
            row(d), pl.BlockSpec((TR // nb, nb * f), lambda i: (i, 0)), row(d), row(d), row(d),
            pl.BlockSpec((1, SUBLANES, N_MOD * d), lambda i: (jnp.where(i < n_lat, 0, 1), 0, 0)),
            _full(wpr.shape), _full(wpf.shape), _full(wo.shape),
            _full((1, d)), _full(rw.shape), _full((n_exp, 1)),
        ],
        scratch_shapes=[pltpu.VMEM((f // LANES, TR, LANES), F32)],
        out_specs=[
            row(d),
            pl.BlockSpec((wpt * lrows, d // 2), lambda i: (i, 0)),
            row(LANES),
            pl.BlockSpec((wpt, n_exp, LANES), lambda i: (i, 0, 0)),
        ],
        out_shape=[
            jax.ShapeDtypeStruct((n_rows, d), F32),
            jax.ShapeDtypeStruct((n_win * lrows, d // 2), jnp.uint32),
            jax.ShapeDtypeStruct((n_rows, LANES), F32),
            jax.ShapeDtypeStruct((n_win, n_exp, LANES), F32),
        ],
        compiler_params=_params(("arbitrary",)),
        name="mix_out",
    )(hg, yf, sa, sb, xa, mod, wpr, wpf, wo, gffn.reshape(1, d), rw, rb.reshape(n_exp, 1))


def _local_rows(n_exp):
    return -(-(2 * TM + n_exp * (SUBLANES - 1)) // LANES) * LANES


def _pack_bf16_pairs(x):
    half = x.shape[1] // 2
    bits = lax.bitcast_convert_type(x, jnp.uint32)
    return (bits[:, :half] & jnp.uint32(0xFFFF0000)) | lax.shift_right_logical(bits[:, half:], jnp.uint32(16))


def _unpack_bf16_pairs(u):
    hi = lax.bitcast_convert_type(u & jnp.uint32(0xFFFF0000), F32)
    lo = lax.bitcast_convert_type(lax.shift_left(u, jnp.uint32(16)), F32)
    return jnp.concatenate([hi, lo], axis=-1)


CPT = MOE_TM // SUBLANES


def _group_copies(table_ref, hbm, buf, sem, *, gather):
    for j in range(CPT):
        far = hbm.at[pl.ds(pl.multiple_of(table_ref[0, 0, j] * SUBLANES, SUBLANES), SUBLANES)]
        near = buf.at[pl.ds(j * SUBLANES, SUBLANES)]
        (pltpu.make_async_copy(far, near, sem) if gather else pltpu.make_async_copy(near, far, sem)).start()


def _tile_wait(hbm, buf, sem, *, gather):
    far = hbm.at[pl.ds(0, MOE_TM)]
    (pltpu.make_async_copy(far, buf, sem) if gather else pltpu.make_async_copy(buf, far, sem)).wait()


def _gmm_body(te_ref, nu_ref, src_ref, nxt_ref, dst_ref, xl_ref, w1_ref, w3_ref, w2_ref, yl_ref,
              w1b, w3b, w2b, xbuf, ybuf, gsem, ssem):
    t = pl.program_id(0)
    n_used = nu_ref[0]
    slot = lax.rem(t, 2)
    other = 1 - slot

    @pl.when((t == 0) | (te_ref[t] != te_ref[jnp.maximum(t - 1, 0)]))
    def _():
        w1b[...] = w1_ref[0, 0].astype(BF16)
        w3b[...] = w3_ref[0, 0].astype(BF16)
        w2b[...] = w2_ref[0, 0].astype(BF16)

    @pl.when(t < n_used)
    def _():
        @pl.when(t == 0)
        def _():
            _group_copies(src_ref, xl_ref, xbuf.at[slot], gsem.at[slot], gather=True)

        _tile_wait(xl_ref, xbuf.at[slot], gsem.at[slot], gather=True)
        _group_copies(nxt_ref, xl_ref, xbuf.at[other], gsem.at[other], gather=True)
        x = _unpack_bf16_pairs(xbuf[slot]).astype(BF16)
        a = _dot(x, w1b[...])
        h = (a * jax.nn.sigmoid(a)) * _dot(x, w3b[...])
        y = _dot(h.astype(BF16), w2b[...])

        @pl.when(t >= 2)
        def _():
            _tile_wait(yl_ref, ybuf.at[slot], ssem.at[slot], gather=False)

        ybuf[slot] = _pack_bf16_pairs(y.astype(BF16).astype(F32))
        _group_copies(dst_ref, yl_ref, ybuf.at[slot], ssem.at[slot], gather=False)

        @pl.when(t == n_used - 1)
        def _():
            _tile_wait(xl_ref, xbuf.at[other], gsem.at[other], gather=True)

            @pl.when(t >= 1)
            def _():
                _tile_wait(yl_ref, ybuf.at[other], ssem.at[other], gather=False)

            _tile_wait(yl_ref, ybuf.at[slot], ssem.at[slot], gather=False)


def _gmm(tile_exp, n_used, src, dst, xl, w1, w3, w2, *, layer, out_rows):
    n_tiles = src.shape[0]
    d, de = w1.shape[2], w1.shape[3]
    wspec = lambda w: pl.BlockSpec((1, 1) + w.shape[2:], lambda t, te, nu: (layer, te[t], 0, 0))
    table = lambda idx: pl.BlockSpec((1, 1, CPT), lambda t, te, nu: (idx(t, nu), 0, 0), memory_space=pltpu.SMEM)
    cur = lambda t, nu: jnp.minimum(t, nu[0] - 1)
    grid_spec = pltpu.PrefetchScalarGridSpec(
        num_scalar_prefetch=2,
        grid=(n_tiles,),
        in_specs=[table(cur), table(lambda t, nu: jnp.minimum(t + 1, nu[0] - 1)), table(cur),
                  pl.BlockSpec(memory_space=pl.ANY), wspec(w1), wspec(w3), wspec(w2)],
        out_specs=pl.BlockSpec(memory_space=pl.ANY),
        scratch_shapes=[
            pltpu.VMEM((d, de), BF16), pltpu.VMEM((d, de), BF16), pltpu.VMEM((de, d), BF16),
            pltpu.VMEM((2, MOE_TM, d // 2), jnp.uint32), pltpu.VMEM((2, MOE_TM, d // 2), jnp.uint32),
            pltpu.SemaphoreType.DMA((2,)), pltpu.SemaphoreType.DMA((2,)),
        ],
    )
    return pl.pallas_call(
        _gmm_body,
        grid_spec=grid_spec,
        out_shape=jax.ShapeDtypeStruct((out_rows, d // 2), jnp.uint32),
        compiler_params=_params(("arbitrary",)),
        name="moe_experts",
    )(tile_exp, n_used, src, src, dst, xl, w1, w3, w2)


def _combine_body(*refs, d, rows, nb, final):
    if final:
        tot_ref, yl_ref, rc_ref, xr_ref, mod_ref, gfin_ref, o_ref, t_scr = refs
    else:
        tot_ref, yl_ref, rc_ref, xr_ref, mod_ref, o_ref = refs
    wpt = TR // TM
    ci = lax.broadcasted_iota(jnp.int32, (TM, rows), 1)
    parts = []
    for k in range(wpt):
        keep = lax.broadcasted_iota(jnp.int32, (rows, 1), 0) < tot_ref[pl.program_id(0) * wpt + k]
        y = _unpack_bf16_pairs(jnp.where(keep, yl_ref[k * rows:(k + 1) * rows, :], jnp.uint32(0))).astype(BF16)
        rc = rc_ref[k * TM:(k + 1) * TM, :]
        pw = (jnp.where(ci == rc[:, 0:1].astype(jnp.int32), rc[:, 2:3], 0.0)
              + jnp.where(ci == rc[:, 1:2].astype(jnp.int32), rc[:, 3:4], 0.0))
        parts.append(_dot(pw.astype(BF16), y))
    moe = jnp.concatenate(parts, axis=0)
    g2 = mod_ref[0][:, 5 * d:6 * d]
    xn = xr_ref[...] + _per_batch(moe, lambda v: v * g2[None])
    if final:
        xn = _rms(xn, gfin_ref[...])
        for b, blk in enumerate(_to_batch_major(xn, t_scr, nb)):
            o_ref[b] = blk
    else:
        o_ref[...] = xn


def _combine(totals, yl, rc, xres, mod, *, n_exp, n_win, n_lat_rows, nb, gfin=None, seq_out=None):
    d = xres.shape[1]
    rows = _local_rows(n_exp)
    final = gfin is not None
    n_lat = n_lat_rows // TR
    wpt = TR // TM
    row = lambda width: pl.BlockSpec((TR, width), lambda w, a: (w, 0))
    in_specs = [pl.BlockSpec((wpt * rows, d // 2), lambda w, a: (w, 0)), row(LANES), row(d),
                pl.BlockSpec((1, SUBLANES, N_MOD * d), lambda w, a: (jnp.where(w < n_lat, 0, 1), 0, 0))]
    args = [totals, yl, rc, xres, mod]
    scratch = []
    if final:
        in_specs.append(pl.BlockSpec((1, d), lambda w, a: (0, 0)))
        args.append(gfin.reshape(1, d))
        out_shape = jax.ShapeDtypeStruct((nb, seq_out, d), F32)
        out_specs = pl.BlockSpec((nb, TR // nb, d), lambda w, a: (0, w, 0))
        scratch.append(pltpu.VMEM((d // LANES, TR, LANES), F32))
        aliases = {}
    else:
        out_shape = jax.ShapeDtypeStruct(xres.shape, F32)
        out_specs = row(d)
        aliases = {3: 0}
    grid_spec = pltpu.PrefetchScalarGridSpec(
        num_scalar_prefetch=1, grid=(n_win // wpt,), in_specs=in_specs, out_specs=out_specs,
        scratch_shapes=scratch)
    return pl.pallas_call(
        functools.partial(_combine_body, d=d, rows=rows, nb=nb, final=final),
        grid_spec=grid_spec,
        out_shape=out_shape,
        input_output_aliases=aliases,
        compiler_params=_params(("arbitrary",)),
        name="moe_combine",
    )(*args)


def _route_tables(cnt, *, n_exp, n_tiles):
    cnt = cnt[..., 0].astype(jnp.int32)
    n_win = cnt.shape[0]
    lrows = _local_rows(n_exp)
    npad = (cnt + (SUBLANES - 1)) & (-SUBLANES)
    seg = -(-jnp.sum(npad, axis=0) // MOE_TM) * MOE_TM
    ends = jnp.cumsum(seg)
    goff = (ends - seg)[None, :] + jnp.cumsum(npad, axis=0) - npad
    loff = jnp.cumsum(npad, axis=1) - npad
    tile_ends = ends // MOE_TM
    n_used = tile_ends[-1]
    tile = jnp.minimum(jnp.arange(n_tiles, dtype=jnp.int32), n_used - 1)
    tile_exp = jnp.sum(tile[:, None] >= tile_ends[None, :], axis=1).astype(jnp.int32)

    starts = (goff.T // SUBLANES).reshape(-1)
    lens = (npad.T // SUBLANES).reshape(-1)
    local = ((jnp.arange(n_win, dtype=jnp.int32)[:, None] * lrows + loff).T // SUBLANES).reshape(-1)
    group = jnp.arange(n_tiles * CPT, dtype=jnp.int32)
    began = starts[None, :] <= group[:, None]
    last = lambda v: jnp.sum(jnp.where(began, jnp.diff(v, prepend=0)[None, :], 0), axis=1)
    valid = group < last(starts + lens)
    src = jnp.where(valid, group + last(local - starts), 0)
    dst = jnp.where(valid, src, n_win * lrows // SUBLANES + group)
    shape = (n_tiles, 1, CPT)
    return (tile_exp, n_used.reshape(1).astype(jnp.int32), src.reshape(shape).astype(jnp.int32),
            dst.reshape(shape).astype(jnp.int32), jnp.sum(npad, axis=1).astype(jnp.int32))


def _channel_dft_tables(n):
    k = jnp.arange(n, dtype=jnp.int32)
    ang = ((k[:, None] * k[None, :]) % n).astype(F32) * (2.0 * math.pi / n)
    scale = 1.0 / math.sqrt(n)
    return jnp.cos(ang) * scale, jnp.sin(ang) * scale


def kernel(x, c, ctx, c_ctx, ada_w, ada_b, norm_mix_g, w_in, conv_w, conv_b, lru_w_a, lru_b_a, lru_w_x, lru_b_x, lru_lambda, w_proj_rnn, w_proj_fourier, w_out, norm_ffn_g, router_w, router_b, moe_w1, moe_w3, moe_w2, final_norm_g):
    nb, l, d = x.shape
    lc = ctx.shape[1]
    depth = ada_w.shape[0]
    r = conv_w.shape[2]
    f = w_proj_fourier.shape[1]
    nh, blk = lru_w_a.shape[2], lru_w_a.shape[3]
    assert nb == SUBLANES and blk == LANES and nb < MOD_ROWS
    assert l % TS == 0 and lc % TS == 0 and TS % GRID_W == 0 and (l * nb) % TR == 0 and (lc * nb) % TR == 0
    assert GRID_W & (GRID_W - 1) == 0 and l % lc == 0 and l % (2 * FOLD_TILE) == 0 and l // LANES <= LANES
    s = l + lc
    n_lat_rows = l * nb

    cc = jnp.zeros((MOD_ROWS, d), F32).at[:nb].set(c).at[nb].set(c_ctx)
    mod = _ada(cc, ada_w, ada_b)
    mod = jnp.stack([mod[:, :nb], jnp.broadcast_to(mod[:, nb:nb + 1], (depth, nb, N_MOD * d))], axis=1)

    cch, sch = _channel_dft_tables(f // FOURIER_GROUPS)
    fcs = jnp.concatenate([cch, sch], axis=1).astype(BF16)
    n_exp = router_w.shape[1]
    rw = jnp.concatenate(_split(jnp.pad(router_w, ((0, 0), (0, LANES - n_exp)))), axis=1)

    out = xa = None
    for li in range(depth):
        last = li == depth - 1
        inproj = functools.partial(_inproj, mod=mod[li], g=norm_mix_g[li], w_in=w_in[li].astype(BF16), fcs=fcs,
                                   n_lat_rows=n_lat_rows, r=r, f=f, nb=nb)
        if li == 0:
            u, gl, sa, sb, pq, xa = inproj(x, ctx=ctx)
        else:
            u, gl, sa, sb, pq = inproj(xa)
        wg = [jnp.concatenate([lru_w_a[li, k], lru_w_x[li, k]], axis=-1).astype(BF16) for k in range(2)]
        bg = [jnp.concatenate([lru_b_a[li, k], lru_b_x[li, k]], axis=-1).reshape(nh, 1, 2 * blk) for k in range(2)]
        lam = [lru_lambda[li, k].reshape(nh, 1, blk) for k in range(2)]
        hb, uc = _scan(u, wg[1], bg[1], lam[1], nb=nb, n_lat_steps=l // TS, conv=(conv_w[li], conv_b[li]))
        hg = _scan(uc, wg[0], bg[0], lam[0], nb=nb, n_lat_steps=l // TS, hb=hb, gl=gl)
        n_rows = n_lat_rows if last else s * nb
        yf = _dft(pq, n=l, nb=nb, f=f, row0=0, out_rows=n_rows // nb, folded=_fold(pq, n=l, nb=nb, f=f))
        if not last:
            yf = _dft(pq, n=lc, nb=nb, f=f, row0=l, out_rows=n_rows // nb, prev=yf)
        x1, xl, rc, cnt = _mix(hg, yf, sa, sb, xa, mod[li], w_proj_rnn[li].astype(BF16),
                               w_proj_fourier[li].astype(BF16), w_out[li].astype(BF16), norm_ffn_g[li],
                               rw, router_b, n_rows=n_rows, n_lat_rows=n_lat_rows, nb=nb)
        n_win = n_rows // TM
        n_tiles = -(-(n_win * (2 * TM + n_exp * (SUBLANES - 1)) + n_exp * MOE_TM) // MOE_TM)
        tile_exp, n_used, src, dst, totals = _route_tables(cnt, n_exp=n_exp, n_tiles=n_tiles)
        yl = _gmm(tile_exp, n_used, src, dst, xl, moe_w1, moe_w3, moe_w2, layer=li,
                  out_rows=xl.shape[0] + n_tiles * MOE_TM)
        comb = functools.partial(_combine, totals, yl, rc, x1, mod[li], n_exp=n_exp, n_win=n_win,
                                 n_lat_rows=n_lat_rows, nb=nb)
        if last:
            out = comb(gfin=final_norm_g, seq_out=l)
        else:
            xa = comb()
    return out
```

```python
import functools
import math

import jax
import jax.numpy as jnp
from jax import lax
from jax.experimental import pallas as pl
from jax.experimental.pallas import tpu as pltpu

F32 = jnp.float32
BF16 = jnp.bfloat16

RMS_EPS = 1e-6
LRU_C = 8.0
GRID_W = 64
CONV_TAPS = ((0, -2), (1, -1), (2, 0), (3, 1))
N_MOD = 6
FOURIER_GROUPS = 4
N_EXPERT_GROUPS = 4
LANES = 128
SUBLANES = 8
PACKED_ROWS = 16
TR = 512
TM = 256
TS = 128
SUB = 64
MOE_TM = 512
MOD_ROWS = 16
VMEM_LIMIT = 56 * 1024 * 1024


def _dot(a, b):
    return jnp.dot(a, b, preferred_element_type=F32)


def _split(a):
    hi = a.astype(BF16)
    lo = (a - hi.astype(F32)).astype(BF16)
    return hi, lo


def _dot3(a, b):
    ah, al = _split(a)
    bh, bl = _split(b)
    return _dot(ah, bh) + _dot(al, bh) + _dot(ah, bl)


def _gelu_tanh(x):
    return 0.5 * x * (1.0 + jnp.tanh(math.sqrt(2.0 / math.pi) * (x + 0.044715 * (x * x * x))))


def _sigmoid(x):
    return 0.5 * jnp.tanh(0.5 * x) + 0.5


def _rms(x, g):
    return x * lax.rsqrt(jnp.mean(x * x, axis=-1, keepdims=True) + RMS_EPS) * g


def _per_batch(x, fn):
    rows, d = x.shape
    return fn(x.reshape(rows // SUBLANES, SUBLANES, d)).reshape(rows, d)


def _params(sem, vmem=VMEM_LIMIT):
    return pltpu.CompilerParams(dimension_semantics=sem, vmem_limit_bytes=vmem)


def _full(shape):
    zeros = (0,) * len(shape)
    return pl.BlockSpec(shape, lambda *_: zeros)


def _ada_body(cc_ref, w_ref, b_ref, o_ref):
    cc = cc_ref[...]
    s = cc * jax.nn.sigmoid(cc)
    o_ref[0] = _dot3(s, w_ref[0]) + b_ref[0]


def _ada(cc, ada_w, ada_b):
    depth, d, n = ada_w.shape
    tn = 1024
    return pl.pallas_call(
        _ada_body,
        grid=(depth, n // tn),
        in_specs=[
            pl.BlockSpec((MOD_ROWS, d), lambda l, j: (0, 0)),
            pl.BlockSpec((1, d, tn), lambda l, j: (l, 0, j)),
            pl.BlockSpec((1, 1, tn), lambda l, j: (l, 0, j)),
        ],
        out_specs=pl.BlockSpec((1, MOD_ROWS, tn), lambda l, j: (l, 0, j)),
        out_shape=jax.ShapeDtypeStruct((depth, MOD_ROWS, n), F32),
        compiler_params=_params(("arbitrary", "arbitrary")),
        name="ada_mod",
    )(cc, ada_w, ada_b.reshape(depth, 1, n))


def _to_time_major(blocks, scr, nb):
    npos, width = blocks[0].shape
    for b in range(nb):
        for g in range(width // LANES):
            scr[g, pl.ds(b, npos, stride=nb), :] = blocks[b][:, g * LANES:(g + 1) * LANES].astype(F32)
    return jnp.concatenate([scr[g] for g in range(width // LANES)], axis=-1)


def _to_batch_major(v, scr, nb):
    rows, width = v.shape
    for g in range(width // LANES):
        scr[g] = v[:, g * LANES:(g + 1) * LANES]
    return [jnp.concatenate([scr[g, pl.ds(b, rows // nb, stride=nb), :] for g in range(width // LANES)], axis=-1)
            for b in range(nb)]


def _inproj_body(*refs, d, r, f, nb, n_lat, first):
    if first:
        (x_ref, c_ref, mod_ref, g_ref, w_ref, fcs_ref,
         u_ref, gl_ref, sa_ref, sb_ref, pq_ref, xa_ref, t_scr) = refs

        @pl.when(pl.program_id(0) < n_lat)
        def _():
            xa_ref[...] = _to_time_major([x_ref[b] for b in range(nb)], t_scr, nb)

        @pl.when(pl.program_id(0) >= n_lat)
        def _():
            xa_ref[...] = _to_time_major([c_ref[b] for b in range(nb)], t_scr, nb)

        x = xa_ref[...]
    else:
        x_ref, mod_ref, g_ref, w_ref, fcs_ref, u_ref, gl_ref, sa_ref, sb_ref, pq_ref, t_scr = refs
        x = x_ref[...]
    mod = mod_ref[0]
    h = _rms(x, g_ref[...])
    h = _per_batch(h, lambda v: v * (1.0 + mod[:, d:2 * d])[None] + mod[:, 0:d][None]).astype(BF16)
    u_ref[...] = _dot(h, w_ref[:, 0:r]).astype(BF16)
    gl_ref[...] = _gelu_tanh(_dot(h, w_ref[:, r:2 * r])).astype(BF16)
    u4 = _dot(h, w_ref[:, 2 * r:2 * r + f]).astype(BF16)
    fg = fcs_ref.shape[0]
    pq_g = [_dot(u4[:, k * fg:(k + 1) * fg], fcs_ref[...]) for k in range(f // fg)]
    pq = jnp.concatenate([v[:, :fg] for v in pq_g] + [v[:, fg:] for v in pq_g], axis=-1)
    for b, blk in enumerate(_to_batch_major(pq, t_scr, nb)):
        pq_ref[:, b * 2 * f:(b + 1) * 2 * f] = blk.astype(BF16)
    s3 = 2 * r + f
    sa_ref[...] = jax.nn.sigmoid(_dot(h, w_ref[:, s3:s3 + d])).astype(BF16)
    sb_ref[...] = jax.nn.sigmoid(_dot(h, w_ref[:, s3 + d:s3 + 2 * d])).astype(BF16)


def _inproj(x, mod, g, w_in, fcs, *, n_lat_rows, r, f, nb, ctx=None):
    first = ctx is not None
    d = x.shape[-1]
    rows = (x.shape[1] + ctx.shape[1]) * nb if first else x.shape[0]
    n_lat = n_lat_rows // TR
    npos = TR // nb
    row = lambda width: pl.BlockSpec((TR, width), lambda i: (i, 0))
    common = [
        pl.BlockSpec((1, SUBLANES, N_MOD * d), lambda i: (jnp.where(i < n_lat, 0, 1), 0, 0)),
        _full((1, d)), _full(w_in.shape), _full(fcs.shape),
    ]
    out_specs = [row(r), row(r), row(d), row(d), pl.BlockSpec((npos, nb * 2 * f), lambda i: (i, 0))]
    out_shape = [
        jax.ShapeDtypeStruct((rows, r), BF16),
        jax.ShapeDtypeStruct((rows, r), BF16),
        jax.ShapeDtypeStruct((rows, d), BF16),
        jax.ShapeDtypeStruct((rows, d), BF16),
        jax.ShapeDtypeStruct((rows // nb, nb * 2 * f), BF16),
    ]
    if first:
        in_specs = [
            pl.BlockSpec((nb, npos, d), lambda i: (0, jnp.minimum(i, n_lat - 1), 0)),
            pl.BlockSpec((nb, npos, d), lambda i: (0, jnp.maximum(i - n_lat, 0), 0)),
        ] + common
        args = [x, ctx]
        out_specs.append(row(d))
        out_shape.append(jax.ShapeDtypeStruct((rows, d), F32))
    else:
        in_specs = [row(d)] + common
        args = [x]
    return pl.pallas_call(
        functools.partial(_inproj_body, d=d, r=r, f=f, nb=nb, n_lat=n_lat, first=first),
        grid=(rows // TR,),
        in_specs=in_specs,
        out_specs=out_specs,
        out_shape=out_shape,
        scratch_shapes=[pltpu.VMEM((d // LANES, TR, LANES), F32)],
        compiler_params=_params(("arbitrary",)),
        name="in_proj",
    )(*args, mod, g.reshape(1, d), w_in, fcs)


def _conv_tile(u_ref, up_ref, un_ref, cw_ref, cb_ref, uc_scr, blk, *, nb, n_lat_steps):
    trows = TS * nb
    n_steps = pl.num_programs(0)
    is_ctx = blk >= n_lat_steps
    keep_prev = jnp.where(is_ctx & (blk > n_lat_steps), 1.0, 0.0)
    keep_next = jnp.where(is_ctx & (blk < n_steps - 1), 1.0, 0.0)
    lat = jnp.where(is_ctx, 0.0, 1.0)
    halo = 2 * nb
    ext = jnp.concatenate([up_ref[...].astype(F32) * keep_prev, u_ref[...].astype(F32),
                           un_ref[...].astype(F32)[0:nb] * keep_next], axis=0)
    uc = cb_ref[...]
    for k, s in CONV_TAPS:
        lo = halo + s * nb
        uc = uc + cw_ref[k:k + 1, :] * ext[lo:lo + trows, :]
    uc_scr[...] = uc
    for p in range(GRID_W, TS, GRID_W):
        at = lambda q: ext[halo + q * nb:halo + (q + 1) * nb, :]
        rows_of = lambda q: pl.ds(q * nb, nb)
        uc_scr[rows_of(p), :] -= lat * (cw_ref[0:1, :] * at(p - 2) + cw_ref[1:2, :] * at(p - 1))
        uc_scr[rows_of(p + 1), :] -= lat * (cw_ref[0:1, :] * at(p - 1))
        uc_scr[rows_of(p - 1), :] -= lat * (cw_ref[3:4, :] * at(p))


def _scan_body(*refs, reverse, merge, nb, n_lat_steps, order):
    if merge:
        uc_ref, wg_ref, bg_ref, lam_ref, hb_ref, gl_ref, o_ref, h_scr, a_scr, b_scr, uc_scr, hs_scr = refs
    else:
        (u_ref, up_ref, un_ref, cw_ref, cb_ref, wg_ref, bg_ref, lam_ref,
         o_ref, uco_ref, h_scr, a_scr, b_scr, uc_scr, hs_scr) = refs
    nh = wg_ref.shape[0]
    rows = SUB * nb
    j = pl.program_id(0)

    @pl.when(j == 0)
    def _():
        h_scr[...] = jnp.zeros_like(h_scr)

    if merge:
        uc_scr[...] = uc_ref[...].astype(F32)
    else:
        _conv_tile(u_ref, up_ref, un_ref, cw_ref, cb_ref, uc_scr, order(j), nb=nb, n_lat_steps=n_lat_steps)
        uco_ref[...] = uc_scr[...].astype(BF16)

    lam = lam_ref[...]
    half_decay = (-0.5 * LRU_C) * (jnp.maximum(-lam, 0.0) + jnp.log1p(jnp.exp(-jnp.abs(lam))))

    subs = range(TS // SUB)
    for sub in (reversed(subs) if reverse else subs):
        r0 = sub * rows
        for g in range(nh):
            ug = uc_scr[r0:r0 + rows, g * LANES:(g + 1) * LANES]
            gates = _dot(ug.astype(BF16), wg_ref[g]) + bg_ref[g]
            log_a = half_decay[g] * jnp.tanh(0.5 * gates[:, :LANES]) + half_decay[g]
            a = jnp.exp(log_a)
            z = (-1.0 - a * a) * jnp.tanh(log_a)
            mult = jnp.where(z > 0.0, z * lax.rsqrt(z), 0.0)
            a_scr[g] = a
            b_scr[g] = mult * (_sigmoid(gates[:, LANES:]) * ug)

        def step(k, hs):
            t = (SUB - 1 - k) if reverse else k
            off = pl.multiple_of(t * nb, nb)
            new = []
            for g in range(nh):
                hg = a_scr[g, pl.ds(off, nb), :] * hs[g] + b_scr[g, pl.ds(off, nb), :]
                hs_scr[pl.ds(r0 + off, nb), g * LANES:(g + 1) * LANES] = hg
                new.append(hg)
            return tuple(new)

        hs = lax.fori_loop(0, SUB, step, tuple(h_scr[g] for g in range(nh)), unroll=8)
        for g in range(nh):
            h_scr[g] = hs[g]

    if merge:
        o_ref[...] = ((hs_scr[...] + hb_ref[...].astype(F32)) * gl_ref[...].astype(F32)).astype(BF16)
    else:
        o_ref[...] = hs_scr[...].astype(BF16)


def _scan(u, wg, bg, lam, *, nb, n_lat_steps, conv=None, hb=None, gl=None):
    rows_all, r = u.shape
    nh = wg.shape[0]
    trows = TS * nb
    n = rows_all // trows
    n_ctx = n - n_lat_steps
    merge = hb is not None
    reverse = not merge
    if reverse:
        order = lambda j: jnp.where(j < n_ctx, n - 1 - j, n_lat_steps - 1 - (j - n_ctx))
    else:
        order = lambda j: jnp.where(j < n_ctx, n_lat_steps + j, j - n_ctx)
    hpb = trows // PACKED_ROWS
    tile = pl.BlockSpec((trows, r), lambda j: (order(j), 0))
    gate_specs = [_full(wg.shape), _full(bg.shape), _full(lam.shape)]
    out = jax.ShapeDtypeStruct((rows_all, r), BF16)
    if merge:
        in_specs = [tile] + gate_specs + [tile, tile]
        args = [u, wg, bg, lam, hb, gl]
        out_specs, out_shape = tile, out
    else:
        conv_w, conv_b = conv
        in_specs = [
            tile,
            pl.BlockSpec((PACKED_ROWS, r), lambda j: (jnp.maximum(order(j) * hpb - 1, 0), 0)),
            pl.BlockSpec((PACKED_ROWS, r), lambda j: (jnp.minimum((order(j) + 1) * hpb, n * hpb - 1), 0)),
            _full(conv_w.shape), _full((1, r)),
        ] + gate_specs
        args = [u, u, u, conv_w, conv_b.reshape(1, r), wg, bg, lam]
        out_specs, out_shape = [tile, tile], [out, out]
    return pl.pallas_call(
        functools.partial(_scan_body, reverse=reverse, merge=merge, nb=nb, n_lat_steps=n_lat_steps,
                          order=order),
        grid=(n,),
        in_specs=in_specs,
        out_specs=out_specs,
        out_shape=out_shape,
        scratch_shapes=[
            pltpu.VMEM((nh, nb, LANES), F32),
            pltpu.VMEM((nh, SUB * nb, LANES), F32),
            pltpu.VMEM((nh, SUB * nb, LANES), F32),
            pltpu.VMEM((trows, r), F32),
            pltpu.VMEM((trows, r), F32),
        ],
        compiler_params=_params(("arbitrary",)),
        name="scan_fwd_merge" if merge else "scan_bwd",
    )(*args)


FOLD_TILE = 256


def _fold_body(a_ref, b1_ref, b2_ref, o_ref, *, f, nb):
    m = pl.program_id(0)
    i = lax.broadcasted_iota(jnp.int32, (FOLD_TILE, 2 * FOLD_TILE), 0)
    col = lax.broadcasted_iota(jnp.int32, (FOLD_TILE, 2 * FOLD_TILE), 1)
    sel = (col == jnp.where(i == 0, FOLD_TILE, FOLD_TILE - i)) & ((i > 0) | (m > 0))
    pad = jnp.zeros((FOLD_TILE - PACKED_ROWS, b1_ref.shape[1]), BF16)
    mirrored = _dot(jnp.where(sel, 1.0, 0.0).astype(BF16),
                    jnp.concatenate([b1_ref[...], b2_ref[...], pad], axis=0))
    for b in range(nb):
        p = slice(b * 2 * f, b * 2 * f + f)
        q = slice(b * 2 * f + f, (b + 1) * 2 * f)
        o_ref[:, p] = (a_ref[:, p].astype(F32) + mirrored[:, p]).astype(BF16)
        o_ref[:, q] = (a_ref[:, q].astype(F32) - mirrored[:, q]).astype(BF16)


def _fold(pq, *, n, nb, f):
    nt = n // FOLD_TILE
    hpt = FOLD_TILE // PACKED_ROWS
    width = nb * 2 * f
    tile = lambda idx: pl.BlockSpec((FOLD_TILE, width), lambda m: (idx(m), 0))
    return pl.pallas_call(
        functools.partial(_fold_body, f=f, nb=nb),
        grid=(nt // 2,),
        in_specs=[tile(lambda m: m), tile(lambda m: nt - 1 - m),
                  pl.BlockSpec((PACKED_ROWS, width), lambda m: (jnp.minimum(nt - m, nt - 1) * hpt, 0))],
        out_specs=tile(lambda m: m),
        out_shape=jax.ShapeDtypeStruct((n // 2, width), BF16),
        compiler_params=_params(("arbitrary",)),
        name="dft_fold",
    )(pq, pq, pq)


def _dft_body(ca_ref, sa_ref, cb_ref, sb_ref, p_ref, q_ref, *rest, kdim, folded, scale):
    o_ref, c_scr, s_scr = rest[-3:]

    @pl.when(pl.program_id(1) == 0)
    def _():
        cb = cb_ref[...]
        sb = sb_ref[...]
        for t1 in range(kdim // LANES):
            ca = ca_ref[:, t1:t1 + 1]
            sa = sa_ref[:, t1:t1 + 1]
            c_scr[:, t1 * LANES:(t1 + 1) * LANES] = (ca * cb - sa * sb).astype(BF16)
            s_scr[:, t1 * LANES:(t1 + 1) * LANES] = (sa * cb + ca * sb).astype(BF16)

    y = _dot(c_scr[...], p_ref[...]) - _dot(s_scr[...], q_ref[...])
    if folded:
        mid_ref = rest[0]
        tmk = o_ref.shape[0]
        k = pl.program_id(0) * tmk + lax.broadcasted_iota(jnp.int32, (tmk, 1), 0)
        sign = (1 - 2 * (k & 1)).astype(F32)
        y = y + sign * (scale * mid_ref[...].astype(F32)[0:1, :])
    o_ref[...] = y.astype(BF16)


def _dft_tables(n):
    k = jnp.arange(n, dtype=jnp.int32)[:, None]
    t1 = jnp.arange(LANES, dtype=jnp.int32)[None, :]
    ang_a = ((k * ((t1 * LANES) % n)) % n).astype(F32) * (2.0 * math.pi / n)
    ang_b = ((k * t1) % n).astype(F32) * (2.0 * math.pi / n)
    scale = 1.0 / math.sqrt(n)
    return jnp.cos(ang_a), jnp.sin(ang_a), jnp.cos(ang_b) * scale, jnp.sin(ang_b) * scale


def _dft(pq, *, n, nb, f, row0, out_rows, prev=None, folded=None):
    tmk = min(n, 512)
    kdim = n // 2 if folded is not None else n
    tab = pl.BlockSpec((tmk, LANES), lambda m, b: (m, 0))
    if folded is not None:
        assert row0 == 0
        src, rb = folded, 0
    else:
        src, rb = pq, row0 // n
    in_specs = [
        tab, tab, tab, tab,
        pl.BlockSpec((kdim, f), lambda m, b: (rb, 2 * b)),
        pl.BlockSpec((kdim, f), lambda m, b: (rb, 2 * b + 1)),
    ]
    args = list(_dft_tables(n)) + [src, src]
    if folded is not None:
        in_specs.append(pl.BlockSpec((PACKED_ROWS, f), lambda m, b: (kdim // PACKED_ROWS, 2 * b)))
        args.append(pq)
    aliases = {}
    if prev is not None:
        in_specs.append(pl.BlockSpec(memory_space=pl.ANY))
        args.append(prev)
        aliases = {len(args) - 1: 0}
    return pl.pallas_call(
        functools.partial(_dft_body, kdim=kdim, folded=folded is not None, scale=1.0 / math.sqrt(n)),
        grid=(n // tmk, nb),
        in_specs=in_specs,
        out_specs=pl.BlockSpec((tmk, f), lambda m, b: (row0 // tmk + m, b)),
        out_shape=jax.ShapeDtypeStruct((out_rows, nb * f), BF16),
        scratch_shapes=[pltpu.VMEM((tmk, kdim), BF16), pltpu.VMEM((tmk, kdim), BF16)],
        input_output_aliases=aliases,
        compiler_params=_params(("arbitrary", "arbitrary")),
        name="pos_dft",
    )(*args)


def _top2(vals):
    def first_max(vs):
        m = functools.reduce(jnp.maximum, vs)
        idx = jnp.full(m.shape, len(vs) - 1, jnp.int32)
        for k in range(len(vs) - 2, -1, -1):
            idx = jnp.where(vs[k] == m, k, idx)
        return m, idx
    m1, i1 = first_max(vals)
    rest = [jnp.where(i1 == k, -jnp.inf, v) for k, v in enumerate(vals)]
    m2, i2 = first_max(rest)
    return m1, i1, m2, i2


def _pick(idx, vals):
    out = vals[-1]
    for k in range(len(vals) - 2, -1, -1):
        out = jnp.where(idx == k, vals[k], out)
    return out


def _route_window(h2, rw, rb, n_exp):
    hh, hl = _split(h2)
    both = _dot(hh, rw)
    logits = both[:, :LANES] + both[:, LANES:] + _dot(hl, rw[:, :LANES])
    aff = jax.nn.sigmoid(logits.T[0:n_exp, :])
    sel = aff + rb
    per = n_exp // N_EXPERT_GROUPS
    srow = [sel[e:e + 1, :] for e in range(n_exp)]
    arow = [aff[e:e + 1, :] for e in range(n_exp)]
    scores = []
    for gi in range(N_EXPERT_GROUPS):
        m1, _, m2, _ = _top2(srow[gi * per:(gi + 1) * per])
        scores.append(m1 + m2)
    best = functools.reduce(jnp.maximum, scores)
    grp = jnp.full(best.shape, N_EXPERT_GROUPS - 1, jnp.int32)
    for gi in range(N_EXPERT_GROUPS - 2, -1, -1):
        grp = jnp.where(scores[gi] == best, gi, grp)
    v = [_pick(grp, [srow[gi * per + k] for gi in range(N_EXPERT_GROUPS)]) for k in range(per)]
    a = [_pick(grp, [arow[gi * per + k] for gi in range(N_EXPERT_GROUPS)]) for k in range(per)]
    _, i1, _, i2 = _top2(v)
    w1 = _pick(i1, a)
    w2 = _pick(i2, a)
    den = w1 + w2
    e1 = grp * per + i1
    e2 = grp * per + i2

    erow = lax.broadcasted_iota(jnp.int32, (n_exp, TM), 0)
    hit1 = erow == e1
    hit2 = erow == e2
    onehot = jnp.where(hit1 | hit2, 1.0, 0.0)
    cnt = jnp.sum(onehot, axis=1, keepdims=True)
    before = (lax.broadcasted_iota(jnp.int32, (TM, TM), 0)
              < lax.broadcasted_iota(jnp.int32, (TM, TM), 1)).astype(BF16)
    rank = _dot(onehot.astype(BF16), before)
    padded = ((cnt.astype(jnp.int32) + (SUBLANES - 1)) & (-SUBLANES)).astype(F32)
    lower = (lax.broadcasted_iota(jnp.int32, (n_exp, n_exp), 1)
             < lax.broadcasted_iota(jnp.int32, (n_exp, n_exp), 0)).astype(BF16)
    starts = _dot(lower, jnp.broadcast_to(padded, (n_exp, LANES)).astype(BF16))
    pos = starts[:, 0:1] + rank
    r1 = jnp.sum(jnp.where(hit1, pos, 0.0), axis=0, keepdims=True)
    r2 = jnp.sum(jnp.where(hit2, pos, 0.0), axis=0, keepdims=True)
    route = jnp.concatenate([r1, r2, w1 / den, w2 / den, jnp.zeros((LANES - 4, TM), F32)], axis=0)
    return route, jnp.broadcast_to(cnt, (n_exp, LANES))


def _mix_body(hg_ref, yf_ref, sa_ref, sb_ref, x_ref, mod_ref, wpr_ref, wpf_ref, wo_ref,
              gffn_ref, rw_ref, rb_ref, xo_ref, xl_ref, rc_ref, cnt_ref, t_scr, *, d, n_exp, nb):
    mod = mod_ref[0]
    f = wpf_ref.shape[0]
    y_r = _dot(hg_ref[...], wpr_ref[...])
    yf = _to_time_major([yf_ref[:, b * f:(b + 1) * f] for b in range(nb)], t_scr, nb)
    y_f = _dot(yf.astype(BF16), wpf_ref[...])
    merged = sa_ref[...].astype(F32) * y_r + sb_ref[...].astype(F32) * y_f
    out = _dot(merged.astype(BF16), wo_ref[...])
    xn = x_ref[...] + _per_batch(out, lambda v: v * mod[:, 2 * d:3 * d][None])
    xo_ref[...] = xn
    h2 = _rms(xn, gffn_ref[...])
    h2 = _per_batch(h2, lambda v: v * (1.0 + mod[:, 4 * d:5 * d])[None] + mod[:, 3 * d:4 * d][None])
    rows = _local_rows(n_exp)
    riota = lax.broadcasted_iota(jnp.int32, (rows, TM), 0)
    for k in range(TR // TM):
        hw = h2[k * TM:(k + 1) * TM, :]
        route, cnt = _route_window(hw, rw_ref[...], rb_ref[...], n_exp)
        rc_ref[k * TM:(k + 1) * TM, :] = route.T
        cnt_ref[k] = cnt
        r12 = route[0:2, :].astype(jnp.int32)
        p = jnp.where((riota == r12[0:1, :]) | (riota == r12[1:2, :]), 1.0, 0.0).astype(BF16)
        xl_ref[k * rows:(k + 1) * rows, :] = _pack_bf16_pairs(_dot(p, hw.astype(BF16)))


def _mix(hg, yf, sa, sb, xa, mod, wpr, wpf, wo, gffn, rw, rb, *, n_rows, n_lat_rows, nb):
    d = xa.shape[1]
    f = wpf.shape[0]
    n_exp = rb.shape[0]
    n_lat = n_lat_rows // TR
    wpt = TR // TM
    n_win = n_rows // TM
    lrows = _local_rows(n_exp)
    row = lambda width: pl.BlockSpec((TR, width), lambda i: (i, 0))
    return pl.pallas_call(
        functools.partial(_mix_body, d=d, n_exp=n_exp, nb=nb),
        grid=(n_rows // TR,),
        in_specs=[
            row(d), pl.BlockSpec((TR // nb, nb * f), lambda i: (i, 0)), row(d), row(d), row(d),
            pl.BlockSpec((1, SUBLANES, N_MOD * d), lambda i: (jnp.where(i < n_lat, 0, 1), 0, 0)),
            _full(wpr.shape), _full(wpf.shape), _full(wo.shape),
            _full((1, d)), _full(rw.shape), _full((n_exp, 1)),
        ],
        scratch_shapes=[pltpu.VMEM((f // LANES, TR, LANES), F32)],
        out_specs=[
            row(d),
            pl.BlockSpec((wpt * lrows, d // 2), lambda i: (i, 0)),
            row(LANES),
            pl.BlockSpec((wpt, n_exp, LANES), lambda i: (i, 0, 0)),
        ],
        out_shape=[
            jax.ShapeDtypeStruct((n_rows, d), F32),
            jax.ShapeDtypeStruct((n_win * lrows, d // 2), jnp.uint32),
            jax.ShapeDtypeStruct((n_rows, LANES), F32),
            jax.ShapeDtypeStruct((n_win, n_exp, LANES), F32),
        ],
        compiler_params=_params(("arbitrary",)),
        name="mix_out",
    )(hg, yf, sa, sb, xa, mod, wpr, wpf, wo, gffn.reshape(1, d), rw, rb.reshape(n_exp, 1))


def _local_rows(n_exp):
    return -(-(2 * TM + n_exp * (SUBLANES - 1)) // LANES) * LANES


def _pack_bf16_pairs(x):
    half = x.shape[1] // 2
    bits = lax.bitcast_convert_type(x, jnp.uint32)
    return (bits[:, :half] & jnp.uint32(0xFFFF0000)) | lax.shift_right_logical(bits[:, half:], jnp.uint32(16))


def _unpack_bf16_pairs(u):
    hi = lax.bitcast_convert_type(u & jnp.uint32(0xFFFF0000), F32)
    lo = lax.bitcast_convert_type(lax.shift_left(u, jnp.uint32(16)), F32)
    return jnp.concatenate([hi, lo], axis=-1)


CPT = MOE_TM // SUBLANES


def _group_copies(table_ref, hbm, buf, sem, *, gather):
    for j in range(CPT):
        far = hbm.at[pl.ds(pl.multiple_of(table_ref[0, 0, j] * SUBLANES, SUBLANES), SUBLANES)]
        near = buf.at[pl.ds(j * SUBLANES, SUBLANES)]
        (pltpu.make_async_copy(far, near, sem) if gather else pltpu.make_async_copy(near, far, sem)).start()


def _tile_wait(hbm, buf, sem, *, gather):
    far = hbm.at[pl.ds(0, MOE_TM)]
    (pltpu.make_async_copy(far, buf, sem) if gather else pltpu.make_async_copy(buf, far, sem)).wait()


def _gmm_body(te_ref, nu_ref, src_ref, nxt_ref, dst_ref, xl_ref, w1_ref, w3_ref, w2_ref, yl_ref,
              w1b, w3b, w2b, xbuf, ybuf, gsem, ssem):
    t = pl.program_id(0)
    n_used = nu_ref[0]
    slot = lax.rem(t, 2)
    other = 1 - slot

    @pl.when((t == 0) | (te_ref[t] != te_ref[jnp.maximum(t - 1, 0)]))
    def _():
        w1b[...] = w1_ref[0, 0].astype(BF16)
        w3b[...] = w3_ref[0, 0].astype(BF16)
        w2b[...] = w2_ref[0, 0].astype(BF16)

    @pl.when(t < n_used)
    def _():
        @pl.when(t == 0)
        def _():
            _group_copies(src_ref, xl_ref, xbuf.at[slot], gsem.at[slot], gather=True)

        _tile_wait(xl_ref, xbuf.at[slot], gsem.at[slot], gather=True)
        _group_copies(nxt_ref, xl_ref, xbuf.at[other], gsem.at[other], gather=True)
        x = _unpack_bf16_pairs(xbuf[slot]).astype(BF16)
        a = _dot(x, w1b[...])
        h = (a * jax.nn.sigmoid(a)) * _dot(x, w3b[...])
        y = _dot(h.astype(BF16), w2b[...])

        @pl.when(t >= 2)
        def _():
            _tile_wait(yl_ref, ybuf.at[slot], ssem.at[slot], gather=False)

        ybuf[slot] = _pack_bf16_pairs(y.astype(BF16).astype(F32))
        _group_copies(dst_ref, yl_ref, ybuf.at[slot], ssem.at[slot], gather=False)

        @pl.when(t == n_used - 1)
        def _():
            _tile_wait(xl_ref, xbuf.at[other], gsem.at[other], gather=True)

            @pl.when(t >= 1)
            def _():
                _tile_wait(yl_ref, ybuf.at[other], ssem.at[other], gather=False)

            _tile_wait(yl_ref, ybuf.at[slot], ssem.at[slot], gather=False)


def _gmm(tile_exp, n_used, src, dst, xl, w1, w3, w2, *, layer, out_rows):
    n_tiles = src.shape[0]
    d, de = w1.shape[2], w1.shape[3]
    wspec = lambda w: pl.BlockSpec((1, 1) + w.shape[2:], lambda t, te, nu: (layer, te[t], 0, 0))
    table = lambda idx: pl.BlockSpec((1, 1, CPT), lambda t, te, nu: (idx(t, nu), 0, 0), memory_space=pltpu.SMEM)
    cur = lambda t, nu: jnp.minimum(t, nu[0] - 1)
    grid_spec = pltpu.PrefetchScalarGridSpec(
        num_scalar_prefetch=2,
        grid=(n_tiles,),
        in_specs=[table(cur), table(lambda t, nu: jnp.minimum(t + 1, nu[0] - 1)), table(cur),
                  pl.BlockSpec(memory_space=pl.ANY), wspec(w1), wspec(w3), wspec(w2)],
        out_specs=pl.BlockSpec(memory_space=pl.ANY),
        scratch_shapes=[
            pltpu.VMEM((d, de), BF16), pltpu.VMEM((d, de), BF16), pltpu.VMEM((de, d), BF16),
            pltpu.VMEM((2, MOE_TM, d // 2), jnp.uint32), pltpu.VMEM((2, MOE_TM, d // 2), jnp.uint32),
            pltpu.SemaphoreType.DMA((2,)), pltpu.SemaphoreType.DMA((2,)),
        ],
    )
    return pl.pallas_call(
        _gmm_body,
        grid_spec=grid_spec,
        out_shape=jax.ShapeDtypeStruct((out_rows, d // 2), jnp.uint32),
        compiler_params=_params(("arbitrary",)),
        name="moe_experts",
    )(tile_exp, n_used, src, src, dst, xl, w1, w3, w2)


def _combine_body(*refs, d, rows, nb, final):
    if final:
        tot_ref, yl_ref, rc_ref, xr_ref, mod_ref, gfin_ref, o_ref, t_scr = refs
    else:
        tot_ref, yl_ref, rc_ref, xr_ref, mod_ref, o_ref = refs
    wpt = TR // TM
    ci = lax.broadcasted_iota(jnp.int32, (TM, rows), 1)
    parts = []
    for k in range(wpt):
        keep = lax.broadcasted_iota(jnp.int32, (rows, 1), 0) < tot_ref[pl.program_id(0) * wpt + k]
        y = _unpack_bf16_pairs(jnp.where(keep, yl_ref[k * rows:(k + 1) * rows, :], jnp.uint32(0))).astype(BF16)
        rc = rc_ref[k * TM:(k + 1) * TM, :]
        pw = (jnp.where(ci == rc[:, 0:1].astype(jnp.int32), rc[:, 2:3], 0.0)
              + jnp.where(ci == rc[:, 1:2].astype(jnp.int32), rc[:, 3:4], 0.0))
        parts.append(_dot(pw.astype(BF16), y))
    moe = jnp.concatenate(parts, axis=0)
    g2 = mod_ref[0][:, 5 * d:6 * d]
    xn = xr_ref[...] + _per_batch(moe, lambda v: v * g2[None])
    if final:
        xn = _rms(xn, gfin_ref[...])
        for b, blk in enumerate(_to_batch_major(xn, t_scr, nb)):
            o_ref[b] = blk
    else:
        o_ref[...] = xn


def _combine(totals, yl, rc, xres, mod, *, n_exp, n_win, n_lat_rows, nb, gfin=None, seq_out=None):
    d = xres.shape[1]
    rows = _local_rows(n_exp)
    final = gfin is not None
    n_lat = n_lat_rows // TR
    wpt = TR // TM
    row = lambda width: pl.BlockSpec((TR, width), lambda w, a: (w, 0))
    in_specs = [pl.BlockSpec((wpt * rows, d // 2), lambda w, a: (w, 0)), row(LANES), row(d),
                pl.BlockSpec((1, SUBLANES, N_MOD * d), lambda w, a: (jnp.where(w < n_lat, 0, 1), 0, 0))]
    args = [totals, yl, rc, xres, mod]
    scratch = []
    if final:
        in_specs.append(pl.BlockSpec((1, d), lambda w, a: (0, 0)))
        args.append(gfin.reshape(1, d))
        out_shape = jax.ShapeDtypeStruct((nb, seq_out, d), F32)
        out_specs = pl.BlockSpec((nb, TR // nb, d), lambda w, a: (0, w, 0))
        scratch.append(pltpu.VMEM((d // LANES, TR, LANES), F32))
        aliases = {}
    else:
        out_shape = jax.ShapeDtypeStruct(xres.shape, F32)
        out_specs = row(d)
        aliases = {3: 0}
    grid_spec = pltpu.PrefetchScalarGridSpec(
        num_scalar_prefetch=1, grid=(n_win // wpt,), in_specs=in_specs, out_specs=out_specs,
        scratch_shapes=scratch)
    return pl.pallas_call(
        functools.partial(_combine_body, d=d, rows=rows, nb=nb, final=final),
        grid_spec=grid_spec,
        out_shape=out_shape,
        input_output_aliases=aliases,
        compiler_params=_params(("arbitrary",)),
        name="moe_combine",
    )(*args)


def _route_tables(cnt, *, n_exp, n_tiles):
    cnt = cnt[..., 0].astype(jnp.int32)
    n_win = cnt.shape[0]
    lrows = _local_rows(n_exp)
    npad = (cnt + (SUBLANES - 1)) & (-SUBLANES)
    seg = -(-jnp.sum(npad, axis=0) // MOE_TM) * MOE_TM
    ends = jnp.cumsum(seg)
    goff = (ends - seg)[None, :] + jnp.cumsum(npad, axis=0) - npad
    loff = jnp.cumsum(npad, axis=1) - npad
    tile_ends = ends // MOE_TM
    n_used = tile_ends[-1]
    tile = jnp.minimum(jnp.arange(n_tiles, dtype=jnp.int32), n_used - 1)
    tile_exp = jnp.sum(tile[:, None] >= tile_ends[None, :], axis=1).astype(jnp.int32)

    starts = (goff.T // SUBLANES).reshape(-1)
    lens = (npad.T // SUBLANES).reshape(-1)
    local = ((jnp.arange(n_win, dtype=jnp.int32)[:, None] * lrows + loff).T // SUBLANES).reshape(-1)
    group = jnp.arange(n_tiles * CPT, dtype=jnp.int32)
    half = 1 << 16
    assert n_tiles * CPT < half // 2 and n_win * lrows // SUBLANES < half // 2
    packed = (starts + lens) * half + (local - starts + half // 2)
    began = starts[None, :] <= group[:, None]
    tot = jnp.sum(jnp.where(began, jnp.diff(packed, prepend=0)[None, :], 0), axis=1)
    valid = group < tot // half
    src = jnp.where(valid, group + (tot % half) - half // 2, 0)
    dst = jnp.where(valid, src, n_win * lrows // SUBLANES + group)
    shape = (n_tiles, 1, CPT)
    return (tile_exp, n_used.reshape(1).astype(jnp.int32), src.reshape(shape).astype(jnp.int32),
            dst.reshape(shape).astype(jnp.int32), jnp.sum(npad, axis=1).astype(jnp.int32))


def _channel_dft_tables(n):
    k = jnp.arange(n, dtype=jnp.int32)
    ang = ((k[:, None] * k[None, :]) % n).astype(F32) * (2.0 * math.pi / n)
    scale = 1.0 / math.sqrt(n)
    return jnp.cos(ang) * scale, jnp.sin(ang) * scale


def kernel(x, c, ctx, c_ctx, ada_w, ada_b, norm_mix_g, w_in, conv_w, conv_b, lru_w_a, lru_b_a, lru_w_x, lru_b_x, lru_lambda, w_proj_rnn, w_proj_fourier, w_out, norm_ffn_g, router_w, router_b, moe_w1, moe_w3, moe_w2, final_norm_g):
    nb, l, d = x.shape
    lc = ctx.shape[1]
    depth = ada_w.shape[0]
    r = conv_w.shape[2]
    f = w_proj_fourier.shape[1]
    nh, blk = lru_w_a.shape[2], lru_w_a.shape[3]
    assert nb == SUBLANES and blk == LANES and nb < MOD_ROWS
    assert l % TS == 0 and lc % TS == 0 and TS % GRID_W == 0 and (l * nb) % TR == 0 and (lc * nb) % TR == 0
    assert GRID_W & (GRID_W - 1) == 0 and l % lc == 0 and l % (2 * FOLD_TILE) == 0 and l // LANES <= LANES
    s = l + lc
    n_lat_rows = l * nb

    cc = jnp.zeros((MOD_ROWS, d), F32).at[:nb].set(c).at[nb].set(c_ctx)
    mod = _ada(cc, ada_w, ada_b)
    mod = jnp.stack([mod[:, :nb], jnp.broadcast_to(mod[:, nb:nb + 1], (depth, nb, N_MOD * d))], axis=1)

    cch, sch = _channel_dft_tables(f // FOURIER_GROUPS)
    fcs = jnp.concatenate([cch, sch], axis=1).astype(BF16)
    n_exp = router_w.shape[1]
    rw = jnp.concatenate(_split(jnp.pad(router_w, ((0, 0), (0, LANES - n_exp)))), axis=1)

    out = xa = None
    for li in range(depth):
        last = li == depth - 1
        inproj = functools.partial(_inproj, mod=mod[li], g=norm_mix_g[li], w_in=w_in[li].astype(BF16), fcs=fcs,
                                   n_lat_rows=n_lat_rows, r=r, f=f, nb=nb)
        if li == 0:
            u, gl, sa, sb, pq, xa = inproj(x, ctx=ctx)
        else:
            u, gl, sa, sb, pq = inproj(xa)
        wg = [jnp.concatenate([lru_w_a[li, k], lru_w_x[li, k]], axis=-1).astype(BF16) for k in range(2)]
        bg = [jnp.concatenate([lru_b_a[li, k], lru_b_x[li, k]], axis=-1).reshape(nh, 1, 2 * blk) for k in range(2)]
        lam = [lru_lambda[li, k].reshape(nh, 1, blk) for k in range(2)]
        hb, uc = _scan(u, wg[1], bg[1], lam[1], nb=nb, n_lat_steps=l // TS, conv=(conv_w[li], conv_b[li]))
        hg = _scan(uc, wg[0], bg[0], lam[0], nb=nb, n_lat_steps=l // TS, hb=hb, gl=gl)
        n_rows = n_lat_rows if last else s * nb
        yf = _dft(pq, n=l, nb=nb, f=f, row0=0, out_rows=n_rows // nb, folded=_fold(pq, n=l, nb=nb, f=f))
        if not last:
            yf = _dft(pq, n=lc, nb=nb, f=f, row0=l, out_rows=n_rows // nb, prev=yf)
        x1, xl, rc, cnt = _mix(hg, yf, sa, sb, xa, mod[li], w_proj_rnn[li].astype(BF16),
                               w_proj_fourier[li].astype(BF16), w_out[li].astype(BF16), norm_ffn_g[li],
                               rw, router_b, n_rows=n_rows, n_lat_rows=n_lat_rows, nb=nb)
        n_win = n_rows // TM
        n_tiles = -(-(n_win * (2 * TM + n_exp * (SUBLANES - 1)) + n_exp * MOE_TM) // MOE_TM)
        tile_exp, n_used, src, dst, totals = _route_tables(cnt, n_exp=n_exp, n_tiles=n_tiles)
        yl = _gmm(tile_exp, n_used, src, dst, xl, moe_w1, moe_w3, moe_w2, layer=li,
                  out_rows=xl.shape[0] + n_tiles * MOE_TM)
        comb = functools.partial(_combine, totals, yl, rc, x1, mod[li], n_exp=n_exp, n_win=n_win,
                                 n_lat_rows=n_lat_rows, nb=nb)
        if last:
            out = comb(gfin=final_norm_g, seq_out=l)
        else:
            xa = comb()
    return out
```

```python
import functools
import math

import jax
import jax.numpy as jnp
from jax import lax
from jax.experimental import pallas as pl
from jax.experimental.pallas import tpu as pltpu

F32 = jnp.float32
BF16 = jnp.bfloat16

RMS_EPS = 1e-6
LRU_C = 8.0
GRID_W = 64
CONV_TAPS = ((0, -2), (1, -1), (2, 0), (3, 1))
N_MOD = 6
FOURIER_GROUPS = 4
N_EXPERT_GROUPS = 4
LANES = 128
SUBLANES = 8
PACKED_ROWS = 16
TR = 512
TM = 256
TS = 128
SUB = 64
MOE_TM = 512
MOD_ROWS = 16
VMEM_LIMIT = 56 * 1024 * 1024


def _dot(a, b):
    return jnp.dot(a, b, preferred_element_type=F32)


def _split(a):
    hi = a.astype(BF16)
    lo = (a - hi.astype(F32)).astype(BF16)
    return hi, lo


def _dot3(a, b):
    ah, al = _split(a)
    bh, bl = _split(b)
    return _dot(ah, bh) + _dot(al, bh) + _dot(ah, bl)


def _gelu_tanh(x):
    return 0.5 * x * (1.0 + jnp.tanh(math.sqrt(2.0 / math.pi) * (x + 0.044715 * (x * x * x))))


def _sigmoid(x):
    return 0.5 * jnp.tanh(0.5 * x) + 0.5


def _rms(x, g):
    return x * lax.rsqrt(jnp.mean(x * x, axis=-1, keepdims=True) + RMS_EPS) * g


def _per_batch(x, fn):
    rows, d = x.shape
    return fn(x.reshape(rows // SUBLANES, SUBLANES, d)).reshape(rows, d)


def _params(sem, vmem=VMEM_LIMIT):
    return pltpu.CompilerParams(dimension_semantics=sem, vmem_limit_bytes=vmem)


def _full(shape):
    zeros = (0,) * len(shape)
    return pl.BlockSpec(shape, lambda *_: zeros)


def _ada_body(cc_ref, w_ref, b_ref, o_ref):
    cc = cc_ref[...]
    s = cc * jax.nn.sigmoid(cc)
    o_ref[0] = _dot3(s, w_ref[0]) + b_ref[0]


def _ada(cc, ada_w, ada_b):
    depth, d, n = ada_w.shape
    tn = 1024
    return pl.pallas_call(
        _ada_body,
        grid=(depth, n // tn),
        in_specs=[
            pl.BlockSpec((MOD_ROWS, d), lambda l, j: (0, 0)),
            pl.BlockSpec((1, d, tn), lambda l, j: (l, 0, j)),
            pl.BlockSpec((1, 1, tn), lambda l, j: (l, 0, j)),
        ],
        out_specs=pl.BlockSpec((1, MOD_ROWS, tn), lambda l, j: (l, 0, j)),
        out_shape=jax.ShapeDtypeStruct((depth, MOD_ROWS, n), F32),
        compiler_params=_params(("arbitrary", "arbitrary")),
        name="ada_mod",
    )(cc, ada_w, ada_b.reshape(depth, 1, n))


def _to_time_major(blocks, scr, nb):
    npos, width = blocks[0].shape
    for b in range(nb):
        for g in range(width // LANES):
            scr[g, pl.ds(b, npos, stride=nb), :] = blocks[b][:, g * LANES:(g + 1) * LANES].astype(F32)
    return jnp.concatenate([scr[g] for g in range(width // LANES)], axis=-1)


def _to_batch_major(v, scr, nb):
    rows, width = v.shape
    for g in range(width // LANES):
        scr[g] = v[:, g * LANES:(g + 1) * LANES]
    return [jnp.concatenate([scr[g, pl.ds(b, rows // nb, stride=nb), :] for g in range(width // LANES)], axis=-1)
            for b in range(nb)]


def _inproj_body(*refs, d, r, f, nb, n_lat, first):
    if first:
        (x_ref, c_ref, mod_ref, g_ref, w_ref, fcs_ref,
         u_ref, gl_ref, sa_ref, sb_ref, pq_ref, xa_ref, t_scr) = refs

        @pl.when(pl.program_id(0) < n_lat)
        def _():
            xa_ref[...] = _to_time_major([x_ref[b] for b in range(nb)], t_scr, nb)

        @pl.when(pl.program_id(0) >= n_lat)
        def _():
            xa_ref[...] = _to_time_major([c_ref[b] for b in range(nb)], t_scr, nb)

        x = xa_ref[...]
    else:
        x_ref, mod_ref, g_ref, w_ref, fcs_ref, u_ref, gl_ref, sa_ref, sb_ref, pq_ref, t_scr = refs
        x = x_ref[...]
    mod = mod_ref[0]
    h = _rms(x, g_ref[...])
    h = _per_batch(h, lambda v: v * (1.0 + mod[:, d:2 * d])[None] + mod[:, 0:d][None]).astype(BF16)
    u_ref[...] = _dot(h, w_ref[:, 0:r]).astype(BF16)
    gl_ref[...] = _gelu_tanh(_dot(h, w_ref[:, r:2 * r])).astype(BF16)
    u4 = _dot(h, w_ref[:, 2 * r:2 * r + f]).astype(BF16)
    fg = fcs_ref.shape[0]
    pq_g = [_dot(u4[:, k * fg:(k + 1) * fg], fcs_ref[...]) for k in range(f // fg)]
    pq = jnp.concatenate([v[:, :fg] for v in pq_g] + [v[:, fg:] for v in pq_g], axis=-1)
    for b, blk in enumerate(_to_batch_major(pq, t_scr, nb)):
        pq_ref[:, b * 2 * f:(b + 1) * 2 * f] = blk.astype(BF16)
    s3 = 2 * r + f
    sa_ref[...] = jax.nn.sigmoid(_dot(h, w_ref[:, s3:s3 + d])).astype(BF16)
    sb_ref[...] = jax.nn.sigmoid(_dot(h, w_ref[:, s3 + d:s3 + 2 * d])).astype(BF16)


def _inproj(x, mod, g, w_in, fcs, *, n_lat_rows, r, f, nb, ctx=None):
    first = ctx is not None
    d = x.shape[-1]
    rows = (x.shape[1] + ctx.shape[1]) * nb if first else x.shape[0]
    n_lat = n_lat_rows // TR
    npos = TR // nb
    row = lambda width: pl.BlockSpec((TR, width), lambda i: (i, 0))
    common = [
        pl.BlockSpec((1, SUBLANES, N_MOD * d), lambda i: (jnp.where(i < n_lat, 0, 1), 0, 0)),
        _full((1, d)), _full(w_in.shape), _full(fcs.shape),
    ]
    out_specs = [row(r), row(r), row(d), row(d), pl.BlockSpec((npos, nb * 2 * f), lambda i: (i, 0))]
    out_shape = [
        jax.ShapeDtypeStruct((rows, r), BF16),
        jax.ShapeDtypeStruct((rows, r), BF16),
        jax.ShapeDtypeStruct((rows, d), BF16),
        jax.ShapeDtypeStruct((rows, d), BF16),
        jax.ShapeDtypeStruct((rows // nb, nb * 2 * f), BF16),
    ]
    if first:
        in_specs = [
            pl.BlockSpec((nb, npos, d), lambda i: (0, jnp.minimum(i, n_lat - 1), 0)),
            pl.BlockSpec((nb, npos, d), lambda i: (0, jnp.maximum(i - n_lat, 0), 0)),
        ] + common
        args = [x, ctx]
        out_specs.append(row(d))
        out_shape.append(jax.ShapeDtypeStruct((rows, d), F32))
    else:
        in_specs = [row(d)] + common
        args = [x]
    return pl.pallas_call(
        functools.partial(_inproj_body, d=d, r=r, f=f, nb=nb, n_lat=n_lat, first=first),
        grid=(rows // TR,),
        in_specs=in_specs,
        out_specs=out_specs,
        out_shape=out_shape,
        scratch_shapes=[pltpu.VMEM((d // LANES, TR, LANES), F32)],
        compiler_params=_params(("arbitrary",)),
        name="in_proj",
    )(*args, mod, g.reshape(1, d), w_in, fcs)


def _conv_tile(u_ref, up_ref, un_ref, cw_ref, cb_ref, uc_scr, blk, *, nb, n_lat_steps):
    trows = TS * nb
    n_steps = pl.num_programs(0)
    is_ctx = blk >= n_lat_steps
    keep_prev = jnp.where(is_ctx & (blk > n_lat_steps), 1.0, 0.0)
    keep_next = jnp.where(is_ctx & (blk < n_steps - 1), 1.0, 0.0)
    lat = jnp.where(is_ctx, 0.0, 1.0)
    halo = 2 * nb
    ext = jnp.concatenate([up_ref[...].astype(F32) * keep_prev, u_ref[...].astype(F32),
                           un_ref[...].astype(F32)[0:nb] * keep_next], axis=0)
    uc = cb_ref[...]
    for k, s in CONV_TAPS:
        lo = halo + s * nb
        uc = uc + cw_ref[k:k + 1, :] * ext[lo:lo + trows, :]
    uc_scr[...] = uc
    for p in range(GRID_W, TS, GRID_W):
        at = lambda q: ext[halo + q * nb:halo + (q + 1) * nb, :]
        rows_of = lambda q: pl.ds(q * nb, nb)
        uc_scr[rows_of(p), :] -= lat * (cw_ref[0:1, :] * at(p - 2) + cw_ref[1:2, :] * at(p - 1))
        uc_scr[rows_of(p + 1), :] -= lat * (cw_ref[0:1, :] * at(p - 1))
        uc_scr[rows_of(p - 1), :] -= lat * (cw_ref[3:4, :] * at(p))


def _scan_body(*refs, reverse, merge, nb, n_lat_steps, order):
    if merge:
        uc_ref, wg_ref, bg_ref, lam_ref, hb_ref, gl_ref, o_ref, h_scr, a_scr, b_scr, uc_scr, hs_scr = refs
    else:
        (u_ref, up_ref, un_ref, cw_ref, cb_ref, wg_ref, bg_ref, lam_ref,
         o_ref, uco_ref, h_scr, a_scr, b_scr, uc_scr, hs_scr) = refs
    nh = wg_ref.shape[0]
    rows = SUB * nb
    j = pl.program_id(0)

    @pl.when(j == 0)
    def _():
        h_scr[...] = jnp.zeros_like(h_scr)

    if merge:
        uc_scr[...] = uc_ref[...].astype(F32)
    else:
        _conv_tile(u_ref, up_ref, un_ref, cw_ref, cb_ref, uc_scr, order(j), nb=nb, n_lat_steps=n_lat_steps)
        uco_ref[...] = uc_scr[...].astype(BF16)

    lam = lam_ref[...]
    half_decay = (-0.5 * LRU_C) * (jnp.maximum(-lam, 0.0) + jnp.log1p(jnp.exp(-jnp.abs(lam))))

    subs = range(TS // SUB)
    for sub in (reversed(subs) if reverse else subs):
        r0 = sub * rows
        for g in range(nh):
            ug = uc_scr[r0:r0 + rows, g * LANES:(g + 1) * LANES]
            gates = _dot(ug.astype(BF16), wg_ref[g]) + bg_ref[g]
            log_a = half_decay[g] * jnp.tanh(0.5 * gates[:, :LANES]) + half_decay[g]
            a = jnp.exp(log_a)
            z = (-1.0 - a * a) * jnp.tanh(log_a)
            mult = jnp.where(z > 0.0, z * lax.rsqrt(z), 0.0)
            a_scr[g] = a
            b_scr[g] = mult * (_sigmoid(gates[:, LANES:]) * ug)

        def step(k, hs):
            t = (SUB - 1 - k) if reverse else k
            off = pl.multiple_of(t * nb, nb)
            new = []
            for g in range(nh):
                hg = a_scr[g, pl.ds(off, nb), :] * hs[g] + b_scr[g, pl.ds(off, nb), :]
                hs_scr[pl.ds(r0 + off, nb), g * LANES:(g + 1) * LANES] = hg
                new.append(hg)
            return tuple(new)

        hs = lax.fori_loop(0, SUB, step, tuple(h_scr[g] for g in range(nh)), unroll=8)
        for g in range(nh):
            h_scr[g] = hs[g]

    if merge:
        o_ref[...] = ((hs_scr[...] + hb_ref[...].astype(F32)) * gl_ref[...].astype(F32)).astype(BF16)
    else:
        o_ref[...] = hs_scr[...].astype(BF16)


def _scan(u, wg, bg, lam, *, nb, n_lat_steps, conv=None, hb=None, gl=None):
    rows_all, r = u.shape
    nh = wg.shape[0]
    trows = TS * nb
    n = rows_all // trows
    n_ctx = n - n_lat_steps
    merge = hb is not None
    reverse = not merge
    if reverse:
        order = lambda j: jnp.where(j < n_ctx, n - 1 - j, n_lat_steps - 1 - (j - n_ctx))
    else:
        order = lambda j: jnp.where(j < n_ctx, n_lat_steps + j, j - n_ctx)
    hpb = trows // PACKED_ROWS
    tile = pl.BlockSpec((trows, r), lambda j: (order(j), 0))
    gate_specs = [_full(wg.shape), _full(bg.shape), _full(lam.shape)]
    out = jax.ShapeDtypeStruct((rows_all, r), BF16)
    if merge:
        in_specs = [tile] + gate_specs + [tile, tile]
        args = [u, wg, bg, lam, hb, gl]
        out_specs, out_shape = tile, out
    else:
        conv_w, conv_b = conv
        in_specs = [
            tile,
            pl.BlockSpec((PACKED_ROWS, r), lambda j: (jnp.maximum(order(j) * hpb - 1, 0), 0)),
            pl.BlockSpec((PACKED_ROWS, r), lambda j: (jnp.minimum((order(j) + 1) * hpb, n * hpb - 1), 0)),
            _full(conv_w.shape), _full((1, r)),
        ] + gate_specs
        args = [u, u, u, conv_w, conv_b.reshape(1, r), wg, bg, lam]
        out_specs, out_shape = [tile, tile], [out, out]
    return pl.pallas_call(
        functools.partial(_scan_body, reverse=reverse, merge=merge, nb=nb, n_lat_steps=n_lat_steps,
                          order=order),
        grid=(n,),
        in_specs=in_specs,
        out_specs=out_specs,
        out_shape=out_shape,
        scratch_shapes=[
            pltpu.VMEM((nh, nb, LANES), F32),
            pltpu.VMEM((nh, SUB * nb, LANES), F32),
            pltpu.VMEM((nh, SUB * nb, LANES), F32),
            pltpu.VMEM((trows, r), F32),
            pltpu.VMEM((trows, r), F32),
        ],
        compiler_params=_params(("arbitrary",)),
        name="scan_fwd_merge" if merge else "scan_bwd",
    )(*args)


FOLD_TILE = 256
DFT_ROWS = 1024


def _fold_body(a_ref, b1_ref, b2_ref, o_ref, *, f, nb):
    m = pl.program_id(0)
    i = lax.broadcasted_iota(jnp.int32, (FOLD_TILE, 2 * FOLD_TILE), 0)
    col = lax.broadcasted_iota(jnp.int32, (FOLD_TILE, 2 * FOLD_TILE), 1)
    sel = (col == jnp.where(i == 0, FOLD_TILE, FOLD_TILE - i)) & ((i > 0) | (m > 0))
    pad = jnp.zeros((FOLD_TILE - PACKED_ROWS, b1_ref.shape[1]), BF16)
    mirrored = _dot(jnp.where(sel, 1.0, 0.0).astype(BF16),
                    jnp.concatenate([b1_ref[...], b2_ref[...], pad], axis=0))
    for b in range(nb):
        p = slice(b * 2 * f, b * 2 * f + f)
        q = slice(b * 2 * f + f, (b + 1) * 2 * f)
        o_ref[:, p] = (a_ref[:, p].astype(F32) + mirrored[:, p]).astype(BF16)
        o_ref[:, q] = (a_ref[:, q].astype(F32) - mirrored[:, q]).astype(BF16)


def _fold(pq, *, n, nb, f):
    nt = n // FOLD_TILE
    hpt = FOLD_TILE // PACKED_ROWS
    width = nb * 2 * f
    tile = lambda idx: pl.BlockSpec((FOLD_TILE, width), lambda m: (idx(m), 0))
    return pl.pallas_call(
        functools.partial(_fold_body, f=f, nb=nb),
        grid=(nt // 2,),
        in_specs=[tile(lambda m: m), tile(lambda m: nt - 1 - m),
                  pl.BlockSpec((PACKED_ROWS, width), lambda m: (jnp.minimum(nt - m, nt - 1) * hpt, 0))],
        out_specs=tile(lambda m: m),
        out_shape=jax.ShapeDtypeStruct((n // 2, width), BF16),
        compiler_params=_params(("arbitrary",)),
        name="dft_fold",
    )(pq, pq, pq)


def _dft_body(ca_ref, sa_ref, cb_ref, sb_ref, p_ref, q_ref, *rest, kdim, folded, scale):
    o_ref, c_scr, s_scr = rest[-3:]

    @pl.when(pl.program_id(1) == 0)
    def _():
        cb = cb_ref[...]
        sb = sb_ref[...]
        for t1 in range(kdim // LANES):
            ca = ca_ref[:, t1:t1 + 1]
            sa = sa_ref[:, t1:t1 + 1]
            c_scr[:, t1 * LANES:(t1 + 1) * LANES] = (ca * cb - sa * sb).astype(BF16)
            s_scr[:, t1 * LANES:(t1 + 1) * LANES] = (sa * cb + ca * sb).astype(BF16)

    y = _dot(c_scr[...], p_ref[...]) - _dot(s_scr[...], q_ref[...])
    if folded:
        mid_ref = rest[0]
        tmk = o_ref.shape[0]
        k = pl.program_id(0) * tmk + lax.broadcasted_iota(jnp.int32, (tmk, 1), 0)
        sign = (1 - 2 * (k & 1)).astype(F32)
        y = y + sign * (scale * mid_ref[...].astype(F32)[0:1, :])
    o_ref[...] = y.astype(BF16)


def _dft_tables(n):
    k = jnp.arange(n, dtype=jnp.int32)[:, None]
    t1 = jnp.arange(LANES, dtype=jnp.int32)[None, :]
    ang_a = ((k * ((t1 * LANES) % n)) % n).astype(F32) * (2.0 * math.pi / n)
    ang_b = ((k * t1) % n).astype(F32) * (2.0 * math.pi / n)
    scale = 1.0 / math.sqrt(n)
    return jnp.cos(ang_a), jnp.sin(ang_a), jnp.cos(ang_b) * scale, jnp.sin(ang_b) * scale


def _dft(pq, *, n, nb, f, row0, out_rows, prev=None, folded=None):
    tmk = min(n, DFT_ROWS)
    kdim = n // 2 if folded is not None else n
    tab = pl.BlockSpec((tmk, LANES), lambda m, b: (m, 0))
    if folded is not None:
        assert row0 == 0
        src, rb = folded, 0
    else:
        src, rb = pq, row0 // n
    in_specs = [
        tab, tab, tab, tab,
        pl.BlockSpec((kdim, f), lambda m, b: (rb, 2 * b)),
        pl.BlockSpec((kdim, f), lambda m, b: (rb, 2 * b + 1)),
    ]
    args = list(_dft_tables(n)) + [src, src]
    if folded is not None:
        in_specs.append(pl.BlockSpec((PACKED_ROWS, f), lambda m, b: (kdim // PACKED_ROWS, 2 * b)))
        args.append(pq)
    aliases = {}
    if prev is not None:
        in_specs.append(pl.BlockSpec(memory_space=pl.ANY))
        args.append(prev)
        aliases = {len(args) - 1: 0}
    return pl.pallas_call(
        functools.partial(_dft_body, kdim=kdim, folded=folded is not None, scale=1.0 / math.sqrt(n)),
        grid=(n // tmk, nb),
        in_specs=in_specs,
        out_specs=pl.BlockSpec((tmk, f), lambda m, b: (row0 // tmk + m, b)),
        out_shape=jax.ShapeDtypeStruct((out_rows, nb * f), BF16),
        scratch_shapes=[pltpu.VMEM((tmk, kdim), BF16), pltpu.VMEM((tmk, kdim), BF16)],
        input_output_aliases=aliases,
        compiler_params=_params(("arbitrary", "arbitrary")),
        name="pos_dft",
    )(*args)


def _top2(vals):
    def first_max(vs):
        m = functools.reduce(jnp.maximum, vs)
        idx = jnp.full(m.shape, len(vs) - 1, jnp.int32)
        for k in range(len(vs) - 2, -1, -1):
            idx = jnp.where(vs[k] == m, k, idx)
        return m, idx
    m1, i1 = first_max(vals)
    rest = [jnp.where(i1 == k, -jnp.inf, v) for k, v in enumerate(vals)]
    m2, i2 = first_max(rest)
    return m1, i1, m2, i2


def _pick(idx, vals):
    out = vals[-1]
    for k in range(len(vals) - 2, -1, -1):
        out = jnp.where(idx == k, vals[k], out)
    return out


def _route_window(h2, rw, rb, n_exp):
    hh, hl = _split(h2)
    both = _dot(hh, rw)
    logits = both[:, :LANES] + both[:, LANES:] + _dot(hl, rw[:, :LANES])
    aff = jax.nn.sigmoid(logits.T[0:n_exp, :])
    sel = aff + rb
    per = n_exp // N_EXPERT_GROUPS
    srow = [sel[e:e + 1, :] for e in range(n_exp)]
    arow = [aff[e:e + 1, :] for e in range(n_exp)]
    scores = []
    for gi in range(N_EXPERT_GROUPS):
        m1, _, m2, _ = _top2(srow[gi * per:(gi + 1) * per])
        scores.append(m1 + m2)
    best = functools.reduce(jnp.maximum, scores)
    grp = jnp.full(best.shape, N_EXPERT_GROUPS - 1, jnp.int32)
    for gi in range(N_EXPERT_GROUPS - 2, -1, -1):
        grp = jnp.where(scores[gi] == best, gi, grp)
    v = [_pick(grp, [srow[gi * per + k] for gi in range(N_EXPERT_GROUPS)]) for k in range(per)]
    a = [_pick(grp, [arow[gi * per + k] for gi in range(N_EXPERT_GROUPS)]) for k in range(per)]
    _, i1, _, i2 = _top2(v)
    w1 = _pick(i1, a)
    w2 = _pick(i2, a)
    den = w1 + w2
    e1 = grp * per + i1
    e2 = grp * per + i2

    erow = lax.broadcasted_iota(jnp.int32, (n_exp, TM), 0)
    hit1 = erow == e1
    hit2 = erow == e2
    onehot = jnp.where(hit1 | hit2, 1.0, 0.0)
    cnt = jnp.sum(onehot, axis=1, keepdims=True)
    before = (lax.broadcasted_iota(jnp.int32, (TM, TM), 0)
              < lax.broadcasted_iota(jnp.int32, (TM, TM), 1)).astype(BF16)
    rank = _dot(onehot.astype(BF16), before)
    padded = ((cnt.astype(jnp.int32) + (SUBLANES - 1)) & (-SUBLANES)).astype(F32)
    lower = (lax.broadcasted_iota(jnp.int32, (n_exp, n_exp), 1)
             < lax.broadcasted_iota(jnp.int32, (n_exp, n_exp), 0)).astype(BF16)
    starts = _dot(lower, jnp.broadcast_to(padded, (n_exp, LANES)).astype(BF16))
    pos = starts[:, 0:1] + rank
    r1 = jnp.sum(jnp.where(hit1, pos, 0.0), axis=0, keepdims=True)
    r2 = jnp.sum(jnp.where(hit2, pos, 0.0), axis=0, keepdims=True)
    route = jnp.concatenate([r1, r2, w1 / den, w2 / den, jnp.zeros((LANES - 4, TM), F32)], axis=0)
    return route, jnp.broadcast_to(cnt, (n_exp, LANES))


def _mix_body(hg_ref, yf_ref, sa_ref, sb_ref, x_ref, mod_ref, wpr_ref, wpf_ref, wo_ref,
              gffn_ref, rw_ref, rb_ref, xo_ref, xl_ref, rc_ref, cnt_ref, t_scr, *, d, n_exp, nb):
    mod = mod_ref[0]
    f = wpf_ref.shape[0]
    y_r = _dot(hg_ref[...], wpr_ref[...])
    yf = _to_time_major([yf_ref[:, b * f:(b + 1) * f] for b in range(nb)], t_scr, nb)
    y_f = _dot(yf.astype(BF16), wpf_ref[...])
    merged = sa_ref[...].astype(F32) * y_r + sb_ref[...].astype(F32) * y_f
    out = _dot(merged.astype(BF16), wo_ref[...])
    xn = x_ref[...] + _per_batch(out, lambda v: v * mod[:, 2 * d:3 * d][None])
    xo_ref[...] = xn
    h2 = _rms(xn, gffn_ref[...])
    h2 = _per_batch(h2, lambda v: v * (1.0 + mod[:, 4 * d:5 * d])[None] + mod[:, 3 * d:4 * d][None])
    rows = _local_rows(n_exp)
    riota = lax.broadcasted_iota(jnp.int32, (rows, TM), 0)
    for k in range(TR // TM):
        hw = h2[k * TM:(k + 1) * TM, :]
        route, cnt = _route_window(hw, rw_ref[...], rb_ref[...], n_exp)
        rc_ref[k * TM:(k + 1) * TM, :] = route.T
        cnt_ref[k] = cnt
        r12 = route[0:2, :].astype(jnp.int32)
        p = jnp.where((riota == r12[0:1, :]) | (riota == r12[1:2, :]), 1.0, 0.0).astype(BF16)
        xl_ref[k * rows:(k + 1) * rows, :] = _pack_bf16_pairs(_dot(p, hw.astype(BF16)))


def _mix(hg, yf, sa, sb, xa, mod, wpr, wpf, wo, gffn, rw, rb, *, n_rows, n_lat_rows, nb):
    d = xa.shape[1]
    f = wpf.shape[0]
    n_exp = rb.shape[0]
    n_lat = n_lat_rows // TR
    wpt = TR // TM
    n_win = n_rows // TM
    lrows = _local_rows(n_exp)
    row = lambda width: pl.BlockSpec((TR, width), lambda i: (i, 0))
    return pl.pallas_call(
        functools.partial(_mix_body, d=d, n_exp=n_exp, nb=nb),
        grid=(n_rows // TR,),
        in_specs=[
            row(d), pl.BlockSpec((TR // nb, nb * f), lambda i: (i, 0)), row(d), row(d), row(d),
            pl.BlockSpec((1, SUBLANES, N_MOD * d), lambda i: (jnp.where(i < n_lat, 0, 1), 0, 0)),
            _full(wpr.shape), _full(wpf.shape), _full(wo.shape),
            _full((1, d)), _full(rw.shape), _full((n_exp, 1)),
        ],
        scratch_shapes=[pltpu.VMEM((f // LANES, TR, LANES), F32)],
        out_specs=[
            row(d),
            pl.BlockSpec((wpt * lrows, d // 2), lambda i: (i, 0)),
            row(LANES),
            pl.BlockSpec((wpt, n_exp, LANES), lambda i: (i, 0, 0)),
        ],
        out_shape=[
            jax.ShapeDtypeStruct((n_rows, d), F32),
            jax.ShapeDtypeStruct((n_win * lrows, d // 2), jnp.uint32),
            jax.ShapeDtypeStruct((n_rows, LANES), F32),
            jax.ShapeDtypeStruct((n_win, n_exp, LANES), F32),
        ],
        compiler_params=_params(("arbitrary",)),
        name="mix_out",
    )(hg, yf, sa, sb, xa, mod, wpr, wpf, wo, gffn.reshape(1, d), rw, rb.reshape(n_exp, 1))


def _local_rows(n_exp):
    return -(-(2 * TM + n_exp * (SUBLANES - 1)) // LANES) * LANES


def _pack_bf16_pairs(x):
    half = x.shape[1] // 2
    bits = lax.bitcast_convert_type(x, jnp.uint32)
    return (bits[:, :half] & jnp.uint32(0xFFFF0000)) | lax.shift_right_logical(bits[:, half:], jnp.uint32(16))


def _unpack_bf16_pairs(u):
    hi = lax.bitcast_convert_type(u & jnp.uint32(0xFFFF0000), F32)
    lo = lax.bitcast_convert_type(lax.shift_left(u, jnp.uint32(16)), F32)
    return jnp.concatenate([hi, lo], axis=-1)


CPT = MOE_TM // SUBLANES


def _group_copies(table_ref, hbm, buf, sem, *, gather):
    for j in range(CPT):
        far = hbm.at[pl.ds(pl.multiple_of(table_ref[0, 0, j] * SUBLANES, SUBLANES), SUBLANES)]
        near = buf.at[pl.ds(j * SUBLANES, SUBLANES)]
        (pltpu.make_async_copy(far, near, sem) if gather else pltpu.make_async_copy(near, far, sem)).start()


def _tile_wait(hbm, buf, sem, *, gather):
    far = hbm.at[pl.ds(0, MOE_TM)]
    (pltpu.make_async_copy(far, buf, sem) if gather else pltpu.make_async_copy(buf, far, sem)).wait()


def _gmm_body(te_ref, nu_ref, src_ref, nxt_ref, dst_ref, xl_ref, w1_ref, w3_ref, w2_ref, yl_ref,
              w1b, w3b, w2b, xbuf, ybuf, gsem, ssem):
    t = pl.program_id(0)
    n_used = nu_ref[0]
    slot = lax.rem(t, 2)
    other = 1 - slot

    @pl.when((t == 0) | (te_ref[t] != te_ref[jnp.maximum(t - 1, 0)]))
    def _():
        w1b[...] = w1_ref[0, 0].astype(BF16)
        w3b[...] = w3_ref[0, 0].astype(BF16)
        w2b[...] = w2_ref[0, 0].astype(BF16)

    @pl.when(t < n_used)
    def _():
        @pl.when(t == 0)
        def _():
            _group_copies(src_ref, xl_ref, xbuf.at[slot], gsem.at[slot], gather=True)

        _tile_wait(xl_ref, xbuf.at[slot], gsem.at[slot], gather=True)
        _group_copies(nxt_ref, xl_ref, xbuf.at[other], gsem.at[other], gather=True)
        x = _unpack_bf16_pairs(xbuf[slot]).astype(BF16)
        a = _dot(x, w1b[...])
        h = (a * jax.nn.sigmoid(a)) * _dot(x, w3b[...])
        y = _dot(h.astype(BF16), w2b[...])

        @pl.when(t >= 2)
        def _():
            _tile_wait(yl_ref, ybuf.at[slot], ssem.at[slot], gather=False)

        ybuf[slot] = _pack_bf16_pairs(y.astype(BF16).astype(F32))
        _group_copies(dst_ref, yl_ref, ybuf.at[slot], ssem.at[slot], gather=False)

        @pl.when(t == n_used - 1)
        def _():
            _tile_wait(xl_ref, xbuf.at[other], gsem.at[other], gather=True)

            @pl.when(t >= 1)
            def _():
                _tile_wait(yl_ref, ybuf.at[other], ssem.at[other], gather=False)

            _tile_wait(yl_ref, ybuf.at[slot], ssem.at[slot], gather=False)


def _gmm(tile_exp, n_used, src, dst, xl, w1, w3, w2, *, layer, out_rows):
    n_tiles = src.shape[0]
    d, de = w1.shape[2], w1.shape[3]
    wspec = lambda w: pl.BlockSpec((1, 1) + w.shape[2:], lambda t, te, nu: (layer, te[t], 0, 0))
    table = lambda idx: pl.BlockSpec((1, 1, CPT), lambda t, te, nu: (idx(t, nu), 0, 0), memory_space=pltpu.SMEM)
    cur = lambda t, nu: jnp.minimum(t, nu[0] - 1)
    grid_spec = pltpu.PrefetchScalarGridSpec(
        num_scalar_prefetch=2,
        grid=(n_tiles,),
        in_specs=[table(cur), table(lambda t, nu: jnp.minimum(t + 1, nu[0] - 1)), table(cur),
                  pl.BlockSpec(memory_space=pl.ANY), wspec(w1), wspec(w3), wspec(w2)],
        out_specs=pl.BlockSpec(memory_space=pl.ANY),
        scratch_shapes=[
            pltpu.VMEM((d, de), BF16), pltpu.VMEM((d, de), BF16), pltpu.VMEM((de, d), BF16),
            pltpu.VMEM((2, MOE_TM, d // 2), jnp.uint32), pltpu.VMEM((2, MOE_TM, d // 2), jnp.uint32),
            pltpu.SemaphoreType.DMA((2,)), pltpu.SemaphoreType.DMA((2,)),
        ],
    )
    return pl.pallas_call(
        _gmm_body,
        grid_spec=grid_spec,
        out_shape=jax.ShapeDtypeStruct((out_rows, d // 2), jnp.uint32),
        compiler_params=_params(("arbitrary",)),
        name="moe_experts",
    )(tile_exp, n_used, src, src, dst, xl, w1, w3, w2)


def _combine_body(*refs, d, rows, nb, final):
    if final:
        tot_ref, yl_ref, rc_ref, xr_ref, mod_ref, gfin_ref, o_ref, t_scr = refs
    else:
        tot_ref, yl_ref, rc_ref, xr_ref, mod_ref, o_ref = refs
    wpt = TR // TM
    ci = lax.broadcasted_iota(jnp.int32, (TM, rows), 1)
    parts = []
    for k in range(wpt):
        keep = lax.broadcasted_iota(jnp.int32, (rows, 1), 0) < tot_ref[pl.program_id(0) * wpt + k]
        y = _unpack_bf16_pairs(jnp.where(keep, yl_ref[k * rows:(k + 1) * rows, :], jnp.uint32(0))).astype(BF16)
        rc = rc_ref[k * TM:(k + 1) * TM, :]
        pw = (jnp.where(ci == rc[:, 0:1].astype(jnp.int32), rc[:, 2:3], 0.0)
              + jnp.where(ci == rc[:, 1:2].astype(jnp.int32), rc[:, 3:4], 0.0))
        parts.append(_dot(pw.astype(BF16), y))
    moe = jnp.concatenate(parts, axis=0)
    g2 = mod_ref[0][:, 5 * d:6 * d]
    xn = xr_ref[...] + _per_batch(moe, lambda v: v * g2[None])
    if final:
        xn = _rms(xn, gfin_ref[...])
        for b, blk in enumerate(_to_batch_major(xn, t_scr, nb)):
            o_ref[b] = blk
    else:
        o_ref[...] = xn


def _combine(totals, yl, rc, xres, mod, *, n_exp, n_win, n_lat_rows, nb, gfin=None, seq_out=None):
    d = xres.shape[1]
    rows = _local_rows(n_exp)
    final = gfin is not None
    n_lat = n_lat_rows // TR
    wpt = TR // TM
    row = lambda width: pl.BlockSpec((TR, width), lambda w, a: (w, 0))
    in_specs = [pl.BlockSpec((wpt * rows, d // 2), lambda w, a: (w, 0)), row(LANES), row(d),
                pl.BlockSpec((1, SUBLANES, N_MOD * d), lambda w, a: (jnp.where(w < n_lat, 0, 1), 0, 0))]
    args = [totals, yl, rc, xres, mod]
    scratch = []
    if final:
        in_specs.append(pl.BlockSpec((1, d), lambda w, a: (0, 0)))
        args.append(gfin.reshape(1, d))
        out_shape = jax.ShapeDtypeStruct((nb, seq_out, d), F32)
        out_specs = pl.BlockSpec((nb, TR // nb, d), lambda w, a: (0, w, 0))
        scratch.append(pltpu.VMEM((d // LANES, TR, LANES), F32))
        aliases = {}
    else:
        out_shape = jax.ShapeDtypeStruct(xres.shape, F32)
        out_specs = row(d)
        aliases = {3: 0}
    grid_spec = pltpu.PrefetchScalarGridSpec(
        num_scalar_prefetch=1, grid=(n_win // wpt,), in_specs=in_specs, out_specs=out_specs,
        scratch_shapes=scratch)
    return pl.pallas_call(
        functools.partial(_combine_body, d=d, rows=rows, nb=nb, final=final),
        grid_spec=grid_spec,
        out_shape=out_shape,
        input_output_aliases=aliases,
        compiler_params=_params(("arbitrary",)),
        name="moe_combine",
    )(*args)


def _route_tables(cnt, *, n_exp, n_tiles):
    cnt = cnt[..., 0].astype(jnp.int32)
    n_win = cnt.shape[0]
    lrows = _local_rows(n_exp)
    npad = (cnt + (SUBLANES - 1)) & (-SUBLANES)
    seg = -(-jnp.sum(npad, axis=0) // MOE_TM) * MOE_TM
    ends = jnp.cumsum(seg)
    goff = (ends - seg)[None, :] + jnp.cumsum(npad, axis=0) - npad
    loff = jnp.cumsum(npad, axis=1) - npad
    tile_ends = ends // MOE_TM
    n_used = tile_ends[-1]
    tile = jnp.minimum(jnp.arange(n_tiles, dtype=jnp.int32), n_used - 1)
    tile_exp = jnp.sum(tile[:, None] >= tile_ends[None, :], axis=1).astype(jnp.int32)

    starts = (goff.T // SUBLANES).reshape(-1)
    lens = (npad.T // SUBLANES).reshape(-1)
    local = ((jnp.arange(n_win, dtype=jnp.int32)[:, None] * lrows + loff).T // SUBLANES).reshape(-1)
    group = jnp.arange(n_tiles * CPT, dtype=jnp.int32)
    half = 1 << 16
    assert n_tiles * CPT < half // 2 and n_win * lrows // SUBLANES < half // 2
    packed = (starts + lens) * half + (local - starts + half // 2)
    began = starts[None, :] <= group[:, None]
    tot = jnp.sum(jnp.where(began, jnp.diff(packed, prepend=0)[None, :], 0), axis=1)
    valid = group < tot // half
    src = jnp.where(valid, group + (tot % half) - half // 2, 0)
    dst = jnp.where(valid, src, n_win * lrows // SUBLANES + group)
    shape = (n_tiles, 1, CPT)
    return (tile_exp, n_used.reshape(1).astype(jnp.int32), src.reshape(shape).astype(jnp.int32),
            dst.reshape(shape).astype(jnp.int32), jnp.sum(npad, axis=1).astype(jnp.int32))


def _channel_dft_tables(n):
    k = jnp.arange(n, dtype=jnp.int32)
    ang = ((k[:, None] * k[None, :]) % n).astype(F32) * (2.0 * math.pi / n)
    scale = 1.0 / math.sqrt(n)
    return jnp.cos(ang) * scale, jnp.sin(ang) * scale


def kernel(x, c, ctx, c_ctx, ada_w, ada_b, norm_mix_g, w_in, conv_w, conv_b, lru_w_a, lru_b_a, lru_w_x, lru_b_x, lru_lambda, w_proj_rnn, w_proj_fourier, w_out, norm_ffn_g, router_w, router_b, moe_w1, moe_w3, moe_w2, final_norm_g):
    nb, l, d = x.shape
    lc = ctx.shape[1]
    depth = ada_w.shape[0]
    r = conv_w.shape[2]
    f = w_proj_fourier.shape[1]
    nh, blk = lru_w_a.shape[2], lru_w_a.shape[3]
    assert nb == SUBLANES and blk == LANES and nb < MOD_ROWS
    assert l % TS == 0 and lc % TS == 0 and TS % GRID_W == 0 and (l * nb) % TR == 0 and (lc * nb) % TR == 0
    assert GRID_W & (GRID_W - 1) == 0 and l % lc == 0 and l % (2 * FOLD_TILE) == 0 and l // LANES <= LANES
    s = l + lc
    n_lat_rows = l * nb

    cc = jnp.zeros((MOD_ROWS, d), F32).at[:nb].set(c).at[nb].set(c_ctx)
    mod = _ada(cc, ada_w, ada_b)
    mod = jnp.stack([mod[:, :nb], jnp.broadcast_to(mod[:, nb:nb + 1], (depth, nb, N_MOD * d))], axis=1)

    cch, sch = _channel_dft_tables(f // FOURIER_GROUPS)
    fcs = jnp.concatenate([cch, sch], axis=1).astype(BF16)
    n_exp = router_w.shape[1]
    rw = jnp.concatenate(_split(jnp.pad(router_w, ((0, 0), (0, LANES - n_exp)))), axis=1)

    out = xa = None
    for li in range(depth):
        last = li == depth - 1
        inproj = functools.partial(_inproj, mod=mod[li], g=norm_mix_g[li], w_in=w_in[li].astype(BF16), fcs=fcs,
                                   n_lat_rows=n_lat_rows, r=r, f=f, nb=nb)
        if li == 0:
            u, gl, sa, sb, pq, xa = inproj(x, ctx=ctx)
        else:
            u, gl, sa, sb, pq = inproj(xa)
        wg = [jnp.concatenate([lru_w_a[li, k], lru_w_x[li, k]], axis=-1).astype(BF16) for k in range(2)]
        bg = [jnp.concatenate([lru_b_a[li, k], lru_b_x[li, k]], axis=-1).reshape(nh, 1, 2 * blk) for k in range(2)]
        lam = [lru_lambda[li, k].reshape(nh, 1, blk) for k in range(2)]
        hb, uc = _scan(u, wg[1], bg[1], lam[1], nb=nb, n_lat_steps=l // TS, conv=(conv_w[li], conv_b[li]))
        hg = _scan(uc, wg[0], bg[0], lam[0], nb=nb, n_lat_steps=l // TS, hb=hb, gl=gl)
        n_rows = n_lat_rows if last else s * nb
        yf = _dft(pq, n=l, nb=nb, f=f, row0=0, out_rows=n_rows // nb, folded=_fold(pq, n=l, nb=nb, f=f))
        if not last:
            yf = _dft(pq, n=lc, nb=nb, f=f, row0=l, out_rows=n_rows // nb, prev=yf)
        x1, xl, rc, cnt = _mix(hg, yf, sa, sb, xa, mod[li], w_proj_rnn[li].astype(BF16),
                               w_proj_fourier[li].astype(BF16), w_out[li].astype(BF16), norm_ffn_g[li],
                               rw, router_b, n_rows=n_rows, n_lat_rows=n_lat_rows, nb=nb)
        n_win = n_rows // TM
        n_tiles = -(-(n_win * (2 * TM + n_exp * (SUBLANES - 1)) + n_exp * MOE_TM) // MOE_TM)
        tile_exp, n_used, src, dst, totals = _route_tables(cnt, n_exp=n_exp, n_tiles=n_tiles)
        yl = _gmm(tile_exp, n_used, src, dst, xl, moe_w1, moe_w3, moe_w2, layer=li,
                  out_rows=xl.shape[0] + n_tiles * MOE_TM)
        comb = functools.partial(_combine, totals, yl, rc, x1, mod[li], n_exp=n_exp, n_win=n_win,
                                 n_lat_rows=n_lat_rows, nb=nb)
        if last:
            out = comb(gfin=final_norm_g, seq_out=l)
        else:
            xa = comb()
    return out
```

```python
import functools
import math

import jax
import jax.numpy as jnp
from jax import lax
from jax.experimental import pallas as pl
from jax.experimental.pallas import tpu as pltpu

F32 = jnp.float32
BF16 = jnp.bfloat16

RMS_EPS = 1e-6
LRU_C = 8.0
GRID_W = 64
CONV_TAPS = ((0, -2), (1, -1), (2, 0), (3, 1))
N_MOD = 6
FOURIER_GROUPS = 4
N_EXPERT_GROUPS = 4
LANES = 128
SUBLANES = 8
PACKED_ROWS = 16
TR = 512
TM = 256
CR = 1024
TS = 128
SUB = 64
MOE_TM = 512
MOD_ROWS = 16
VMEM_LIMIT = 56 * 1024 * 1024


def _dot(a, b):
    return jnp.dot(a, b, preferred_element_type=F32)


def _split(a):
    hi = a.astype(BF16)
    lo = (a - hi.astype(F32)).astype(BF16)
    return hi, lo


def _dot3(a, b):
    ah, al = _split(a)
    bh, bl = _split(b)
    return _dot(ah, bh) + _dot(al, bh) + _dot(ah, bl)


def _gelu_tanh(x):
    return 0.5 * x * (1.0 + jnp.tanh(math.sqrt(2.0 / math.pi) * (x + 0.044715 * (x * x * x))))


def _sigmoid(x):
    return 0.5 * jnp.tanh(0.5 * x) + 0.5


def _rms(x, g):
    return x * lax.rsqrt(jnp.mean(x * x, axis=-1, keepdims=True) + RMS_EPS) * g


def _per_batch(x, fn):
    rows, d = x.shape
    return fn(x.reshape(rows // SUBLANES, SUBLANES, d)).reshape(rows, d)


def _params(sem, vmem=VMEM_LIMIT):
    return pltpu.CompilerParams(dimension_semantics=sem, vmem_limit_bytes=vmem)


def _full(shape):
    zeros = (0,) * len(shape)
    return pl.BlockSpec(shape, lambda *_: zeros)


def _ada_body(cc_ref, w_ref, b_ref, o_ref):
    cc = cc_ref[...]
    s = cc * jax.nn.sigmoid(cc)
    o_ref[0] = _dot3(s, w_ref[0]) + b_ref[0]


def _ada(cc, ada_w, ada_b):
    depth, d, n = ada_w.shape
    tn = 1024
    return pl.pallas_call(
        _ada_body,
        grid=(depth, n // tn),
        in_specs=[
            pl.BlockSpec((MOD_ROWS, d), lambda l, j: (0, 0)),
            pl.BlockSpec((1, d, tn), lambda l, j: (l, 0, j)),
            pl.BlockSpec((1, 1, tn), lambda l, j: (l, 0, j)),
        ],
        out_specs=pl.BlockSpec((1, MOD_ROWS, tn), lambda l, j: (l, 0, j)),
        out_shape=jax.ShapeDtypeStruct((depth, MOD_ROWS, n), F32),
        compiler_params=_params(("arbitrary", "arbitrary")),
        name="ada_mod",
    )(cc, ada_w, ada_b.reshape(depth, 1, n))


def _to_time_major(blocks, scr, nb):
    npos, width = blocks[0].shape
    for b in range(nb):
        for g in range(width // LANES):
            scr[g, pl.ds(b, npos, stride=nb), :] = blocks[b][:, g * LANES:(g + 1) * LANES].astype(F32)
    return jnp.concatenate([scr[g] for g in range(width // LANES)], axis=-1)


def _to_batch_major(v, scr, nb):
    rows, width = v.shape
    for g in range(width // LANES):
        scr[g] = v[:, g * LANES:(g + 1) * LANES]
    return [jnp.concatenate([scr[g, pl.ds(b, rows // nb, stride=nb), :] for g in range(width // LANES)], axis=-1)
            for b in range(nb)]


def _inproj_body(*refs, d, r, f, nb, n_lat, first):
    if first:
        (x_ref, c_ref, mod_ref, g_ref, w_ref, fcs_ref,
         u_ref, gl_ref, sa_ref, sb_ref, pq_ref, xa_ref, t_scr) = refs

        @pl.when(pl.program_id(0) < n_lat)
        def _():
            xa_ref[...] = _to_time_major([x_ref[b] for b in range(nb)], t_scr, nb)

        @pl.when(pl.program_id(0) >= n_lat)
        def _():
            xa_ref[...] = _to_time_major([c_ref[b] for b in range(nb)], t_scr, nb)

        x = xa_ref[...]
    else:
        x_ref, mod_ref, g_ref, w_ref, fcs_ref, u_ref, gl_ref, sa_ref, sb_ref, pq_ref, t_scr = refs
        x = x_ref[...]
    mod = mod_ref[0]
    h = _rms(x, g_ref[...])
    h = _per_batch(h, lambda v: v * (1.0 + mod[:, d:2 * d])[None] + mod[:, 0:d][None]).astype(BF16)
    u_ref[...] = _dot(h, w_ref[:, 0:r]).astype(BF16)
    gl_ref[...] = _gelu_tanh(_dot(h, w_ref[:, r:2 * r])).astype(BF16)
    u4 = _dot(h, w_ref[:, 2 * r:2 * r + f]).astype(BF16)
    fg = fcs_ref.shape[0]
    pq_g = [_dot(u4[:, k * fg:(k + 1) * fg], fcs_ref[...]) for k in range(f // fg)]
    pq = jnp.concatenate([v[:, :fg] for v in pq_g] + [v[:, fg:] for v in pq_g], axis=-1)
    for b, blk in enumerate(_to_batch_major(pq, t_scr, nb)):
        pq_ref[:, b * 2 * f:(b + 1) * 2 * f] = blk.astype(BF16)
    s3 = 2 * r + f
    sa_ref[...] = jax.nn.sigmoid(_dot(h, w_ref[:, s3:s3 + d])).astype(BF16)
    sb_ref[...] = jax.nn.sigmoid(_dot(h, w_ref[:, s3 + d:s3 + 2 * d])).astype(BF16)


def _inproj(x, mod, g, w_in, fcs, *, n_lat_rows, r, f, nb, ctx=None):
    first = ctx is not None
    d = x.shape[-1]
    rows = (x.shape[1] + ctx.shape[1]) * nb if first else x.shape[0]
    n_lat = n_lat_rows // TR
    npos = TR // nb
    row = lambda width: pl.BlockSpec((TR, width), lambda i: (i, 0))
    common = [
        pl.BlockSpec((1, SUBLANES, N_MOD * d), lambda i: (jnp.where(i < n_lat, 0, 1), 0, 0)),
        _full((1, d)), _full(w_in.shape), _full(fcs.shape),
    ]
    out_specs = [row(r), row(r), row(d), row(d), pl.BlockSpec((npos, nb * 2 * f), lambda i: (i, 0))]
    out_shape = [
        jax.ShapeDtypeStruct((rows, r), BF16),
        jax.ShapeDtypeStruct((rows, r), BF16),
        jax.ShapeDtypeStruct((rows, d), BF16),
        jax.ShapeDtypeStruct((rows, d), BF16),
        jax.ShapeDtypeStruct((rows // nb, nb * 2 * f), BF16),
    ]
    if first:
        in_specs = [
            pl.BlockSpec((nb, npos, d), lambda i: (0, jnp.minimum(i, n_lat - 1), 0)),
            pl.BlockSpec((nb, npos, d), lambda i: (0, jnp.maximum(i - n_lat, 0), 0)),
        ] + common
        args = [x, ctx]
        out_specs.append(row(d))
        out_shape.append(jax.ShapeDtypeStruct((rows, d), F32))
    else:
        in_specs = [row(d)] + common
        args = [x]
    return pl.pallas_call(
        functools.partial(_inproj_body, d=d, r=r, f=f, nb=nb, n_lat=n_lat, first=first),
        grid=(rows // TR,),
        in_specs=in_specs,
        out_specs=out_specs,
        out_shape=out_shape,
        scratch_shapes=[pltpu.VMEM((d // LANES, TR, LANES), F32)],
        compiler_params=_params(("arbitrary",)),
        name="in_proj",
    )(*args, mod, g.reshape(1, d), w_in, fcs)


def _conv_tile(u_ref, up_ref, un_ref, cw_ref, cb_ref, uc_scr, blk, *, nb, n_lat_steps):
    trows = TS * nb
    n_steps = pl.num_programs(0)
    is_ctx = blk >= n_lat_steps
    keep_prev = jnp.where(is_ctx & (blk > n_lat_steps), 1.0, 0.0)
    keep_next = jnp.where(is_ctx & (blk < n_steps - 1), 1.0, 0.0)
    lat = jnp.where(is_ctx, 0.0, 1.0)
    halo = 2 * nb
    ext = jnp.concatenate([up_ref[...].astype(F32) * keep_prev, u_ref[...].astype(F32),
                           un_ref[...].astype(F32)[0:nb] * keep_next], axis=0)
    uc = cb_ref[...]
    for k, s in CONV_TAPS:
        lo = halo + s * nb
        uc = uc + cw_ref[k:k + 1, :] * ext[lo:lo + trows, :]
    uc_scr[...] = uc
    for p in range(GRID_W, TS, GRID_W):
        at = lambda q: ext[halo + q * nb:halo + (q + 1) * nb, :]
        rows_of = lambda q: pl.ds(q * nb, nb)
        uc_scr[rows_of(p), :] -= lat * (cw_ref[0:1, :] * at(p - 2) + cw_ref[1:2, :] * at(p - 1))
        uc_scr[rows_of(p + 1), :] -= lat * (cw_ref[0:1, :] * at(p - 1))
        uc_scr[rows_of(p - 1), :] -= lat * (cw_ref[3:4, :] * at(p))


def _scan_body(*refs, reverse, merge, nb, n_lat_steps, order):
    if merge:
        uc_ref, wg_ref, bg_ref, lam_ref, hb_ref, gl_ref, o_ref, h_scr, a_scr, b_scr, uc_scr, hs_scr = refs
    else:
        (u_ref, up_ref, un_ref, cw_ref, cb_ref, wg_ref, bg_ref, lam_ref,
         o_ref, uco_ref, h_scr, a_scr, b_scr, uc_scr, hs_scr) = refs
    nh = wg_ref.shape[0]
    rows = SUB * nb
    j = pl.program_id(0)

    @pl.when(j == 0)
    def _():
        h_scr[...] = jnp.zeros_like(h_scr)

    if merge:
        uc_scr[...] = uc_ref[...].astype(F32)
    else:
        _conv_tile(u_ref, up_ref, un_ref, cw_ref, cb_ref, uc_scr, order(j), nb=nb, n_lat_steps=n_lat_steps)
        uco_ref[...] = uc_scr[...].astype(BF16)

    lam = lam_ref[...]
    half_decay = (-0.5 * LRU_C) * (jnp.maximum(-lam, 0.0) + jnp.log1p(jnp.exp(-jnp.abs(lam))))

    subs = range(TS // SUB)
    for sub in (reversed(subs) if reverse else subs):
        r0 = sub * rows
        for g in range(nh):
            ug = uc_scr[r0:r0 + rows, g * LANES:(g + 1) * LANES]
            gates = _dot(ug.astype(BF16), wg_ref[g]) + bg_ref[g]
            log_a = half_decay[g] * jnp.tanh(0.5 * gates[:, :LANES]) + half_decay[g]
            a = jnp.exp(log_a)
            z = (-1.0 - a * a) * jnp.tanh(log_a)
            mult = jnp.where(z > 0.0, z * lax.rsqrt(z), 0.0)
            a_scr[g] = a
            b_scr[g] = mult * (_sigmoid(gates[:, LANES:]) * ug)

        def step(k, hs):
            t = (SUB - 1 - k) if reverse else k
            off = pl.multiple_of(t * nb, nb)
            new = []
            for g in range(nh):
                hg = a_scr[g, pl.ds(off, nb), :] * hs[g] + b_scr[g, pl.ds(off, nb), :]
                hs_scr[pl.ds(r0 + off, nb), g * LANES:(g + 1) * LANES] = hg
                new.append(hg)
            return tuple(new)

        hs = lax.fori_loop(0, SUB, step, tuple(h_scr[g] for g in range(nh)), unroll=8)
        for g in range(nh):
            h_scr[g] = hs[g]

    if merge:
        o_ref[...] = ((hs_scr[...] + hb_ref[...].astype(F32)) * gl_ref[...].astype(F32)).astype(BF16)
    else:
        o_ref[...] = hs_scr[...].astype(BF16)


def _scan(u, wg, bg, lam, *, nb, n_lat_steps, conv=None, hb=None, gl=None):
    rows_all, r = u.shape
    nh = wg.shape[0]
    trows = TS * nb
    n = rows_all // trows
    n_ctx = n - n_lat_steps
    merge = hb is not None
    reverse = not merge
    if reverse:
        order = lambda j: jnp.where(j < n_ctx, n - 1 - j, n_lat_steps - 1 - (j - n_ctx))
    else:
        order = lambda j: jnp.where(j < n_ctx, n_lat_steps + j, j - n_ctx)
    hpb = trows // PACKED_ROWS
    tile = pl.BlockSpec((trows, r), lambda j: (order(j), 0))
    gate_specs = [_full(wg.shape), _full(bg.shape), _full(lam.shape)]
    out = jax.ShapeDtypeStruct((rows_all, r), BF16)
    if merge:
        in_specs = [tile] + gate_specs + [tile, tile]
        args = [u, wg, bg, lam, hb, gl]
        out_specs, out_shape = tile, out
    else:
        conv_w, conv_b = conv
        in_specs = [
            tile,
            pl.BlockSpec((PACKED_ROWS, r), lambda j: (jnp.maximum(order(j) * hpb - 1, 0), 0)),
            pl.BlockSpec((PACKED_ROWS, r), lambda j: (jnp.minimum((order(j) + 1) * hpb, n * hpb - 1), 0)),
            _full(conv_w.shape), _full((1, r)),
        ] + gate_specs
        args = [u, u, u, conv_w, conv_b.reshape(1, r), wg, bg, lam]
        out_specs, out_shape = [tile, tile], [out, out]
    return pl.pallas_call(
        functools.partial(_scan_body, reverse=reverse, merge=merge, nb=nb, n_lat_steps=n_lat_steps,
                          order=order),
        grid=(n,),
        in_specs=in_specs,
        out_specs=out_specs,
        out_shape=out_shape,
        scratch_shapes=[
            pltpu.VMEM((nh, nb, LANES), F32),
            pltpu.VMEM((nh, SUB * nb, LANES), F32),
            pltpu.VMEM((nh, SUB * nb, LANES), F32),
            pltpu.VMEM((trows, r), F32),
            pltpu.VMEM((trows, r), F32),
        ],
        compiler_params=_params(("arbitrary",)),
        name="scan_fwd_merge" if merge else "scan_bwd",
    )(*args)


FOLD_TILE = 256
DFT_ROWS = 1024


def _fold_body(a_ref, b1_ref, b2_ref, o_ref, *, f, nb):
    m = pl.program_id(0)
    i = lax.broadcasted_iota(jnp.int32, (FOLD_TILE, 2 * FOLD_TILE), 0)
    col = lax.broadcasted_iota(jnp.int32, (FOLD_TILE, 2 * FOLD_TILE), 1)
    sel = (col == jnp.where(i == 0, FOLD_TILE, FOLD_TILE - i)) & ((i > 0) | (m > 0))
    pad = jnp.zeros((FOLD_TILE - PACKED_ROWS, b1_ref.shape[1]), BF16)
    mirrored = _dot(jnp.where(sel, 1.0, 0.0).astype(BF16),
                    jnp.concatenate([b1_ref[...], b2_ref[...], pad], axis=0))
    for b in range(nb):
        p = slice(b * 2 * f, b * 2 * f + f)
        q = slice(b * 2 * f + f, (b + 1) * 2 * f)
        o_ref[:, p] = (a_ref[:, p].astype(F32) + mirrored[:, p]).astype(BF16)
        o_ref[:, q] = (a_ref[:, q].astype(F32) - mirrored[:, q]).astype(BF16)


def _fold(pq, *, n, nb, f):
    nt = n // FOLD_TILE
    hpt = FOLD_TILE // PACKED_ROWS
    width = nb * 2 * f
    tile = lambda idx: pl.BlockSpec((FOLD_TILE, width), lambda m: (idx(m), 0))
    return pl.pallas_call(
        functools.partial(_fold_body, f=f, nb=nb),
        grid=(nt // 2,),
        in_specs=[tile(lambda m: m), tile(lambda m: nt - 1 - m),
                  pl.BlockSpec((PACKED_ROWS, width), lambda m: (jnp.minimum(nt - m, nt - 1) * hpt, 0))],
        out_specs=tile(lambda m: m),
        out_shape=jax.ShapeDtypeStruct((n // 2, width), BF16),
        compiler_params=_params(("arbitrary",)),
        name="dft_fold",
    )(pq, pq, pq)


def _dft_body(ca_ref, sa_ref, cb_ref, sb_ref, p_ref, q_ref, *rest, kdim, folded, scale):
    o_ref, c_scr, s_scr = rest[-3:]

    @pl.when(pl.program_id(1) == 0)
    def _():
        cb = cb_ref[...]
        sb = sb_ref[...]
        for t1 in range(kdim // LANES):
            ca = ca_ref[:, t1:t1 + 1]
            sa = sa_ref[:, t1:t1 + 1]
            c_scr[:, t1 * LANES:(t1 + 1) * LANES] = (ca * cb - sa * sb).astype(BF16)
            s_scr[:, t1 * LANES:(t1 + 1) * LANES] = (sa * cb + ca * sb).astype(BF16)

    y = _dot(c_scr[...], p_ref[...]) - _dot(s_scr[...], q_ref[...])
    if folded:
        mid_ref = rest[0]
        tmk = o_ref.shape[0]
        k = pl.program_id(0) * tmk + lax.broadcasted_iota(jnp.int32, (tmk, 1), 0)
        sign = (1 - 2 * (k & 1)).astype(F32)
        y = y + sign * (scale * mid_ref[...].astype(F32)[0:1, :])
    o_ref[...] = y.astype(BF16)


def _dft_tables(n):
    k = jnp.arange(n, dtype=jnp.int32)[:, None]
    t1 = jnp.arange(LANES, dtype=jnp.int32)[None, :]
    ang_a = ((k * ((t1 * LANES) % n)) % n).astype(F32) * (2.0 * math.pi / n)
    ang_b = ((k * t1) % n).astype(F32) * (2.0 * math.pi / n)
    scale = 1.0 / math.sqrt(n)
    return jnp.cos(ang_a), jnp.sin(ang_a), jnp.cos(ang_b) * scale, jnp.sin(ang_b) * scale


def _dft(pq, *, n, nb, f, row0, out_rows, prev=None, folded=None):
    tmk = min(n, DFT_ROWS)
    kdim = n // 2 if folded is not None else n
    tab = pl.BlockSpec((tmk, LANES), lambda m, b: (m, 0))
    if folded is not None:
        assert row0 == 0
        src, rb = folded, 0
    else:
        src, rb = pq, row0 // n
    in_specs = [
        tab, tab, tab, tab,
        pl.BlockSpec((kdim, f), lambda m, b: (rb, 2 * b)),
        pl.BlockSpec((kdim, f), lambda m, b: (rb, 2 * b + 1)),
    ]
    args = list(_dft_tables(n)) + [src, src]
    if folded is not None:
        in_specs.append(pl.BlockSpec((PACKED_ROWS, f), lambda m, b: (kdim // PACKED_ROWS, 2 * b)))
        args.append(pq)
    aliases = {}
    if prev is not None:
        in_specs.append(pl.BlockSpec(memory_space=pl.ANY))
        args.append(prev)
        aliases = {len(args) - 1: 0}
    return pl.pallas_call(
        functools.partial(_dft_body, kdim=kdim, folded=folded is not None, scale=1.0 / math.sqrt(n)),
        grid=(n // tmk, nb),
        in_specs=in_specs,
        out_specs=pl.BlockSpec((tmk, f), lambda m, b: (row0 // tmk + m, b)),
        out_shape=jax.ShapeDtypeStruct((out_rows, nb * f), BF16),
        scratch_shapes=[pltpu.VMEM((tmk, kdim), BF16), pltpu.VMEM((tmk, kdim), BF16)],
        input_output_aliases=aliases,
        compiler_params=_params(("arbitrary", "arbitrary")),
        name="pos_dft",
    )(*args)


def _top2(vals):
    def first_max(vs):
        m = functools.reduce(jnp.maximum, vs)
        idx = jnp.full(m.shape, len(vs) - 1, jnp.int32)
        for k in range(len(vs) - 2, -1, -1):
            idx = jnp.where(vs[k] == m, k, idx)
        return m, idx
    m1, i1 = first_max(vals)
    rest = [jnp.where(i1 == k, -jnp.inf, v) for k, v in enumerate(vals)]
    m2, i2 = first_max(rest)
    return m1, i1, m2, i2


def _pick(idx, vals):
    out = vals[-1]
    for k in range(len(vals) - 2, -1, -1):
        out = jnp.where(idx == k, vals[k], out)
    return out


def _route_window(h2, rw, rb, n_exp):
    hh, hl = _split(h2)
    both = _dot(hh, rw)
    logits = both[:, :LANES] + both[:, LANES:] + _dot(hl, rw[:, :LANES])
    aff = jax.nn.sigmoid(logits.T[0:n_exp, :])
    sel = aff + rb
    per = n_exp // N_EXPERT_GROUPS
    srow = [sel[e:e + 1, :] for e in range(n_exp)]
    arow = [aff[e:e + 1, :] for e in range(n_exp)]
    scores = []
    for gi in range(N_EXPERT_GROUPS):
        m1, _, m2, _ = _top2(srow[gi * per:(gi + 1) * per])
        scores.append(m1 + m2)
    best = functools.reduce(jnp.maximum, scores)
    grp = jnp.full(best.shape, N_EXPERT_GROUPS - 1, jnp.int32)
    for gi in range(N_EXPERT_GROUPS - 2, -1, -1):
        grp = jnp.where(scores[gi] == best, gi, grp)
    v = [_pick(grp, [srow[gi * per + k] for gi in range(N_EXPERT_GROUPS)]) for k in range(per)]
    a = [_pick(grp, [arow[gi * per + k] for gi in range(N_EXPERT_GROUPS)]) for k in range(per)]
    _, i1, _, i2 = _top2(v)
    w1 = _pick(i1, a)
    w2 = _pick(i2, a)
    den = w1 + w2
    e1 = grp * per + i1
    e2 = grp * per + i2

    erow = lax.broadcasted_iota(jnp.int32, (n_exp, TM), 0)
    hit1 = erow == e1
    hit2 = erow == e2
    onehot = jnp.where(hit1 | hit2, 1.0, 0.0)
    cnt = jnp.sum(onehot, axis=1, keepdims=True)
    before = (lax.broadcasted_iota(jnp.int32, (TM, TM), 0)
              < lax.broadcasted_iota(jnp.int32, (TM, TM), 1)).astype(BF16)
    rank = _dot(onehot.astype(BF16), before)
    padded = ((cnt.astype(jnp.int32) + (SUBLANES - 1)) & (-SUBLANES)).astype(F32)
    lower = (lax.broadcasted_iota(jnp.int32, (n_exp, n_exp), 1)
             < lax.broadcasted_iota(jnp.int32, (n_exp, n_exp), 0)).astype(BF16)
    starts = _dot(lower, jnp.broadcast_to(padded, (n_exp, LANES)).astype(BF16))
    pos = starts[:, 0:1] + rank
    r1 = jnp.sum(jnp.where(hit1, pos, 0.0), axis=0, keepdims=True)
    r2 = jnp.sum(jnp.where(hit2, pos, 0.0), axis=0, keepdims=True)
    route = jnp.concatenate([r1, r2, w1 / den, w2 / den, jnp.zeros((LANES - 4, TM), F32)], axis=0)
    return route, jnp.broadcast_to(cnt, (n_exp, LANES))


def _mix_body(hg_ref, yf_ref, sa_ref, sb_ref, x_ref, mod_ref, wpr_ref, wpf_ref, wo_ref,
              gffn_ref, rw_ref, rb_ref, xo_ref, xl_ref, rc_ref, cnt_ref, t_scr, *, d, n_exp, nb):
    mod = mod_ref[0]
    f = wpf_ref.shape[0]
    y_r = _dot(hg_ref[...], wpr_ref[...])
    yf = _to_time_major([yf_ref[:, b * f:(b + 1) * f] for b in range(nb)], t_scr, nb)
    y_f = _dot(yf.astype(BF16), wpf_ref[...])
    merged = sa_ref[...].astype(F32) * y_r + sb_ref[...].astype(F32) * y_f
    out = _dot(merged.astype(BF16), wo_ref[...])
    xn = x_ref[...] + _per_batch(out, lambda v: v * mod[:, 2 * d:3 * d][None])
    xo_ref[...] = xn
    h2 = _rms(xn, gffn_ref[...])
    h2 = _per_batch(h2, lambda v: v * (1.0 + mod[:, 4 * d:5 * d])[None] + mod[:, 3 * d:4 * d][None])
    rows = _local_rows(n_exp)
    riota = lax.broadcasted_iota(jnp.int32, (rows, TM), 0)
    for k in range(TR // TM):
        hw = h2[k * TM:(k + 1) * TM, :]
        route, cnt = _route_window(hw, rw_ref[...], rb_ref[...], n_exp)
        rc_ref[k * TM:(k + 1) * TM, :] = route.T
        cnt_ref[k] = cnt
        r12 = route[0:2, :].astype(jnp.int32)
        p = jnp.where((riota == r12[0:1, :]) | (riota == r12[1:2, :]), 1.0, 0.0).astype(BF16)
        xl_ref[k * rows:(k + 1) * rows, :] = _pack_bf16_pairs(_dot(p, hw.astype(BF16)))


def _mix(hg, yf, sa, sb, xa, mod, wpr, wpf, wo, gffn, rw, rb, *, n_rows, n_lat_rows, nb):
    d = xa.shape[1]
    f = wpf.shape[0]
    n_exp = rb.shape[0]
    n_lat = n_lat_rows // TR
    wpt = TR // TM
    n_win = n_rows // TM
    lrows = _local_rows(n_exp)
    row = lambda width: pl.BlockSpec((TR, width), lambda i: (i, 0))
    return pl.pallas_call(
        functools.partial(_mix_body, d=d, n_exp=n_exp, nb=nb),
        grid=(n_rows // TR,),
        in_specs=[
            row(d), pl.BlockSpec((TR // nb, nb * f), lambda i: (i, 0)), row(d), row(d), row(d),
            pl.BlockSpec((1, SUBLANES, N_MOD * d), lambda i: (jnp.where(i < n_lat, 0, 1), 0, 0)),
            _full(wpr.shape), _full(wpf.shape), _full(wo.shape),
            _full((1, d)), _full(rw.shape), _full((n_exp, 1)),
        ],
        scratch_shapes=[pltpu.VMEM((f // LANES, TR, LANES), F32)],
        out_specs=[
            row(d),
            pl.BlockSpec((wpt * lrows, d // 2), lambda i: (i, 0)),
            row(LANES),
            pl.BlockSpec((wpt, n_exp, LANES), lambda i: (i, 0, 0)),
        ],
        out_shape=[
            jax.ShapeDtypeStruct((n_rows, d), F32),
            jax.ShapeDtypeStruct((n_win * lrows, d // 2), jnp.uint32),
            jax.ShapeDtypeStruct((n_rows, LANES), F32),
            jax.ShapeDtypeStruct((n_win, n_exp, LANES), F32),
        ],
        compiler_params=_params(("arbitrary",)),
        name="mix_out",
    )(hg, yf, sa, sb, xa, mod, wpr, wpf, wo, gffn.reshape(1, d), rw, rb.reshape(n_exp, 1))


def _local_rows(n_exp):
    return -(-(2 * TM + n_exp * (SUBLANES - 1)) // LANES) * LANES


def _pack_bf16_pairs(x):
    half = x.shape[1] // 2
    bits = lax.bitcast_convert_type(x, jnp.uint32)
    return (bits[:, :half] & jnp.uint32(0xFFFF0000)) | lax.shift_right_logical(bits[:, half:], jnp.uint32(16))


def _unpack_bf16_pairs(u):
    hi = lax.bitcast_convert_type(u & jnp.uint32(0xFFFF0000), F32)
    lo = lax.bitcast_convert_type(lax.shift_left(u, jnp.uint32(16)), F32)
    return jnp.concatenate([hi, lo], axis=-1)


CPT = MOE_TM // SUBLANES


def _group_copies(table_ref, hbm, buf, sem, *, gather):
    for j in range(CPT):
        far = hbm.at[pl.ds(pl.multiple_of(table_ref[0, 0, j] * SUBLANES, SUBLANES), SUBLANES)]
        near = buf.at[pl.ds(j * SUBLANES, SUBLANES)]
        (pltpu.make_async_copy(far, near, sem) if gather else pltpu.make_async_copy(near, far, sem)).start()


def _tile_wait(hbm, buf, sem, *, gather):
    far = hbm.at[pl.ds(0, MOE_TM)]
    (pltpu.make_async_copy(far, buf, sem) if gather else pltpu.make_async_copy(buf, far, sem)).wait()


def _gmm_body(te_ref, nu_ref, src_ref, nxt_ref, dst_ref, xl_ref, w1_ref, w3_ref, w2_ref, yl_ref,
              w1b, w3b, w2b, xbuf, ybuf, gsem, ssem):
    t = pl.program_id(0)
    n_used = nu_ref[0]
    slot = lax.rem(t, 2)
    other = 1 - slot

    @pl.when((t == 0) | (te_ref[t] != te_ref[jnp.maximum(t - 1, 0)]))
    def _():
        w1b[...] = w1_ref[0, 0].astype(BF16)
        w3b[...] = w3_ref[0, 0].astype(BF16)
        w2b[...] = w2_ref[0, 0].astype(BF16)

    @pl.when(t < n_used)
    def _():
        @pl.when(t == 0)
        def _():
            _group_copies(src_ref, xl_ref, xbuf.at[slot], gsem.at[slot], gather=True)

        _tile_wait(xl_ref, xbuf.at[slot], gsem.at[slot], gather=True)
        _group_copies(nxt_ref, xl_ref, xbuf.at[other], gsem.at[other], gather=True)
        x = _unpack_bf16_pairs(xbuf[slot]).astype(BF16)
        a = _dot(x, w1b[...])
        h = (a * jax.nn.sigmoid(a)) * _dot(x, w3b[...])
        y = _dot(h.astype(BF16), w2b[...])

        @pl.when(t >= 2)
        def _():
            _tile_wait(yl_ref, ybuf.at[slot], ssem.at[slot], gather=False)

        ybuf[slot] = _pack_bf16_pairs(y.astype(BF16).astype(F32))
        _group_copies(dst_ref, yl_ref, ybuf.at[slot], ssem.at[slot], gather=False)

        @pl.when(t == n_used - 1)
        def _():
            _tile_wait(xl_ref, xbuf.at[other], gsem.at[other], gather=True)

            @pl.when(t >= 1)
            def _():
                _tile_wait(yl_ref, ybuf.at[other], ssem.at[other], gather=False)

            _tile_wait(yl_ref, ybuf.at[slot], ssem.at[slot], gather=False)


def _gmm(tile_exp, n_used, src, dst, xl, w1, w3, w2, *, layer, out_rows):
    n_tiles = src.shape[0]
    d, de = w1.shape[2], w1.shape[3]
    wspec = lambda w: pl.BlockSpec((1, 1) + w.shape[2:], lambda t, te, nu: (layer, te[t], 0, 0))
    table = lambda idx: pl.BlockSpec((1, 1, CPT), lambda t, te, nu: (idx(t, nu), 0, 0), memory_space=pltpu.SMEM)
    cur = lambda t, nu: jnp.minimum(t, nu[0] - 1)
    grid_spec = pltpu.PrefetchScalarGridSpec(
        num_scalar_prefetch=2,
        grid=(n_tiles,),
        in_specs=[table(cur), table(lambda t, nu: jnp.minimum(t + 1, nu[0] - 1)), table(cur),
                  pl.BlockSpec(memory_space=pl.ANY), wspec(w1), wspec(w3), wspec(w2)],
        out_specs=pl.BlockSpec(memory_space=pl.ANY),
        scratch_shapes=[
            pltpu.VMEM((d, de), BF16), pltpu.VMEM((d, de), BF16), pltpu.VMEM((de, d), BF16),
            pltpu.VMEM((2, MOE_TM, d // 2), jnp.uint32), pltpu.VMEM((2, MOE_TM, d // 2), jnp.uint32),
            pltpu.SemaphoreType.DMA((2,)), pltpu.SemaphoreType.DMA((2,)),
        ],
    )
    return pl.pallas_call(
        _gmm_body,
        grid_spec=grid_spec,
        out_shape=jax.ShapeDtypeStruct((out_rows, d // 2), jnp.uint32),
        compiler_params=_params(("arbitrary",)),
        name="moe_experts",
    )(tile_exp, n_used, src, src, dst, xl, w1, w3, w2)


def _combine_body(*refs, d, rows, nb, final):
    if final:
        tot_ref, yl_ref, rc_ref, xr_ref, mod_ref, gfin_ref, o_ref, t_scr = refs
    else:
        tot_ref, yl_ref, rc_ref, xr_ref, mod_ref, o_ref = refs
    wpt = CR // TM
    ci = lax.broadcasted_iota(jnp.int32, (TM, rows), 1)
    parts = []
    for k in range(wpt):
        keep = lax.broadcasted_iota(jnp.int32, (rows, 1), 0) < tot_ref[pl.program_id(0) * wpt + k]
        y = _unpack_bf16_pairs(jnp.where(keep, yl_ref[k * rows:(k + 1) * rows, :], jnp.uint32(0))).astype(BF16)
        rc = rc_ref[k * TM:(k + 1) * TM, :]
        pw = (jnp.where(ci == rc[:, 0:1].astype(jnp.int32), rc[:, 2:3], 0.0)
              + jnp.where(ci == rc[:, 1:2].astype(jnp.int32), rc[:, 3:4], 0.0))
        parts.append(_dot(pw.astype(BF16), y))
    moe = jnp.concatenate(parts, axis=0)
    g2 = mod_ref[0][:, 5 * d:6 * d]
    xn = xr_ref[...] + _per_batch(moe, lambda v: v * g2[None])
    if final:
        xn = _rms(xn, gfin_ref[...])
        for b, blk in enumerate(_to_batch_major(xn, t_scr, nb)):
            o_ref[b] = blk
    else:
        o_ref[...] = xn


def _combine(totals, yl, rc, xres, mod, *, n_exp, n_win, n_lat_rows, nb, gfin=None, seq_out=None):
    d = xres.shape[1]
    rows = _local_rows(n_exp)
    final = gfin is not None
    n_lat = n_lat_rows // CR
    wpt = CR // TM
    row = lambda width: pl.BlockSpec((CR, width), lambda w, a: (w, 0))
    in_specs = [pl.BlockSpec((wpt * rows, d // 2), lambda w, a: (w, 0)), row(LANES), row(d),
                pl.BlockSpec((1, SUBLANES, N_MOD * d), lambda w, a: (jnp.where(w < n_lat, 0, 1), 0, 0))]
    args = [totals, yl, rc, xres, mod]
    scratch = []
    if final:
        in_specs.append(pl.BlockSpec((1, d), lambda w, a: (0, 0)))
        args.append(gfin.reshape(1, d))
        out_shape = jax.ShapeDtypeStruct((nb, seq_out, d), F32)
        out_specs = pl.BlockSpec((nb, CR // nb, d), lambda w, a: (0, w, 0))
        scratch.append(pltpu.VMEM((d // LANES, CR, LANES), F32))
        aliases = {}
    else:
        out_shape = jax.ShapeDtypeStruct(xres.shape, F32)
        out_specs = row(d)
        aliases = {3: 0}
    grid_spec = pltpu.PrefetchScalarGridSpec(
        num_scalar_prefetch=1, grid=(n_win // wpt,), in_specs=in_specs, out_specs=out_specs,
        scratch_shapes=scratch)
    return pl.pallas_call(
        functools.partial(_combine_body, d=d, rows=rows, nb=nb, final=final),
        grid_spec=grid_spec,
        out_shape=out_shape,
        input_output_aliases=aliases,
        compiler_params=_params(("arbitrary",)),
        name="moe_combine",
    )(*args)


def _route_tables(cnt, *, n_exp, n_tiles):
    cnt = cnt[..., 0].astype(jnp.int32)
    n_win = cnt.shape[0]
    lrows = _local_rows(n_exp)
    npad = (cnt + (SUBLANES - 1)) & (-SUBLANES)
    seg = -(-jnp.sum(npad, axis=0) // MOE_TM) * MOE_TM
    ends = jnp.cumsum(seg)
    goff = (ends - seg)[None, :] + jnp.cumsum(npad, axis=0) - npad
    loff = jnp.cumsum(npad, axis=1) - npad
    tile_ends = ends // MOE_TM
    n_used = tile_ends[-1]
    tile = jnp.minimum(jnp.arange(n_tiles, dtype=jnp.int32), n_used - 1)
    tile_exp = jnp.sum(tile[:, None] >= tile_ends[None, :], axis=1).astype(jnp.int32)

    starts = (goff.T // SUBLANES).reshape(-1)
    lens = (npad.T // SUBLANES).reshape(-1)
    local = ((jnp.arange(n_win, dtype=jnp.int32)[:, None] * lrows + loff).T // SUBLANES).reshape(-1)
    group = jnp.arange(n_tiles * CPT, dtype=jnp.int32)
    half = 1 << 16
    assert n_tiles * CPT < half // 2 and n_win * lrows // SUBLANES < half // 2
    packed = (starts + lens) * half + (local - starts + half // 2)
    began = starts[None, :] <= group[:, None]
    tot = jnp.sum(jnp.where(began, jnp.diff(packed, prepend=0)[None, :], 0), axis=1)
    valid = group < tot // half
    src = jnp.where(valid, group + (tot % half) - half // 2, 0)
    dst = jnp.where(valid, src, n_win * lrows // SUBLANES + group)
    shape = (n_tiles, 1, CPT)
    return (tile_exp, n_used.reshape(1).astype(jnp.int32), src.reshape(shape).astype(jnp.int32),
            dst.reshape(shape).astype(jnp.int32), jnp.sum(npad, axis=1).astype(jnp.int32))


def _channel_dft_tables(n):
    k = jnp.arange(n, dtype=jnp.int32)
    ang = ((k[:, None] * k[None, :]) % n).astype(F32) * (2.0 * math.pi / n)
    scale = 1.0 / math.sqrt(n)
    return jnp.cos(ang) * scale, jnp.sin(ang) * scale


def kernel(x, c, ctx, c_ctx, ada_w, ada_b, norm_mix_g, w_in, conv_w, conv_b, lru_w_a, lru_b_a, lru_w_x, lru_b_x, lru_lambda, w_proj_rnn, w_proj_fourier, w_out, norm_ffn_g, router_w, router_b, moe_w1, moe_w3, moe_w2, final_norm_g):
    nb, l, d = x.shape
    lc = ctx.shape[1]
    depth = ada_w.shape[0]
    r = conv_w.shape[2]
    f = w_proj_fourier.shape[1]
    nh, blk = lru_w_a.shape[2], lru_w_a.shape[3]
    assert nb == SUBLANES and blk == LANES and nb < MOD_ROWS
    assert l % TS == 0 and lc % TS == 0 and TS % GRID_W == 0 and (l * nb) % CR == 0 and (lc * nb) % CR == 0 and CR % TR == 0
    assert GRID_W & (GRID_W - 1) == 0 and l % lc == 0 and l % (2 * FOLD_TILE) == 0 and l // LANES <= LANES
    s = l + lc
    n_lat_rows = l * nb

    cc = jnp.zeros((MOD_ROWS, d), F32).at[:nb].set(c).at[nb].set(c_ctx)
    mod = _ada(cc, ada_w, ada_b)
    mod = jnp.stack([mod[:, :nb], jnp.broadcast_to(mod[:, nb:nb + 1], (depth, nb, N_MOD * d))], axis=1)

    cch, sch = _channel_dft_tables(f // FOURIER_GROUPS)
    fcs = jnp.concatenate([cch, sch], axis=1).astype(BF16)
    n_exp = router_w.shape[1]
    rw = jnp.concatenate(_split(jnp.pad(router_w, ((0, 0), (0, LANES - n_exp)))), axis=1)

    out = xa = None
    for li in range(depth):
        last = li == depth - 1
        inproj = functools.partial(_inproj, mod=mod[li], g=norm_mix_g[li], w_in=w_in[li].astype(BF16), fcs=fcs,
                                   n_lat_rows=n_lat_rows, r=r, f=f, nb=nb)
        if li == 0:
            u, gl, sa, sb, pq, xa = inproj(x, ctx=ctx)
        else:
            u, gl, sa, sb, pq = inproj(xa)
        wg = [jnp.concatenate([lru_w_a[li, k], lru_w_x[li, k]], axis=-1).astype(BF16) for k in range(2)]
        bg = [jnp.concatenate([lru_b_a[li, k], lru_b_x[li, k]], axis=-1).reshape(nh, 1, 2 * blk) for k in range(2)]
        lam = [lru_lambda[li, k].reshape(nh, 1, blk) for k in range(2)]
        hb, uc = _scan(u, wg[1], bg[1], lam[1], nb=nb, n_lat_steps=l // TS, conv=(conv_w[li], conv_b[li]))
        hg = _scan(uc, wg[0], bg[0], lam[0], nb=nb, n_lat_steps=l // TS, hb=hb, gl=gl)
        n_rows = n_lat_rows if last else s * nb
        yf = _dft(pq, n=l, nb=nb, f=f, row0=0, out_rows=n_rows // nb, folded=_fold(pq, n=l, nb=nb, f=f))
        if not last:
            yf = _dft(pq, n=lc, nb=nb, f=f, row0=l, out_rows=n_rows // nb, prev=yf)
        x1, xl, rc, cnt = _mix(hg, yf, sa, sb, xa, mod[li], w_proj_rnn[li].astype(BF16),
                               w_proj_fourier[li].astype(BF16), w_out[li].astype(BF16), norm_ffn_g[li],
                               rw, router_b, n_rows=n_rows, n_lat_rows=n_lat_rows, nb=nb)
        n_win = n_rows // TM
        n_tiles = -(-(n_win * (2 * TM + n_exp * (SUBLANES - 1)) + n_exp * MOE_TM) // MOE_TM)
        tile_exp, n_used, src, dst, totals = _route_tables(cnt, n_exp=n_exp, n_tiles=n_tiles)
        yl = _gmm(tile_exp, n_used, src, dst, xl, moe_w1, moe_w3, moe_w2, layer=li,
                  out_rows=xl.shape[0] + n_tiles * MOE_TM)
        comb = functools.partial(_combine, totals, yl, rc, x1, mod[li], n_exp=n_exp, n_win=n_win,
                                 n_lat_rows=n_lat_rows, nb=nb)
        if last:
            out = comb(gfin=final_norm_g, seq_out=l)
        else:
            xa = comb()
    return out
```

```python
import functools
import math

import jax
import jax.numpy as jnp
from jax import lax
from jax.experimental import pallas as pl
from jax.experimental.pallas import tpu as pltpu

F32 = jnp.float32
BF16 = jnp.bfloat16

RMS_EPS = 1e-6
LRU_C = 8.0
GRID_W = 64
CONV_TAPS = ((0, -2), (1, -1), (2, 0), (3, 1))
N_MOD = 6
FOURIER_GROUPS = 4
N_EXPERT_GROUPS = 4
LANES = 128
SUBLANES = 8
PACKED_ROWS = 16
TR = 512
TM = 256
CR = 1024
TS = 128
SUB = 64
MOE_TM = 512
MOD_ROWS = 16
VMEM_LIMIT = 56 * 1024 * 1024


def _dot(a, b):
    return jnp.dot(a, b, preferred_element_type=F32)


def _split(a):
    hi = a.astype(BF16)
    lo = (a - hi.astype(F32)).astype(BF16)
    return hi, lo


def _dot3(a, b):
    ah, al = _split(a)
    bh, bl = _split(b)
    return _dot(ah, bh) + _dot(al, bh) + _dot(ah, bl)


def _gelu_tanh(x):
    return 0.5 * x * (1.0 + jnp.tanh(math.sqrt(2.0 / math.pi) * (x + 0.044715 * (x * x * x))))


def _sigmoid(x):
    return 0.5 * jnp.tanh(0.5 * x) + 0.5


def _rms(x, g):
    return x * lax.rsqrt(jnp.mean(x * x, axis=-1, keepdims=True) + RMS_EPS) * g


def _per_batch(x, fn):
    rows, d = x.shape
    return fn(x.reshape(rows // SUBLANES, SUBLANES, d)).reshape(rows, d)


def _params(sem, vmem=VMEM_LIMIT):
    return pltpu.CompilerParams(dimension_semantics=sem, vmem_limit_bytes=vmem)


def _full(shape):
    zeros = (0,) * len(shape)
    return pl.BlockSpec(shape, lambda *_: zeros)


def _ada_body(cc_ref, w_ref, b_ref, o_ref):
    cc = cc_ref[...]
    s = cc * jax.nn.sigmoid(cc)
    o_ref[0] = _dot3(s, w_ref[0]) + b_ref[0]


def _ada(cc, ada_w, ada_b):
    depth, d, n = ada_w.shape
    tn = 1024
    return pl.pallas_call(
        _ada_body,
        grid=(depth, n // tn),
        in_specs=[
            pl.BlockSpec((MOD_ROWS, d), lambda l, j: (0, 0)),
            pl.BlockSpec((1, d, tn), lambda l, j: (l, 0, j)),
            pl.BlockSpec((1, 1, tn), lambda l, j: (l, 0, j)),
        ],
        out_specs=pl.BlockSpec((1, MOD_ROWS, tn), lambda l, j: (l, 0, j)),
        out_shape=jax.ShapeDtypeStruct((depth, MOD_ROWS, n), F32),
        compiler_params=_params(("arbitrary", "arbitrary")),
        name="ada_mod",
    )(cc, ada_w, ada_b.reshape(depth, 1, n))


def _to_time_major(blocks, scr, nb):
    npos, width = blocks[0].shape
    for b in range(nb):
        for g in range(width // LANES):
            scr[g, pl.ds(b, npos, stride=nb), :] = blocks[b][:, g * LANES:(g + 1) * LANES].astype(F32)
    return jnp.concatenate([scr[g] for g in range(width // LANES)], axis=-1)


def _to_batch_major(v, scr, nb):
    rows, width = v.shape
    for g in range(width // LANES):
        scr[g] = v[:, g * LANES:(g + 1) * LANES]
    return [jnp.concatenate([scr[g, pl.ds(b, rows // nb, stride=nb), :] for g in range(width // LANES)], axis=-1)
            for b in range(nb)]


def _inproj_body(*refs, d, r, f, nb, n_lat, first):
    if first:
        (x_ref, c_ref, mod_ref, g_ref, w_ref, fcs_ref,
         u_ref, gl_ref, sa_ref, sb_ref, pq_ref, xa_ref, t_scr) = refs

        @pl.when(pl.program_id(0) < n_lat)
        def _():
            xa_ref[...] = _to_time_major([x_ref[b] for b in range(nb)], t_scr, nb)

        @pl.when(pl.program_id(0) >= n_lat)
        def _():
            xa_ref[...] = _to_time_major([c_ref[b] for b in range(nb)], t_scr, nb)

        x = xa_ref[...]
    else:
        x_ref, mod_ref, g_ref, w_ref, fcs_ref, u_ref, gl_ref, sa_ref, sb_ref, pq_ref, t_scr = refs
        x = x_ref[...]
    mod = mod_ref[0]
    h = _rms(x, g_ref[...])
    h = _per_batch(h, lambda v: v * (1.0 + mod[:, d:2 * d])[None] + mod[:, 0:d][None]).astype(BF16)
    u_ref[...] = _dot(h, w_ref[:, 0:r]).astype(BF16)
    gl_ref[...] = _gelu_tanh(_dot(h, w_ref[:, r:2 * r])).astype(BF16)
    u4 = _dot(h, w_ref[:, 2 * r:2 * r + f]).astype(BF16)
    fg = fcs_ref.shape[0]
    pq_g = [_dot(u4[:, k * fg:(k + 1) * fg], fcs_ref[...]) for k in range(f // fg)]
    pq = jnp.concatenate([v[:, :fg] for v in pq_g] + [v[:, fg:] for v in pq_g], axis=-1)
    for b, blk in enumerate(_to_batch_major(pq, t_scr, nb)):
        pq_ref[:, b * 2 * f:(b + 1) * 2 * f] = blk.astype(BF16)
    s3 = 2 * r + f
    sa_ref[...] = jax.nn.sigmoid(_dot(h, w_ref[:, s3:s3 + d])).astype(BF16)
    sb_ref[...] = jax.nn.sigmoid(_dot(h, w_ref[:, s3 + d:s3 + 2 * d])).astype(BF16)


def _inproj(x, mod, g, w_in, fcs, *, n_lat_rows, r, f, nb, ctx=None):
    first = ctx is not None
    d = x.shape[-1]
    rows = (x.shape[1] + ctx.shape[1]) * nb if first else x.shape[0]
    n_lat = n_lat_rows // TR
    npos = TR // nb
    row = lambda width: pl.BlockSpec((TR, width), lambda i: (i, 0))
    common = [
        pl.BlockSpec((1, SUBLANES, N_MOD * d), lambda i: (jnp.where(i < n_lat, 0, 1), 0, 0)),
        _full((1, d)), _full(w_in.shape), _full(fcs.shape),
    ]
    out_specs = [row(r), row(r), row(d), row(d), pl.BlockSpec((npos, nb * 2 * f), lambda i: (i, 0))]
    out_shape = [
        jax.ShapeDtypeStruct((rows, r), BF16),
        jax.ShapeDtypeStruct((rows, r), BF16),
        jax.ShapeDtypeStruct((rows, d), BF16),
        jax.ShapeDtypeStruct((rows, d), BF16),
        jax.ShapeDtypeStruct((rows // nb, nb * 2 * f), BF16),
    ]
    if first:
        in_specs = [
            pl.BlockSpec((nb, npos, d), lambda i: (0, jnp.minimum(i, n_lat - 1), 0)),
            pl.BlockSpec((nb, npos, d), lambda i: (0, jnp.maximum(i - n_lat, 0), 0)),
        ] + common
        args = [x, ctx]
        out_specs.append(row(d))
        out_shape.append(jax.ShapeDtypeStruct((rows, d), F32))
    else:
        in_specs = [row(d)] + common
        args = [x]
    return pl.pallas_call(
        functools.partial(_inproj_body, d=d, r=r, f=f, nb=nb, n_lat=n_lat, first=first),
        grid=(rows // TR,),
        in_specs=in_specs,
        out_specs=out_specs,
        out_shape=out_shape,
        scratch_shapes=[pltpu.VMEM((d // LANES, TR, LANES), F32)],
        compiler_params=_params(("arbitrary",)),
        name="in_proj",
    )(*args, mod, g.reshape(1, d), w_in, fcs)


def _conv_tile(u_ref, up_ref, un_ref, cw_ref, cb_ref, uc_scr, blk, *, nb, n_lat_steps):
    trows = TS * nb
    n_steps = pl.num_programs(0)
    is_ctx = blk >= n_lat_steps
    keep_prev = jnp.where(is_ctx & (blk > n_lat_steps), 1.0, 0.0)
    keep_next = jnp.where(is_ctx & (blk < n_steps - 1), 1.0, 0.0)
    lat = jnp.where(is_ctx, 0.0, 1.0)
    halo = 2 * nb
    ext = jnp.concatenate([up_ref[...].astype(F32) * keep_prev, u_ref[...].astype(F32),
                           un_ref[...].astype(F32)[0:nb] * keep_next], axis=0)
    uc = cb_ref[...]
    for k, s in CONV_TAPS:
        lo = halo + s * nb
        uc = uc + cw_ref[k:k + 1, :] * ext[lo:lo + trows, :]
    uc_scr[...] = uc
    for p in range(GRID_W, TS, GRID_W):
        at = lambda q: ext[halo + q * nb:halo + (q + 1) * nb, :]
        rows_of = lambda q: pl.ds(q * nb, nb)
        uc_scr[rows_of(p), :] -= lat * (cw_ref[0:1, :] * at(p - 2) + cw_ref[1:2, :] * at(p - 1))
        uc_scr[rows_of(p + 1), :] -= lat * (cw_ref[0:1, :] * at(p - 1))
        uc_scr[rows_of(p - 1), :] -= lat * (cw_ref[3:4, :] * at(p))


def _scan_body(*refs, reverse, merge, nb, n_lat_steps, order):
    if merge:
        uc_ref, wg_ref, bg_ref, lam_ref, hb_ref, gl_ref, o_ref, h_scr, a_scr, b_scr, uc_scr, hs_scr = refs
    else:
        (u_ref, up_ref, un_ref, cw_ref, cb_ref, wg_ref, bg_ref, lam_ref,
         o_ref, uco_ref, h_scr, a_scr, b_scr, uc_scr, hs_scr) = refs
    nh = wg_ref.shape[0]
    rows = SUB * nb
    j = pl.program_id(0)

    @pl.when(j == 0)
    def _():
        h_scr[...] = jnp.zeros_like(h_scr)

    if merge:
        uc_scr[...] = uc_ref[...].astype(F32)
    else:
        _conv_tile(u_ref, up_ref, un_ref, cw_ref, cb_ref, uc_scr, order(j), nb=nb, n_lat_steps=n_lat_steps)
        uco_ref[...] = uc_scr[...].astype(BF16)

    lam = lam_ref[...]
    half_decay = (-0.5 * LRU_C) * (jnp.maximum(-lam, 0.0) + jnp.log1p(jnp.exp(-jnp.abs(lam))))

    subs = range(TS // SUB)
    for sub in (reversed(subs) if reverse else subs):
        r0 = sub * rows
        for g in range(nh):
            ug = uc_scr[r0:r0 + rows, g * LANES:(g + 1) * LANES]
            gates = _dot(ug.astype(BF16), wg_ref[g]) + bg_ref[g]
            log_a = half_decay[g] * jnp.tanh(0.5 * gates[:, :LANES]) + half_decay[g]
            a = jnp.exp(log_a)
            z = (-1.0 - a * a) * jnp.tanh(log_a)
            mult = jnp.where(z > 0.0, z * lax.rsqrt(z), 0.0)
            a_scr[g] = a
            b_scr[g] = mult * (_sigmoid(gates[:, LANES:]) * ug)

        def step(k, hs):
            t = (SUB - 1 - k) if reverse else k
            off = pl.multiple_of(t * nb, nb)
            new = []
            for g in range(nh):
                hg = a_scr[g, pl.ds(off, nb), :] * hs[g] + b_scr[g, pl.ds(off, nb), :]
                hs_scr[pl.ds(r0 + off, nb), g * LANES:(g + 1) * LANES] = hg
                new.append(hg)
            return tuple(new)

        hs = lax.fori_loop(0, SUB, step, tuple(h_scr[g] for g in range(nh)), unroll=8)
        for g in range(nh):
            h_scr[g] = hs[g]

    if merge:
        o_ref[...] = ((hs_scr[...] + hb_ref[...].astype(F32)) * gl_ref[...].astype(F32)).astype(BF16)
    else:
        o_ref[...] = hs_scr[...].astype(BF16)


def _scan(u, wg, bg, lam, *, nb, n_lat_steps, conv=None, hb=None, gl=None):
    rows_all, r = u.shape
    nh = wg.shape[0]
    trows = TS * nb
    n = rows_all // trows
    n_ctx = n - n_lat_steps
    merge = hb is not None
    reverse = not merge
    if reverse:
        order = lambda j: jnp.where(j < n_ctx, n - 1 - j, n_lat_steps - 1 - (j - n_ctx))
    else:
        order = lambda j: jnp.where(j < n_ctx, n_lat_steps + j, j - n_ctx)
    hpb = trows // PACKED_ROWS
    tile = pl.BlockSpec((trows, r), lambda j: (order(j), 0))
    gate_specs = [_full(wg.shape), _full(bg.shape), _full(lam.shape)]
    out = jax.ShapeDtypeStruct((rows_all, r), BF16)
    if merge:
        in_specs = [tile] + gate_specs + [tile, tile]
        args = [u, wg, bg, lam, hb, gl]
        out_specs, out_shape = tile, out
    else:
        conv_w, conv_b = conv
        in_specs = [
            tile,
            pl.BlockSpec((PACKED_ROWS, r), lambda j: (jnp.maximum(order(j) * hpb - 1, 0), 0)),
            pl.BlockSpec((PACKED_ROWS, r), lambda j: (jnp.minimum((order(j) + 1) * hpb, n * hpb - 1), 0)),
            _full(conv_w.shape), _full((1, r)),
        ] + gate_specs
        args = [u, u, u, conv_w, conv_b.reshape(1, r), wg, bg, lam]
        out_specs, out_shape = [tile, tile], [out, out]
    return pl.pallas_call(
        functools.partial(_scan_body, reverse=reverse, merge=merge, nb=nb, n_lat_steps=n_lat_steps,
                          order=order),
        grid=(n,),
        in_specs=in_specs,
        out_specs=out_specs,
        out_shape=out_shape,
        scratch_shapes=[
            pltpu.VMEM((nh, nb, LANES), F32),
            pltpu.VMEM((nh, SUB * nb, LANES), F32),
            pltpu.VMEM((nh, SUB * nb, LANES), F32),
            pltpu.VMEM((trows, r), F32),
            pltpu.VMEM((trows, r), F32),
        ],
        compiler_params=_params(("arbitrary",)),
        name="scan_fwd_merge" if merge else "scan_bwd",
    )(*args)


FOLD_TILE = 256
DFT_ROWS = 1024


def _fold_body(a_ref, b1_ref, b2_ref, o_ref, *, f, nb):
    m = pl.program_id(0)
    i = lax.broadcasted_iota(jnp.int32, (FOLD_TILE, 2 * FOLD_TILE), 0)
    col = lax.broadcasted_iota(jnp.int32, (FOLD_TILE, 2 * FOLD_TILE), 1)
    sel = (col == jnp.where(i == 0, FOLD_TILE, FOLD_TILE - i)) & ((i > 0) | (m > 0))
    pad = jnp.zeros((FOLD_TILE - PACKED_ROWS, b1_ref.shape[1]), BF16)
    mirrored = _dot(jnp.where(sel, 1.0, 0.0).astype(BF16),
                    jnp.concatenate([b1_ref[...], b2_ref[...], pad], axis=0))
    for b in range(nb):
        p = slice(b * 2 * f, b * 2 * f + f)
        q = slice(b * 2 * f + f, (b + 1) * 2 * f)
        o_ref[:, p] = (a_ref[:, p].astype(F32) + mirrored[:, p]).astype(BF16)
        o_ref[:, q] = (a_ref[:, q].astype(F32) - mirrored[:, q]).astype(BF16)


def _fold(pq, *, n, nb, f):
    nt = n // FOLD_TILE
    hpt = FOLD_TILE // PACKED_ROWS
    width = nb * 2 * f
    tile = lambda idx: pl.BlockSpec((FOLD_TILE, width), lambda m: (idx(m), 0))
    return pl.pallas_call(
        functools.partial(_fold_body, f=f, nb=nb),
        grid=(nt // 2,),
        in_specs=[tile(lambda m: m), tile(lambda m: nt - 1 - m),
                  pl.BlockSpec((PACKED_ROWS, width), lambda m: (jnp.minimum(nt - m, nt - 1) * hpt, 0))],
        out_specs=tile(lambda m: m),
        out_shape=jax.ShapeDtypeStruct((n // 2, width), BF16),
        compiler_params=_params(("arbitrary",)),
        name="dft_fold",
    )(pq, pq, pq)


def _dft_body(ca_ref, sa_ref, cb_ref, sb_ref, p_ref, q_ref, *rest, kdim, folded, scale):
    o_ref, c_scr, s_scr = rest[-3:]

    @pl.when(pl.program_id(1) == 0)
    def _():
        cb = cb_ref[...]
        sb = sb_ref[...]
        for t1 in range(kdim // LANES):
            ca = ca_ref[:, t1:t1 + 1]
            sa = sa_ref[:, t1:t1 + 1]
            c_scr[:, t1 * LANES:(t1 + 1) * LANES] = (ca * cb - sa * sb).astype(BF16)
            s_scr[:, t1 * LANES:(t1 + 1) * LANES] = (sa * cb + ca * sb).astype(BF16)

    y = _dot(c_scr[...], p_ref[...]) - _dot(s_scr[...], q_ref[...])
    if folded:
        mid_ref = rest[0]
        tmk = o_ref.shape[0]
        k = pl.program_id(0) * tmk + lax.broadcasted_iota(jnp.int32, (tmk, 1), 0)
        sign = (1 - 2 * (k & 1)).astype(F32)
        y = y + sign * (scale * mid_ref[...].astype(F32)[0:1, :])
    o_ref[...] = y.astype(BF16)


def _dft_tables(n):
    k = jnp.arange(n, dtype=jnp.int32)[:, None]
    t1 = jnp.arange(LANES, dtype=jnp.int32)[None, :]
    ang_a = ((k * ((t1 * LANES) % n)) % n).astype(F32) * (2.0 * math.pi / n)
    ang_b = ((k * t1) % n).astype(F32) * (2.0 * math.pi / n)
    scale = 1.0 / math.sqrt(n)
    return jnp.cos(ang_a), jnp.sin(ang_a), jnp.cos(ang_b) * scale, jnp.sin(ang_b) * scale


def _dft(pq, *, n, nb, f, row0, out_rows, prev=None, folded=None):
    tmk = min(n, DFT_ROWS)
    kdim = n // 2 if folded is not None else n
    tab = pl.BlockSpec((tmk, LANES), lambda m, b: (m, 0))
    if folded is not None:
        assert row0 == 0
        src, rb = folded, 0
    else:
        src, rb = pq, row0 // n
    in_specs = [
        tab, tab, tab, tab,
        pl.BlockSpec((kdim, f), lambda m, b: (rb, 2 * b)),
        pl.BlockSpec((kdim, f), lambda m, b: (rb, 2 * b + 1)),
    ]
    args = list(_dft_tables(n)) + [src, src]
    if folded is not None:
        in_specs.append(pl.BlockSpec((PACKED_ROWS, f), lambda m, b: (kdim // PACKED_ROWS, 2 * b)))
        args.append(pq)
    aliases = {}
    if prev is not None:
        in_specs.append(pl.BlockSpec(memory_space=pl.ANY))
        args.append(prev)
        aliases = {len(args) - 1: 0}
    return pl.pallas_call(
        functools.partial(_dft_body, kdim=kdim, folded=folded is not None, scale=1.0 / math.sqrt(n)),
        grid=(n // tmk, nb),
        in_specs=in_specs,
        out_specs=pl.BlockSpec((tmk, f), lambda m, b: (row0 // tmk + m, b)),
        out_shape=jax.ShapeDtypeStruct((out_rows, nb * f), BF16),
        scratch_shapes=[pltpu.VMEM((tmk, kdim), BF16), pltpu.VMEM((tmk, kdim), BF16)],
        input_output_aliases=aliases,
        compiler_params=_params(("arbitrary", "arbitrary")),
        name="pos_dft",
    )(*args)


def _top2(vals):
    def first_max(vs):
        m = functools.reduce(jnp.maximum, vs)
        idx = jnp.full(m.shape, len(vs) - 1, jnp.int32)
        for k in range(len(vs) - 2, -1, -1):
            idx = jnp.where(vs[k] == m, k, idx)
        return m, idx
    m1, i1 = first_max(vals)
    rest = [jnp.where(i1 == k, -jnp.inf, v) for k, v in enumerate(vals)]
    m2, i2 = first_max(rest)
    return m1, i1, m2, i2


def _pick(idx, vals):
    out = vals[-1]
    for k in range(len(vals) - 2, -1, -1):
        out = jnp.where(idx == k, vals[k], out)
    return out


def _route_window(h2, rw, rb, n_exp):
    hh, hl = _split(h2)
    both = _dot(hh, rw)
    logits = both[:, :LANES] + both[:, LANES:] + _dot(hl, rw[:, :LANES])
    aff = jax.nn.sigmoid(logits.T[0:n_exp, :])
    sel = aff + rb
    per = n_exp // N_EXPERT_GROUPS
    srow = [sel[e:e + 1, :] for e in range(n_exp)]
    arow = [aff[e:e + 1, :] for e in range(n_exp)]
    scores = []
    for gi in range(N_EXPERT_GROUPS):
        m1, _, m2, _ = _top2(srow[gi * per:(gi + 1) * per])
        scores.append(m1 + m2)
    best = functools.reduce(jnp.maximum, scores)
    grp = jnp.full(best.shape, N_EXPERT_GROUPS - 1, jnp.int32)
    for gi in range(N_EXPERT_GROUPS - 2, -1, -1):
        grp = jnp.where(scores[gi] == best, gi, grp)
    v = [_pick(grp, [srow[gi * per + k] for gi in range(N_EXPERT_GROUPS)]) for k in range(per)]
    a = [_pick(grp, [arow[gi * per + k] for gi in range(N_EXPERT_GROUPS)]) for k in range(per)]
    _, i1, _, i2 = _top2(v)
    w1 = _pick(i1, a)
    w2 = _pick(i2, a)
    den = w1 + w2
    e1 = grp * per + i1
    e2 = grp * per + i2

    erow = lax.broadcasted_iota(jnp.int32, (n_exp, TM), 0)
    hit1 = erow == e1
    hit2 = erow == e2
    onehot = jnp.where(hit1 | hit2, 1.0, 0.0)
    cnt = jnp.sum(onehot, axis=1, keepdims=True)
    before = (lax.broadcasted_iota(jnp.int32, (TM, TM), 0)
              < lax.broadcasted_iota(jnp.int32, (TM, TM), 1)).astype(BF16)
    rank = _dot(onehot.astype(BF16), before)
    padded = ((cnt.astype(jnp.int32) + (SUBLANES - 1)) & (-SUBLANES)).astype(F32)
    lower = (lax.broadcasted_iota(jnp.int32, (n_exp, n_exp), 1)
             < lax.broadcasted_iota(jnp.int32, (n_exp, n_exp), 0)).astype(BF16)
    starts = _dot(lower, jnp.broadcast_to(padded, (n_exp, LANES)).astype(BF16))
    pos = starts[:, 0:1] + rank
    r1 = jnp.sum(jnp.where(hit1, pos, 0.0), axis=0, keepdims=True)
    r2 = jnp.sum(jnp.where(hit2, pos, 0.0), axis=0, keepdims=True)
    route = jnp.concatenate([r1, r2, w1 / den, w2 / den, jnp.zeros((LANES - 4, TM), F32)], axis=0)
    return route, jnp.broadcast_to(cnt, (n_exp, LANES))


def _mix_body(hg_ref, yf_ref, sa_ref, sb_ref, x_ref, mod_ref, wpr_ref, wpf_ref, wo_ref,
              gffn_ref, rw_ref, rb_ref, xo_ref, xl_ref, rc_ref, cnt_ref, t_scr, *, d, n_exp, nb):
    mod = mod_ref[0]
    f = wpf_ref.shape[0]
    y_r = _dot(hg_ref[...], wpr_ref[...])
    yf = _to_time_major([yf_ref[:, b * f:(b + 1) * f] for b in range(nb)], t_scr, nb)
    y_f = _dot(yf.astype(BF16), wpf_ref[...])
    merged = sa_ref[...].astype(F32) * y_r + sb_ref[...].astype(F32) * y_f
    out = _dot(merged.astype(BF16), wo_ref[...])
    xn = x_ref[...] + _per_batch(out, lambda v: v * mod[:, 2 * d:3 * d][None])
    xo_ref[...] = xn
    h2 = _rms(xn, gffn_ref[...])
    h2 = _per_batch(h2, lambda v: v * (1.0 + mod[:, 4 * d:5 * d])[None] + mod[:, 3 * d:4 * d][None])
    rows = _local_rows(n_exp)
    riota = lax.broadcasted_iota(jnp.int32, (rows, TM), 0)
    for k in range(TR // TM):
        hw = h2[k * TM:(k + 1) * TM, :]
        route, cnt = _route_window(hw, rw_ref[...], rb_ref[...], n_exp)
        rc_ref[k * TM:(k + 1) * TM, :] = route.T
        cnt_ref[k] = cnt
        r12 = route[0:2, :].astype(jnp.int32)
        p = jnp.where((riota == r12[0:1, :]) | (riota == r12[1:2, :]), 1.0, 0.0).astype(BF16)
        xl_ref[k * rows:(k + 1) * rows, :] = _pack_bf16_pairs(_dot(p, hw.astype(BF16)))


def _mix(hg, yf, sa, sb, xa, mod, wpr, wpf, wo, gffn, rw, rb, *, n_rows, n_lat_rows, nb):
    d = xa.shape[1]
    f = wpf.shape[0]
    n_exp = rb.shape[0]
    n_lat = n_lat_rows // TR
    wpt = TR // TM
    n_win = n_rows // TM
    lrows = _local_rows(n_exp)
    row = lambda width: pl.BlockSpec((TR, width), lambda i: (i, 0))
    return pl.pallas_call(
        functools.partial(_mix_body, d=d, n_exp=n_exp, nb=nb),
        grid=(n_rows // TR,),
        in_specs=[
            row(d), pl.BlockSpec((TR // nb, nb * f), lambda i: (i, 0)), row(d), row(d), row(d),
            pl.BlockSpec((1, SUBLANES, N_MOD * d), lambda i: (jnp.where(i < n_lat, 0, 1), 0, 0)),
            _full(wpr.shape), _full(wpf.shape), _full(wo.shape),
            _full((1, d)), _full(rw.shape), _full((n_exp, 1)),
        ],
        scratch_shapes=[pltpu.VMEM((f // LANES, TR, LANES), F32)],
        out_specs=[
            row(d),
            pl.BlockSpec((wpt * lrows, d // 2), lambda i: (i, 0)),
            row(LANES),
            pl.BlockSpec((wpt, n_exp, LANES), lambda i: (i, 0, 0)),
        ],
        out_shape=[
            jax.ShapeDtypeStruct((n_rows, d), F32),
            jax.ShapeDtypeStruct((n_win * lrows, d // 2), jnp.uint32),
            jax.ShapeDtypeStruct((n_rows, LANES), F32),
            jax.ShapeDtypeStruct((n_win, n_exp, LANES), F32),
        ],
        compiler_params=_params(("arbitrary",)),
        name="mix_out",
    )(hg, yf, sa, sb, xa, mod, wpr, wpf, wo, gffn.reshape(1, d), rw, rb.reshape(n_exp, 1))


def _local_rows(n_exp):
    return -(-(2 * TM + n_exp * (SUBLANES - 1)) // LANES) * LANES


def _pack_bf16_pairs(x):
    half = x.shape[1] // 2
    bits = lax.bitcast_convert_type(x, jnp.uint32)
    return (bits[:, :half] & jnp.uint32(0xFFFF0000)) | lax.shift_right_logical(bits[:, half:], jnp.uint32(16))


def _unpack_bf16_pairs(u):
    hi = lax.bitcast_convert_type(u & jnp.uint32(0xFFFF0000), F32)
    lo = lax.bitcast_convert_type(lax.shift_left(u, jnp.uint32(16)), F32)
    return jnp.concatenate([hi, lo], axis=-1)


CPT = MOE_TM // SUBLANES


def _group_copies(table_ref, hbm, buf, sem, *, gather):
    for j in range(CPT):
        far = hbm.at[pl.ds(pl.multiple_of(table_ref[0, 0, j] * SUBLANES, SUBLANES), SUBLANES)]
        near = buf.at[pl.ds(j * SUBLANES, SUBLANES)]
        copy = pltpu.make_async_copy(far, near, sem) if gather else pltpu.make_async_copy(near, far, sem)
        copy.start(priority=j % 2)


def _tile_wait(hbm, buf, sem, *, gather):
    far = hbm.at[pl.ds(0, MOE_TM)]
    (pltpu.make_async_copy(far, buf, sem) if gather else pltpu.make_async_copy(buf, far, sem)).wait()


def _gmm_body(te_ref, nu_ref, src_ref, nxt_ref, dst_ref, xl_ref, w1_ref, w3_ref, w2_ref, yl_ref,
              w1b, w3b, w2b, xbuf, ybuf, gsem, ssem):
    t = pl.program_id(0)
    n_used = nu_ref[0]
    slot = lax.rem(t, 2)
    other = 1 - slot

    @pl.when((t == 0) | (te_ref[t] != te_ref[jnp.maximum(t - 1, 0)]))
    def _():
        w1b[...] = w1_ref[0, 0].astype(BF16)
        w3b[...] = w3_ref[0, 0].astype(BF16)
        w2b[...] = w2_ref[0, 0].astype(BF16)

    @pl.when(t < n_used)
    def _():
        @pl.when(t == 0)
        def _():
            _group_copies(src_ref, xl_ref, xbuf.at[slot], gsem.at[slot], gather=True)

        _tile_wait(xl_ref, xbuf.at[slot], gsem.at[slot], gather=True)
        _group_copies(nxt_ref, xl_ref, xbuf.at[other], gsem.at[other], gather=True)
        x = _unpack_bf16_pairs(xbuf[slot]).astype(BF16)
        a = _dot(x, w1b[...])
        h = (a * jax.nn.sigmoid(a)) * _dot(x, w3b[...])
        y = _dot(h.astype(BF16), w2b[...])

        @pl.when(t >= 2)
        def _():
            _tile_wait(yl_ref, ybuf.at[slot], ssem.at[slot], gather=False)

        ybuf[slot] = _pack_bf16_pairs(y.astype(BF16).astype(F32))
        _group_copies(dst_ref, yl_ref, ybuf.at[slot], ssem.at[slot], gather=False)

        @pl.when(t == n_used - 1)
        def _():
            _tile_wait(xl_ref, xbuf.at[other], gsem.at[other], gather=True)

            @pl.when(t >= 1)
            def _():
                _tile_wait(yl_ref, ybuf.at[other], ssem.at[other], gather=False)

            _tile_wait(yl_ref, ybuf.at[slot], ssem.at[slot], gather=False)


def _gmm(tile_exp, n_used, src, dst, xl, w1, w3, w2, *, layer, out_rows):
    n_tiles = src.shape[0]
    d, de = w1.shape[2], w1.shape[3]
    wspec = lambda w: pl.BlockSpec((1, 1) + w.shape[2:], lambda t, te, nu: (layer, te[t], 0, 0))
    table = lambda idx: pl.BlockSpec((1, 1, CPT), lambda t, te, nu: (idx(t, nu), 0, 0), memory_space=pltpu.SMEM)
    cur = lambda t, nu: jnp.minimum(t, nu[0] - 1)
    grid_spec = pltpu.PrefetchScalarGridSpec(
        num_scalar_prefetch=2,
        grid=(n_tiles,),
        in_specs=[table(cur), table(lambda t, nu: jnp.minimum(t + 1, nu[0] - 1)), table(cur),
                  pl.BlockSpec(memory_space=pl.ANY), wspec(w1), wspec(w3), wspec(w2)],
        out_specs=pl.BlockSpec(memory_space=pl.ANY),
        scratch_shapes=[
            pltpu.VMEM((d, de), BF16), pltpu.VMEM((d, de), BF16), pltpu.VMEM((de, d), BF16),
            pltpu.VMEM((2, MOE_TM, d // 2), jnp.uint32), pltpu.VMEM((2, MOE_TM, d // 2), jnp.uint32),
            pltpu.SemaphoreType.DMA((2,)), pltpu.SemaphoreType.DMA((2,)),
        ],
    )
    return pl.pallas_call(
        _gmm_body,
        grid_spec=grid_spec,
        out_shape=jax.ShapeDtypeStruct((out_rows, d // 2), jnp.uint32),
        compiler_params=_params(("arbitrary",)),
        name="moe_experts",
    )(tile_exp, n_used, src, src, dst, xl, w1, w3, w2)


def _combine_body(*refs, d, rows, nb, final):
    if final:
        tot_ref, yl_ref, rc_ref, xr_ref, mod_ref, gfin_ref, o_ref, t_scr = refs
    else:
        tot_ref, yl_ref, rc_ref, xr_ref, mod_ref, o_ref = refs
    wpt = CR // TM
    ci = lax.broadcasted_iota(jnp.int32, (TM, rows), 1)
    parts = []
    for k in range(wpt):
        keep = lax.broadcasted_iota(jnp.int32, (rows, 1), 0) < tot_ref[pl.program_id(0) * wpt + k]
        y = _unpack_bf16_pairs(jnp.where(keep, yl_ref[k * rows:(k + 1) * rows, :], jnp.uint32(0))).astype(BF16)
        rc = rc_ref[k * TM:(k + 1) * TM, :]
        pw = (jnp.where(ci == rc[:, 0:1].astype(jnp.int32), rc[:, 2:3], 0.0)
              + jnp.where(ci == rc[:, 1:2].astype(jnp.int32), rc[:, 3:4], 0.0))
        parts.append(_dot(pw.astype(BF16), y))
    moe = jnp.concatenate(parts, axis=0)
    g2 = mod_ref[0][:, 5 * d:6 * d]
    xn = xr_ref[...] + _per_batch(moe, lambda v: v * g2[None])
    if final:
        xn = _rms(xn, gfin_ref[...])
        for b, blk in enumerate(_to_batch_major(xn, t_scr, nb)):
            o_ref[b] = blk
    else:
        o_ref[...] = xn


def _combine(totals, yl, rc, xres, mod, *, n_exp, n_win, n_lat_rows, nb, gfin=None, seq_out=None):
    d = xres.shape[1]
    rows = _local_rows(n_exp)
    final = gfin is not None
    n_lat = n_lat_rows // CR
    wpt = CR // TM
    row = lambda width: pl.BlockSpec((CR, width), lambda w, a: (w, 0))
    in_specs = [pl.BlockSpec((wpt * rows, d // 2), lambda w, a: (w, 0)), row(LANES), row(d),
                pl.BlockSpec((1, SUBLANES, N_MOD * d), lambda w, a: (jnp.where(w < n_lat, 0, 1), 0, 0))]
    args = [totals, yl, rc, xres, mod]
    scratch = []
    if final:
        in_specs.append(pl.BlockSpec((1, d), lambda w, a: (0, 0)))
        args.append(gfin.reshape(1, d))
        out_shape = jax.ShapeDtypeStruct((nb, seq_out, d), F32)
        out_specs = pl.BlockSpec((nb, CR // nb, d), lambda w, a: (0, w, 0))
        scratch.append(pltpu.VMEM((d // LANES, CR, LANES), F32))
        aliases = {}
    else:
        out_shape = jax.ShapeDtypeStruct(xres.shape, F32)
        out_specs = row(d)
        aliases = {3: 0}
    grid_spec = pltpu.PrefetchScalarGridSpec(
        num_scalar_prefetch=1, grid=(n_win // wpt,), in_specs=in_specs, out_specs=out_specs,
        scratch_shapes=scratch)
    return pl.pallas_call(
        functools.partial(_combine_body, d=d, rows=rows, nb=nb, final=final),
        grid_spec=grid_spec,
        out_shape=out_shape,
        input_output_aliases=aliases,
        compiler_params=_params(("arbitrary",)),
        name="moe_combine",
    )(*args)


def _route_tables(cnt, *, n_exp, n_tiles):
    cnt = cnt[..., 0].astype(jnp.int32)
    n_win = cnt.shape[0]
    lrows = _local_rows(n_exp)
    npad = (cnt + (SUBLANES - 1)) & (-SUBLANES)
    seg = -(-jnp.sum(npad, axis=0) // MOE_TM) * MOE_TM
    ends = jnp.cumsum(seg)
    goff = (ends - seg)[None, :] + jnp.cumsum(npad, axis=0) - npad
    loff = jnp.cumsum(npad, axis=1) - npad
    tile_ends = ends // MOE_TM
    n_used = tile_ends[-1]
    tile = jnp.minimum(jnp.arange(n_tiles, dtype=jnp.int32), n_used - 1)
    tile_exp = jnp.sum(tile[:, None] >= tile_ends[None, :], axis=1).astype(jnp.int32)

    starts = (goff.T // SUBLANES).reshape(-1)
    lens = (npad.T // SUBLANES).reshape(-1)
    local = ((jnp.arange(n_win, dtype=jnp.int32)[:, None] * lrows + loff).T // SUBLANES).reshape(-1)
    group = jnp.arange(n_tiles * CPT, dtype=jnp.int32)
    half = 1 << 16
    assert n_tiles * CPT < half // 2 and n_win * lrows // SUBLANES < half // 2
    packed = (starts + lens) * half + (local - starts + half // 2)
    began = starts[None, :] <= group[:, None]
    tot = jnp.sum(jnp.where(began, jnp.diff(packed, prepend=0)[None, :], 0), axis=1)
    valid = group < tot // half
    src = jnp.where(valid, group + (tot % half) - half // 2, 0)
    dst = jnp.where(valid, src, n_win * lrows // SUBLANES + group)
    shape = (n_tiles, 1, CPT)
    return (tile_exp, n_used.reshape(1).astype(jnp.int32), src.reshape(shape).astype(jnp.int32),
            dst.reshape(shape).astype(jnp.int32), jnp.sum(npad, axis=1).astype(jnp.int32))


def _channel_dft_tables(n):
    k = jnp.arange(n, dtype=jnp.int32)
    ang = ((k[:, None] * k[None, :]) % n).astype(F32) * (2.0 * math.pi / n)
    scale = 1.0 / math.sqrt(n)
    return jnp.cos(ang) * scale, jnp.sin(ang) * scale


def kernel(x, c, ctx, c_ctx, ada_w, ada_b, norm_mix_g, w_in, conv_w, conv_b, lru_w_a, lru_b_a, lru_w_x, lru_b_x, lru_lambda, w_proj_rnn, w_proj_fourier, w_out, norm_ffn_g, router_w, router_b, moe_w1, moe_w3, moe_w2, final_norm_g):
    nb, l, d = x.shape
    lc = ctx.shape[1]
    depth = ada_w.shape[0]
    r = conv_w.shape[2]
    f = w_proj_fourier.shape[1]
    nh, blk = lru_w_a.shape[2], lru_w_a.shape[3]
    assert nb == SUBLANES and blk == LANES and nb < MOD_ROWS
    assert l % TS == 0 and lc % TS == 0 and TS % GRID_W == 0 and (l * nb) % CR == 0 and (lc * nb) % CR == 0 and CR % TR == 0
    assert GRID_W & (GRID_W - 1) == 0 and l % lc == 0 and l % (2 * FOLD_TILE) == 0 and l // LANES <= LANES
    s = l + lc
    n_lat_rows = l * nb

    cc = jnp.zeros((MOD_ROWS, d), F32).at[:nb].set(c).at[nb].set(c_ctx)
    mod = _ada(cc, ada_w, ada_b)
    mod = jnp.stack([mod[:, :nb], jnp.broadcast_to(mod[:, nb:nb + 1], (depth, nb, N_MOD * d))], axis=1)

    cch, sch = _channel_dft_tables(f // FOURIER_GROUPS)
    fcs = jnp.concatenate([cch, sch], axis=1).astype(BF16)
    n_exp = router_w.shape[1]
    rw = jnp.concatenate(_split(jnp.pad(router_w, ((0, 0), (0, LANES - n_exp)))), axis=1)

    out = xa = None
    for li in range(depth):
        last = li == depth - 1
        inproj = functools.partial(_inproj, mod=mod[li], g=norm_mix_g[li], w_in=w_in[li].astype(BF16), fcs=fcs,
                                   n_lat_rows=n_lat_rows, r=r, f=f, nb=nb)
        if li == 0:
            u, gl, sa, sb, pq, xa = inproj(x, ctx=ctx)
        else:
            u, gl, sa, sb, pq = inproj(xa)
        wg = [jnp.concatenate([lru_w_a[li, k], lru_w_x[li, k]], axis=-1).astype(BF16) for k in range(2)]
        bg = [jnp.concatenate([lru_b_a[li, k], lru_b_x[li, k]], axis=-1).reshape(nh, 1, 2 * blk) for k in range(2)]
        lam = [lru_lambda[li, k].reshape(nh, 1, blk) for k in range(2)]
        hb, uc = _scan(u, wg[1], bg[1], lam[1], nb=nb, n_lat_steps=l // TS, conv=(conv_w[li], conv_b[li]))
        hg = _scan(uc, wg[0], bg[0], lam[0], nb=nb, n_lat_steps=l // TS, hb=hb, gl=gl)
        n_rows = n_lat_rows if last else s * nb
        yf = _dft(pq, n=l, nb=nb, f=f, row0=0, out_rows=n_rows // nb, folded=_fold(pq, n=l, nb=nb, f=f))
        if not last:
            yf = _dft(pq, n=lc, nb=nb, f=f, row0=l, out_rows=n_rows // nb, prev=yf)
        x1, xl, rc, cnt = _mix(hg, yf, sa, sb, xa, mod[li], w_proj_rnn[li].astype(BF16),
                               w_proj_fourier[li].astype(BF16), w_out[li].astype(BF16), norm_ffn_g[li],
                               rw, router_b, n_rows=n_rows, n_lat_rows=n_lat_rows, nb=nb)
        n_win = n_rows // TM
        n_tiles = -(-(n_win * (2 * TM + n_exp * (SUBLANES - 1)) + n_exp * MOE_TM) // MOE_TM)
        tile_exp, n_used, src, dst, totals = _route_tables(cnt, n_exp=n_exp, n_tiles=n_tiles)
        yl = _gmm(tile_exp, n_used, src, dst, xl, moe_w1, moe_w3, moe_w2, layer=li,
                  out_rows=xl.shape[0] + n_tiles * MOE_TM)
        comb = functools.partial(_combine, totals, yl, rc, x1, mod[li], n_exp=n_exp, n_win=n_win,
                                 n_lat_rows=n_lat_rows, nb=nb)
        if last:
            out = comb(gfin=final_norm_g, seq_out=l)
        else:
            xa = comb()
    return out
```
